```python
import math
import jax, jax.numpy as jnp
from jax import lax
import numpy as np

D_MODEL = 2048
BATCH = 4
SEQ = 2048
DEPTH = 4
DEC_BATCH = 32
DEC_SEQ = 32
PAST_LEN = 4096

CHUNK = 64
MIX_WIDTH = D_MODEL
D_SSM = MIX_WIDTH // 2
SSM_HEADDIM = 64
SSM_HEADS = D_SSM // SSM_HEADDIM
SSM_GROUPS = 2
SSM_STATE = 128
CONV_W = 4
CONV_DIM = D_SSM + 2 * SSM_GROUPS * SSM_STATE
ATT_WIDTH = MIX_WIDTH - D_SSM
HEAD_DIM = 128
N_HEADS = ATT_WIDTH // HEAD_DIM
N_KV = 2
KV_GROUP = N_HEADS // N_KV
ROPE_DIM = HEAD_DIM // 4
IDX_HEADS = 16
IDX_DIM = 64
IDX_ROPE_DIM = IDX_DIM // 4
TOPK_MAX = 256
ROPE_THETA = 500000.0
D_FF = 5632
Q_BLOCK = 128
EPS = 1e-6
SPLIT_SIZES = (D_SSM, CONV_DIM, SSM_HEADS, N_HEADS * HEAD_DIM, N_KV * HEAD_DIM, N_KV * HEAD_DIM, IDX_HEADS * IDX_DIM, IDX_DIM, IDX_HEADS)
IN_COLS = sum(SPLIT_SIZES)

kernel_name = 'hybrid_ssd_dsa_streaming_encoder_step'

F32 = jnp.float32


def rmsnorm(x, w):
    xf = x.astype(F32)
    y = xf * lax.rsqrt(jnp.mean(xf * xf, axis=-1, keepdims=True) + EPS)
    return (y * w.astype(F32)).astype(x.dtype)


def swiglu(h, w1, w3, w2):
    return (jax.nn.silu(h @ w1) * (h @ w3)) @ w2


def rope(x, pos, rot_dim):
    half = rot_dim // 2
    inv_freq = jnp.float32(ROPE_THETA) ** (-jnp.arange(half, dtype=F32) * 2.0 / rot_dim)
    ang = pos.astype(F32)[:, None] * inv_freq[None, :]
    cos = jnp.cos(ang)[None, :, None, :]
    sin = jnp.sin(ang)[None, :, None, :]
    xf = x.astype(F32)
    x1 = xf[..., :half]
    x2 = xf[..., half:rot_dim]
    out = jnp.concatenate([x1 * cos - x2 * sin, x2 * cos + x1 * sin, xf[..., rot_dim:]], axis=-1)
    return out.astype(x.dtype)


def split_cols(proj):
    points = []
    acc = 0
    for s in SPLIT_SIZES[:-1]:
        acc += s
        points.append(acc)
    return jnp.split(proj, points, axis=-1)


def ssd_scan(x, dt, a, bh, ch, state0):
    bsz, t_len = x.shape[:2]
    pad = (-t_len) % CHUNK
    nc = (t_len + pad) // CHUNK

    def blocks(u):
        u = jnp.pad(u, [(0, 0), (0, pad)] + [(0, 0)] * (u.ndim - 2))
        return jnp.moveaxis(u.reshape((bsz, nc, CHUNK) + u.shape[2:]), 1, 0)

    tri = jnp.tril(jnp.ones((CHUNK, CHUNK), dtype=bool))[None, :, :, None]

    def step(s, inp):
        xc, dtc, bc, cc = inp
        cum = jnp.cumsum(dtc * a, axis=1)
        seg = jnp.exp(jnp.where(tri, cum[:, :, None, :] - cum[:, None, :, :], -jnp.inf))
        xdt = xc * dtc[..., None]
        cb = jnp.einsum('bthn,bshn->btsh', cc, bc)
        y = jnp.einsum('btsh,bshp->bthp', cb * seg, xdt)
        y = y + jnp.einsum('bthn,bhpn->bthp', cc, s) * jnp.exp(cum)[..., None]
        tail = jnp.exp(cum[:, -1:, :] - cum)
        s_new = s * jnp.exp(cum[:, -1, :])[:, :, None, None] + jnp.einsum('bshn,bshp->bhpn', bc * tail[..., None], xdt)
        return s_new, y

    s_fin, ys = lax.scan(step, state0, (blocks(x), blocks(dt), blocks(bh), blocks(ch)))
    y = jnp.moveaxis(ys, 0, 1).reshape((bsz, nc * CHUNK) + x.shape[2:])[:, :t_len]
    return y, s_fin


def dsa_attention(q, iq, iw, pos, k_all, v_all, ik_all):
    bsz, t_len = q.shape[:2]
    n_keys = k_all.shape[1]
    topk = min(TOPK_MAX, n_keys // 4)
    qb = min(Q_BLOCK, t_len)
    nb = t_len // qb
    key_pos = jnp.arange(n_keys, dtype=jnp.int32)
    ik_f = ik_all.astype(F32)
    scale = HEAD_DIM ** -0.5
    gather = jax.vmap(lambda rows, ids: rows[ids])

    def one_block(args):
        qx, iqx, iwx, px = args
        idx_logits = jnp.einsum('bqhd,bsd->bqhs', iqx.astype(F32), ik_f)
        score = jnp.einsum('bqh,bqhs->bqs', iwx.astype(F32), jax.nn.relu(idx_logits))
        limit = (px // CHUNK + 1) * CHUNK
        admissible = key_pos[None, :] < limit[:, None]
        score = jnp.where(admissible[None], score, -jnp.inf)
        vals, sel = lax.top_k(score, topk)
        valid = jnp.isfinite(vals)
        k_sel = gather(k_all, sel).astype(F32)
        v_sel = gather(v_all, sel).astype(F32)
        qg = qx.astype(F32).reshape(bsz, qb, N_KV, KV_GROUP, HEAD_DIM)
        logits = jnp.einsum('bqhgd,bqshd->bqhgs', qg, k_sel) * scale
        logits = jnp.where(valid[:, :, None, None, :], logits, -jnp.inf)
        probs = jax.nn.softmax(logits, axis=-1)
        out = jnp.einsum('bqhgs,bqshd->bqhgd', probs, v_sel)
        return out.reshape(bsz, qb, N_HEADS * HEAD_DIM).astype(q.dtype)

    def blocks(u):
        return jnp.moveaxis(u.reshape((bsz, nb, qb) + u.shape[2:]), 1, 0)

    outs = lax.map(one_block, (blocks(q), blocks(iq), blocks(iw), pos.reshape(nb, qb)))
    return jnp.moveaxis(outs, 0, 1).reshape(bsz, t_len, N_HEADS * HEAD_DIM)


def hybrid_mixer(h, pos, k_past, v_past, ik_past, ssm0, conv0, w_in, conv_w, conv_b, dt_bias, a_log, d_skip, ssm_norm_w, w_out):
    bsz, t_len = h.shape[:2]
    z, xbc, dt_raw, q, k, v, iq, ik, iw = split_cols(h @ w_in)
    ext = jnp.concatenate([conv0.astype(xbc.dtype), xbc], axis=1)
    conv = conv_b + sum(ext[:, j:j + t_len] * conv_w[j] for j in range(CONV_W))
    xbc_c = jax.nn.silu(conv).astype(F32)
    xs, bm, cm = jnp.split(xbc_c, [D_SSM, D_SSM + SSM_GROUPS * SSM_STATE], axis=-1)
    xs = xs.reshape(bsz, t_len, SSM_HEADS, SSM_HEADDIM)
    rep = SSM_HEADS // SSM_GROUPS
    bh = jnp.repeat(bm.reshape(bsz, t_len, SSM_GROUPS, SSM_STATE), rep, axis=2)
    ch = jnp.repeat(cm.reshape(bsz, t_len, SSM_GROUPS, SSM_STATE), rep, axis=2)
    dt = jax.nn.softplus(dt_raw.astype(F32) + dt_bias.astype(F32))
    a = -jnp.exp(a_log.astype(F32))
    y, ssm1 = ssd_scan(xs, dt, a, bh, ch, ssm0.astype(F32))
    y = (y + d_skip.astype(F32)[:, None] * xs).reshape(bsz, t_len, D_SSM)
    y_ssd = rmsnorm(y * jax.nn.silu(z.astype(F32)), ssm_norm_w).astype(h.dtype)
    conv1 = ext[:, -(CONV_W - 1):]
    q = rope(q.reshape(bsz, t_len, N_HEADS, HEAD_DIM), pos, ROPE_DIM)
    k = rope(k.reshape(bsz, t_len, N_KV, HEAD_DIM), pos, ROPE_DIM)
    v = v.reshape(bsz, t_len, N_KV, HEAD_DIM)
    iq = rope(iq.reshape(bsz, t_len, IDX_HEADS, IDX_DIM), pos, IDX_ROPE_DIM)
    ik = rope(ik[:, :, None, :], pos, IDX_ROPE_DIM)[:, :, 0]
    iw = iw * (IDX_HEADS ** -0.5)
    k_all = jnp.concatenate([k_past.astype(k.dtype), k], axis=1)
    v_all = jnp.concatenate([v_past.astype(v.dtype), v], axis=1)
    ik_all = jnp.concatenate([ik_past.astype(ik.dtype), ik], axis=1)
    att = dsa_attention(q, iq, iw, pos, k_all, v_all, ik_all)
    out = jnp.concatenate([y_ssd, att], axis=-1) @ w_out
    return out, k, v, ik, ssm1.astype(h.dtype), conv1


def layer(x, pos, k_past, v_past, ik_past, ssm0, conv0,
          norm_ffn1, ffn1_w1, ffn1_w3, ffn1_w2, norm_mix, w_in, conv_w, conv_b, dt_bias, a_log, d_skip,
          ssm_norm_w, w_out, norm_ffn2, ffn2_w1, ffn2_w3, ffn2_w2):
    x = x + 0.5 * swiglu(rmsnorm(x, norm_ffn1), ffn1_w1, ffn1_w3, ffn1_w2)
    mixed, k_new, v_new, ik_new, ssm1, conv1 = hybrid_mixer(
        rmsnorm(x, norm_mix), pos, k_past, v_past, ik_past, ssm0, conv0,
        w_in, conv_w, conv_b, dt_bias, a_log, d_skip, ssm_norm_w, w_out)
    x = x + mixed
    x = x + 0.5 * swiglu(rmsnorm(x, norm_ffn2), ffn2_w1, ffn2_w3, ffn2_w2)
    return x, k_new, v_new, ik_new, ssm1, conv1


def setup_inputs(seed: int = 0) -> dict:
    key = jax.random.key(seed)
    ks = jax.random.split(key, 32)

    def nrm(k, shape, scale=1.0):
        return jax.random.normal(k, shape, F32) * scale

    def gain(k, shape):
        return 1.0 + nrm(k, shape, 0.01)

    dt0 = jnp.exp(jax.random.uniform(ks[20], (DEPTH, SSM_HEADS), F32, math.log(1e-3), math.log(1e-1)))
    dt_bias = dt0 + jnp.log(-jnp.expm1(-dt0))
    a_log = jnp.log(jax.random.uniform(ks[21], (DEPTH, SSM_HEADS), F32, 1.0, 16.0))
    return {
        'x_prompt': nrm(ks[0], (BATCH, SEQ, D_MODEL)),
        'x_sample': nrm(ks[1], (DEC_BATCH, DEC_SEQ, D_MODEL)),
        'cache_k': nrm(ks[2], (DEPTH, DEC_BATCH, PAST_LEN, N_KV, HEAD_DIM)),
        'cache_v': nrm(ks[3], (DEPTH, DEC_BATCH, PAST_LEN, N_KV, HEAD_DIM)),
        'cache_idx_k': nrm(ks[4], (DEPTH, DEC_BATCH, PAST_LEN, IDX_DIM)),
        'state_ssm': nrm(ks[5], (DEPTH, DEC_BATCH, SSM_HEADS, SSM_HEADDIM, SSM_STATE), 0.1),
        'state_conv': nrm(ks[6], (DEPTH, DEC_BATCH, CONV_W - 1, CONV_DIM)),
        'norm_ffn1': gain(ks[7], (DEPTH, D_MODEL)),
        'ffn1_w1': nrm(ks[8], (DEPTH, D_MODEL, D_FF), D_MODEL ** -0.5),
        'ffn1_w3': nrm(ks[9], (DEPTH, D_MODEL, D_FF), D_MODEL ** -0.5),
        'ffn1_w2': nrm(ks[10], (DEPTH, D_FF, D_MODEL), D_FF ** -0.5),
        'norm_mix': gain(ks[11], (DEPTH, D_MODEL)),
        'w_in': nrm(ks[12], (DEPTH, D_MODEL, IN_COLS), D_MODEL ** -0.5),
        'conv_w': nrm(ks[13], (DEPTH, CONV_W, CONV_DIM), CONV_W ** -0.5),
        'conv_b': nrm(ks[14], (DEPTH, CONV_DIM), 0.01),
        'dt_bias': dt_bias,
        'a_log': a_log,
        'd_skip': gain(ks[15], (DEPTH, SSM_HEADS)),
        'ssm_norm_w': gain(ks[16], (DEPTH, D_SSM)),
        'w_out': nrm(ks[17], (DEPTH, MIX_WIDTH, D_MODEL), MIX_WIDTH ** -0.5),
        'norm_ffn2': gain(ks[18], (DEPTH, D_MODEL)),
        'ffn2_w1': nrm(ks[22], (DEPTH, D_MODEL, D_FF), D_MODEL ** -0.5),
        'ffn2_w3': nrm(ks[23], (DEPTH, D_MODEL, D_FF), D_MODEL ** -0.5),
        'ffn2_w2': nrm(ks[24], (DEPTH, D_FF, D_MODEL), D_FF ** -0.5),
        'final_norm': gain(ks[19], (D_MODEL,)),
    }


def reference(x_prompt, x_sample, cache_k, cache_v, cache_idx_k, state_ssm, state_conv,
              norm_ffn1, ffn1_w1, ffn1_w3, ffn1_w2, norm_mix, w_in, conv_w, conv_b, dt_bias, a_log, d_skip,
              ssm_norm_w, w_out, norm_ffn2, ffn2_w1, ffn2_w3, ffn2_w2, final_norm):
    bp, tp = x_prompt.shape[:2]
    ts = x_sample.shape[1]
    past = cache_k.shape[2]
    pos_p = jnp.arange(tp, dtype=jnp.int32)
    pos_s = past + jnp.arange(ts, dtype=jnp.int32)
    dtp = x_prompt.dtype
    empty_k = jnp.zeros((bp, 0, N_KV, HEAD_DIM), dtp)
    empty_ik = jnp.zeros((bp, 0, IDX_DIM), dtp)
    zero_ssm = jnp.zeros((bp, SSM_HEADS, SSM_HEADDIM, SSM_STATE), dtp)
    zero_conv = jnp.zeros((bp, CONV_W - 1, CONV_DIM), dtp)
    xp, xs = x_prompt, x_sample
    kp_l, vp_l, ikp_l, sp_l, cp_l = [], [], [], [], []
    ks_l, vs_l, iks_l, ss_l, cs_l = [], [], [], [], []
    for l in range(DEPTH):
        lp = (norm_ffn1[l], ffn1_w1[l], ffn1_w3[l], ffn1_w2[l], norm_mix[l], w_in[l], conv_w[l], conv_b[l],
              dt_bias[l], a_log[l], d_skip[l], ssm_norm_w[l], w_out[l], norm_ffn2[l], ffn2_w1[l], ffn2_w3[l], ffn2_w2[l])
        xp, kp, vp, ikp, sp, cp = layer(xp, pos_p, empty_k, empty_k, empty_ik, zero_ssm, zero_conv, *lp)
        xs, kn, vn, ikn, sn, cn = layer(xs, pos_s, cache_k[l], cache_v[l], cache_idx_k[l], state_ssm[l], state_conv[l], *lp)
        kp_l.append(kp); vp_l.append(vp); ikp_l.append(ikp); sp_l.append(sp); cp_l.append(cp)
        ks_l.append(kn); vs_l.append(vn); iks_l.append(ikn); ss_l.append(sn); cs_l.append(cn)
    y_prompt = rmsnorm(xp, final_norm)
    y_sample = rmsnorm(xs, final_norm)
    return (y_prompt, y_sample,
            jnp.stack(kp_l), jnp.stack(vp_l), jnp.stack(ikp_l), jnp.stack(sp_l), jnp.stack(cp_l),
            jnp.stack(ks_l), jnp.stack(vs_l), jnp.stack(iks_l), jnp.stack(ss_l), jnp.stack(cs_l))
```

```python
import functools
import math

import jax
import jax.numpy as jnp
from jax import lax
from jax.experimental import pallas as pl
from jax.experimental.pallas import tpu as pltpu

F32 = jnp.float32
MXU_DTYPE = jnp.bfloat16

D_MODEL = 2048
CHUNK = 64
D_SSM = 1024
SSM_HEADDIM = 64
SSM_HEADS = 16
SSM_GROUPS = 2
SSM_STATE = 128
CONV_W = 4
CONV_DIM = D_SSM + 2 * SSM_GROUPS * SSM_STATE
HEAD_DIM = 128
N_HEADS = 8
N_KV = 2
KV_GROUP = N_HEADS // N_KV
ROPE_DIM = HEAD_DIM // 4
IDX_HEADS = 16
IDX_DIM = 64
IDX_ROPE_DIM = IDX_DIM // 4
TOPK_MAX = 256
ROPE_THETA = 500000.0
D_FF = 5632
EPS = 1e-6

LANE = 128
SUBLANE = 8
VMEM_LIMIT = 56 * 1024 * 1024

COL_Z = 0
COL_Q = 1024
COL_IQ = 2048
COL_XBC = 3072
COL_K = 4608
COL_V = 4864
COL_DT = 5120
COL_IKW = 5248
PROJ_COLS = 5376

KEY_NEG_INF = -2139095041
KEY_POS_INF = 2139095040
INT_MIN = -2147483648

NT_DIMS = (((1,), (1,)), ((), ()))


def _mm(a, b):
    return jnp.dot(a, b, preferred_element_type=F32)


def _mm_nt(a, b):
    return lax.dot_general(a, b, NT_DIMS, preferred_element_type=F32)


def _mm_exact(a, b):
    return jnp.dot(a, b, preferred_element_type=F32, precision=lax.Precision.HIGHEST)


def _silu(x):
    return x * jax.nn.sigmoid(x)


def _params(sem, vmem=VMEM_LIMIT):
    return pltpu.CompilerParams(dimension_semantics=sem, vmem_limit_bytes=vmem)


def _row_tile(m, pref):
    t = pref
    while m % t:
        t //= 2
    return t


def _ffn_kernel(x_ref, nw_ref, w1_ref, w3_ref, w2_ref, o_ref, h_scr, acc_scr):
    f = pl.program_id(1)

    @pl.when(f == 0)
    def _():
        x = x_ref[...]
        ms = jnp.mean(x * x, axis=-1, keepdims=True)
        h_scr[...] = (x * lax.rsqrt(ms + EPS) * nw_ref[...]).astype(h_scr.dtype)
        acc_scr[...] = jnp.zeros_like(acc_scr)

    h = h_scr[...]
    a = _mm(h, w1_ref[...])
    b = _mm(h, w3_ref[...])
    g = (_silu(a) * b).astype(h_scr.dtype)
    acc_scr[...] += _mm(g, w2_ref[...])

    @pl.when(f == pl.num_programs(1) - 1)
    def _():
        o_ref[...] = x_ref[...] + 0.5 * acc_scr[...]


def _ffn(x, nw, w1, w3, w2, tm=512, tf=512):
    m, d = x.shape
    dff = w1.shape[1]
    tm = _row_tile(m, tm)
    return pl.pallas_call(
        _ffn_kernel,
        grid=(m // tm, dff // tf),
        in_specs=[
            pl.BlockSpec((tm, d), lambda i, f: (i, 0)),
            pl.BlockSpec((1, d), lambda i, f: (0, 0)),
            pl.BlockSpec((d, tf), lambda i, f: (0, f)),
            pl.BlockSpec((d, tf), lambda i, f: (0, f)),
            pl.BlockSpec((tf, d), lambda i, f: (f, 0)),
        ],
        out_specs=pl.BlockSpec((tm, d), lambda i, f: (i, 0)),
        out_shape=jax.ShapeDtypeStruct((m, d), F32),
        scratch_shapes=[pltpu.VMEM((tm, d), MXU_DTYPE), pltpu.VMEM((tm, d), F32)],
        compiler_params=_params(("parallel", "arbitrary")),
        name="ffn",
    )(x, nw, w1, w3, w2)


def _in_proj_kernel(x_ref, nw_ref, w_ref, o_ref, h_scr):
    @pl.when(pl.program_id(1) == 0)
    def _():
        x = x_ref[...]
        ms = jnp.mean(x * x, axis=-1, keepdims=True)
        h_scr[...] = (x * lax.rsqrt(ms + EPS) * nw_ref[...]).astype(h_scr.dtype)

    o_ref[...] = _mm(h_scr[...], w_ref[...])


def _in_proj(x, nw, w, tm=1024, tn=768):
    m, d = x.shape
    n = w.shape[1]
    tm = _row_tile(m, tm)
    return pl.pallas_call(
        _in_proj_kernel,
        grid=(m // tm, n // tn),
        in_specs=[
            pl.BlockSpec((tm, d), lambda i, j: (i, 0)),
            pl.BlockSpec((1, d), lambda i, j: (0, 0)),
            pl.BlockSpec((d, tn), lambda i, j: (0, j)),
        ],
        out_specs=pl.BlockSpec((tm, tn), lambda i, j: (i, j)),
        out_shape=jax.ShapeDtypeStruct((m, n), F32),
        scratch_shapes=[pltpu.VMEM((tm, d), MXU_DTYPE)],
        compiler_params=_params(("parallel", "arbitrary")),
        name="in_proj",
    )(x, nw, w)


def _out_proj_kernel(x_ref, ys_ref, att_ref, w_ref, o_ref):
    half = ys_ref.shape[1]
    o_ref[...] = (x_ref[...] + _mm(ys_ref[...], w_ref[:half, :])
                  + _mm(att_ref[...], w_ref[half:, :]))


def _out_proj(x, ys, att, w, tm=512):
    m, d = x.shape
    half = ys.shape[1]
    tm = _row_tile(m, tm)
    return pl.pallas_call(
        _out_proj_kernel,
        grid=(m // tm,),
        in_specs=[
            pl.BlockSpec((tm, d), lambda i: (i, 0)),
            pl.BlockSpec((tm, half), lambda i: (i, 0)),
            pl.BlockSpec((tm, half), lambda i: (i, 0)),
            pl.BlockSpec((2 * half, d), lambda i: (0, 0)),
        ],
        out_specs=pl.BlockSpec((tm, d), lambda i: (i, 0)),
        out_shape=jax.ShapeDtypeStruct((m, d), F32),
        compiler_params=_params(("parallel",)),
        name="out_proj",
    )(x, ys, att, w)


def _final_norm_kernel(x_ref, nw_ref, o_ref):
    x = x_ref[...]
    ms = jnp.mean(x * x, axis=-1, keepdims=True)
    o_ref[...] = x * lax.rsqrt(ms + EPS) * nw_ref[...]


def _final_norm(x, nw, tm=512):
    m, d = x.shape
    tm = _row_tile(m, tm)
    return pl.pallas_call(
        _final_norm_kernel,
        grid=(m // tm,),
        in_specs=[pl.BlockSpec((tm, d), lambda i: (i, 0)),
                  pl.BlockSpec((1, d), lambda i: (0, 0))],
        out_specs=pl.BlockSpec((tm, d), lambda i: (i, 0)),
        out_shape=jax.ShapeDtypeStruct((m, d), F32),
        compiler_params=_params(("parallel",)),
        name="final_norm",
    )(x, nw)


def _rope_lanes(x, c, s1, s2, half):
    outs = []
    for t in range(x.shape[1] // LANE):
        xt = x[:, t * LANE:(t + 1) * LANE]
        outs.append(xt * c + pltpu.roll(xt, half, 1) * s1
                    + pltpu.roll(xt, LANE - half, 1) * s2)
    return outs[0] if len(outs) == 1 else jnp.concatenate(outs, axis=1)


def _prep_kernel(q_ref, iq_ref, k_ref, v_ref, ikw_ref, tab_ref,
                 qo_ref, iqo_ref, ko_ref, kbo_ref, vo_ref, vbo_ref, ikwo_ref):
    tab = [tab_ref[:, i * LANE:(i + 1) * LANE] for i in range(9)]
    qo_ref[...] = _rope_lanes(q_ref[...], tab[0], tab[1], tab[2], ROPE_DIM // 2).astype(qo_ref.dtype)
    k = _rope_lanes(k_ref[...], tab[0], tab[1], tab[2], ROPE_DIM // 2)
    ko_ref[...] = k
    kbo_ref[...] = k.astype(kbo_ref.dtype)
    iqo_ref[...] = _rope_lanes(iq_ref[...], tab[3], tab[4], tab[5], IDX_ROPE_DIM // 2).astype(iqo_ref.dtype)
    ikwo_ref[...] = _rope_lanes(ikw_ref[...], tab[6], tab[7], tab[8], IDX_ROPE_DIM // 2)
    v = v_ref[...]
    vo_ref[...] = v
    vbo_ref[...] = v.astype(vbo_ref.dtype)


def _prep(proj, tab, tm=512):
    m = proj.shape[0]
    tm = _row_tile(m, tm)
    wq = N_HEADS * HEAD_DIM
    wkv = N_KV * HEAD_DIM
    row = lambda w, c: pl.BlockSpec((tm, w), lambda i: (i, c))
    return pl.pallas_call(
        _prep_kernel,
        grid=(m // tm,),
        in_specs=[row(wq, COL_Q // wq), row(wq, COL_IQ // wq), row(wkv, COL_K // wkv),
                  row(wkv, COL_V // wkv), row(LANE, COL_IKW // LANE), row(9 * LANE, 0)],
        out_specs=[row(wq, 0), row(wq, 0), row(wkv, 0), row(wkv, 0), row(wkv, 0), row(wkv, 0),
                   row(LANE, 0)],
        out_shape=[jax.ShapeDtypeStruct((m, wq), MXU_DTYPE),
                   jax.ShapeDtypeStruct((m, wq), MXU_DTYPE),
                   jax.ShapeDtypeStruct((m, wkv), F32),
                   jax.ShapeDtypeStruct((m, wkv), MXU_DTYPE),
                   jax.ShapeDtypeStruct((m, wkv), F32),
                   jax.ShapeDtypeStruct((m, wkv), MXU_DTYPE),
                   jax.ShapeDtypeStruct((m, LANE), F32)],
        compiler_params=_params(("parallel",)),
        name="prep",
    )(proj, proj, proj, proj, proj, tab)


def _rope_table(pos, rot_dim, period, tail=None):
    half = rot_dim // 2
    inv_freq = jnp.float32(ROPE_THETA) ** (-jnp.arange(half, dtype=F32) * 2.0 / rot_dim)
    ang = pos.astype(F32)[:, None] * inv_freq[None, :]
    cos, sin = jnp.cos(ang), jnp.sin(ang)
    n = pos.shape[0]
    ones = jnp.ones((n, period - rot_dim), F32)
    zeros = jnp.zeros((n, period - rot_dim), F32)
    zh = jnp.zeros((n, half), F32)
    c = jnp.concatenate([cos, cos, ones], axis=1)
    s1 = jnp.concatenate([zh, sin, zeros], axis=1)
    s2 = jnp.concatenate([-sin, zh, zeros], axis=1)
    if tail is None:
        reps = LANE // period
        return [jnp.tile(a, (1, reps)) for a in (c, s1, s2)]
    zt = jnp.zeros((n, LANE - period), F32)
    return [jnp.concatenate([c, jnp.broadcast_to(tail[None, :], (n, LANE - period))], axis=1),
            jnp.concatenate([s1, zt], axis=1), jnp.concatenate([s2, zt], axis=1)]


def _ssd_kernel(*refs, qv, qp, has_state):
    if has_state:
        (z_ref, xbc_ref, dt_ref, ssm0_ref, conv0_ref, cw_ref, cb_ref, dtb_ref, aneg_ref,
         dsk_ref, nw_ref, y_ref, ssm_ref, conv_ref, ext_scr, st_scr) = refs
    else:
        (z_ref, xbc_ref, dt_ref, cw_ref, cb_ref, dtb_ref, aneg_ref,
         dsk_ref, nw_ref, y_ref, ssm_ref, conv_ref, ext_scr, st_scr) = refs
    c = pl.program_id(1)
    hp = SSM_HEADS * SSM_HEADDIM
    gw = SSM_STATE
    hpg = hp // SSM_GROUPS

    @pl.when(c == 0)
    def _():
        if has_state:
            ext_scr[0:SUBLANE, :] = conv0_ref[0]
            st_scr[...] = ssm0_ref[0].T
        else:
            ext_scr[0:SUBLANE, :] = jnp.zeros((SUBLANE, CONV_DIM), F32)
            st_scr[...] = jnp.zeros_like(st_scr)

    xbc_raw = xbc_ref[...]
    ext_scr[SUBLANE:SUBLANE + qv, :] = xbc_raw
    conv = cb_ref[...] + xbc_raw * cw_ref[CONV_W - 1:CONV_W, :]
    for j in range(CONV_W - 1):
        off = SUBLANE - (CONV_W - 1) + j
        conv = conv + ext_scr[off:off + qv, :] * cw_ref[j:j + 1, :]
    xc = _silu(conv)
    dtr = dt_ref[...]
    if qv < qp:
        xc = jnp.concatenate([xc, jnp.zeros((qp - qv, CONV_DIM), F32)], axis=0)
        dtr = jnp.concatenate([dtr, jnp.zeros((qp - qv, LANE), F32)], axis=0)
    xs = xc[:, :hp]
    bm = xc[:, hp:hp + SSM_GROUPS * gw]
    cm = xc[:, hp + SSM_GROUPS * gw:]

    xdt_in = dtr + dtb_ref[...]
    dt = jnp.maximum(xdt_in, 0.0) + jnp.log1p(jnp.exp(-jnp.abs(xdt_in)))
    row = lax.broadcasted_iota(jnp.int32, (qp, LANE), 0)
    dt = jnp.where(row < qv, dt, 0.0)
    la = dt * aneg_ref[...]
    ti = lax.broadcasted_iota(jnp.int32, (qp, qp), 0)
    si = lax.broadcasted_iota(jnp.int32, (qp, qp), 1)
    tri = ti >= si
    cum = _mm_exact(tri.astype(F32), la)
    cum_t = cum.T

    eh = lax.broadcasted_iota(jnp.int32, (LANE, hp), 0)
    ec = lax.broadcasted_iota(jnp.int32, (LANE, hp), 1)
    expand = (ec // SSM_HEADDIM == eh).astype(F32)
    dt_x = _mm_exact(dt, expand)
    cum_x = _mm_exact(cum, expand)
    xdt = xs * dt_x

    cb = [_mm_nt(cm[:, g * gw:(g + 1) * gw].astype(MXU_DTYPE),
                 bm[:, g * gw:(g + 1) * gw].astype(MXU_DTYPE)) for g in range(SSM_GROUPS)]
    lane = lax.broadcasted_iota(jnp.int32, (qp, LANE), 1)
    y_parts = []
    for j in range(SSM_HEADS // 2):
        ms = []
        for h in (2 * j, 2 * j + 1):
            d = cum[:, h:h + 1] - cum_t[h:h + 1, :]
            seg = jnp.exp(jnp.where(tri, d, -jnp.inf))
            ms.append((cb[h // (SSM_HEADS // SSM_GROUPS)] * seg).astype(MXU_DTYPE))
        xp = xdt[:, j * LANE:(j + 1) * LANE]
        rhs = jnp.concatenate([jnp.where(lane < SSM_HEADDIM, xp, 0.0),
                               jnp.where(lane >= SSM_HEADDIM, xp, 0.0)], axis=0)
        y_parts.append(_mm(jnp.concatenate(ms, axis=1), rhs.astype(MXU_DTYPE)))
    y = jnp.concatenate(y_parts, axis=1)

    st = st_scr[...]
    y_st = jnp.concatenate(
        [_mm(cm[:, g * gw:(g + 1) * gw].astype(MXU_DTYPE),
             st[:, g * hpg:(g + 1) * hpg].astype(MXU_DTYPE)) for g in range(SSM_GROUPS)], axis=1)
    y = y + y_st * jnp.exp(cum_x) + dsk_ref[...] * xs

    cum_last = cum_x[qp - 1:qp, :]
    xt = (xdt * jnp.exp(cum_last - cum_x)).astype(MXU_DTYPE)
    upd = jnp.concatenate(
        [_mm(bm[:, g * gw:(g + 1) * gw].T.astype(MXU_DTYPE), xt[:, g * hpg:(g + 1) * hpg])
         for g in range(SSM_GROUPS)], axis=1)
    st_new = st * jnp.exp(cum_last) + upd
    st_scr[...] = st_new

    zz = z_ref[...]
    gated = y[:qv, :] * _silu(zz)
    ms2 = jnp.mean(gated * gated, axis=-1, keepdims=True)
    y_ref[...] = (gated * lax.rsqrt(ms2 + EPS) * nw_ref[...]).astype(y_ref.dtype)

    tail_rows = ext_scr[qv:qv + SUBLANE, :]
    ext_scr[0:SUBLANE, :] = tail_rows

    @pl.when(c == pl.num_programs(1) - 1)
    def _():
        ssm_ref[0] = st_new.T
        conv_ref[0] = tail_rows


def _ssd(proj, row0, nb, t, qv, qp, state, consts):
    nc = t // qv
    hp = SSM_HEADS * SSM_HEADDIM
    blk0 = row0 // qv
    rowspec = lambda w, cb_: pl.BlockSpec((qv, w), lambda b, c: (blk0 + b * nc + c, cb_))
    const = lambda a: pl.BlockSpec(a.shape, lambda b, c: (0,) * a.ndim)
    in_specs = [rowspec(hp, COL_Z // hp), rowspec(CONV_DIM, COL_XBC // CONV_DIM),
                rowspec(LANE, COL_DT // LANE)]
    args = [proj, proj, proj]
    if state is not None:
        ssm0, conv0 = state
        in_specs += [pl.BlockSpec((1, hp, SSM_STATE), lambda b, c: (b, 0, 0)),
                     pl.BlockSpec((1, SUBLANE, CONV_DIM), lambda b, c: (b, 0, 0))]
        args += [ssm0, conv0]
    in_specs += [const(a) for a in consts]
    args += list(consts)
    return pl.pallas_call(
        functools.partial(_ssd_kernel, qv=qv, qp=qp, has_state=state is not None),
        grid=(nb, nc),
        in_specs=in_specs,
        out_specs=[pl.BlockSpec((qv, hp), lambda b, c: (b * nc + c, 0)),
                   pl.BlockSpec((1, hp, SSM_STATE), lambda b, c: (b, 0, 0)),
                   pl.BlockSpec((1, SUBLANE, CONV_DIM), lambda b, c: (b, 0, 0))],
        out_shape=[jax.ShapeDtypeStruct((nb * t, hp), MXU_DTYPE),
                   jax.ShapeDtypeStruct((nb, hp, SSM_STATE), F32),
                   jax.ShapeDtypeStruct((nb, SUBLANE, CONV_DIM), F32)],
        scratch_shapes=[pltpu.VMEM((qv + SUBLANE, CONV_DIM), F32),
                        pltpu.VMEM((SSM_STATE, hp), F32)],
        compiler_params=_params(("parallel", "arbitrary")),
        name="ssd_state" if state is not None else "ssd_prompt",
    )(*args)


def _count(mask):
    return jnp.sum(mask.astype(F32), axis=0, keepdims=True)


def _select_bias(sc_ref, key_ref, bias_ref, limit, n_keys, topk):
    s_pad = sc_ref.shape[0]
    s_io = lax.broadcasted_iota(jnp.int32, (s_pad, LANE), 0)
    adm = (s_io < limit) & (s_io < n_keys)
    sc = jnp.where(adm, sc_ref[...], -jnp.inf)
    bits = pltpu.bitcast(sc, jnp.int32)
    key_ref[...] = bits ^ ((bits >> 31) & jnp.int32(0x7FFFFFFF))
    kf = jnp.float32(topk)

    cnt0 = _count(key_ref[...] >= 0)
    prefix = jnp.where(cnt0 >= kf, jnp.int32(0), jnp.int32(INT_MIN))

    def body(i, prefix):
        cand = prefix | jnp.left_shift(jnp.int32(1), 30 - i)
        cnt = _count(key_ref[...] >= cand)
        return jnp.where(cnt >= kf, cand, prefix)

    thr = lax.fori_loop(0, 31, body, prefix)
    key = key_ref[...]
    finite = (key > KEY_NEG_INF) & (key < KEY_POS_INF)
    bias_ref[...] = jnp.where((key >= thr) & finite, 0.0, -jnp.inf)

    need = kf - _count(key > thr)
    tie = (_count(key == thr) > need) & (thr > KEY_NEG_INF)
    any_tie = jnp.max(tie.astype(F32)) > 0.0

    @pl.when(any_tie)
    def _():
        eq = key_ref[...] == thr
        nbits = max(1, (s_pad - 1).bit_length())

        def ibody(i, v):
            cand = v | jnp.left_shift(jnp.int32(1), nbits - 1 - i)
            below = _count(eq & (s_io < cand))
            return jnp.where(below < need, cand, v)

        last = lax.fori_loop(0, nbits, ibody, jnp.zeros((1, LANE), jnp.int32))
        k2 = key_ref[...]
        fin2 = (k2 > KEY_NEG_INF) & (k2 < KEY_POS_INF)
        sel = (k2 > thr) | ((k2 == thr) & (s_io <= last))
        bias_ref[...] = jnp.where(sel & fin2, 0.0, -jnp.inf)


def _sel_prompt_kernel(iq_ref, ikw_q_ref, ikw_all_ref, bias_out_ref, sc_scr, key_scr, bias_scr,
                       *, topk, tk):
    j = pl.program_id(1)
    t_len = ikw_all_ref.shape[0]
    w_t = ikw_q_ref[...].T
    iq = iq_ref[...]
    for kt in range(t_len // tk):
        ik = ikw_all_ref[kt * tk:(kt + 1) * tk, 0:IDX_DIM].astype(MXU_DTYPE)
        acc = jnp.zeros((tk, LANE), F32)
        for h in range(IDX_HEADS):
            lt = _mm_nt(ik, iq[:, h * IDX_DIM:(h + 1) * IDX_DIM])
            acc = acc + jnp.maximum(lt, 0.0) * w_t[IDX_DIM + h:IDX_DIM + h + 1, :]
        sc_scr[kt * tk:(kt + 1) * tk, :] = acc
    pos = j * LANE + lax.broadcasted_iota(jnp.int32, (1, LANE), 1)
    limit = (pos // CHUNK + 1) * CHUNK
    _select_bias(sc_scr, key_scr, bias_scr, limit, t_len, topk)
    for kt in range(t_len // LANE):
        bias_out_ref[:, kt * LANE:(kt + 1) * LANE] = (
            bias_scr[kt * LANE:(kt + 1) * LANE, :].T.astype(bias_out_ref.dtype))


def _sel_prompt(iq_rot, ikw_rot, nb, t):
    topk = min(TOPK_MAX, t // 4)
    nq = t // LANE
    wi = IDX_HEADS * IDX_DIM
    return pl.pallas_call(
        functools.partial(_sel_prompt_kernel, topk=topk, tk=256),
        grid=(nb, nq),
        in_specs=[pl.BlockSpec((LANE, wi), lambda b, j: (b * nq + j, 0)),
                  pl.BlockSpec((LANE, LANE), lambda b, j: (b * nq + j, 0)),
                  pl.BlockSpec((t, LANE), lambda b, j: (b, 0))],
        out_specs=pl.BlockSpec((LANE, t), lambda b, j: (b * nq + j, 0)),
        out_shape=jax.ShapeDtypeStruct((nb * t, t), MXU_DTYPE),
        scratch_shapes=[pltpu.VMEM((t, LANE), F32), pltpu.VMEM((t, LANE), jnp.int32),
                        pltpu.VMEM((t, LANE), F32)],
        compiler_params=_params(("parallel", "arbitrary")),
        name="sel_prompt",
    )(iq_rot, ikw_rot, ikw_rot)


def _sel_sample_kernel(iqp_ref, w_ref, ikp_ref, ikn_ref, bias_out_ref,
                       ik_scr, sc_scr, key_scr, bias_scr, *, ts, past, topk, n_keys, tk):
    s_pad = sc_scr.shape[0]
    nbp = LANE // ts
    lane = lax.broadcasted_iota(jnp.int32, (tk, LANE), 1)
    for b in range(nbp):
        ik_scr[0:past, :] = ikp_ref[b].astype(MXU_DTYPE)
        ik_scr[past:past + ts, :] = ikn_ref[b * ts:(b + 1) * ts, 0:IDX_DIM].astype(MXU_DTYPE)
        ik_scr[past + ts:s_pad, :] = jnp.zeros((s_pad - past - ts, IDX_DIM), MXU_DTYPE)
        iqp = iqp_ref[b]
        w = w_ref[b]
        for kt in range(s_pad // tk):
            lt = _mm_nt(ik_scr[kt * tk:(kt + 1) * tk, :], iqp)
            r = jnp.maximum(lt, 0.0) * w
            y = r[:, 0:LANE]
            for c in range(1, (IDX_HEADS * ts) // LANE):
                y = y + r[:, c * LANE:(c + 1) * LANE]
            sh = LANE // 2
            while sh >= ts:
                y = y + pltpu.roll(y, sh, 1)
                sh //= 2
            if b == 0:
                sc_scr[kt * tk:(kt + 1) * tk, :] = y
            else:
                sc_scr[kt * tk:(kt + 1) * tk, :] = jnp.where(
                    lane // ts == b, y, sc_scr[kt * tk:(kt + 1) * tk, :])
    q = lax.broadcasted_iota(jnp.int32, (1, LANE), 1) % ts
    limit = ((past + q) // CHUNK + 1) * CHUNK
    _select_bias(sc_scr, key_scr, bias_scr, limit, n_keys, topk)
    for kt in range(s_pad // LANE):
        bias_out_ref[:, kt * LANE:(kt + 1) * LANE] = (
            bias_scr[kt * LANE:(kt + 1) * LANE, :].T.astype(bias_out_ref.dtype))


def _sel_sample(iqp, wrow, ik_past, ikw_rot, row0, nb, ts, past, s_pad):
    n_keys = past + ts
    topk = min(TOPK_MAX, n_keys // 4)
    nbp = LANE // ts
    blk0 = row0 // LANE
    return pl.pallas_call(
        functools.partial(_sel_sample_kernel, ts=ts, past=past, topk=topk, n_keys=n_keys,
                          tk=s_pad // 4),
        grid=(nb // nbp,),
        in_specs=[pl.BlockSpec((nbp, IDX_HEADS * ts, IDX_DIM), lambda g: (g, 0, 0)),
                  pl.BlockSpec((nbp, 1, IDX_HEADS * ts), lambda g: (g, 0, 0)),
                  pl.BlockSpec((nbp, past, IDX_DIM), lambda g: (g, 0, 0)),
                  pl.BlockSpec((LANE, LANE), lambda g: (blk0 + g, 0))],
        out_specs=pl.BlockSpec((LANE, s_pad), lambda g: (g, 0)),
        out_shape=jax.ShapeDtypeStruct((nb * ts, s_pad), MXU_DTYPE),
        scratch_shapes=[pltpu.VMEM((s_pad, IDX_DIM), MXU_DTYPE), pltpu.VMEM((s_pad, LANE), F32),
                        pltpu.VMEM((s_pad, LANE), jnp.int32), pltpu.VMEM((s_pad, LANE), F32)],
        compiler_params=_params(("parallel",)),
        name="sel_sample",
    )(iqp, wrow, ik_past, ikw_rot)


def _attend(q_rows, k_all, v_all, bias, o_ref):
    nq = bias.shape[0]
    scale = HEAD_DIM ** -0.5
    bias4 = jnp.concatenate([bias] * KV_GROUP, axis=0)
    for g in range(N_KV):
        qg = jnp.concatenate([q_rows(g * KV_GROUP + i) for i in range(KV_GROUP)], axis=0)
        lg = _mm_nt(qg, k_all[:, g * HEAD_DIM:(g + 1) * HEAD_DIM]) * scale + bias4
        m = jnp.max(lg, axis=-1, keepdims=True)
        p = jnp.exp(lg - m)
        s = jnp.sum(p, axis=-1, keepdims=True)
        o = _mm(p.astype(MXU_DTYPE), v_all[:, g * HEAD_DIM:(g + 1) * HEAD_DIM]) / s
        for i in range(KV_GROUP):
            h = g * KV_GROUP + i
            o_ref[:, h * HEAD_DIM:(h + 1) * HEAD_DIM] = o[i * nq:(i + 1) * nq, :].astype(o_ref.dtype)


def _attn_prompt_kernel(q_ref, k_ref, v_ref, bias_ref, o_ref):
    _attend(lambda h: q_ref[:, h * HEAD_DIM:(h + 1) * HEAD_DIM], k_ref[...], v_ref[...],
            bias_ref[...].astype(F32), o_ref)


def _attn_prompt(q_rot, k_bf, v_bf, bias, nb, t):
    nq = t // LANE
    wq = N_HEADS * HEAD_DIM
    wkv = N_KV * HEAD_DIM
    return pl.pallas_call(
        _attn_prompt_kernel,
        grid=(nb, nq),
        in_specs=[pl.BlockSpec((LANE, wq), lambda b, j: (b * nq + j, 0)),
                  pl.BlockSpec((t, wkv), lambda b, j: (b, 0)),
                  pl.BlockSpec((t, wkv), lambda b, j: (b, 0)),
                  pl.BlockSpec((LANE, t), lambda b, j: (b * nq + j, 0))],
        out_specs=pl.BlockSpec((LANE, wq), lambda b, j: (b * nq + j, 0)),
        out_shape=jax.ShapeDtypeStruct((nb * t, wq), MXU_DTYPE),
        compiler_params=_params(("parallel", "arbitrary")),
        name="attn_prompt",
    )(q_rot, k_bf, v_bf, bias)


def _attn_sample_kernel(q_ref, kp_ref, vp_ref, kn_ref, vn_ref, bias_ref, o_ref, k_scr, v_scr,
                        *, ts, past):
    s_pad = k_scr.shape[0]
    wkv = N_KV * HEAD_DIM
    k_scr[0:past, :] = kp_ref[0].astype(MXU_DTYPE)
    v_scr[0:past, :] = vp_ref[0].astype(MXU_DTYPE)
    k_scr[past:past + ts, :] = kn_ref[...]
    v_scr[past:past + ts, :] = vn_ref[...]
    k_scr[past + ts:s_pad, :] = jnp.zeros((s_pad - past - ts, wkv), MXU_DTYPE)
    v_scr[past + ts:s_pad, :] = jnp.zeros((s_pad - past - ts, wkv), MXU_DTYPE)
    _attend(lambda h: q_ref[:, h * HEAD_DIM:(h + 1) * HEAD_DIM], k_scr[...], v_scr[...],
            bias_ref[...].astype(F32), o_ref)


def _attn_sample(q_rot, k_past, v_past, k_bf, v_bf, bias, row0, nb, ts, past, s_pad):
    wq = N_HEADS * HEAD_DIM
    wkv = N_KV * HEAD_DIM
    blk0 = row0 // ts
    return pl.pallas_call(
        functools.partial(_attn_sample_kernel, ts=ts, past=past),
        grid=(nb,),
        in_specs=[pl.BlockSpec((ts, wq), lambda b: (blk0 + b, 0)),
                  pl.BlockSpec((1, past, wkv), lambda b: (b, 0, 0)),
                  pl.BlockSpec((1, past, wkv), lambda b: (b, 0, 0)),
                  pl.BlockSpec((ts, wkv), lambda b: (blk0 + b, 0)),
                  pl.BlockSpec((ts, wkv), lambda b: (blk0 + b, 0)),
                  pl.BlockSpec((ts, s_pad), lambda b: (b, 0))],
        out_specs=pl.BlockSpec((ts, wq), lambda b: (b, 0)),
        out_shape=jax.ShapeDtypeStruct((nb * ts, wq), MXU_DTYPE),
        scratch_shapes=[pltpu.VMEM((s_pad, wkv), MXU_DTYPE), pltpu.VMEM((s_pad, wkv), MXU_DTYPE)],
        compiler_params=_params(("parallel",)),
        name="attn_sample",
    )(q_rot, k_past, v_past, k_bf, v_bf, bias)


def _pack_w_in(w):
    d = w.shape[0]
    o = 0
    parts = {}
    for name, width in (("z", D_SSM), ("xbc", CONV_DIM), ("dt", SSM_HEADS),
                        ("q", N_HEADS * HEAD_DIM), ("k", N_KV * HEAD_DIM), ("v", N_KV * HEAD_DIM),
                        ("iq", IDX_HEADS * IDX_DIM), ("ik", IDX_DIM), ("iw", IDX_HEADS)):
        parts[name] = w[:, o:o + width]
        o += width
    zpad = lambda n: jnp.zeros((d, n), w.dtype)
    return jnp.concatenate(
        [parts["z"], parts["q"], parts["iq"], parts["xbc"], parts["k"], parts["v"],
         parts["dt"], zpad(LANE - SSM_HEADS),
         parts["ik"], parts["iw"], zpad(LANE - IDX_DIM - IDX_HEADS)], axis=1)


def kernel(x_prompt, x_sample, cache_k, cache_v, cache_idx_k, state_ssm, state_conv, norm_ffn1, ffn1_w1, ffn1_w3, ffn1_w2, norm_mix, w_in, conv_w, conv_b, dt_bias, a_log, d_skip, ssm_norm_w, w_out, norm_ffn2, ffn2_w1, ffn2_w3, ffn2_w2, final_norm):
    bp, tp, d = x_prompt.shape
    bs, ts = x_sample.shape[:2]
    depth, _, past = cache_k.shape[:3]
    mp, ms = bp * tp, bs * ts
    n_keys_s = past + ts
    s_pad = -(-n_keys_s // LANE) * LANE
    hp = SSM_HEADS * SSM_HEADDIM
    wkv = N_KV * HEAD_DIM

    x = jnp.concatenate([x_prompt.reshape(mp, d), x_sample.reshape(ms, d)], axis=0)

    pos = jnp.concatenate([jnp.tile(jnp.arange(tp, dtype=jnp.int32), bp),
                           jnp.tile(past + jnp.arange(ts, dtype=jnp.int32), bs)])
    iw_scale = jnp.concatenate([jnp.full((IDX_HEADS,), IDX_HEADS ** -0.5, F32),
                                jnp.ones((LANE - IDX_DIM - IDX_HEADS,), F32)])
    tab = jnp.concatenate(_rope_table(pos, ROPE_DIM, HEAD_DIM)
                          + _rope_table(pos, IDX_ROPE_DIM, IDX_DIM)
                          + _rope_table(pos, IDX_ROPE_DIM, IDX_DIM, tail=iw_scale), axis=1)

    cast = lambda a: a.astype(MXU_DTYPE)
    w_in_p = cast(jnp.stack([_pack_w_in(w_in[l]) for l in range(depth)]))
    f1w1, f1w3, f1w2 = cast(ffn1_w1), cast(ffn1_w3), cast(ffn1_w2)
    f2w1, f2w3, f2w2 = cast(ffn2_w1), cast(ffn2_w3), cast(ffn2_w2)
    w_out_b = cast(w_out)
    conv0_pad = jnp.pad(state_conv, ((0, 0), (0, 0), (SUBLANE - (CONV_W - 1), 0), (0, 0)))
    lane_pad = lambda a: jnp.pad(a, (0, LANE - a.shape[0]))[None, :]

    outs = {k: [] for k in ("kp", "vp", "ikp", "sp", "cp", "ks", "vs", "iks", "ss", "cs")}
    for l in range(depth):
        x = _ffn(x, norm_ffn1[l][None, :], f1w1[l], f1w3[l], f1w2[l])
        proj = _in_proj(x, norm_mix[l][None, :], w_in_p[l])
        q_rot, iq_rot, k_rot, k_bf, v_f, v_bf, ikw_rot = _prep(proj, tab)

        consts = (conv_w[l], conv_b[l][None, :], lane_pad(dt_bias[l]),
                  lane_pad(-jnp.exp(a_log[l])), jnp.repeat(d_skip[l], SSM_HEADDIM)[None, :],
                  ssm_norm_w[l][None, :])
        y_p, ssm_p, conv_p = _ssd(proj, 0, bp, tp, LANE, LANE, None, consts)
        y_s, ssm_s, conv_s = _ssd(proj, mp, bs, ts, ts, LANE,
                                  (state_ssm[l].reshape(bs, hp, SSM_STATE), conv0_pad[l]), consts)

        bias_p = _sel_prompt(iq_rot, ikw_rot, bp, tp)
        att_p = _attn_prompt(q_rot, k_bf, v_bf, bias_p, bp, tp)

        iqp = iq_rot[mp:].reshape(bs, ts, IDX_HEADS, IDX_DIM).transpose(0, 2, 1, 3)
        iqp = iqp.reshape(bs, IDX_HEADS * ts, IDX_DIM)
        wrow = ikw_rot[mp:, IDX_DIM:IDX_DIM + IDX_HEADS].reshape(bs, ts, IDX_HEADS)
        wrow = wrow.transpose(0, 2, 1).reshape(bs, 1, IDX_HEADS * ts)
        bias_s = _sel_sample(iqp, wrow, cache_idx_k[l], ikw_rot, mp, bs, ts, past, s_pad)
        att_s = _attn_sample(q_rot, cache_k[l].reshape(bs, past, wkv),
                             cache_v[l].reshape(bs, past, wkv), k_bf, v_bf, bias_s,
                             mp, bs, ts, past, s_pad)

        x = _out_proj(x, jnp.concatenate([y_p, y_s], axis=0),
                      jnp.concatenate([att_p, att_s], axis=0), w_out_b[l])
        x = _ffn(x, norm_ffn2[l][None, :], f2w1[l], f2w3[l], f2w2[l])

        outs["kp"].append(k_rot[:mp].reshape(bp, tp, N_KV, HEAD_DIM))
        outs["vp"].append(v_f[:mp].reshape(bp, tp, N_KV, HEAD_DIM))
        outs["ikp"].append(ikw_rot[:mp, :IDX_DIM].reshape(bp, tp, IDX_DIM))
        outs["sp"].append(ssm_p.reshape(bp, SSM_HEADS, SSM_HEADDIM, SSM_STATE))
        outs["cp"].append(conv_p[:, SUBLANE - (CONV_W - 1):, :])
        outs["ks"].append(k_rot[mp:].reshape(bs, ts, N_KV, HEAD_DIM))
        outs["vs"].append(v_f[mp:].reshape(bs, ts, N_KV, HEAD_DIM))
        outs["iks"].append(ikw_rot[mp:, :IDX_DIM].reshape(bs, ts, IDX_DIM))
        outs["ss"].append(ssm_s.reshape(bs, SSM_HEADS, SSM_HEADDIM, SSM_STATE))
        outs["cs"].append(conv_s[:, SUBLANE - (CONV_W - 1):, :])

    y = _final_norm(x, final_norm[None, :])
    st = {k: jnp.stack(v) for k, v in outs.items()}
    return (y[:mp].reshape(bp, tp, d), y[mp:].reshape(bs, ts, d),
            st["kp"], st["vp"], st["ikp"], st["sp"], st["cp"],
            st["ks"], st["vs"], st["iks"], st["ss"], st["cs"])
```

```python
import functools
import math

import jax
import jax.numpy as jnp
from jax import lax
from jax.experimental import pallas as pl
from jax.experimental.pallas import tpu as pltpu

F32 = jnp.float32
MXU_DTYPE = jnp.bfloat16

D_MODEL = 2048
CHUNK = 64
D_SSM = 1024
SSM_HEADDIM = 64
SSM_HEADS = 16
SSM_GROUPS = 2
SSM_STATE = 128
CONV_W = 4
CONV_DIM = D_SSM + 2 * SSM_GROUPS * SSM_STATE
HEAD_DIM = 128
N_HEADS = 8
N_KV = 2
KV_GROUP = N_HEADS // N_KV
ROPE_DIM = HEAD_DIM // 4
IDX_HEADS = 16
IDX_DIM = 64
IDX_ROPE_DIM = IDX_DIM // 4
TOPK_MAX = 256
ROPE_THETA = 500000.0
D_FF = 5632
EPS = 1e-6

LANE = 128
SUBLANE = 8
COUNT_TILE = 128
VMEM_LIMIT = 56 * 1024 * 1024

COL_Z = 0
COL_Q = 1024
COL_IQ = 2048
COL_XBC = 3072
COL_K = 4608
COL_V = 4864
COL_DT = 5120
COL_IKW = 5248
PROJ_COLS = 5376

KEY_NEG_INF = -2139095041
KEY_POS_INF = 2139095040
INT_MIN = -2147483648

NT_DIMS = (((1,), (1,)), ((), ()))


def _mm(a, b):
    return jnp.dot(a, b, preferred_element_type=F32)


def _mm_nt(a, b):
    return lax.dot_general(a, b, NT_DIMS, preferred_element_type=F32)


def _mm_exact(a, b):
    return jnp.dot(a, b, preferred_element_type=F32, precision=lax.Precision.HIGHEST)


def _silu(x):
    return x * jax.nn.sigmoid(x)


def _params(sem, vmem=VMEM_LIMIT):
    return pltpu.CompilerParams(dimension_semantics=sem, vmem_limit_bytes=vmem)


def _row_tile(m, pref):
    t = pref
    while m % t:
        t //= 2
    return t


def _split_rows(tm, rows_a, rows_b):
    assert rows_a % tm == 0 and rows_b % tm == 0
    na = rows_a // tm

    def spec_a(width, *grid_rest):
        return pl.BlockSpec((tm, width), lambda i, *_: (jnp.minimum(i, na - 1), 0))

    def spec_b(width, *grid_rest):
        return pl.BlockSpec((tm, width), lambda i, *_: (jnp.maximum(i - na, 0), 0))

    return na, spec_a, spec_b


def _pick(i, na, a_ref, b_ref):
    return jnp.where(i < na, a_ref[...], b_ref[...])


def _ffn_kernel(*refs, na):
    if na is None:
        x_ref, nw_ref, w1_ref, w3_ref, w2_ref, o_ref, h_scr, acc_scr = refs
        load_x = lambda: x_ref[...]
    else:
        xa_ref, xb_ref, nw_ref, w1_ref, w3_ref, w2_ref, o_ref, h_scr, acc_scr = refs
        load_x = lambda: _pick(pl.program_id(0), na, xa_ref, xb_ref)
    f = pl.program_id(1)

    @pl.when(f == 0)
    def _():
        x = load_x()
        ms = jnp.mean(x * x, axis=-1, keepdims=True)
        h_scr[...] = (x * lax.rsqrt(ms + EPS) * nw_ref[...]).astype(h_scr.dtype)
        acc_scr[...] = jnp.zeros_like(acc_scr)

    h = h_scr[...]
    a = _mm(h, w1_ref[...])
    b = _mm(h, w3_ref[...])
    g = (_silu(a) * b).astype(h_scr.dtype)
    acc_scr[...] += _mm(g, w2_ref[...])

    @pl.when(f == pl.num_programs(1) - 1)
    def _():
        o_ref[...] = load_x() + 0.5 * acc_scr[...]


def _ffn(xs, nw, w1, w3, w2, tm=512, tf=512):
    d = xs[0].shape[1]
    m = sum(x.shape[0] for x in xs)
    dff = w1.shape[1]
    tm = _row_tile(math.gcd(*[x.shape[0] for x in xs]), tm)
    if len(xs) == 1:
        na, x_specs = None, [pl.BlockSpec((tm, d), lambda i, f: (i, 0))]
    else:
        na, spec_a, spec_b = _split_rows(tm, xs[0].shape[0], xs[1].shape[0])
        x_specs = [spec_a(d), spec_b(d)]
    return pl.pallas_call(
        functools.partial(_ffn_kernel, na=na),
        grid=(m // tm, dff // tf),
        in_specs=x_specs + [
            pl.BlockSpec((1, d), lambda i, f: (0, 0)),
            pl.BlockSpec((d, tf), lambda i, f: (0, f)),
            pl.BlockSpec((d, tf), lambda i, f: (0, f)),
            pl.BlockSpec((tf, d), lambda i, f: (f, 0)),
        ],
        out_specs=pl.BlockSpec((tm, d), lambda i, f: (i, 0)),
        out_shape=jax.ShapeDtypeStruct((m, d), F32),
        scratch_shapes=[pltpu.VMEM((tm, d), MXU_DTYPE), pltpu.VMEM((tm, d), F32)],
        compiler_params=_params(("parallel", "arbitrary")),
        name="ffn",
    )(*xs, nw, w1, w3, w2)


def _in_proj_kernel(x_ref, nw_ref, w_ref, o_ref, h_scr):
    @pl.when(pl.program_id(1) == 0)
    def _():
        x = x_ref[...]
        ms = jnp.mean(x * x, axis=-1, keepdims=True)
        h_scr[...] = (x * lax.rsqrt(ms + EPS) * nw_ref[...]).astype(h_scr.dtype)

    o_ref[...] = _mm(h_scr[...], w_ref[...])


def _in_proj(x, nw, w, tm=1024, tn=768):
    m, d = x.shape
    n = w.shape[1]
    tm = _row_tile(m, tm)
    return pl.pallas_call(
        _in_proj_kernel,
        grid=(m // tm, n // tn),
        in_specs=[
            pl.BlockSpec((tm, d), lambda i, j: (i, 0)),
            pl.BlockSpec((1, d), lambda i, j: (0, 0)),
            pl.BlockSpec((d, tn), lambda i, j: (0, j)),
        ],
        out_specs=pl.BlockSpec((tm, tn), lambda i, j: (i, j)),
        out_shape=jax.ShapeDtypeStruct((m, n), F32),
        scratch_shapes=[pltpu.VMEM((tm, d), MXU_DTYPE)],
        compiler_params=_params(("parallel", "arbitrary")),
        name="in_proj",
    )(x, nw, w)


def _out_proj_kernel(x_ref, ya_ref, yb_ref, aa_ref, ab_ref, w_ref, o_ref, *, na):
    i = pl.program_id(0)
    half = ya_ref.shape[1]
    o_ref[...] = (x_ref[...] + _mm(_pick(i, na, ya_ref, yb_ref), w_ref[:half, :])
                  + _mm(_pick(i, na, aa_ref, ab_ref), w_ref[half:, :]))


def _out_proj(x, y_pair, att_pair, w, tm=512):
    m, d = x.shape
    half = y_pair[0].shape[1]
    tm = _row_tile(math.gcd(y_pair[0].shape[0], y_pair[1].shape[0]), tm)
    na, spec_a, spec_b = _split_rows(tm, y_pair[0].shape[0], y_pair[1].shape[0])
    return pl.pallas_call(
        functools.partial(_out_proj_kernel, na=na),
        grid=(m // tm,),
        in_specs=[
            pl.BlockSpec((tm, d), lambda i: (i, 0)),
            spec_a(half), spec_b(half), spec_a(half), spec_b(half),
            pl.BlockSpec((2 * half, d), lambda i: (0, 0)),
        ],
        out_specs=pl.BlockSpec((tm, d), lambda i: (i, 0)),
        out_shape=jax.ShapeDtypeStruct((m, d), F32),
        compiler_params=_params(("arbitrary",)),
        name="out_proj",
    )(x, *y_pair, *att_pair, w)


def _final_norm_kernel(x_ref, nw_ref, oa_ref, ob_ref, *, na):
    i = pl.program_id(0)
    x = x_ref[...]
    ms = jnp.mean(x * x, axis=-1, keepdims=True)
    y = x * lax.rsqrt(ms + EPS) * nw_ref[...]

    @pl.when(i < na)
    def _():
        oa_ref[...] = y

    @pl.when(i >= na)
    def _():
        ob_ref[...] = y


def _final_norm(x, nw, rows_a, rows_b, tm=512):
    d = x.shape[1]
    tm = _row_tile(math.gcd(rows_a, rows_b), tm)
    na, spec_a, spec_b = _split_rows(tm, rows_a, rows_b)
    return pl.pallas_call(
        functools.partial(_final_norm_kernel, na=na),
        grid=((rows_a + rows_b) // tm,),
        in_specs=[pl.BlockSpec((tm, d), lambda i: (i, 0)),
                  pl.BlockSpec((1, d), lambda i: (0, 0))],
        out_specs=[spec_a(d), spec_b(d)],
        out_shape=[jax.ShapeDtypeStruct((rows_a, d), F32),
                   jax.ShapeDtypeStruct((rows_b, d), F32)],
        compiler_params=_params(("arbitrary",)),
        name="final_norm",
    )(x, nw)


def _cast_kernel(x_ref, o_ref):
    o_ref[...] = x_ref[...].astype(o_ref.dtype)


def _cast_weights(w, rows):
    depth, r, c = w.shape
    w2 = w.reshape(depth * r, c)
    rows = _row_tile(depth * r, rows)
    out = pl.pallas_call(
        _cast_kernel,
        grid=(depth * r // rows,),
        in_specs=[pl.BlockSpec((rows, c), lambda i: (i, 0))],
        out_specs=pl.BlockSpec((rows, c), lambda i: (i, 0)),
        out_shape=jax.ShapeDtypeStruct(w2.shape, MXU_DTYPE),
        compiler_params=_params(("parallel",)),
        name="cast_w",
    )(w2)
    return out.reshape(depth, r, c)


_W_IN_PARTS = (
    (COL_Z, 0, D_SSM),
    (COL_XBC, D_SSM, CONV_DIM),
    (COL_DT, D_SSM + CONV_DIM, SSM_HEADS),
    (COL_Q, D_SSM + CONV_DIM + SSM_HEADS, N_HEADS * HEAD_DIM),
    (COL_K, D_SSM + CONV_DIM + SSM_HEADS + N_HEADS * HEAD_DIM, N_KV * HEAD_DIM),
    (COL_V, D_SSM + CONV_DIM + SSM_HEADS + (N_HEADS + N_KV) * HEAD_DIM, N_KV * HEAD_DIM),
    (COL_IQ, D_SSM + CONV_DIM + SSM_HEADS + (N_HEADS + 2 * N_KV) * HEAD_DIM, IDX_HEADS * IDX_DIM),
    (COL_IKW, D_SSM + CONV_DIM + SSM_HEADS + (N_HEADS + 2 * N_KV) * HEAD_DIM + IDX_HEADS * IDX_DIM,
     IDX_DIM + IDX_HEADS),
)


def _pack_w_in_kernel(w_ref, o_ref):
    rows = o_ref.shape[0]
    for dst, src, width in _W_IN_PARTS:
        o_ref[:, dst:dst + width] = w_ref[:, src:src + width].astype(o_ref.dtype)
        pad = -width % LANE
        if pad:
            o_ref[:, dst + width:dst + width + pad] = jnp.zeros((rows, pad), o_ref.dtype)


def _pack_w_in(w, rows=256):
    depth, r, c = w.shape
    w2 = w.reshape(depth * r, c)
    out = pl.pallas_call(
        _pack_w_in_kernel,
        grid=(depth * r // rows,),
        in_specs=[pl.BlockSpec((rows, c), lambda i: (i, 0))],
        out_specs=pl.BlockSpec((rows, PROJ_COLS), lambda i: (i, 0)),
        out_shape=jax.ShapeDtypeStruct((depth * r, PROJ_COLS), MXU_DTYPE),
        compiler_params=_params(("parallel",)),
        name="pack_w_in",
    )(w2)
    return out.reshape(depth, r, PROJ_COLS)


def _rope_lanes(x, c, s1, s2, half):
    outs = []
    for t in range(x.shape[1] // LANE):
        xt = x[:, t * LANE:(t + 1) * LANE]
        outs.append(xt * c + pltpu.roll(xt, half, 1) * s1
                    + pltpu.roll(xt, LANE - half, 1) * s2)
    return outs[0] if len(outs) == 1 else jnp.concatenate(outs, axis=1)


def _prep_kernel(q_ref, iq_ref, k_ref, v_ref, ikw_ref, tab_ref,
                 qo_ref, iqo_ref, ko_ref, kbo_ref, vo_ref, vbo_ref, ikwo_ref):
    tab = [tab_ref[:, i * LANE:(i + 1) * LANE] for i in range(9)]
    qo_ref[...] = _rope_lanes(q_ref[...], tab[0], tab[1], tab[2], ROPE_DIM // 2).astype(qo_ref.dtype)
    k = _rope_lanes(k_ref[...], tab[0], tab[1], tab[2], ROPE_DIM // 2)
    ko_ref[...] = k
    kbo_ref[...] = k.astype(kbo_ref.dtype)
    iqo_ref[...] = _rope_lanes(iq_ref[...], tab[3], tab[4], tab[5], IDX_ROPE_DIM // 2).astype(iqo_ref.dtype)
    ikwo_ref[...] = _rope_lanes(ikw_ref[...], tab[6], tab[7], tab[8], IDX_ROPE_DIM // 2)
    v = v_ref[...]
    vo_ref[...] = v
    vbo_ref[...] = v.astype(vbo_ref.dtype)


def _prep(proj, tab, tm=512):
    m = proj.shape[0]
    tm = _row_tile(m, tm)
    wq = N_HEADS * HEAD_DIM
    wkv = N_KV * HEAD_DIM
    row = lambda w, c: pl.BlockSpec((tm, w), lambda i: (i, c))
    return pl.pallas_call(
        _prep_kernel,
        grid=(m // tm,),
        in_specs=[row(wq, COL_Q // wq), row(wq, COL_IQ // wq), row(wkv, COL_K // wkv),
                  row(wkv, COL_V // wkv), row(LANE, COL_IKW // LANE), row(9 * LANE, 0)],
        out_specs=[row(wq, 0), row(wq, 0), row(wkv, 0), row(wkv, 0), row(wkv, 0), row(wkv, 0),
                   row(LANE, 0)],
        out_shape=[jax.ShapeDtypeStruct((m, wq), MXU_DTYPE),
                   jax.ShapeDtypeStruct((m, wq), MXU_DTYPE),
                   jax.ShapeDtypeStruct((m, wkv), F32),
                   jax.ShapeDtypeStruct((m, wkv), MXU_DTYPE),
                   jax.ShapeDtypeStruct((m, wkv), F32),
                   jax.ShapeDtypeStruct((m, wkv), MXU_DTYPE),
                   jax.ShapeDtypeStruct((m, LANE), F32)],
        compiler_params=_params(("parallel",)),
        name="prep",
    )(proj, proj, proj, proj, proj, tab)


def _rope_table(pos, rot_dim, period, tail=None):
    half = rot_dim // 2
    inv_freq = jnp.float32(ROPE_THETA) ** (-jnp.arange(half, dtype=F32) * 2.0 / rot_dim)
    ang = pos.astype(F32)[:, None] * inv_freq[None, :]
    cos, sin = jnp.cos(ang), jnp.sin(ang)
    n = pos.shape[0]
    ones = jnp.ones((n, period - rot_dim), F32)
    zeros = jnp.zeros((n, period - rot_dim), F32)
    zh = jnp.zeros((n, half), F32)
    c = jnp.concatenate([cos, cos, ones], axis=1)
    s1 = jnp.concatenate([zh, sin, zeros], axis=1)
    s2 = jnp.concatenate([-sin, zh, zeros], axis=1)
    if tail is None:
        reps = LANE // period
        return [jnp.tile(a, (1, reps)) for a in (c, s1, s2)]
    zt = jnp.zeros((n, LANE - period), F32)
    return [jnp.concatenate([c, jnp.broadcast_to(tail[None, :], (n, LANE - period))], axis=1),
            jnp.concatenate([s1, zt], axis=1), jnp.concatenate([s2, zt], axis=1)]


def _ssd_kernel(*refs, qv, qp, has_state):
    if has_state:
        (z_ref, xbc_ref, dt_ref, ssm0_ref, conv0_ref, cw_ref, cb_ref, dtb_ref, aneg_ref,
         dsk_ref, nw_ref, y_ref, ssm_ref, conv_ref, ext_scr, st_scr) = refs
    else:
        (z_ref, xbc_ref, dt_ref, cw_ref, cb_ref, dtb_ref, aneg_ref,
         dsk_ref, nw_ref, y_ref, ssm_ref, conv_ref, ext_scr, st_scr) = refs
    c = pl.program_id(1)
    hp = SSM_HEADS * SSM_HEADDIM
    gw = SSM_STATE
    hpg = hp // SSM_GROUPS

    @pl.when(c == 0)
    def _():
        if has_state:
            ext_scr[0:SUBLANE, :] = conv0_ref[0]
            st_scr[...] = ssm0_ref[0].T
        else:
            ext_scr[0:SUBLANE, :] = jnp.zeros((SUBLANE, CONV_DIM), F32)
            st_scr[...] = jnp.zeros_like(st_scr)

    xbc_raw = xbc_ref[...]
    ext_scr[SUBLANE:SUBLANE + qv, :] = xbc_raw
    conv = cb_ref[...] + xbc_raw * cw_ref[CONV_W - 1:CONV_W, :]
    for j in range(CONV_W - 1):
        off = SUBLANE - (CONV_W - 1) + j
        conv = conv + ext_scr[off:off + qv, :] * cw_ref[j:j + 1, :]
    xc = _silu(conv)
    dtr = dt_ref[...]
    if qv < qp:
        xc = jnp.concatenate([xc, jnp.zeros((qp - qv, CONV_DIM), F32)], axis=0)
        dtr = jnp.concatenate([dtr, jnp.zeros((qp - qv, LANE), F32)], axis=0)
    xs = xc[:, :hp]
    bm = xc[:, hp:hp + SSM_GROUPS * gw]
    cm = xc[:, hp + SSM_GROUPS * gw:]

    xdt_in = dtr + dtb_ref[...]
    dt = jnp.maximum(xdt_in, 0.0) + jnp.log1p(jnp.exp(-jnp.abs(xdt_in)))
    row = lax.broadcasted_iota(jnp.int32, (qp, LANE), 0)
    dt = jnp.where(row < qv, dt, 0.0)
    la = dt * aneg_ref[...]
    ti = lax.broadcasted_iota(jnp.int32, (qp, qp), 0)
    si = lax.broadcasted_iota(jnp.int32, (qp, qp), 1)
    tri = ti >= si
    cum = _mm_exact(tri.astype(F32), la)
    cum_t = cum.T

    eh = lax.broadcasted_iota(jnp.int32, (LANE, hp), 0)
    ec = lax.broadcasted_iota(jnp.int32, (LANE, hp), 1)
    expand = (ec // SSM_HEADDIM == eh).astype(F32)
    dt_x = _mm_exact(dt, expand)
    cum_x = _mm_exact(cum, expand)
    xdt = xs * dt_x

    cb = [_mm_nt(cm[:, g * gw:(g + 1) * gw].astype(MXU_DTYPE),
                 bm[:, g * gw:(g + 1) * gw].astype(MXU_DTYPE)) for g in range(SSM_GROUPS)]
    lane = lax.broadcasted_iota(jnp.int32, (qp, LANE), 1)
    y_parts = []
    for j in range(SSM_HEADS // 2):
        ms = []
        for h in (2 * j, 2 * j + 1):
            d = cum[:, h:h + 1] - cum_t[h:h + 1, :]
            seg = jnp.exp(jnp.where(tri, d, -jnp.inf))
            ms.append((cb[h // (SSM_HEADS // SSM_GROUPS)] * seg).astype(MXU_DTYPE))
        xp = xdt[:, j * LANE:(j + 1) * LANE]
        rhs = jnp.concatenate([jnp.where(lane < SSM_HEADDIM, xp, 0.0),
                               jnp.where(lane >= SSM_HEADDIM, xp, 0.0)], axis=0)
        y_parts.append(_mm(jnp.concatenate(ms, axis=1), rhs.astype(MXU_DTYPE)))
    y = jnp.concatenate(y_parts, axis=1)

    st = st_scr[...]
    y_st = jnp.concatenate(
        [_mm(cm[:, g * gw:(g + 1) * gw].astype(MXU_DTYPE),
             st[:, g * hpg:(g + 1) * hpg].astype(MXU_DTYPE)) for g in range(SSM_GROUPS)], axis=1)
    y = y + y_st * jnp.exp(cum_x) + dsk_ref[...] * xs

    cum_last = cum_x[qp - 1:qp, :]
    xt = (xdt * jnp.exp(cum_last - cum_x)).astype(MXU_DTYPE)
    upd = jnp.concatenate(
        [_mm(bm[:, g * gw:(g + 1) * gw].T.astype(MXU_DTYPE), xt[:, g * hpg:(g + 1) * hpg])
         for g in range(SSM_GROUPS)], axis=1)
    st_new = st * jnp.exp(cum_last) + upd
    st_scr[...] = st_new

    zz = z_ref[...]
    gated = y[:qv, :] * _silu(zz)
    ms2 = jnp.mean(gated * gated, axis=-1, keepdims=True)
    y_ref[...] = (gated * lax.rsqrt(ms2 + EPS) * nw_ref[...]).astype(y_ref.dtype)

    tail_rows = ext_scr[qv:qv + SUBLANE, :]
    ext_scr[0:SUBLANE, :] = tail_rows

    @pl.when(c == pl.num_programs(1) - 1)
    def _():
        ssm_ref[0] = st_new.T
        conv_ref[0] = tail_rows


def _ssd(proj, row0, nb, t, qv, qp, state, consts):
    nc = t // qv
    hp = SSM_HEADS * SSM_HEADDIM
    blk0 = row0 // qv
    rowspec = lambda w, cb_: pl.BlockSpec((qv, w), lambda b, c: (blk0 + b * nc + c, cb_))
    const = lambda a: pl.BlockSpec(a.shape, lambda b, c: (0,) * a.ndim)
    in_specs = [rowspec(hp, COL_Z // hp), rowspec(CONV_DIM, COL_XBC // CONV_DIM),
                rowspec(LANE, COL_DT // LANE)]
    args = [proj, proj, proj]
    if state is not None:
        ssm0, conv0 = state
        in_specs += [pl.BlockSpec((1, hp, SSM_STATE), lambda b, c: (b, 0, 0)),
                     pl.BlockSpec((1, SUBLANE, CONV_DIM), lambda b, c: (b, 0, 0))]
        args += [ssm0, conv0]
    in_specs += [const(a) for a in consts]
    args += list(consts)
    return pl.pallas_call(
        functools.partial(_ssd_kernel, qv=qv, qp=qp, has_state=state is not None),
        grid=(nb, nc),
        in_specs=in_specs,
        out_specs=[pl.BlockSpec((qv, hp), lambda b, c: (b * nc + c, 0)),
                   pl.BlockSpec((1, hp, SSM_STATE), lambda b, c: (b, 0, 0)),
                   pl.BlockSpec((1, SUBLANE, CONV_DIM), lambda b, c: (b, 0, 0))],
        out_shape=[jax.ShapeDtypeStruct((nb * t, hp), MXU_DTYPE),
                   jax.ShapeDtypeStruct((nb, hp, SSM_STATE), F32),
                   jax.ShapeDtypeStruct((nb, SUBLANE, CONV_DIM), F32)],
        scratch_shapes=[pltpu.VMEM((qv + SUBLANE, CONV_DIM), F32),
                        pltpu.VMEM((SSM_STATE, hp), F32)],
        compiler_params=_params(("parallel", "arbitrary")),
        name="ssd_state" if state is not None else "ssd_prompt",
    )(*args)


def _row_tiles(n_rows):
    assert n_rows % COUNT_TILE == 0
    return [(r0, r0 + COUNT_TILE) for r0 in range(0, n_rows, COUNT_TILE)]


def _row_iota(r0, r1):
    return lax.broadcasted_iota(jnp.int32, (r1 - r0, LANE), 0) + r0


def _count_rows(fn, n_rows):
    acc = None
    for r0, r1 in _row_tiles(n_rows):
        part = fn(r0, r1).astype(F32)
        acc = part if acc is None else acc + part
    return jnp.sum(acc, axis=0, keepdims=True)


def _select_bias(sc_ref, key_ref, bias_ref, limit, n_keys, topk):
    s_pad = sc_ref.shape[0]
    tiles = _row_tiles(s_pad)
    for r0, r1 in tiles:
        s_io = _row_iota(r0, r1)
        adm = (s_io < limit) & (s_io < n_keys)
        bits = pltpu.bitcast(jnp.where(adm, sc_ref[r0:r1, :], -jnp.inf), jnp.int32)
        key_ref[r0:r1, :] = bits ^ ((bits >> 31) & jnp.int32(0x7FFFFFFF))
    kf = jnp.float32(topk)

    cnt0 = _count_rows(lambda r0, r1: key_ref[r0:r1, :] >= 0, s_pad)
    prefix = jnp.where(cnt0 >= kf, jnp.int32(0), jnp.int32(INT_MIN))

    def body(i, prefix):
        cand = prefix | jnp.left_shift(jnp.int32(1), 30 - i)
        cnt = _count_rows(lambda r0, r1: key_ref[r0:r1, :] >= cand, s_pad)
        return jnp.where(cnt >= kf, cand, prefix)

    thr = lax.fori_loop(0, 31, body, prefix)

    def finite(key):
        return (key > KEY_NEG_INF) & (key < KEY_POS_INF)

    for r0, r1 in tiles:
        key = key_ref[r0:r1, :]
        bias_ref[r0:r1, :] = jnp.where((key >= thr) & finite(key), 0.0, -jnp.inf)

    need = kf - _count_rows(lambda r0, r1: key_ref[r0:r1, :] > thr, s_pad)
    n_eq = _count_rows(lambda r0, r1: key_ref[r0:r1, :] == thr, s_pad)
    tie = (n_eq > need) & (thr > KEY_NEG_INF)
    any_tie = jnp.max(tie.astype(F32)) > 0.0

    @pl.when(any_tie)
    def _():
        nbits = max(1, (s_pad - 1).bit_length())

        def ibody(i, v):
            cand = v | jnp.left_shift(jnp.int32(1), nbits - 1 - i)
            below = _count_rows(
                lambda r0, r1: (key_ref[r0:r1, :] == thr) & (_row_iota(r0, r1) < cand), s_pad)
            return jnp.where(below < need, cand, v)

        last = lax.fori_loop(0, nbits, ibody, jnp.zeros((1, LANE), jnp.int32))
        for r0, r1 in tiles:
            key = key_ref[r0:r1, :]
            sel = (key > thr) | ((key == thr) & (_row_iota(r0, r1) <= last))
            bias_ref[r0:r1, :] = jnp.where(sel & finite(key), 0.0, -jnp.inf)


def _sel_prompt_kernel(iq_ref, ikw_q_ref, ikw_all_ref, bias_out_ref, sc_scr, key_scr, bias_scr,
                       *, topk, tk):
    j = pl.program_id(1)
    t_len = ikw_all_ref.shape[0]
    seg = _causal_seg(t_len)
    pos = j * LANE + lax.broadcasted_iota(jnp.int32, (1, LANE), 1)
    limit = (pos // CHUNK + 1) * CHUNK

    def run(s_eff):
        w_t = ikw_q_ref[...].T
        iq = iq_ref[...]
        for kt in range(s_eff // tk):
            ik = ikw_all_ref[kt * tk:(kt + 1) * tk, 0:IDX_DIM].astype(MXU_DTYPE)
            acc = jnp.zeros((tk, LANE), F32)
            for h in range(IDX_HEADS):
                lt = _mm_nt(ik, iq[:, h * IDX_DIM:(h + 1) * IDX_DIM])
                acc = acc + jnp.maximum(lt, 0.0) * w_t[IDX_DIM + h:IDX_DIM + h + 1, :]
            sc_scr[kt * tk:(kt + 1) * tk, :] = acc
        _select_bias(sc_scr.at[0:s_eff], key_scr.at[0:s_eff], bias_scr.at[0:s_eff],
                     limit, t_len, topk)
        for kt in range(s_eff // LANE):
            bias_out_ref[:, kt * LANE:(kt + 1) * LANE] = (
                bias_scr[kt * LANE:(kt + 1) * LANE, :].T.astype(bias_out_ref.dtype))
        if s_eff < t_len:
            bias_out_ref[:, s_eff:t_len] = jnp.full((LANE, t_len - s_eff), -jnp.inf,
                                                    bias_out_ref.dtype)

    for v in range(t_len // seg):
        pl.when((j * LANE) // seg == v)(functools.partial(run, (v + 1) * seg))


def _causal_seg(t_len):
    return min(4 * LANE, t_len)


def _sel_prompt(iq_rot, ikw_rot, nb, t):
    topk = min(TOPK_MAX, t // 4)
    nq = t // LANE
    wi = IDX_HEADS * IDX_DIM
    return pl.pallas_call(
        functools.partial(_sel_prompt_kernel, topk=topk, tk=min(256, t)),
        grid=(nb, nq),
        in_specs=[pl.BlockSpec((LANE, wi), lambda b, j: (b * nq + j, 0)),
                  pl.BlockSpec((LANE, LANE), lambda b, j: (b * nq + j, 0)),
                  pl.BlockSpec((t, LANE), lambda b, j: (b, 0))],
        out_specs=pl.BlockSpec((LANE, t), lambda b, j: (b * nq + j, 0)),
        out_shape=jax.ShapeDtypeStruct((nb * t, t), MXU_DTYPE),
        scratch_shapes=[pltpu.VMEM((t, LANE), F32), pltpu.VMEM((t, LANE), jnp.int32),
                        pltpu.VMEM((t, LANE), F32)],
        compiler_params=_params(("parallel", "arbitrary")),
        name="sel_prompt",
    )(iq_rot, ikw_rot, ikw_rot)


def _sel_sample_kernel(iqp_ref, w_ref, ikp_ref, ikn_ref, bias_out_ref,
                       ik_scr, sc_scr, key_scr, bias_scr, *, ts, past, topk, n_keys, tk):
    s_pad = sc_scr.shape[0]
    nbp = LANE // ts
    lane = lax.broadcasted_iota(jnp.int32, (tk, LANE), 1)
    for b in range(nbp):
        ik_scr[0:past, :] = ikp_ref[b].astype(MXU_DTYPE)
        ik_scr[past:past + ts, :] = ikn_ref[b * ts:(b + 1) * ts, 0:IDX_DIM].astype(MXU_DTYPE)
        ik_scr[past + ts:s_pad, :] = jnp.zeros((s_pad - past - ts, IDX_DIM), MXU_DTYPE)
        iqp = iqp_ref[b]
        w = w_ref[b]
        for kt in range(s_pad // tk):
            lt = _mm_nt(ik_scr[kt * tk:(kt + 1) * tk, :], iqp)
            r = jnp.maximum(lt, 0.0) * w
            y = r[:, 0:LANE]
            for c in range(1, (IDX_HEADS * ts) // LANE):
                y = y + r[:, c * LANE:(c + 1) * LANE]
            sh = LANE // 2
            while sh >= ts:
                y = y + pltpu.roll(y, sh, 1)
                sh //= 2
            if b == 0:
                sc_scr[kt * tk:(kt + 1) * tk, :] = y
            else:
                sc_scr[kt * tk:(kt + 1) * tk, :] = jnp.where(
                    lane // ts == b, y, sc_scr[kt * tk:(kt + 1) * tk, :])
    q = lax.broadcasted_iota(jnp.int32, (1, LANE), 1) % ts
    limit = ((past + q) // CHUNK + 1) * CHUNK
    _select_bias(sc_scr, key_scr, bias_scr, limit, n_keys, topk)
    for kt in range(s_pad // LANE):
        bias_out_ref[:, kt * LANE:(kt + 1) * LANE] = (
            bias_scr[kt * LANE:(kt + 1) * LANE, :].T.astype(bias_out_ref.dtype))


def _sel_sample(iqp, wrow, ik_past, ikw_rot, row0, nb, ts, past, s_pad):
    n_keys = past + ts
    topk = min(TOPK_MAX, n_keys // 4)
    nbp = LANE // ts
    blk0 = row0 // LANE
    return pl.pallas_call(
        functools.partial(_sel_sample_kernel, ts=ts, past=past, topk=topk, n_keys=n_keys,
                          tk=s_pad // 4),
        grid=(nb // nbp,),
        in_specs=[pl.BlockSpec((nbp, IDX_HEADS * ts, IDX_DIM), lambda g: (g, 0, 0)),
                  pl.BlockSpec((nbp, 1, IDX_HEADS * ts), lambda g: (g, 0, 0)),
                  pl.BlockSpec((nbp, past, IDX_DIM), lambda g: (g, 0, 0)),
                  pl.BlockSpec((LANE, LANE), lambda g: (blk0 + g, 0))],
        out_specs=pl.BlockSpec((LANE, s_pad), lambda g: (g, 0)),
        out_shape=jax.ShapeDtypeStruct((nb * ts, s_pad), MXU_DTYPE),
        scratch_shapes=[pltpu.VMEM((s_pad, IDX_DIM), MXU_DTYPE), pltpu.VMEM((s_pad, LANE), F32),
                        pltpu.VMEM((s_pad, LANE), jnp.int32), pltpu.VMEM((s_pad, LANE), F32)],
        compiler_params=_params(("parallel",)),
        name="sel_sample",
    )(iqp, wrow, ik_past, ikw_rot)


def _attend(q_rows, k_all, v_all, bias, o_ref):
    nq = bias.shape[0]
    scale = HEAD_DIM ** -0.5
    bias4 = jnp.concatenate([bias] * KV_GROUP, axis=0)
    for g in range(N_KV):
        qg = jnp.concatenate([q_rows(g * KV_GROUP + i) for i in range(KV_GROUP)], axis=0)
        lg = _mm_nt(qg, k_all[:, g * HEAD_DIM:(g + 1) * HEAD_DIM]) * scale + bias4
        m = jnp.max(lg, axis=-1, keepdims=True)
        p = jnp.exp(lg - m)
        s = jnp.sum(p, axis=-1, keepdims=True)
        o = _mm(p.astype(MXU_DTYPE), v_all[:, g * HEAD_DIM:(g + 1) * HEAD_DIM]) / s
        for i in range(KV_GROUP):
            h = g * KV_GROUP + i
            o_ref[:, h * HEAD_DIM:(h + 1) * HEAD_DIM] = o[i * nq:(i + 1) * nq, :].astype(o_ref.dtype)


def _attn_prompt_kernel(q_ref, k_ref, v_ref, bias_ref, o_ref):
    j = pl.program_id(1)
    t_len = k_ref.shape[0]
    seg = _causal_seg(t_len)

    def run(s_eff):
        _attend(lambda h: q_ref[:, h * HEAD_DIM:(h + 1) * HEAD_DIM], k_ref[0:s_eff, :],
                v_ref[0:s_eff, :], bias_ref[:, 0:s_eff].astype(F32), o_ref)

    for v in range(t_len // seg):
        pl.when((j * LANE) // seg == v)(functools.partial(run, (v + 1) * seg))


def _attn_prompt(q_rot, k_bf, v_bf, bias, nb, t):
    nq = t // LANE
    wq = N_HEADS * HEAD_DIM
    wkv = N_KV * HEAD_DIM
    return pl.pallas_call(
        _attn_prompt_kernel,
        grid=(nb, nq),
        in_specs=[pl.BlockSpec((LANE, wq), lambda b, j: (b * nq + j, 0)),
                  pl.BlockSpec((t, wkv), lambda b, j: (b, 0)),
                  pl.BlockSpec((t, wkv), lambda b, j: (b, 0)),
                  pl.BlockSpec((LANE, t), lambda b, j: (b * nq + j, 0))],
        out_specs=pl.BlockSpec((LANE, wq), lambda b, j: (b * nq + j, 0)),
        out_shape=jax.ShapeDtypeStruct((nb * t, wq), MXU_DTYPE),
        compiler_params=_params(("parallel", "arbitrary")),
        name="attn_prompt",
    )(q_rot, k_bf, v_bf, bias)


def _attn_sample_kernel(q_ref, kp_ref, vp_ref, kn_ref, vn_ref, bias_ref, o_ref, k_scr, v_scr,
                        *, ts, past):
    s_pad = k_scr.shape[0]
    wkv = N_KV * HEAD_DIM
    for g in range(N_KV):
        cols = slice(g * HEAD_DIM, (g + 1) * HEAD_DIM)
        k_scr[0:past, cols] = kp_ref[0, pl.ds(g, past, stride=N_KV), :].astype(MXU_DTYPE)
        v_scr[0:past, cols] = vp_ref[0, pl.ds(g, past, stride=N_KV), :].astype(MXU_DTYPE)
    k_scr[past:past + ts, :] = kn_ref[...]
    v_scr[past:past + ts, :] = vn_ref[...]
    k_scr[past + ts:s_pad, :] = jnp.zeros((s_pad - past - ts, wkv), MXU_DTYPE)
    v_scr[past + ts:s_pad, :] = jnp.zeros((s_pad - past - ts, wkv), MXU_DTYPE)
    _attend(lambda h: q_ref[:, h * HEAD_DIM:(h + 1) * HEAD_DIM], k_scr[...], v_scr[...],
            bias_ref[...].astype(F32), o_ref)


def _attn_sample(q_rot, k_past, v_past, layer, k_bf, v_bf, bias, row0, nb, ts, past, s_pad):
    wq = N_HEADS * HEAD_DIM
    wkv = N_KV * HEAD_DIM
    blk0 = row0 // ts
    return pl.pallas_call(
        functools.partial(_attn_sample_kernel, ts=ts, past=past),
        grid=(nb,),
        in_specs=[pl.BlockSpec((ts, wq), lambda b: (blk0 + b, 0)),
                  pl.BlockSpec((1, past * N_KV, HEAD_DIM), lambda b: (layer * nb + b, 0, 0)),
                  pl.BlockSpec((1, past * N_KV, HEAD_DIM), lambda b: (layer * nb + b, 0, 0)),
                  pl.BlockSpec((ts, wkv), lambda b: (blk0 + b, 0)),
                  pl.BlockSpec((ts, wkv), lambda b: (blk0 + b, 0)),
                  pl.BlockSpec((ts, s_pad), lambda b: (b, 0))],
        out_specs=pl.BlockSpec((ts, wq), lambda b: (b, 0)),
        out_shape=jax.ShapeDtypeStruct((nb * ts, wq), MXU_DTYPE),
        scratch_shapes=[pltpu.VMEM((s_pad, wkv), MXU_DTYPE), pltpu.VMEM((s_pad, wkv), MXU_DTYPE)],
        compiler_params=_params(("parallel",)),
        name="attn_sample",
    )(q_rot, k_past, v_past, k_bf, v_bf, bias)


def kernel(x_prompt, x_sample, cache_k, cache_v, cache_idx_k, state_ssm, state_conv, norm_ffn1, ffn1_w1, ffn1_w3, ffn1_w2, norm_mix, w_in, conv_w, conv_b, dt_bias, a_log, d_skip, ssm_norm_w, w_out, norm_ffn2, ffn2_w1, ffn2_w3, ffn2_w2, final_norm):
    bp, tp, d = x_prompt.shape
    bs, ts = x_sample.shape[:2]
    depth, _, past = cache_k.shape[:3]
    mp, ms = bp * tp, bs * ts
    n_keys_s = past + ts
    s_pad = -(-n_keys_s // LANE) * LANE
    hp = SSM_HEADS * SSM_HEADDIM
    wkv = N_KV * HEAD_DIM

    xs = (x_prompt.reshape(mp, d), x_sample.reshape(ms, d))

    pos =jnp.concatenate([jnp.tile(jnp.arange(tp, dtype=jnp.int32), bp),
                           jnp.tile(past + jnp.arange(ts, dtype=jnp.int32), bs)])
    iw_scale = jnp.concatenate([jnp.full((IDX_HEADS,), IDX_HEADS ** -0.5, F32),
                                jnp.ones((LANE - IDX_DIM - IDX_HEADS,), F32)])
    tab = jnp.concatenate(_rope_table(pos, ROPE_DIM, HEAD_DIM)
                          + _rope_table(pos, IDX_ROPE_DIM, IDX_DIM)
                          + _rope_table(pos, IDX_ROPE_DIM, IDX_DIM, tail=iw_scale), axis=1)

    w_in_p = _pack_w_in(w_in)
    f1w1, f1w3, f1w2 = _cast_weights(ffn1_w1, 512), _cast_weights(ffn1_w3, 512), _cast_weights(ffn1_w2, 1024)
    f2w1, f2w3, f2w2 = _cast_weights(ffn2_w1, 512), _cast_weights(ffn2_w3, 512), _cast_weights(ffn2_w2, 1024)
    w_out_b = _cast_weights(w_out, 1024)
    k_cache = cache_k.reshape(depth * bs, past * N_KV, HEAD_DIM)
    v_cache = cache_v.reshape(depth * bs, past * N_KV, HEAD_DIM)
    conv0_pad = jnp.pad(state_conv, ((0, 0), (0, 0), (SUBLANE - (CONV_W - 1), 0), (0, 0)))
    lane_pad = lambda a: jnp.pad(a, (0, LANE - a.shape[0]))[None, :]

    outs = {k: [] for k in ("kp", "vp", "ikp", "sp", "cp", "ks", "vs", "iks", "ss", "cs")}
    for l in range(depth):
        x = _ffn(xs if l == 0 else (x,), norm_ffn1[l][None, :], f1w1[l], f1w3[l], f1w2[l])
        proj = _in_proj(x, norm_mix[l][None, :], w_in_p[l])
        q_rot, iq_rot, k_rot, k_bf, v_f, v_bf, ikw_rot = _prep(proj, tab)

        consts = (conv_w[l], conv_b[l][None, :], lane_pad(dt_bias[l]),
                  lane_pad(-jnp.exp(a_log[l])), jnp.repeat(d_skip[l], SSM_HEADDIM)[None, :],
                  ssm_norm_w[l][None, :])
        y_p, ssm_p, conv_p = _ssd(proj, 0, bp, tp, LANE, LANE, None, consts)
        y_s, ssm_s, conv_s = _ssd(proj, mp, bs, ts, ts, LANE,
                                  (state_ssm[l].reshape(bs, hp, SSM_STATE), conv0_pad[l]), consts)

        bias_p = _sel_prompt(iq_rot, ikw_rot, bp, tp)
        att_p = _attn_prompt(q_rot, k_bf, v_bf, bias_p, bp, tp)

        iqp = iq_rot[mp:].reshape(bs, ts, IDX_HEADS, IDX_DIM).transpose(0, 2, 1, 3)
        iqp = iqp.reshape(bs, IDX_HEADS * ts, IDX_DIM)
        wrow = ikw_rot[mp:, IDX_DIM:IDX_DIM + IDX_HEADS].reshape(bs, ts, IDX_HEADS)
        wrow = wrow.transpose(0, 2, 1).reshape(bs, 1, IDX_HEADS * ts)
        bias_s = _sel_sample(iqp, wrow, cache_idx_k[l], ikw_rot, mp, bs, ts, past, s_pad)
        att_s = _attn_sample(q_rot, k_cache, v_cache, l, k_bf, v_bf, bias_s,
                             mp, bs, ts, past, s_pad)

        x = _out_proj(x, (y_p, y_s), (att_p, att_s), w_out_b[l])
        x = _ffn((x,), norm_ffn2[l][None, :], f2w1[l], f2w3[l], f2w2[l])

        outs["kp"].append(k_rot[:mp].reshape(bp, tp, N_KV, HEAD_DIM))
        outs["vp"].append(v_f[:mp].reshape(bp, tp, N_KV, HEAD_DIM))
        outs["ikp"].append(ikw_rot[:mp, :IDX_DIM].reshape(bp, tp, IDX_DIM))
        outs["sp"].append(ssm_p.reshape(bp, SSM_HEADS, SSM_HEADDIM, SSM_STATE))
        outs["cp"].append(conv_p[:, SUBLANE - (CONV_W - 1):, :])
        outs["ks"].append(k_rot[mp:].reshape(bs, ts, N_KV, HEAD_DIM))
        outs["vs"].append(v_f[mp:].reshape(bs, ts, N_KV, HEAD_DIM))
        outs["iks"].append(ikw_rot[mp:, :IDX_DIM].reshape(bs, ts, IDX_DIM))
        outs["ss"].append(ssm_s.reshape(bs, SSM_HEADS, SSM_HEADDIM, SSM_STATE))
        outs["cs"].append(conv_s[:, SUBLANE - (CONV_W - 1):, :])

    y_p, y_s = _final_norm(x, final_norm[None, :], mp, ms)
    st = {k: jnp.stack(v) for k, v in outs.items()}
    return (y_p.reshape(bp, tp, d), y_s.reshape(bs, ts, d),
            st["kp"], st["vp"], st["ikp"], st["sp"], st["cp"],
            st["ks"], st["vs"], st["iks"], st["ss"], st["cs"])
```

```python
import functools
import math

import jax
import jax.numpy as jnp
from jax import lax
from jax.experimental import pallas as pl
from jax.experimental.pallas import tpu as pltpu

F32 = jnp.float32
MXU_DTYPE = jnp.bfloat16

D_MODEL = 2048
CHUNK = 64
D_SSM = 1024
SSM_HEADDIM = 64
SSM_HEADS = 16
SSM_GROUPS = 2
SSM_STATE = 128
CONV_W = 4
CONV_DIM = D_SSM + 2 * SSM_GROUPS * SSM_STATE
HEAD_DIM = 128
N_HEADS = 8
N_KV = 2
KV_GROUP = N_HEADS // N_KV
ROPE_DIM = HEAD_DIM // 4
IDX_HEADS = 16
IDX_DIM = 64
IDX_ROPE_DIM = IDX_DIM // 4
TOPK_MAX = 256
ROPE_THETA = 500000.0
D_FF = 5632
EPS = 1e-6

LANE = 128
SUBLANE = 8
COUNT_TILE = 128
VMEM_LIMIT = 56 * 1024 * 1024

COL_Z = 0
COL_Q = 1024
COL_IQ = 2048
COL_XBC = 3072
COL_K = 4608
COL_V = 4864
COL_DT = 5120
COL_IKW = 5248
PROJ_COLS = 5376

KEY_NEG_INF = -2139095041
KEY_POS_INF = 2139095040
INT_MIN = -2147483648

NT_DIMS = (((1,), (1,)), ((), ()))


def _mm(a, b):
    return jnp.dot(a, b, preferred_element_type=F32)


def _mm_nt(a, b):
    return lax.dot_general(a, b, NT_DIMS, preferred_element_type=F32)


def _mm_exact(a, b):
    return jnp.dot(a, b, preferred_element_type=F32, precision=lax.Precision.HIGHEST)


def _silu(x):
    return x * jax.nn.sigmoid(x)


def _params(sem, vmem=VMEM_LIMIT):
    return pltpu.CompilerParams(dimension_semantics=sem, vmem_limit_bytes=vmem)


def _row_tile(m, pref):
    t = pref
    while m % t:
        t //= 2
    return t


def _split_rows(tm, rows_a, rows_b):
    assert rows_a % tm == 0 and rows_b % tm == 0
    na = rows_a // tm

    def spec_a(width, *grid_rest):
        return pl.BlockSpec((tm, width), lambda i, *_: (jnp.minimum(i, na - 1), 0))

    def spec_b(width, *grid_rest):
        return pl.BlockSpec((tm, width), lambda i, *_: (jnp.maximum(i - na, 0), 0))

    return na, spec_a, spec_b


def _pick(i, na, a_ref, b_ref):
    return jnp.where(i < na, a_ref[...], b_ref[...])


def _ffn_kernel(*refs, na):
    if na is None:
        x_ref, nw_ref, w1_ref, w3_ref, w2_ref, o_ref, h_scr = refs
        load_x = lambda: x_ref[...]
    else:
        xa_ref, xb_ref, nw_ref, w1_ref, w3_ref, w2_ref, o_ref, h_scr = refs
        load_x = lambda: _pick(pl.program_id(0), na, xa_ref, xb_ref)
    f = pl.program_id(1)

    @pl.when(f == 0)
    def _():
        x = load_x()
        ms = jnp.mean(x * x, axis=-1, keepdims=True)
        h_scr[...] = (x * lax.rsqrt(ms + EPS) * nw_ref[...]).astype(h_scr.dtype)
        o_ref[...] = jnp.zeros_like(o_ref)

    h = h_scr[...]
    a = _mm(h, w1_ref[...].astype(h_scr.dtype))
    b = _mm(h, w3_ref[...].astype(h_scr.dtype))
    g = (_silu(a) * b).astype(h_scr.dtype)
    o_ref[...] += _mm(g, w2_ref[...].astype(h_scr.dtype))

    @pl.when(f == pl.num_programs(1) - 1)
    def _():
        o_ref[...] = load_x() + 0.5 * o_ref[...]


def _ffn_row_tile(rows):
    g = math.gcd(*rows)
    for tm in (768, 512, 256, 128):
        if g % tm == 0:
            return tm
    raise ValueError(rows)


def _ffn(xs, nw, w1, w3, w2, layer, tf=256):
    d = xs[0].shape[1]
    m = sum(x.shape[0] for x in xs)
    dff = w1.shape[1]
    nf = dff // tf
    tm = _ffn_row_tile([x.shape[0] for x in xs])
    if len(xs) == 1:
        na, x_specs = None, [pl.BlockSpec((tm, d), lambda i, f: (i, 0))]
    else:
        na, spec_a, spec_b = _split_rows(tm, xs[0].shape[0], xs[1].shape[0])
        x_specs = [spec_a(d), spec_b(d)]
    return pl.pallas_call(
        functools.partial(_ffn_kernel, na=na),
        grid=(m // tm, dff // tf),
        in_specs=x_specs + [
            pl.BlockSpec((1, d), lambda i, f: (0, 0)),
            pl.BlockSpec((d, tf), lambda i, f: (layer, f)),
            pl.BlockSpec((d, tf), lambda i, f: (layer, f)),
            pl.BlockSpec((tf, d), lambda i, f: (layer * nf + f, 0)),
        ],
        out_specs=pl.BlockSpec((tm, d), lambda i, f: (i, 0)),
        out_shape=jax.ShapeDtypeStruct((m, d), F32),
        scratch_shapes=[pltpu.VMEM((tm, d), MXU_DTYPE)],
        compiler_params=_params(("parallel", "arbitrary")),
        name="ffn",
    )(*xs, nw, w1, w3, w2)


def _in_proj_kernel(x_ref, nw_ref, w_ref, o_ref, h_scr):
    @pl.when(pl.program_id(1) == 0)
    def _():
        x = x_ref[...]
        ms = jnp.mean(x * x, axis=-1, keepdims=True)
        h_scr[...] = (x * lax.rsqrt(ms + EPS) * nw_ref[...]).astype(h_scr.dtype)

    o_ref[...] = _mm(h_scr[...], w_ref[...])


def _in_proj(x, nw, w, layer, tm=1024, tn=768):
    m, d = x.shape
    n = w.shape[1]
    tm = _row_tile(m, tm)
    return pl.pallas_call(
        _in_proj_kernel,
        grid=(m // tm, n // tn),
        in_specs=[
            pl.BlockSpec((tm, d), lambda i, j: (i, 0)),
            pl.BlockSpec((1, d), lambda i, j: (0, 0)),
            pl.BlockSpec((d, tn), lambda i, j: (layer, j)),
        ],
        out_specs=pl.BlockSpec((tm, tn), lambda i, j: (i, j)),
        out_shape=jax.ShapeDtypeStruct((m, n), F32),
        scratch_shapes=[pltpu.VMEM((tm, d), MXU_DTYPE)],
        compiler_params=_params(("parallel", "arbitrary")),
        name="in_proj",
    )(x, nw, w)


def _out_proj_kernel(x_ref, ya_ref, yb_ref, aa_ref, ab_ref, w_ref, o_ref, *, na):
    i = pl.program_id(0)
    half = ya_ref.shape[1]
    o_ref[...] = (x_ref[...] + _mm(_pick(i, na, ya_ref, yb_ref), w_ref[:half, :])
                  + _mm(_pick(i, na, aa_ref, ab_ref), w_ref[half:, :]))


def _out_proj(x, y_pair, att_pair, w, layer, tm=512):
    m, d = x.shape
    half = y_pair[0].shape[1]
    tm = _row_tile(math.gcd(y_pair[0].shape[0], y_pair[1].shape[0]), tm)
    na, spec_a, spec_b = _split_rows(tm, y_pair[0].shape[0], y_pair[1].shape[0])
    return pl.pallas_call(
        functools.partial(_out_proj_kernel, na=na),
        grid=(m // tm,),
        in_specs=[
            pl.BlockSpec((tm, d), lambda i: (i, 0)),
            spec_a(half), spec_b(half), spec_a(half), spec_b(half),
            pl.BlockSpec((2 * half, d), lambda i: (layer, 0)),
        ],
        out_specs=pl.BlockSpec((tm, d), lambda i: (i, 0)),
        out_shape=jax.ShapeDtypeStruct((m, d), F32),
        compiler_params=_params(("arbitrary",)),
        name="out_proj",
    )(x, *y_pair, *att_pair, w)


def _final_norm_kernel(x_ref, nw_ref, oa_ref, ob_ref, *, na):
    i = pl.program_id(0)
    x = x_ref[...]
    ms = jnp.mean(x * x, axis=-1, keepdims=True)
    y = x * lax.rsqrt(ms + EPS) * nw_ref[...]

    @pl.when(i < na)
    def _():
        oa_ref[...] = y

    @pl.when(i >= na)
    def _():
        ob_ref[...] = y


def _final_norm(x, nw, rows_a, rows_b, tm=512):
    d = x.shape[1]
    tm = _row_tile(math.gcd(rows_a, rows_b), tm)
    na, spec_a, spec_b = _split_rows(tm, rows_a, rows_b)
    return pl.pallas_call(
        functools.partial(_final_norm_kernel, na=na),
        grid=((rows_a + rows_b) // tm,),
        in_specs=[pl.BlockSpec((tm, d), lambda i: (i, 0)),
                  pl.BlockSpec((1, d), lambda i: (0, 0))],
        out_specs=[spec_a(d), spec_b(d)],
        out_shape=[jax.ShapeDtypeStruct((rows_a, d), F32),
                   jax.ShapeDtypeStruct((rows_b, d), F32)],
        compiler_params=_params(("arbitrary",)),
        name="final_norm",
    )(x, nw)


def _cast_kernel(x_ref, o_ref):
    o_ref[...] = x_ref[...].astype(o_ref.dtype)


def _cast_weights(w, rows):
    depth, r, c = w.shape
    w2 = w.reshape(depth * r, c)
    rows = _row_tile(depth * r, rows)
    return pl.pallas_call(
        _cast_kernel,
        grid=(depth * r // rows,),
        in_specs=[pl.BlockSpec((rows, c), lambda i: (i, 0))],
        out_specs=pl.BlockSpec((rows, c), lambda i: (i, 0)),
        out_shape=jax.ShapeDtypeStruct(w2.shape, MXU_DTYPE),
        compiler_params=_params(("parallel",)),
        name="cast_w",
    )(w2)


_W_IN_PARTS = (
    (COL_Z, 0, D_SSM),
    (COL_XBC, D_SSM, CONV_DIM),
    (COL_DT, D_SSM + CONV_DIM, SSM_HEADS),
    (COL_Q, D_SSM + CONV_DIM + SSM_HEADS, N_HEADS * HEAD_DIM),
    (COL_K, D_SSM + CONV_DIM + SSM_HEADS + N_HEADS * HEAD_DIM, N_KV * HEAD_DIM),
    (COL_V, D_SSM + CONV_DIM + SSM_HEADS + (N_HEADS + N_KV) * HEAD_DIM, N_KV * HEAD_DIM),
    (COL_IQ, D_SSM + CONV_DIM + SSM_HEADS + (N_HEADS + 2 * N_KV) * HEAD_DIM, IDX_HEADS * IDX_DIM),
    (COL_IKW, D_SSM + CONV_DIM + SSM_HEADS + (N_HEADS + 2 * N_KV) * HEAD_DIM + IDX_HEADS * IDX_DIM,
     IDX_DIM + IDX_HEADS),
)


def _pack_w_in_kernel(w_ref, o_ref):
    rows = o_ref.shape[0]
    for dst, src, width in _W_IN_PARTS:
        o_ref[:, dst:dst + width] = w_ref[:, src:src + width].astype(o_ref.dtype)
        pad = -width % LANE
        if pad:
            o_ref[:, dst + width:dst + width + pad] = jnp.zeros((rows, pad), o_ref.dtype)


def _pack_w_in(w, rows=256):
    depth, r, c = w.shape
    w2 = w.reshape(depth * r, c)
    return pl.pallas_call(
        _pack_w_in_kernel,
        grid=(depth * r // rows,),
        in_specs=[pl.BlockSpec((rows, c), lambda i: (i, 0))],
        out_specs=pl.BlockSpec((rows, PROJ_COLS), lambda i: (i, 0)),
        out_shape=jax.ShapeDtypeStruct((depth * r, PROJ_COLS), MXU_DTYPE),
        compiler_params=_params(("parallel",)),
        name="pack_w_in",
    )(w2)


def _rope_lanes(x, c, s1, s2, half):
    outs = []
    for t in range(x.shape[1] // LANE):
        xt = x[:, t * LANE:(t + 1) * LANE]
        outs.append(xt * c + pltpu.roll(xt, half, 1) * s1
                    + pltpu.roll(xt, LANE - half, 1) * s2)
    return outs[0] if len(outs) == 1 else jnp.concatenate(outs, axis=1)


def _prep_kernel(q_ref, iq_ref, k_ref, v_ref, ikw_ref, tab_ref,
                 qo_ref, iqo_ref, ko_ref, kbo_ref, vo_ref, vbo_ref, ikwo_ref):
    tab = [tab_ref[:, i * LANE:(i + 1) * LANE] for i in range(9)]
    qo_ref[...] = _rope_lanes(q_ref[...], tab[0], tab[1], tab[2], ROPE_DIM // 2).astype(qo_ref.dtype)
    k = _rope_lanes(k_ref[...], tab[0], tab[1], tab[2], ROPE_DIM // 2)
    ko_ref[...] = k
    kbo_ref[...] = k.astype(kbo_ref.dtype)
    iqo_ref[...] = _rope_lanes(iq_ref[...], tab[3], tab[4], tab[5], IDX_ROPE_DIM // 2).astype(iqo_ref.dtype)
    ikwo_ref[...] = _rope_lanes(ikw_ref[...], tab[6], tab[7], tab[8], IDX_ROPE_DIM // 2)
    v = v_ref[...]
    vo_ref[...] = v
    vbo_ref[...] = v.astype(vbo_ref.dtype)


def _prep_tile(tp, ms, ts, tm=512):
    tm = _row_tile(math.gcd(tp, ms), tm)
    assert tm % ts == 0
    return tm


def _prep(proj, tab, mp, tp, tm):
    m = proj.shape[0]
    wq = N_HEADS * HEAD_DIM
    wkv = N_KV * HEAD_DIM
    row = lambda w, c: pl.BlockSpec((tm, w), lambda i: (i, c))
    tab_spec = pl.BlockSpec(
        (tm, 9 * LANE), lambda i: (jnp.where(i < mp // tm, i % (tp // tm), tp // tm), 0))
    return pl.pallas_call(
        _prep_kernel,
        grid=(m // tm,),
        in_specs=[row(wq, COL_Q // wq), row(wq, COL_IQ // wq), row(wkv, COL_K // wkv),
                  row(wkv, COL_V // wkv), row(LANE, COL_IKW // LANE), tab_spec],
        out_specs=[row(wq, 0), row(wq, 0), row(wkv, 0), row(wkv, 0), row(wkv, 0), row(wkv, 0),
                   row(LANE, 0)],
        out_shape=[jax.ShapeDtypeStruct((m, wq), MXU_DTYPE),
                   jax.ShapeDtypeStruct((m, wq), MXU_DTYPE),
                   jax.ShapeDtypeStruct((m, wkv), F32),
                   jax.ShapeDtypeStruct((m, wkv), MXU_DTYPE),
                   jax.ShapeDtypeStruct((m, wkv), F32),
                   jax.ShapeDtypeStruct((m, wkv), MXU_DTYPE),
                   jax.ShapeDtypeStruct((m, LANE), F32)],
        compiler_params=_params(("parallel",)),
        name="prep",
    )(proj, proj, proj, proj, proj, tab)


def _rope_table(pos, rot_dim, period, tail=None):
    half = rot_dim // 2
    inv_freq = jnp.float32(ROPE_THETA) ** (-jnp.arange(half, dtype=F32) * 2.0 / rot_dim)
    ang = pos.astype(F32)[:, None] * inv_freq[None, :]
    cos, sin = jnp.cos(ang), jnp.sin(ang)
    n = pos.shape[0]
    ones = jnp.ones((n, period - rot_dim), F32)
    zeros = jnp.zeros((n, period - rot_dim), F32)
    zh = jnp.zeros((n, half), F32)
    c = jnp.concatenate([cos, cos, ones], axis=1)
    s1 = jnp.concatenate([zh, sin, zeros], axis=1)
    s2 = jnp.concatenate([-sin, zh, zeros], axis=1)
    if tail is None:
        reps = LANE // period
        return [jnp.tile(a, (1, reps)) for a in (c, s1, s2)]
    zt = jnp.zeros((n, LANE - period), F32)
    return [jnp.concatenate([c, jnp.broadcast_to(tail[None, :], (n, LANE - period))], axis=1),
            jnp.concatenate([s1, zt], axis=1), jnp.concatenate([s2, zt], axis=1)]


def _ssd_kernel(*refs, qv, qp, has_state):
    if has_state:
        (z_ref, xbc_ref, dt_ref, ssm0_ref, conv0_ref, cw_ref, cb_ref, dtb_ref, aneg_ref,
         dsk_ref, nw_ref, y_ref, ssm_ref, conv_ref, ext_scr, st_scr) = refs
    else:
        (z_ref, xbc_ref, dt_ref, cw_ref, cb_ref, dtb_ref, aneg_ref,
         dsk_ref, nw_ref, y_ref, ssm_ref, conv_ref, ext_scr, st_scr) = refs
    c = pl.program_id(1)
    hp = SSM_HEADS * SSM_HEADDIM
    gw = SSM_STATE
    hpg = hp // SSM_GROUPS

    @pl.when(c == 0)
    def _():
        if has_state:
            ext_scr[0:SUBLANE, :] = conv0_ref[0]
            st_scr[...] = ssm0_ref[0].T
        else:
            ext_scr[0:SUBLANE, :] = jnp.zeros((SUBLANE, CONV_DIM), F32)
            st_scr[...] = jnp.zeros_like(st_scr)

    xbc_raw = xbc_ref[...]
    ext_scr[SUBLANE:SUBLANE + qv, :] = xbc_raw
    conv = cb_ref[...] + xbc_raw * cw_ref[CONV_W - 1:CONV_W, :]
    for j in range(CONV_W - 1):
        off = SUBLANE - (CONV_W - 1) + j
        conv = conv + ext_scr[off:off + qv, :] * cw_ref[j:j + 1, :]
    xc = _silu(conv)
    dtr = dt_ref[...]
    if qv < qp:
        xc = jnp.concatenate([xc, jnp.zeros((qp - qv, CONV_DIM), F32)], axis=0)
        dtr = jnp.concatenate([dtr, jnp.zeros((qp - qv, LANE), F32)], axis=0)
    xs = xc[:, :hp]
    bm = xc[:, hp:hp + SSM_GROUPS * gw]
    cm = xc[:, hp + SSM_GROUPS * gw:]

    xdt_in = dtr + dtb_ref[...]
    dt = jnp.maximum(xdt_in, 0.0) + jnp.log1p(jnp.exp(-jnp.abs(xdt_in)))
    row = lax.broadcasted_iota(jnp.int32, (qp, LANE), 0)
    dt = jnp.where(row < qv, dt, 0.0)
    la = dt * aneg_ref[...]
    ti = lax.broadcasted_iota(jnp.int32, (qp, qp), 0)
    si = lax.broadcasted_iota(jnp.int32, (qp, qp), 1)
    tri = ti >= si
    cum = _mm_exact(tri.astype(F32), la)
    cum_t = cum.T

    eh = lax.broadcasted_iota(jnp.int32, (LANE, hp), 0)
    ec = lax.broadcasted_iota(jnp.int32, (LANE, hp), 1)
    expand = (ec // SSM_HEADDIM == eh).astype(F32)
    dt_x = _mm_exact(dt, expand)
    cum_x = _mm_exact(cum, expand)
    xdt = xs * dt_x

    cb = [_mm_nt(cm[:, g * gw:(g + 1) * gw].astype(MXU_DTYPE),
                 bm[:, g * gw:(g + 1) * gw].astype(MXU_DTYPE)) for g in range(SSM_GROUPS)]
    lane = lax.broadcasted_iota(jnp.int32, (qp, LANE), 1)
    y_parts = []
    for j in range(SSM_HEADS // 2):
        ms = []
        for h in (2 * j, 2 * j + 1):
            d = cum[:, h:h + 1] - cum_t[h:h + 1, :]
            seg = jnp.exp(jnp.where(tri, d, -jnp.inf))
            ms.append((cb[h // (SSM_HEADS // SSM_GROUPS)] * seg).astype(MXU_DTYPE))
        xp = xdt[:, j * LANE:(j + 1) * LANE]
        rhs = jnp.concatenate([jnp.where(lane < SSM_HEADDIM, xp, 0.0),
                               jnp.where(lane >= SSM_HEADDIM, xp, 0.0)], axis=0)
        y_parts.append(_mm(jnp.concatenate(ms, axis=1), rhs.astype(MXU_DTYPE)))
    y = jnp.concatenate(y_parts, axis=1)

    st = st_scr[...]
    y_st = jnp.concatenate(
        [_mm(cm[:, g * gw:(g + 1) * gw].astype(MXU_DTYPE),
             st[:, g * hpg:(g + 1) * hpg].astype(MXU_DTYPE)) for g in range(SSM_GROUPS)], axis=1)
    y = y + y_st * jnp.exp(cum_x) + dsk_ref[...] * xs

    cum_last = cum_x[qp - 1:qp, :]
    xt = (xdt * jnp.exp(cum_last - cum_x)).astype(MXU_DTYPE)
    upd = jnp.concatenate(
        [_mm(bm[:, g * gw:(g + 1) * gw].T.astype(MXU_DTYPE), xt[:, g * hpg:(g + 1) * hpg])
         for g in range(SSM_GROUPS)], axis=1)
    st_new = st * jnp.exp(cum_last) + upd
    st_scr[...] = st_new

    zz = z_ref[...]
    gated = y[:qv, :] * _silu(zz)
    ms2 = jnp.mean(gated * gated, axis=-1, keepdims=True)
    y_ref[...] = (gated * lax.rsqrt(ms2 + EPS) * nw_ref[...]).astype(y_ref.dtype)

    tail_rows = ext_scr[qv:qv + SUBLANE, :]
    ext_scr[0:SUBLANE, :] = tail_rows

    @pl.when(c == pl.num_programs(1) - 1)
    def _():
        ssm_ref[0] = st_new.T
        conv_ref[0] = tail_rows


def _ssd(proj, row0, nb, t, qv, qp, state, consts):
    nc = t // qv
    hp = SSM_HEADS * SSM_HEADDIM
    blk0 = row0 // qv
    rowspec = lambda w, cb_: pl.BlockSpec((qv, w), lambda b, c: (blk0 + b * nc + c, cb_))
    const = lambda a: pl.BlockSpec(a.shape, lambda b, c: (0,) * a.ndim)
    in_specs = [rowspec(hp, COL_Z // hp), rowspec(CONV_DIM, COL_XBC // CONV_DIM),
                rowspec(LANE, COL_DT // LANE)]
    args = [proj, proj, proj]
    if state is not None:
        ssm0, conv0, layer = state
        in_specs += [pl.BlockSpec((1, hp, SSM_STATE), lambda b, c: (layer * nb + b, 0, 0)),
                     pl.BlockSpec((1, SUBLANE, CONV_DIM), lambda b, c: (layer * nb + b, 0, 0))]
        args += [ssm0, conv0]
    in_specs += [const(a) for a in consts]
    args += list(consts)
    return pl.pallas_call(
        functools.partial(_ssd_kernel, qv=qv, qp=qp, has_state=state is not None),
        grid=(nb, nc),
        in_specs=in_specs,
        out_specs=[pl.BlockSpec((qv, hp), lambda b, c: (b * nc + c, 0)),
                   pl.BlockSpec((1, hp, SSM_STATE), lambda b, c: (b, 0, 0)),
                   pl.BlockSpec((1, SUBLANE, CONV_DIM), lambda b, c: (b, 0, 0))],
        out_shape=[jax.ShapeDtypeStruct((nb * t, hp), MXU_DTYPE),
                   jax.ShapeDtypeStruct((nb, hp, SSM_STATE), F32),
                   jax.ShapeDtypeStruct((nb, SUBLANE, CONV_DIM), F32)],
        scratch_shapes=[pltpu.VMEM((qv + SUBLANE, CONV_DIM), F32),
                        pltpu.VMEM((SSM_STATE, hp), F32)],
        compiler_params=_params(("parallel", "arbitrary")),
        name="ssd_state" if state is not None else "ssd_prompt",
    )(*args)


def _row_tiles(n_rows):
    assert n_rows % COUNT_TILE == 0
    return [(r0, r0 + COUNT_TILE) for r0 in range(0, n_rows, COUNT_TILE)]


def _row_iota(r0, r1):
    return lax.broadcasted_iota(jnp.int32, (r1 - r0, LANE), 0) + r0


def _count_rows(fn, n_rows):
    acc = None
    for r0, r1 in _row_tiles(n_rows):
        part = fn(r0, r1).astype(F32)
        acc = part if acc is None else acc + part
    return jnp.sum(acc, axis=0, keepdims=True)


def _select_bias(sc_ref, key_ref, bias_ref, limit, n_keys, topk):
    s_pad = sc_ref.shape[0]
    tiles = _row_tiles(s_pad)
    for r0, r1 in tiles:
        s_io = _row_iota(r0, r1)
        adm = (s_io < limit) & (s_io < n_keys)
        bits = pltpu.bitcast(jnp.where(adm, sc_ref[r0:r1, :], -jnp.inf), jnp.int32)
        key_ref[r0:r1, :] = bits ^ ((bits >> 31) & jnp.int32(0x7FFFFFFF))
    kf = jnp.float32(topk)

    cnt0 = _count_rows(lambda r0, r1: key_ref[r0:r1, :] >= 0, s_pad)
    prefix = jnp.where(cnt0 >= kf, jnp.int32(0), jnp.int32(INT_MIN))

    def body(i, prefix):
        cand = prefix | jnp.left_shift(jnp.int32(1), 30 - i)
        cnt = _count_rows(lambda r0, r1: key_ref[r0:r1, :] >= cand, s_pad)
        return jnp.where(cnt >= kf, cand, prefix)

    thr = lax.fori_loop(0, 31, body, prefix)

    def finite(key):
        return (key > KEY_NEG_INF) & (key < KEY_POS_INF)

    for r0, r1 in tiles:
        key = key_ref[r0:r1, :]
        bias_ref[r0:r1, :] = jnp.where((key >= thr) & finite(key), 0.0, -jnp.inf)

    need = kf - _count_rows(lambda r0, r1: key_ref[r0:r1, :] > thr, s_pad)
    n_eq = _count_rows(lambda r0, r1: key_ref[r0:r1, :] == thr, s_pad)
    tie = (n_eq > need) & (thr > KEY_NEG_INF)
    any_tie = jnp.max(tie.astype(F32)) > 0.0

    @pl.when(any_tie)
    def _():
        nbits = max(1, (s_pad - 1).bit_length())

        def ibody(i, v):
            cand = v | jnp.left_shift(jnp.int32(1), nbits - 1 - i)
            below = _count_rows(
                lambda r0, r1: (key_ref[r0:r1, :] == thr) & (_row_iota(r0, r1) < cand), s_pad)
            return jnp.where(below < need, cand, v)

        last = lax.fori_loop(0, nbits, ibody, jnp.zeros((1, LANE), jnp.int32))
        for r0, r1 in tiles:
            key = key_ref[r0:r1, :]
            sel = (key > thr) | ((key == thr) & (_row_iota(r0, r1) <= last))
            bias_ref[r0:r1, :] = jnp.where(sel & finite(key), 0.0, -jnp.inf)


def _sel_prompt_kernel(iq_ref, ikw_q_ref, ikw_all_ref, bias_out_ref, sc_scr, key_scr, bias_scr,
                       *, topk, tk):
    j = pl.program_id(1)
    t_len = ikw_all_ref.shape[0]
    seg = _causal_seg(t_len)
    pos = j * LANE + lax.broadcasted_iota(jnp.int32, (1, LANE), 1)
    limit = (pos // CHUNK + 1) * CHUNK

    def run(s_eff):
        w_t = ikw_q_ref[...].T
        iq = iq_ref[...]
        for kt in range(s_eff // tk):
            ik = ikw_all_ref[kt * tk:(kt + 1) * tk, 0:IDX_DIM].astype(MXU_DTYPE)
            acc = jnp.zeros((tk, LANE), F32)
            for h in range(IDX_HEADS):
                lt = _mm_nt(ik, iq[:, h * IDX_DIM:(h + 1) * IDX_DIM])
                acc = acc + jnp.maximum(lt, 0.0) * w_t[IDX_DIM + h:IDX_DIM + h + 1, :]
            sc_scr[kt * tk:(kt + 1) * tk, :] = acc
        _select_bias(sc_scr.at[0:s_eff], key_scr.at[0:s_eff], bias_scr.at[0:s_eff],
                     limit, t_len, topk)
        for kt in range(s_eff // LANE):
            bias_out_ref[:, kt * LANE:(kt + 1) * LANE] = (
                bias_scr[kt * LANE:(kt + 1) * LANE, :].T.astype(bias_out_ref.dtype))
        if s_eff < t_len:
            bias_out_ref[:, s_eff:t_len] = jnp.full((LANE, t_len - s_eff), -jnp.inf,
                                                    bias_out_ref.dtype)

    for v in range(t_len // seg):
        pl.when((j * LANE) // seg == v)(functools.partial(run, (v + 1) * seg))


def _causal_seg(t_len):
    return min(4 * LANE, t_len)


def _sel_prompt(iq_rot, ikw_rot, nb, t):
    topk = min(TOPK_MAX, t // 4)
    nq = t // LANE
    wi = IDX_HEADS * IDX_DIM
    return pl.pallas_call(
        functools.partial(_sel_prompt_kernel, topk=topk, tk=min(256, t)),
        grid=(nb, nq),
        in_specs=[pl.BlockSpec((LANE, wi), lambda b, j: (b * nq + j, 0)),
                  pl.BlockSpec((LANE, LANE), lambda b, j: (b * nq + j, 0)),
                  pl.BlockSpec((t, LANE), lambda b, j: (b, 0))],
        out_specs=pl.BlockSpec((LANE, t), lambda b, j: (b * nq + j, 0)),
        out_shape=jax.ShapeDtypeStruct((nb * t, t), MXU_DTYPE),
        scratch_shapes=[pltpu.VMEM((t, LANE), F32), pltpu.VMEM((t, LANE), jnp.int32),
                        pltpu.VMEM((t, LANE), F32)],
        compiler_params=_params(("parallel", "arbitrary")),
        name="sel_prompt",
    )(iq_rot, ikw_rot, ikw_rot)


def _sel_sample_kernel(iqp_ref, w_ref, ikp_ref, ikn_ref, bias_out_ref,
                       ik_scr, sc_scr, key_scr, bias_scr, *, ts, past, topk, n_keys, tk):
    s_pad = sc_scr.shape[0]
    nbp = LANE // ts
    lane = lax.broadcasted_iota(jnp.int32, (tk, LANE), 1)
    for b in range(nbp):
        ik_scr[0:past, :] = ikp_ref[b].astype(MXU_DTYPE)
        ik_scr[past:past + ts, :] = ikn_ref[b * ts:(b + 1) * ts, 0:IDX_DIM].astype(MXU_DTYPE)
        ik_scr[past + ts:s_pad, :] = jnp.zeros((s_pad - past - ts, IDX_DIM), MXU_DTYPE)
        iqp = iqp_ref[b]
        w = w_ref[b]
        for kt in range(s_pad // tk):
            lt = _mm_nt(ik_scr[kt * tk:(kt + 1) * tk, :], iqp)
            r = jnp.maximum(lt, 0.0) * w
            y = r[:, 0:LANE]
            for c in range(1, (IDX_HEADS * ts) // LANE):
                y = y + r[:, c * LANE:(c + 1) * LANE]
            sh = LANE // 2
            while sh >= ts:
                y = y + pltpu.roll(y, sh, 1)
                sh //= 2
            if b == 0:
                sc_scr[kt * tk:(kt + 1) * tk, :] = y
            else:
                sc_scr[kt * tk:(kt + 1) * tk, :] = jnp.where(
                    lane // ts == b, y, sc_scr[kt * tk:(kt + 1) * tk, :])
    q = lax.broadcasted_iota(jnp.int32, (1, LANE), 1) % ts
    limit = ((past + q) // CHUNK + 1) * CHUNK
    _select_bias(sc_scr, key_scr, bias_scr, limit, n_keys, topk)
    for kt in range(s_pad // LANE):
        bias_out_ref[:, kt * LANE:(kt + 1) * LANE] = (
            bias_scr[kt * LANE:(kt + 1) * LANE, :].T.astype(bias_out_ref.dtype))


def _sel_sample(iqp, wrow, ik_past, layer, ikw_rot, row0, nb, ts, past, s_pad):
    n_keys = past + ts
    topk = min(TOPK_MAX, n_keys // 4)
    nbp = LANE // ts
    blk0 = row0 // LANE
    ng = nb // nbp
    return pl.pallas_call(
        functools.partial(_sel_sample_kernel, ts=ts, past=past, topk=topk, n_keys=n_keys,
                          tk=s_pad // 4),
        grid=(nb // nbp,),
        in_specs=[pl.BlockSpec((nbp, IDX_HEADS * ts, IDX_DIM), lambda g: (g, 0, 0)),
                  pl.BlockSpec((nbp, 1, IDX_HEADS * ts), lambda g: (g, 0, 0)),
                  pl.BlockSpec((nbp, past, IDX_DIM), lambda g: (layer * ng + g, 0, 0)),
                  pl.BlockSpec((LANE, LANE), lambda g: (blk0 + g, 0))],
        out_specs=pl.BlockSpec((LANE, s_pad), lambda g: (g, 0)),
        out_shape=jax.ShapeDtypeStruct((nb * ts, s_pad), MXU_DTYPE),
        scratch_shapes=[pltpu.VMEM((s_pad, IDX_DIM), MXU_DTYPE), pltpu.VMEM((s_pad, LANE), F32),
                        pltpu.VMEM((s_pad, LANE), jnp.int32), pltpu.VMEM((s_pad, LANE), F32)],
        compiler_params=_params(("parallel",)),
        name="sel_sample",
    )(iqp, wrow, ik_past, ikw_rot)


def _attend(q_rows, k_all, v_all, bias, o_ref):
    nq = bias.shape[0]
    scale = HEAD_DIM ** -0.5
    bias4 = jnp.concatenate([bias] * KV_GROUP, axis=0)
    for g in range(N_KV):
        qg = jnp.concatenate([q_rows(g * KV_GROUP + i) for i in range(KV_GROUP)], axis=0)
        lg = _mm_nt(qg, k_all[:, g * HEAD_DIM:(g + 1) * HEAD_DIM]) * scale + bias4
        m = jnp.max(lg, axis=-1, keepdims=True)
        p = jnp.exp(lg - m)
        s = jnp.sum(p, axis=-1, keepdims=True)
        o = _mm(p.astype(MXU_DTYPE), v_all[:, g * HEAD_DIM:(g + 1) * HEAD_DIM]) / s
        for i in range(KV_GROUP):
            h = g * KV_GROUP + i
            o_ref[:, h * HEAD_DIM:(h + 1) * HEAD_DIM] = o[i * nq:(i + 1) * nq, :].astype(o_ref.dtype)


def _attn_prompt_kernel(q_ref, k_ref, v_ref, bias_ref, o_ref):
    j = pl.program_id(1)
    t_len = k_ref.shape[0]
    seg = _causal_seg(t_len)

    def run(s_eff):
        _attend(lambda h: q_ref[:, h * HEAD_DIM:(h + 1) * HEAD_DIM], k_ref[0:s_eff, :],
                v_ref[0:s_eff, :], bias_ref[:, 0:s_eff].astype(F32), o_ref)

    for v in range(t_len // seg):
        pl.when((j * LANE) // seg == v)(functools.partial(run, (v + 1) * seg))


def _attn_prompt(q_rot, k_bf, v_bf, bias, nb, t):
    nq = t // LANE
    wq = N_HEADS * HEAD_DIM
    wkv = N_KV * HEAD_DIM
    return pl.pallas_call(
        _attn_prompt_kernel,
        grid=(nb, nq),
        in_specs=[pl.BlockSpec((LANE, wq), lambda b, j: (b * nq + j, 0)),
                  pl.BlockSpec((t, wkv), lambda b, j: (b, 0)),
                  pl.BlockSpec((t, wkv), lambda b, j: (b, 0)),
                  pl.BlockSpec((LANE, t), lambda b, j: (b * nq + j, 0))],
        out_specs=pl.BlockSpec((LANE, wq), lambda b, j: (b * nq + j, 0)),
        out_shape=jax.ShapeDtypeStruct((nb * t, wq), MXU_DTYPE),
        compiler_params=_params(("parallel", "arbitrary")),
        name="attn_prompt",
    )(q_rot, k_bf, v_bf, bias)


def _attn_sample_kernel(q_ref, kp_ref, vp_ref, kn_ref, vn_ref, bias_ref, o_ref, k_scr, v_scr,
                        *, ts, past):
    s_pad = k_scr.shape[0]
    wkv = N_KV * HEAD_DIM
    for g in range(N_KV):
        cols = slice(g * HEAD_DIM, (g + 1) * HEAD_DIM)
        k_scr[0:past, cols] = kp_ref[0, pl.ds(g, past, stride=N_KV), :].astype(MXU_DTYPE)
        v_scr[0:past, cols] = vp_ref[0, pl.ds(g, past, stride=N_KV), :].astype(MXU_DTYPE)
    k_scr[past:past + ts, :] = kn_ref[...]
    v_scr[past:past + ts, :] = vn_ref[...]
    k_scr[past + ts:s_pad, :] = jnp.zeros((s_pad - past - ts, wkv), MXU_DTYPE)
    v_scr[past + ts:s_pad, :] = jnp.zeros((s_pad - past - ts, wkv), MXU_DTYPE)
    _attend(lambda h: q_ref[:, h * HEAD_DIM:(h + 1) * HEAD_DIM], k_scr[...], v_scr[...],
            bias_ref[...].astype(F32), o_ref)


def _attn_sample(q_rot, k_past, v_past, layer, k_bf, v_bf, bias, row0, nb, ts, past, s_pad):
    wq = N_HEADS * HEAD_DIM
    wkv = N_KV * HEAD_DIM
    blk0 = row0 // ts
    return pl.pallas_call(
        functools.partial(_attn_sample_kernel, ts=ts, past=past),
        grid=(nb,),
        in_specs=[pl.BlockSpec((ts, wq), lambda b: (blk0 + b, 0)),
                  pl.BlockSpec((1, past * N_KV, HEAD_DIM), lambda b: (layer * nb + b, 0, 0)),
                  pl.BlockSpec((1, past * N_KV, HEAD_DIM), lambda b: (layer * nb + b, 0, 0)),
                  pl.BlockSpec((ts, wkv), lambda b: (blk0 + b, 0)),
                  pl.BlockSpec((ts, wkv), lambda b: (blk0 + b, 0)),
                  pl.BlockSpec((ts, s_pad), lambda b: (b, 0))],
        out_specs=pl.BlockSpec((ts, wq), lambda b: (b, 0)),
        out_shape=jax.ShapeDtypeStruct((nb * ts, wq), MXU_DTYPE),
        scratch_shapes=[pltpu.VMEM((s_pad, wkv), MXU_DTYPE), pltpu.VMEM((s_pad, wkv), MXU_DTYPE)],
        compiler_params=_params(("parallel",)),
        name="attn_sample",
    )(q_rot, k_past, v_past, k_bf, v_bf, bias)


def kernel(x_prompt, x_sample, cache_k, cache_v, cache_idx_k, state_ssm, state_conv, norm_ffn1, ffn1_w1, ffn1_w3, ffn1_w2, norm_mix, w_in, conv_w, conv_b, dt_bias, a_log, d_skip, ssm_norm_w, w_out, norm_ffn2, ffn2_w1, ffn2_w3, ffn2_w2, final_norm):
    bp, tp, d = x_prompt.shape
    bs, ts = x_sample.shape[:2]
    depth, _, past = cache_k.shape[:3]
    mp, ms = bp * tp, bs * ts
    n_keys_s = past + ts
    s_pad = -(-n_keys_s // LANE) * LANE
    hp = SSM_HEADS * SSM_HEADDIM
    wkv = N_KV * HEAD_DIM

    xs = (x_prompt.reshape(mp, d), x_sample.reshape(ms, d))

    tm_prep = _prep_tile(tp, ms, ts)
    pos = jnp.concatenate([jnp.arange(tp, dtype=jnp.int32),
                           jnp.tile(past + jnp.arange(ts, dtype=jnp.int32), tm_prep // ts)])
    iw_scale = jnp.concatenate([jnp.full((IDX_HEADS,), IDX_HEADS ** -0.5, F32),
                                jnp.ones((LANE - IDX_DIM - IDX_HEADS,), F32)])
    tab = jnp.concatenate(_rope_table(pos, ROPE_DIM, HEAD_DIM)
                          + _rope_table(pos, IDX_ROPE_DIM, IDX_DIM)
                          + _rope_table(pos, IDX_ROPE_DIM, IDX_DIM, tail=iw_scale), axis=1)

    w_in_p = _pack_w_in(w_in)
    stack2d = lambda w: w.reshape(w.shape[0] * w.shape[1], w.shape[2])
    f1w1, f1w3, f1w2 = stack2d(ffn1_w1), stack2d(ffn1_w3), stack2d(ffn1_w2)
    f2w1, f2w3, f2w2 = stack2d(ffn2_w1), stack2d(ffn2_w3), stack2d(ffn2_w2)
    w_out_b = _cast_weights(w_out, 1024)
    k_cache = cache_k.reshape(depth * bs, past * N_KV, HEAD_DIM)
    v_cache = cache_v.reshape(depth * bs, past * N_KV, HEAD_DIM)
    ik_cache = cache_idx_k.reshape(depth * bs, past, IDX_DIM)
    ssm0 = state_ssm.reshape(depth * bs, hp, SSM_STATE)
    conv0_pad = jnp.pad(state_conv, ((0, 0), (0, 0), (SUBLANE - (CONV_W - 1), 0), (0, 0)))
    conv0_pad = conv0_pad.reshape(depth * bs, SUBLANE, CONV_DIM)
    lane_pad = lambda a: jnp.pad(a, (0, LANE - a.shape[0]))[None, :]

    outs = {k: [] for k in ("kp", "vp", "ikp", "sp", "cp", "ks", "vs", "iks", "ss", "cs")}
    for l in range(depth):
        x = _ffn(xs if l == 0 else (x,), norm_ffn1[l][None, :], f1w1, f1w3, f1w2, l)
        proj = _in_proj(x, norm_mix[l][None, :], w_in_p, l)
        q_rot, iq_rot, k_rot, k_bf, v_f, v_bf, ikw_rot = _prep(proj, tab, mp, tp, tm_prep)

        consts = (conv_w[l], conv_b[l][None, :], lane_pad(dt_bias[l]),
                  lane_pad(-jnp.exp(a_log[l])), jnp.repeat(d_skip[l], SSM_HEADDIM)[None, :],
                  ssm_norm_w[l][None, :])
        y_p, ssm_p, conv_p = _ssd(proj, 0, bp, tp, LANE, LANE, None, consts)
        y_s, ssm_s, conv_s = _ssd(proj, mp, bs, ts, ts, LANE, (ssm0, conv0_pad, l), consts)

        bias_p = _sel_prompt(iq_rot, ikw_rot, bp, tp)
        att_p = _attn_prompt(q_rot, k_bf, v_bf, bias_p, bp, tp)

        iqp = iq_rot[mp:].reshape(bs, ts, IDX_HEADS, IDX_DIM).transpose(0, 2, 1, 3)
        iqp = iqp.reshape(bs, IDX_HEADS * ts, IDX_DIM)
        wrow = ikw_rot[mp:, IDX_DIM:IDX_DIM + IDX_HEADS].reshape(bs, ts, IDX_HEADS)
        wrow = wrow.transpose(0, 2, 1).reshape(bs, 1, IDX_HEADS * ts)
        bias_s = _sel_sample(iqp, wrow, ik_cache, l, ikw_rot, mp, bs, ts, past, s_pad)
        att_s = _attn_sample(q_rot, k_cache, v_cache, l, k_bf, v_bf, bias_s,
                             mp, bs, ts, past, s_pad)

        x = _out_proj(x, (y_p, y_s), (att_p, att_s), w_out_b, l)
        x = _ffn((x,), norm_ffn2[l][None, :], f2w1, f2w3, f2w2, l)

        outs["kp"].append(k_rot[:mp].reshape(bp, tp, N_KV, HEAD_DIM))
        outs["vp"].append(v_f[:mp].reshape(bp, tp, N_KV, HEAD_DIM))
        outs["ikp"].append(ikw_rot[:mp, :IDX_DIM].reshape(bp, tp, IDX_DIM))
        outs["sp"].append(ssm_p.reshape(bp, SSM_HEADS, SSM_HEADDIM, SSM_STATE))
        outs["cp"].append(conv_p[:, SUBLANE - (CONV_W - 1):, :])
        outs["ks"].append(k_rot[mp:].reshape(bs, ts, N_KV, HEAD_DIM))
        outs["vs"].append(v_f[mp:].reshape(bs, ts, N_KV, HEAD_DIM))
        outs["iks"].append(ikw_rot[mp:, :IDX_DIM].reshape(bs, ts, IDX_DIM))
        outs["ss"].append(ssm_s.reshape(bs, SSM_HEADS, SSM_HEADDIM, SSM_STATE))
        outs["cs"].append(conv_s[:, SUBLANE - (CONV_W - 1):, :])

    y_p, y_s = _final_norm(x, final_norm[None, :], mp, ms)
    st = {k: jnp.stack(v) for k, v in outs.items()}
    return (y_p.reshape(bp, tp, d), y_s.reshape(bs, ts, d),
            st["kp"], st["vp"], st["ikp"], st["sp"], st["cp"],
            st["ks"], st["vs"], st["iks"], st["ss"], st["cs"])
```

```python
import functools
import math

import jax
import jax.numpy as jnp
from jax import lax
from jax.experimental import pallas as pl
from jax.experimental.pallas import tpu as pltpu

F32 = jnp.float32
MXU_DTYPE = jnp.bfloat16

D_MODEL = 2048
CHUNK = 64
D_SSM = 1024
SSM_HEADDIM = 64
SSM_HEADS = 16
SSM_GROUPS = 2
SSM_STATE = 128
CONV_W = 4
CONV_DIM = D_SSM + 2 * SSM_GROUPS * SSM_STATE
HEAD_DIM = 128
N_HEADS = 8
N_KV = 2
KV_GROUP = N_HEADS // N_KV
ROPE_DIM = HEAD_DIM // 4
IDX_HEADS = 16
IDX_DIM = 64
IDX_ROPE_DIM = IDX_DIM // 4
TOPK_MAX = 256
ROPE_THETA = 500000.0
D_FF = 5632
EPS = 1e-6

LANE = 128
SUBLANE = 8
COUNT_TILE = 128
VMEM_LIMIT = 56 * 1024 * 1024

COL_Z = 0
COL_Q = 1024
COL_IQ = 2048
COL_XBC = 3072
COL_K = 4608
COL_V = 4864
COL_DT = 5120
COL_IKW = 5248
PROJ_COLS = 5376

KEY_NEG_INF = -2139095041
KEY_POS_INF = 2139095040
INT_MIN = -2147483648

NT_DIMS = (((1,), (1,)), ((), ()))


def _mm(a, b):
    return jnp.dot(a, b, preferred_element_type=F32)


def _mm_nt(a, b):
    return lax.dot_general(a, b, NT_DIMS, preferred_element_type=F32)


def _mm_exact(a, b):
    return jnp.dot(a, b, preferred_element_type=F32, precision=lax.Precision.HIGHEST)


def _silu(x):
    return x * jax.nn.sigmoid(x)


def _params(sem, vmem=VMEM_LIMIT):
    return pltpu.CompilerParams(dimension_semantics=sem, vmem_limit_bytes=vmem)


def _row_tile(m, pref):
    t = pref
    while m % t:
        t //= 2
    return t


def _split_rows(tm, rows_a, rows_b):
    assert rows_a % tm == 0 and rows_b % tm == 0
    na = rows_a // tm

    def spec_a(width, *grid_rest):
        return pl.BlockSpec((tm, width), lambda i, *_: (jnp.minimum(i, na - 1), 0))

    def spec_b(width, *grid_rest):
        return pl.BlockSpec((tm, width), lambda i, *_: (jnp.maximum(i - na, 0), 0))

    return na, spec_a, spec_b


def _pick(i, na, a_ref, b_ref):
    return jnp.where(i < na, a_ref[...], b_ref[...])


def _ffn_kernel(*refs, na):
    if na is None:
        x_ref, nw_ref, w1_ref, w3_ref, w2_ref, o_ref, h_scr = refs
        load_x = lambda: x_ref[...]
    else:
        xa_ref, xb_ref, nw_ref, w1_ref, w3_ref, w2_ref, o_ref, h_scr = refs
        load_x = lambda: _pick(pl.program_id(0), na, xa_ref, xb_ref)
    f = pl.program_id(1)

    @pl.when(f == 0)
    def _():
        x = load_x()
        ms = jnp.mean(x * x, axis=-1, keepdims=True)
        h_scr[...] = (x * lax.rsqrt(ms + EPS) * nw_ref[...]).astype(h_scr.dtype)
        o_ref[...] = jnp.zeros_like(o_ref)

    h = h_scr[...]
    a = _mm(h, w1_ref[...])
    b = _mm(h, w3_ref[...])
    g = (_silu(a) * b).astype(h_scr.dtype)
    o_ref[...] += _mm(g, w2_ref[...])

    @pl.when(f == pl.num_programs(1) - 1)
    def _():
        o_ref[...] = load_x() + 0.5 * o_ref[...]


def _ffn_row_tile(rows):
    g = math.gcd(*rows)
    for tm in (768, 512, 256, 128):
        if g % tm == 0:
            return tm
    raise ValueError(rows)


def _ffn(xs, nw, w1, w3, w2, layer, tf=512):
    d = xs[0].shape[1]
    m = sum(x.shape[0] for x in xs)
    dff = w1.shape[1]
    nf = dff // tf
    tm = _ffn_row_tile([x.shape[0] for x in xs])
    if len(xs) == 1:
        na, x_specs = None, [pl.BlockSpec((tm, d), lambda i, f: (i, 0))]
    else:
        na, spec_a, spec_b = _split_rows(tm, xs[0].shape[0], xs[1].shape[0])
        x_specs = [spec_a(d), spec_b(d)]
    return pl.pallas_call(
        functools.partial(_ffn_kernel, na=na),
        grid=(m // tm, dff // tf),
        in_specs=x_specs + [
            pl.BlockSpec((1, d), lambda i, f: (0, 0)),
            pl.BlockSpec((d, tf), lambda i, f: (layer, f)),
            pl.BlockSpec((d, tf), lambda i, f: (layer, f)),
            pl.BlockSpec((tf, d), lambda i, f: (layer * nf + f, 0)),
        ],
        out_specs=pl.BlockSpec((tm, d), lambda i, f: (i, 0)),
        out_shape=jax.ShapeDtypeStruct((m, d), F32),
        scratch_shapes=[pltpu.VMEM((tm, d), MXU_DTYPE)],
        compiler_params=_params(("parallel", "arbitrary")),
        name="ffn",
    )(*xs, nw, w1, w3, w2)


def _in_proj_kernel(x_ref, nw_ref, w_ref, o_ref, h_scr):
    @pl.when(pl.program_id(1) == 0)
    def _():
        x = x_ref[...]
        ms = jnp.mean(x * x, axis=-1, keepdims=True)
        h_scr[...] = (x * lax.rsqrt(ms + EPS) * nw_ref[...]).astype(h_scr.dtype)

    o_ref[...] = _mm_nt(h_scr[...], w_ref[...])


def _in_proj(x, nw, w, layer, tm=1024, tn=768):
    m, d = x.shape
    n = PROJ_COLS
    nn = n // tn
    tm = _row_tile(m, tm)
    return pl.pallas_call(
        _in_proj_kernel,
        grid=(m // tm, nn),
        in_specs=[
            pl.BlockSpec((tm, d), lambda i, j: (i, 0)),
            pl.BlockSpec((1, d), lambda i, j: (0, 0)),
            pl.BlockSpec((tn, d), lambda i, j: (layer * nn + j, 0)),
        ],
        out_specs=pl.BlockSpec((tm, tn), lambda i, j: (i, j)),
        out_shape=jax.ShapeDtypeStruct((m, n), F32),
        scratch_shapes=[pltpu.VMEM((tm, d), MXU_DTYPE)],
        compiler_params=_params(("parallel", "arbitrary")),
        name="in_proj",
    )(x, nw, w)


def _out_proj_kernel(x_ref, ya_ref, yb_ref, aa_ref, ab_ref, w_ref, o_ref, *, na):
    i = pl.program_id(0)
    half = ya_ref.shape[1]
    o_ref[...] = (x_ref[...] + _mm(_pick(i, na, ya_ref, yb_ref), w_ref[:half, :])
                  + _mm(_pick(i, na, aa_ref, ab_ref), w_ref[half:, :]))


def _out_proj(x, y_pair, att_pair, w, layer, tm=512):
    m, d = x.shape
    half = y_pair[0].shape[1]
    tm = _row_tile(math.gcd(y_pair[0].shape[0], y_pair[1].shape[0]), tm)
    na, spec_a, spec_b = _split_rows(tm, y_pair[0].shape[0], y_pair[1].shape[0])
    return pl.pallas_call(
        functools.partial(_out_proj_kernel, na=na),
        grid=(m // tm,),
        in_specs=[
            pl.BlockSpec((tm, d), lambda i: (i, 0)),
            spec_a(half), spec_b(half), spec_a(half), spec_b(half),
            pl.BlockSpec((2 * half, d), lambda i: (layer, 0)),
        ],
        out_specs=pl.BlockSpec((tm, d), lambda i: (i, 0)),
        out_shape=jax.ShapeDtypeStruct((m, d), F32),
        compiler_params=_params(("arbitrary",)),
        name="out_proj",
    )(x, *y_pair, *att_pair, w)


def _final_norm_kernel(x_ref, nw_ref, oa_ref, ob_ref, *, na):
    i = pl.program_id(0)
    x = x_ref[...]
    ms = jnp.mean(x * x, axis=-1, keepdims=True)
    y = x * lax.rsqrt(ms + EPS) * nw_ref[...]

    @pl.when(i < na)
    def _():
        oa_ref[...] = y

    @pl.when(i >= na)
    def _():
        ob_ref[...] = y


def _final_norm(x, nw, rows_a, rows_b, tm=512):
    d = x.shape[1]
    tm = _row_tile(math.gcd(rows_a, rows_b), tm)
    na, spec_a, spec_b = _split_rows(tm, rows_a, rows_b)
    return pl.pallas_call(
        functools.partial(_final_norm_kernel, na=na),
        grid=((rows_a + rows_b) // tm,),
        in_specs=[pl.BlockSpec((tm, d), lambda i: (i, 0)),
                  pl.BlockSpec((1, d), lambda i: (0, 0))],
        out_specs=[spec_a(d), spec_b(d)],
        out_shape=[jax.ShapeDtypeStruct((rows_a, d), F32),
                   jax.ShapeDtypeStruct((rows_b, d), F32)],
        compiler_params=_params(("arbitrary",)),
        name="final_norm",
    )(x, nw)


def _cast_kernel(x_ref, o_ref):
    o_ref[...] = x_ref[...].astype(o_ref.dtype)


def _cast_weights(w, rows):
    depth, r, c = w.shape
    w2 = w.reshape(depth * r, c)
    rows = _row_tile(depth * r, rows)
    return pl.pallas_call(
        _cast_kernel,
        grid=(depth * r // rows,),
        in_specs=[pl.BlockSpec((rows, c), lambda i: (i, 0))],
        out_specs=pl.BlockSpec((rows, c), lambda i: (i, 0)),
        out_shape=jax.ShapeDtypeStruct(w2.shape, MXU_DTYPE),
        compiler_params=_params(("parallel",)),
        name="cast_w",
    )(w2)


_W_IN_PARTS = (
    (COL_Z, 0, D_SSM),
    (COL_XBC, D_SSM, CONV_DIM),
    (COL_DT, D_SSM + CONV_DIM, SSM_HEADS),
    (COL_Q, D_SSM + CONV_DIM + SSM_HEADS, N_HEADS * HEAD_DIM),
    (COL_K, D_SSM + CONV_DIM + SSM_HEADS + N_HEADS * HEAD_DIM, N_KV * HEAD_DIM),
    (COL_V, D_SSM + CONV_DIM + SSM_HEADS + (N_HEADS + N_KV) * HEAD_DIM, N_KV * HEAD_DIM),
    (COL_IQ, D_SSM + CONV_DIM + SSM_HEADS + (N_HEADS + 2 * N_KV) * HEAD_DIM, IDX_HEADS * IDX_DIM),
    (COL_IKW, D_SSM + CONV_DIM + SSM_HEADS + (N_HEADS + 2 * N_KV) * HEAD_DIM + IDX_HEADS * IDX_DIM,
     IDX_DIM + IDX_HEADS),
)


def _pack_w_in_kernel(w_ref, o_ref):
    cols = o_ref.shape[1]
    for dst, src, width in _W_IN_PARTS:
        o_ref[dst:dst + width, :] = w_ref[src:src + width, :].astype(o_ref.dtype)
        pad = -width % LANE
        if pad:
            o_ref[dst + width:dst + width + pad, :] = jnp.zeros((pad, cols), o_ref.dtype)


def _pack_w_in(w, kc=256):
    depth, d, c = w.shape
    wt = jnp.transpose(w, (0, 2, 1)).reshape(depth * c, d)
    return pl.pallas_call(
        _pack_w_in_kernel,
        grid=(depth, d // kc),
        in_specs=[pl.BlockSpec((c, kc), lambda l, k: (l, k))],
        out_specs=pl.BlockSpec((PROJ_COLS, kc), lambda l, k: (l, k)),
        out_shape=jax.ShapeDtypeStruct((depth * PROJ_COLS, d), MXU_DTYPE),
        compiler_params=_params(("parallel", "parallel")),
        name="pack_w_in",
    )(wt)


def _rope_lanes(x, c, s1, s2, half):
    outs = []
    for t in range(x.shape[1] // LANE):
        xt = x[:, t * LANE:(t + 1) * LANE]
        outs.append(xt * c + pltpu.roll(xt, half, 1) * s1
                    + pltpu.roll(xt, LANE - half, 1) * s2)
    return outs[0] if len(outs) == 1 else jnp.concatenate(outs, axis=1)


def _prep_kernel(q_ref, iq_ref, k_ref, v_ref, ikw_ref, tab_ref,
                 qo_ref, iqo_ref, ko_ref, kbo_ref, vo_ref, vbo_ref, ikwo_ref):
    tab = [tab_ref[:, i * LANE:(i + 1) * LANE] for i in range(9)]
    qo_ref[...] = _rope_lanes(q_ref[...], tab[0], tab[1], tab[2], ROPE_DIM // 2).astype(qo_ref.dtype)
    k = _rope_lanes(k_ref[...], tab[0], tab[1], tab[2], ROPE_DIM // 2)
    ko_ref[...] = k
    kbo_ref[...] = k.astype(kbo_ref.dtype)
    iqo_ref[...] = _rope_lanes(iq_ref[...], tab[3], tab[4], tab[5], IDX_ROPE_DIM // 2).astype(iqo_ref.dtype)
    ikwo_ref[...] = _rope_lanes(ikw_ref[...], tab[6], tab[7], tab[8], IDX_ROPE_DIM // 2)
    v = v_ref[...]
    vo_ref[...] = v
    vbo_ref[...] = v.astype(vbo_ref.dtype)


def _prep_tile(tp, ms, ts, tm=512):
    tm = _row_tile(math.gcd(tp, ms), tm)
    assert tm % ts == 0
    return tm


def _prep(proj, tab, mp, tp, tm):
    m = proj.shape[0]
    wq = N_HEADS * HEAD_DIM
    wkv = N_KV * HEAD_DIM
    row = lambda w, c: pl.BlockSpec((tm, w), lambda i: (i, c))
    tab_spec = pl.BlockSpec(
        (tm, 9 * LANE), lambda i: (jnp.where(i < mp // tm, i % (tp // tm), tp // tm), 0))
    return pl.pallas_call(
        _prep_kernel,
        grid=(m // tm,),
        in_specs=[row(wq, COL_Q // wq), row(wq, COL_IQ // wq), row(wkv, COL_K // wkv),
                  row(wkv, COL_V // wkv), row(LANE, COL_IKW // LANE), tab_spec],
        out_specs=[row(wq, 0), row(wq, 0), row(wkv, 0), row(wkv, 0), row(wkv, 0), row(wkv, 0),
                   row(LANE, 0)],
        out_shape=[jax.ShapeDtypeStruct((m, wq), MXU_DTYPE),
                   jax.ShapeDtypeStruct((m, wq), MXU_DTYPE),
                   jax.ShapeDtypeStruct((m, wkv), F32),
                   jax.ShapeDtypeStruct((m, wkv), MXU_DTYPE),
                   jax.ShapeDtypeStruct((m, wkv), F32),
                   jax.ShapeDtypeStruct((m, wkv), MXU_DTYPE),
                   jax.ShapeDtypeStruct((m, LANE), F32)],
        compiler_params=_params(("parallel",)),
        name="prep",
    )(proj, proj, proj, proj, proj, tab)


def _rope_table(pos, rot_dim, period, tail=None):
    half = rot_dim // 2
    inv_freq = jnp.float32(ROPE_THETA) ** (-jnp.arange(half, dtype=F32) * 2.0 / rot_dim)
    ang = pos.astype(F32)[:, None] * inv_freq[None, :]
    cos, sin = jnp.cos(ang), jnp.sin(ang)
    n = pos.shape[0]
    ones = jnp.ones((n, period - rot_dim), F32)
    zeros = jnp.zeros((n, period - rot_dim), F32)
    zh = jnp.zeros((n, half), F32)
    c = jnp.concatenate([cos, cos, ones], axis=1)
    s1 = jnp.concatenate([zh, sin, zeros], axis=1)
    s2 = jnp.concatenate([-sin, zh, zeros], axis=1)
    if tail is None:
        reps = LANE // period
        return [jnp.tile(a, (1, reps)) for a in (c, s1, s2)]
    zt = jnp.zeros((n, LANE - period), F32)
    return [jnp.concatenate([c, jnp.broadcast_to(tail[None, :], (n, LANE - period))], axis=1),
            jnp.concatenate([s1, zt], axis=1), jnp.concatenate([s2, zt], axis=1)]


def _ssd_kernel(*refs, qv, qp, has_state):
    if has_state:
        (z_ref, xbc_ref, dt_ref, ssm0_ref, conv0_ref, cw_ref, cb_ref, dtb_ref, aneg_ref,
         dsk_ref, nw_ref, y_ref, ssm_ref, conv_ref, ext_scr, st_scr) = refs
    else:
        (z_ref, xbc_ref, dt_ref, cw_ref, cb_ref, dtb_ref, aneg_ref,
         dsk_ref, nw_ref, y_ref, ssm_ref, conv_ref, ext_scr, st_scr) = refs
    c = pl.program_id(1)
    hp = SSM_HEADS * SSM_HEADDIM
    gw = SSM_STATE
    hpg = hp // SSM_GROUPS

    @pl.when(c == 0)
    def _():
        if has_state:
            ext_scr[0:SUBLANE, :] = conv0_ref[0]
            st_scr[...] = ssm0_ref[0].T
        else:
            ext_scr[0:SUBLANE, :] = jnp.zeros((SUBLANE, CONV_DIM), F32)
            st_scr[...] = jnp.zeros_like(st_scr)

    xbc_raw = xbc_ref[...]
    ext_scr[SUBLANE:SUBLANE + qv, :] = xbc_raw
    conv = cb_ref[...] + xbc_raw * cw_ref[CONV_W - 1:CONV_W, :]
    for j in range(CONV_W - 1):
        off = SUBLANE - (CONV_W - 1) + j
        conv = conv + ext_scr[off:off + qv, :] * cw_ref[j:j + 1, :]
    xc = _silu(conv)
    dtr = dt_ref[...]
    if qv < qp:
        xc = jnp.concatenate([xc, jnp.zeros((qp - qv, CONV_DIM), F32)], axis=0)
        dtr = jnp.concatenate([dtr, jnp.zeros((qp - qv, LANE), F32)], axis=0)
    xs = xc[:, :hp]
    bm = xc[:, hp:hp + SSM_GROUPS * gw]
    cm = xc[:, hp + SSM_GROUPS * gw:]

    xdt_in = dtr + dtb_ref[...]
    dt = jnp.maximum(xdt_in, 0.0) + jnp.log1p(jnp.exp(-jnp.abs(xdt_in)))
    row = lax.broadcasted_iota(jnp.int32, (qp, LANE), 0)
    dt = jnp.where(row < qv, dt, 0.0)
    la = dt * aneg_ref[...]
    ti = lax.broadcasted_iota(jnp.int32, (qp, qp), 0)
    si = lax.broadcasted_iota(jnp.int32, (qp, qp), 1)
    tri = ti >= si
    cum = _mm_exact(tri.astype(F32), la)
    cum_t = cum.T

    eh = lax.broadcasted_iota(jnp.int32, (LANE, hp), 0)
    ec = lax.broadcasted_iota(jnp.int32, (LANE, hp), 1)
    expand = (ec // SSM_HEADDIM == eh).astype(F32)
    dt_x = _mm_exact(dt, expand)
    cum_x = _mm_exact(cum, expand)
    xdt = xs * dt_x

    cb = [_mm_nt(cm[:, g * gw:(g + 1) * gw].astype(MXU_DTYPE),
                 bm[:, g * gw:(g + 1) * gw].astype(MXU_DTYPE)) for g in range(SSM_GROUPS)]
    lane = lax.broadcasted_iota(jnp.int32, (qp, LANE), 1)
    y_parts = []
    for j in range(SSM_HEADS // 2):
        ms = []
        for h in (2 * j, 2 * j + 1):
            d = cum[:, h:h + 1] - cum_t[h:h + 1, :]
            seg = jnp.exp(jnp.where(tri, d, -jnp.inf))
            ms.append((cb[h // (SSM_HEADS // SSM_GROUPS)] * seg).astype(MXU_DTYPE))
        xp = xdt[:, j * LANE:(j + 1) * LANE]
        rhs = jnp.concatenate([jnp.where(lane < SSM_HEADDIM, xp, 0.0),
                               jnp.where(lane >= SSM_HEADDIM, xp, 0.0)], axis=0)
        y_parts.append(_mm(jnp.concatenate(ms, axis=1), rhs.astype(MXU_DTYPE)))
    y = jnp.concatenate(y_parts, axis=1)

    st = st_scr[...]
    y_st = jnp.concatenate(
        [_mm(cm[:, g * gw:(g + 1) * gw].astype(MXU_DTYPE),
             st[:, g * hpg:(g + 1) * hpg].astype(MXU_DTYPE)) for g in range(SSM_GROUPS)], axis=1)
    y = y + y_st * jnp.exp(cum_x) + dsk_ref[...] * xs

    cum_last = cum_x[qp - 1:qp, :]
    xt = (xdt * jnp.exp(cum_last - cum_x)).astype(MXU_DTYPE)
    upd = jnp.concatenate(
        [_mm(bm[:, g * gw:(g + 1) * gw].T.astype(MXU_DTYPE), xt[:, g * hpg:(g + 1) * hpg])
         for g in range(SSM_GROUPS)], axis=1)
    st_new = st * jnp.exp(cum_last) + upd
    st_scr[...] = st_new

    zz = z_ref[...]
    gated = y[:qv, :] * _silu(zz)
    ms2 = jnp.mean(gated * gated, axis=-1, keepdims=True)
    y_ref[...] = (gated * lax.rsqrt(ms2 + EPS) * nw_ref[...]).astype(y_ref.dtype)

    tail_rows = ext_scr[qv:qv + SUBLANE, :]
    ext_scr[0:SUBLANE, :] = tail_rows

    @pl.when(c == pl.num_programs(1) - 1)
    def _():
        ssm_ref[0] = st_new.T
        conv_ref[0] = tail_rows


def _ssd(proj, row0, nb, t, qv, qp, state, consts):
    nc = t // qv
    hp = SSM_HEADS * SSM_HEADDIM
    blk0 = row0 // qv
    rowspec = lambda w, cb_: pl.BlockSpec((qv, w), lambda b, c: (blk0 + b * nc + c, cb_))
    const = lambda a: pl.BlockSpec(a.shape, lambda b, c: (0,) * a.ndim)
    in_specs = [rowspec(hp, COL_Z // hp), rowspec(CONV_DIM, COL_XBC // CONV_DIM),
                rowspec(LANE, COL_DT // LANE)]
    args = [proj, proj, proj]
    if state is not None:
        ssm0, conv0, layer = state
        in_specs += [pl.BlockSpec((1, hp, SSM_STATE), lambda b, c: (layer * nb + b, 0, 0)),
                     pl.BlockSpec((1, SUBLANE, CONV_DIM), lambda b, c: (layer * nb + b, 0, 0))]
        args += [ssm0, conv0]
    in_specs += [const(a) for a in consts]
    args += list(consts)
    return pl.pallas_call(
        functools.partial(_ssd_kernel, qv=qv, qp=qp, has_state=state is not None),
        grid=(nb, nc),
        in_specs=in_specs,
        out_specs=[pl.BlockSpec((qv, hp), lambda b, c: (b * nc + c, 0)),
                   pl.BlockSpec((1, hp, SSM_STATE), lambda b, c: (b, 0, 0)),
                   pl.BlockSpec((1, SUBLANE, CONV_DIM), lambda b, c: (b, 0, 0))],
        out_shape=[jax.ShapeDtypeStruct((nb * t, hp), MXU_DTYPE),
                   jax.ShapeDtypeStruct((nb, hp, SSM_STATE), F32),
                   jax.ShapeDtypeStruct((nb, SUBLANE, CONV_DIM), F32)],
        scratch_shapes=[pltpu.VMEM((qv + SUBLANE, CONV_DIM), F32),
                        pltpu.VMEM((SSM_STATE, hp), F32)],
        compiler_params=_params(("parallel", "arbitrary")),
        name="ssd_state" if state is not None else "ssd_prompt",
    )(*args)


def _row_tiles(n_rows):
    assert n_rows % COUNT_TILE == 0
    return [(r0, r0 + COUNT_TILE) for r0 in range(0, n_rows, COUNT_TILE)]


def _row_iota(r0, r1):
    return lax.broadcasted_iota(jnp.int32, (r1 - r0, LANE), 0) + r0


def _count_rows(fn, n_rows):
    acc = None
    for r0, r1 in _row_tiles(n_rows):
        part = fn(r0, r1).astype(F32)
        acc = part if acc is None else acc + part
    return jnp.sum(acc, axis=0, keepdims=True)


def _select_bias(sc_ref, key_ref, bias_ref, limit, n_keys, topk):
    s_pad = sc_ref.shape[0]
    tiles = _row_tiles(s_pad)
    for r0, r1 in tiles:
        s_io = _row_iota(r0, r1)
        adm = (s_io < limit) & (s_io < n_keys)
        bits = pltpu.bitcast(jnp.where(adm, sc_ref[r0:r1, :], -jnp.inf), jnp.int32)
        key_ref[r0:r1, :] = bits ^ ((bits >> 31) & jnp.int32(0x7FFFFFFF))
    kf = jnp.float32(topk)

    cnt0 = _count_rows(lambda r0, r1: key_ref[r0:r1, :] >= 0, s_pad)
    prefix = jnp.where(cnt0 >= kf, jnp.int32(0), jnp.int32(INT_MIN))

    def body(i, prefix):
        cand = prefix | jnp.left_shift(jnp.int32(1), 30 - i)
        cnt = _count_rows(lambda r0, r1: key_ref[r0:r1, :] >= cand, s_pad)
        return jnp.where(cnt >= kf, cand, prefix)

    thr = lax.fori_loop(0, 31, body, prefix)

    def finite(key):
        return (key > KEY_NEG_INF) & (key < KEY_POS_INF)

    for r0, r1 in tiles:
        key = key_ref[r0:r1, :]
        bias_ref[r0:r1, :] = jnp.where((key >= thr) & finite(key), 0.0, -jnp.inf)

    need = kf - _count_rows(lambda r0, r1: key_ref[r0:r1, :] > thr, s_pad)
    n_eq = _count_rows(lambda r0, r1: key_ref[r0:r1, :] == thr, s_pad)
    tie = (n_eq > need) & (thr > KEY_NEG_INF)
    any_tie = jnp.max(tie.astype(F32)) > 0.0

    @pl.when(any_tie)
    def _():
        nbits = max(1, (s_pad - 1).bit_length())

        def ibody(i, v):
            cand = v | jnp.left_shift(jnp.int32(1), nbits - 1 - i)
            below = _count_rows(
                lambda r0, r1: (key_ref[r0:r1, :] == thr) & (_row_iota(r0, r1) < cand), s_pad)
            return jnp.where(below < need, cand, v)

        last = lax.fori_loop(0, nbits, ibody, jnp.zeros((1, LANE), jnp.int32))
        for r0, r1 in tiles:
            key = key_ref[r0:r1, :]
            sel = (key > thr) | ((key == thr) & (_row_iota(r0, r1) <= last))
            bias_ref[r0:r1, :] = jnp.where(sel & finite(key), 0.0, -jnp.inf)


def _sel_prompt_kernel(iq_ref, ikw_q_ref, ikw_all_ref, bias_out_ref, sc_scr, key_scr, bias_scr,
                       *, topk, tk):
    j = pl.program_id(1)
    t_len = ikw_all_ref.shape[0]
    seg = _causal_seg(t_len)
    pos = j * LANE + lax.broadcasted_iota(jnp.int32, (1, LANE), 1)
    limit = (pos // CHUNK + 1) * CHUNK

    def run(s_eff):
        w_t = ikw_q_ref[...].T
        iq = iq_ref[...]
        for kt in range(s_eff // tk):
            ik = ikw_all_ref[kt * tk:(kt + 1) * tk, 0:IDX_DIM].astype(MXU_DTYPE)
            acc = jnp.zeros((tk, LANE), F32)
            for h in range(IDX_HEADS):
                lt = _mm_nt(ik, iq[:, h * IDX_DIM:(h + 1) * IDX_DIM])
                acc = acc + jnp.maximum(lt, 0.0) * w_t[IDX_DIM + h:IDX_DIM + h + 1, :]
            sc_scr[kt * tk:(kt + 1) * tk, :] = acc
        _select_bias(sc_scr.at[0:s_eff], key_scr.at[0:s_eff], bias_scr.at[0:s_eff],
                     limit, t_len, topk)
        for kt in range(s_eff // LANE):
            bias_out_ref[:, kt * LANE:(kt + 1) * LANE] = (
                bias_scr[kt * LANE:(kt + 1) * LANE, :].T.astype(bias_out_ref.dtype))
        if s_eff < t_len:
            bias_out_ref[:, s_eff:t_len] = jnp.full((LANE, t_len - s_eff), -jnp.inf,
                                                    bias_out_ref.dtype)

    for v in range(t_len // seg):
        pl.when((j * LANE) // seg == v)(functools.partial(run, (v + 1) * seg))


def _causal_seg(t_len):
    return min(4 * LANE, t_len)


def _sel_prompt(iq_rot, ikw_rot, nb, t):
    topk = min(TOPK_MAX, t // 4)
    nq = t // LANE
    wi = IDX_HEADS * IDX_DIM
    return pl.pallas_call(
        functools.partial(_sel_prompt_kernel, topk=topk, tk=min(256, t)),
        grid=(nb, nq),
        in_specs=[pl.BlockSpec((LANE, wi), lambda b, j: (b * nq + j, 0)),
                  pl.BlockSpec((LANE, LANE), lambda b, j: (b * nq + j, 0)),
                  pl.BlockSpec((t, LANE), lambda b, j: (b, 0))],
        out_specs=pl.BlockSpec((LANE, t), lambda b, j: (b * nq + j, 0)),
        out_shape=jax.ShapeDtypeStruct((nb * t, t), MXU_DTYPE),
        scratch_shapes=[pltpu.VMEM((t, LANE), F32), pltpu.VMEM((t, LANE), jnp.int32),
                        pltpu.VMEM((t, LANE), F32)],
        compiler_params=_params(("parallel", "arbitrary")),
        name="sel_prompt",
    )(iq_rot, ikw_rot, ikw_rot)


def _sel_sample_kernel(iqp_ref, w_ref, ikp_ref, ikn_ref, bias_out_ref,
                       sc_scr, key_scr, bias_scr, *, ts, past, topk, n_keys, tk):
    s_pad = sc_scr.shape[0]
    nbp = LANE // ts
    tiles = [(r0, r0 + tk) for r0 in range(0, past, tk)] + [(past, s_pad)]
    for b in range(nbp):
        iqp = iqp_ref[b]
        w = w_ref[b]
        for r0, r1 in tiles:
            if r0 < past:
                ik = ikp_ref[b, :, r0:r1].T
            else:
                ik = jnp.concatenate([ikn_ref[b * ts:(b + 1) * ts, 0:IDX_DIM],
                                      jnp.zeros((r1 - r0 - ts, IDX_DIM), F32)], axis=0)
            lt = _mm_nt(ik.astype(MXU_DTYPE), iqp)
            r = jnp.maximum(lt, 0.0) * w
            y = r[:, 0:LANE]
            for c in range(1, (IDX_HEADS * ts) // LANE):
                y = y + r[:, c * LANE:(c + 1) * LANE]
            sh = LANE // 2
            while sh >= ts:
                y = y + pltpu.roll(y, sh, 1)
                sh //= 2
            if b == 0:
                sc_scr[r0:r1, :] = y
            else:
                lane = lax.broadcasted_iota(jnp.int32, (r1 - r0, LANE), 1)
                sc_scr[r0:r1, :] = jnp.where(lane // ts == b, y, sc_scr[r0:r1, :])
    q = lax.broadcasted_iota(jnp.int32, (1, LANE), 1) % ts
    limit = ((past + q) // CHUNK + 1) * CHUNK
    _select_bias(sc_scr, key_scr, bias_scr, limit, n_keys, topk)
    for kt in range(s_pad // LANE):
        bias_out_ref[:, kt * LANE:(kt + 1) * LANE] = (
            bias_scr[kt * LANE:(kt + 1) * LANE, :].T.astype(bias_out_ref.dtype))


def _sel_sample(iqp, wrow, ik_past, layer, ikw_rot, row0, nb, ts, past, s_pad):
    n_keys = past + ts
    topk = min(TOPK_MAX, n_keys // 4)
    nbp = LANE // ts
    blk0 = row0 // LANE
    ng = nb // nbp
    return pl.pallas_call(
        functools.partial(_sel_sample_kernel, ts=ts, past=past, topk=topk, n_keys=n_keys,
                          tk=math.gcd(past, 1024)),
        grid=(nb // nbp,),
        in_specs=[pl.BlockSpec((nbp, IDX_HEADS * ts, IDX_DIM), lambda g: (g, 0, 0)),
                  pl.BlockSpec((nbp, 1, IDX_HEADS * ts), lambda g: (g, 0, 0)),
                  pl.BlockSpec((nbp, IDX_DIM, past), lambda g: (layer * ng + g, 0, 0)),
                  pl.BlockSpec((LANE, LANE), lambda g: (blk0 + g, 0))],
        out_specs=pl.BlockSpec((LANE, s_pad), lambda g: (g, 0)),
        out_shape=jax.ShapeDtypeStruct((nb * ts, s_pad), MXU_DTYPE),
        scratch_shapes=[pltpu.VMEM((s_pad, LANE), F32),
                        pltpu.VMEM((s_pad, LANE), jnp.int32), pltpu.VMEM((s_pad, LANE), F32)],
        compiler_params=_params(("parallel",)),
        name="sel_sample",
    )(iqp, wrow, ik_past, ikw_rot)


def _attend(q_rows, k_all, v_all, bias, o_ref):
    nq = bias.shape[0]
    scale = HEAD_DIM ** -0.5
    bias4 = jnp.concatenate([bias] * KV_GROUP, axis=0)
    for g in range(N_KV):
        qg = jnp.concatenate([q_rows(g * KV_GROUP + i) for i in range(KV_GROUP)], axis=0)
        lg = _mm_nt(qg, k_all[:, g * HEAD_DIM:(g + 1) * HEAD_DIM]) * scale + bias4
        m = jnp.max(lg, axis=-1, keepdims=True)
        p = jnp.exp(lg - m)
        s = jnp.sum(p, axis=-1, keepdims=True)
        o = _mm(p.astype(MXU_DTYPE), v_all[:, g * HEAD_DIM:(g + 1) * HEAD_DIM]) / s
        for i in range(KV_GROUP):
            h = g * KV_GROUP + i
            o_ref[:, h * HEAD_DIM:(h + 1) * HEAD_DIM] = o[i * nq:(i + 1) * nq, :].astype(o_ref.dtype)


def _attn_prompt_kernel(q_ref, k_ref, v_ref, bias_ref, o_ref):
    j = pl.program_id(1)
    t_len = k_ref.shape[0]
    seg = _causal_seg(t_len)

    def run(s_eff):
        _attend(lambda h: q_ref[:, h * HEAD_DIM:(h + 1) * HEAD_DIM], k_ref[0:s_eff, :],
                v_ref[0:s_eff, :], bias_ref[:, 0:s_eff].astype(F32), o_ref)

    for v in range(t_len // seg):
        pl.when((j * LANE) // seg == v)(functools.partial(run, (v + 1) * seg))


def _attn_prompt(q_rot, k_bf, v_bf, bias, nb, t):
    nq = t // LANE
    wq = N_HEADS * HEAD_DIM
    wkv = N_KV * HEAD_DIM
    return pl.pallas_call(
        _attn_prompt_kernel,
        grid=(nb, nq),
        in_specs=[pl.BlockSpec((LANE, wq), lambda b, j: (b * nq + j, 0)),
                  pl.BlockSpec((t, wkv), lambda b, j: (b, 0)),
                  pl.BlockSpec((t, wkv), lambda b, j: (b, 0)),
                  pl.BlockSpec((LANE, t), lambda b, j: (b * nq + j, 0))],
        out_specs=pl.BlockSpec((LANE, wq), lambda b, j: (b * nq + j, 0)),
        out_shape=jax.ShapeDtypeStruct((nb * t, wq), MXU_DTYPE),
        compiler_params=_params(("parallel", "arbitrary")),
        name="attn_prompt",
    )(q_rot, k_bf, v_bf, bias)


def _attn_sample_kernel(q_ref, kp_ref, vp_ref, kn_ref, vn_ref, bias_ref, o_ref, k_scr, v_scr,
                        *, ts, past):
    s_pad = k_scr.shape[0]
    wkv = N_KV * HEAD_DIM
    for g in range(N_KV):
        cols = slice(g * HEAD_DIM, (g + 1) * HEAD_DIM)
        k_scr[0:past, cols] = kp_ref[0, pl.ds(g, past, stride=N_KV), :].astype(MXU_DTYPE)
        v_scr[0:past, cols] = vp_ref[0, pl.ds(g, past, stride=N_KV), :].astype(MXU_DTYPE)
    k_scr[past:past + ts, :] = kn_ref[...]
    v_scr[past:past + ts, :] = vn_ref[...]
    k_scr[past + ts:s_pad, :] = jnp.zeros((s_pad - past - ts, wkv), MXU_DTYPE)
    v_scr[past + ts:s_pad, :] = jnp.zeros((s_pad - past - ts, wkv), MXU_DTYPE)
    _attend(lambda h: q_ref[:, h * HEAD_DIM:(h + 1) * HEAD_DIM], k_scr[...], v_scr[...],
            bias_ref[...].astype(F32), o_ref)


def _attn_sample(q_rot, k_past, v_past, layer, k_bf, v_bf, bias, row0, nb, ts, past, s_pad):
    wq = N_HEADS * HEAD_DIM
    wkv = N_KV * HEAD_DIM
    blk0 = row0 // ts
    return pl.pallas_call(
        functools.partial(_attn_sample_kernel, ts=ts, past=past),
        grid=(nb,),
        in_specs=[pl.BlockSpec((ts, wq), lambda b: (blk0 + b, 0)),
                  pl.BlockSpec((1, past * N_KV, HEAD_DIM), lambda b: (layer * nb + b, 0, 0)),
                  pl.BlockSpec((1, past * N_KV, HEAD_DIM), lambda b: (layer * nb + b, 0, 0)),
                  pl.BlockSpec((ts, wkv), lambda b: (blk0 + b, 0)),
                  pl.BlockSpec((ts, wkv), lambda b: (blk0 + b, 0)),
                  pl.BlockSpec((ts, s_pad), lambda b: (b, 0))],
        out_specs=pl.BlockSpec((ts, wq), lambda b: (b, 0)),
        out_shape=jax.ShapeDtypeStruct((nb * ts, wq), MXU_DTYPE),
        scratch_shapes=[pltpu.VMEM((s_pad, wkv), MXU_DTYPE), pltpu.VMEM((s_pad, wkv), MXU_DTYPE)],
        compiler_params=_params(("parallel",)),
        name="attn_sample",
    )(q_rot, k_past, v_past, k_bf, v_bf, bias)


def kernel(x_prompt, x_sample, cache_k, cache_v, cache_idx_k, state_ssm, state_conv, norm_ffn1, ffn1_w1, ffn1_w3, ffn1_w2, norm_mix, w_in, conv_w, conv_b, dt_bias, a_log, d_skip, ssm_norm_w, w_out, norm_ffn2, ffn2_w1, ffn2_w3, ffn2_w2, final_norm):
    bp, tp, d = x_prompt.shape
    bs, ts = x_sample.shape[:2]
    depth, _, past = cache_k.shape[:3]
    mp, ms = bp * tp, bs * ts
    n_keys_s = past + ts
    s_pad = -(-n_keys_s // LANE) * LANE
    hp = SSM_HEADS * SSM_HEADDIM
    wkv = N_KV * HEAD_DIM

    xs = (x_prompt.reshape(mp, d), x_sample.reshape(ms, d))

    tm_prep = _prep_tile(tp, ms, ts)
    pos = jnp.concatenate([jnp.arange(tp, dtype=jnp.int32),
                           jnp.tile(past + jnp.arange(ts, dtype=jnp.int32), tm_prep // ts)])
    iw_scale = jnp.concatenate([jnp.full((IDX_HEADS,), IDX_HEADS ** -0.5, F32),
                                jnp.ones((LANE - IDX_DIM - IDX_HEADS,), F32)])
    tab = jnp.concatenate(_rope_table(pos, ROPE_DIM, HEAD_DIM)
                          + _rope_table(pos, IDX_ROPE_DIM, IDX_DIM)
                          + _rope_table(pos, IDX_ROPE_DIM, IDX_DIM, tail=iw_scale), axis=1)

    w_in_p = _pack_w_in(w_in)
    f1w1, f1w3, f1w2 = _cast_weights(ffn1_w1, 512), _cast_weights(ffn1_w3, 512), _cast_weights(ffn1_w2, 1024)
    f2w1, f2w3, f2w2 = _cast_weights(ffn2_w1, 512), _cast_weights(ffn2_w3, 512), _cast_weights(ffn2_w2, 1024)
    w_out_b = _cast_weights(w_out, 1024)
    k_cache = cache_k.reshape(depth * bs, past * N_KV, HEAD_DIM)
    v_cache = cache_v.reshape(depth * bs, past * N_KV, HEAD_DIM)
    ik_cache = jnp.swapaxes(cache_idx_k, 2, 3).reshape(depth * bs, IDX_DIM, past)
    ssm0 = state_ssm.reshape(depth * bs, hp, SSM_STATE)
    conv0_pad = jnp.pad(state_conv, ((0, 0), (0, 0), (SUBLANE - (CONV_W - 1), 0), (0, 0)))
    conv0_pad = conv0_pad.reshape(depth * bs, SUBLANE, CONV_DIM)
    lane_pad = lambda a: jnp.pad(a, (0, LANE - a.shape[0]))[None, :]

    outs = {k: [] for k in ("kp", "vp", "ikp", "sp", "cp", "ks", "vs", "iks", "ss", "cs")}
    for l in range(depth):
        x = _ffn(xs if l == 0 else (x,), norm_ffn1[l][None, :], f1w1, f1w3, f1w2, l)
        proj = _in_proj(x, norm_mix[l][None, :], w_in_p, l)
        q_rot, iq_rot, k_rot, k_bf, v_f, v_bf, ikw_rot = _prep(proj, tab, mp, tp, tm_prep)

        consts = (conv_w[l], conv_b[l][None, :], lane_pad(dt_bias[l]),
                  lane_pad(-jnp.exp(a_log[l])), jnp.repeat(d_skip[l], SSM_HEADDIM)[None, :],
                  ssm_norm_w[l][None, :])
        y_p, ssm_p, conv_p = _ssd(proj, 0, bp, tp, LANE, LANE, None, consts)
        y_s, ssm_s, conv_s = _ssd(proj, mp, bs, ts, ts, LANE, (ssm0, conv0_pad, l), consts)

        bias_p = _sel_prompt(iq_rot, ikw_rot, bp, tp)
        att_p = _attn_prompt(q_rot, k_bf, v_bf, bias_p, bp, tp)

        iqp = iq_rot[mp:].reshape(bs, ts, IDX_HEADS, IDX_DIM).transpose(0, 2, 1, 3)
        iqp = iqp.reshape(bs, IDX_HEADS * ts, IDX_DIM)
        wrow = ikw_rot[mp:, IDX_DIM:IDX_DIM + IDX_HEADS].reshape(bs, ts, IDX_HEADS)
        wrow = wrow.transpose(0, 2, 1).reshape(bs, 1, IDX_HEADS * ts)
        bias_s = _sel_sample(iqp, wrow, ik_cache, l, ikw_rot, mp, bs, ts, past, s_pad)
        att_s = _attn_sample(q_rot, k_cache, v_cache, l, k_bf, v_bf, bias_s,
                             mp, bs, ts, past, s_pad)

        x = _out_proj(x, (y_p, y_s), (att_p, att_s), w_out_b, l)
        x = _ffn((x,), norm_ffn2[l][None, :], f2w1, f2w3, f2w2, l)

        outs["kp"].append(k_rot[:mp].reshape(bp, tp, N_KV, HEAD_DIM))
        outs["vp"].append(v_f[:mp].reshape(bp, tp, N_KV, HEAD_DIM))
        outs["ikp"].append(ikw_rot[:mp, :IDX_DIM].reshape(bp, tp, IDX_DIM))
        outs["sp"].append(ssm_p.reshape(bp, SSM_HEADS, SSM_HEADDIM, SSM_STATE))
        outs["cp"].append(conv_p[:, SUBLANE - (CONV_W - 1):, :])
        outs["ks"].append(k_rot[mp:].reshape(bs, ts, N_KV, HEAD_DIM))
        outs["vs"].append(v_f[mp:].reshape(bs, ts, N_KV, HEAD_DIM))
        outs["iks"].append(ikw_rot[mp:, :IDX_DIM].reshape(bs, ts, IDX_DIM))
        outs["ss"].append(ssm_s.reshape(bs, SSM_HEADS, SSM_HEADDIM, SSM_STATE))
        outs["cs"].append(conv_s[:, SUBLANE - (CONV_W - 1):, :])

    y_p, y_s = _final_norm(x, final_norm[None, :], mp, ms)
    st = {k: jnp.stack(v) for k, v in outs.items()}
    return (y_p.reshape(bp, tp, d), y_s.reshape(bs, ts, d),
            st["kp"], st["vp"], st["ikp"], st["sp"], st["cp"],
            st["ks"], st["vs"], st["iks"], st["ss"], st["cs"])
```

```python
import functools
import math

import jax
import jax.numpy as jnp
from jax import lax
from jax.experimental import pallas as pl
from jax.experimental.pallas import tpu as pltpu

F32 = jnp.float32
MXU_DTYPE = jnp.bfloat16

D_MODEL = 2048
CHUNK = 64
D_SSM = 1024
SSM_HEADDIM = 64
SSM_HEADS = 16
SSM_GROUPS = 2
SSM_STATE = 128
CONV_W = 4
CONV_DIM = D_SSM + 2 * SSM_GROUPS * SSM_STATE
HEAD_DIM = 128
N_HEADS = 8
N_KV = 2
KV_GROUP = N_HEADS // N_KV
ROPE_DIM = HEAD_DIM // 4
IDX_HEADS = 16
IDX_DIM = 64
IDX_ROPE_DIM = IDX_DIM // 4
TOPK_MAX = 256
ROPE_THETA = 500000.0
D_FF = 5632
EPS = 1e-6

LANE = 128
SUBLANE = 8
COUNT_TILE = 128
VMEM_LIMIT = 56 * 1024 * 1024

COL_Z = 0
COL_Q = 1024
COL_IQ = 2048
COL_XBC = 3072
COL_K = 4608
COL_V = 4864
COL_DT = 5120
COL_IKW = 5248
PROJ_COLS = 5376

KEY_NEG_INF = -2139095041
KEY_POS_INF = 2139095040
INT_MIN = -2147483648

NT_DIMS = (((1,), (1,)), ((), ()))


def _mm(a, b):
    return jnp.dot(a, b, preferred_element_type=F32)


def _mm_nt(a, b):
    return lax.dot_general(a, b, NT_DIMS, preferred_element_type=F32)


def _mm_exact(a, b):
    return jnp.dot(a, b, preferred_element_type=F32, precision=lax.Precision.HIGHEST)


def _silu(x):
    return x * jax.nn.sigmoid(x)


def _params(sem, vmem=VMEM_LIMIT):
    return pltpu.CompilerParams(dimension_semantics=sem, vmem_limit_bytes=vmem)


def _row_tile(m, pref):
    t = pref
    while m % t:
        t //= 2
    return t


def _split_rows(tm, rows_a, rows_b):
    assert rows_a % tm == 0 and rows_b % tm == 0
    na = rows_a // tm

    def spec_a(width, *grid_rest):
        return pl.BlockSpec((tm, width), lambda i, *_: (jnp.minimum(i, na - 1), 0))

    def spec_b(width, *grid_rest):
        return pl.BlockSpec((tm, width), lambda i, *_: (jnp.maximum(i - na, 0), 0))

    return na, spec_a, spec_b


def _pick(i, na, a_ref, b_ref):
    return jnp.where(i < na, a_ref[...], b_ref[...])


def _ffn_kernel(*refs, na, side):
    refs = list(refs)
    if na is None:
        x_ref = refs.pop(0)
        load_x = lambda: x_ref[...]
    else:
        xa_ref, xb_ref = refs.pop(0), refs.pop(0)
        load_x = lambda: _pick(pl.program_id(0), na, xa_ref, xb_ref)
    nw_ref, w1_ref, w3_ref, w2_ref = refs[:4]
    refs = refs[4:]
    if side:
        s1_ref, s3_ref, s2_ref, o_ref, c1_ref, c3_ref, c2_ref, h_scr = refs
    else:
        o_ref, h_scr = refs
    f = pl.program_id(1)

    @pl.when(f == 0)
    def _():
        x = load_x()
        ms = jnp.mean(x * x, axis=-1, keepdims=True)
        h_scr[...] = (x * lax.rsqrt(ms + EPS) * nw_ref[...]).astype(h_scr.dtype)
        o_ref[...] = jnp.zeros_like(o_ref)

    h = h_scr[...]
    a = _mm(h, w1_ref[...])
    b = _mm(h, w3_ref[...])
    g = (_silu(a) * b).astype(h_scr.dtype)
    o_ref[...] += _mm(g, w2_ref[...])

    if side:
        c1_ref[...] = s1_ref[...].astype(c1_ref.dtype)
        c3_ref[...] = s3_ref[...].astype(c3_ref.dtype)
        c2_ref[...] = s2_ref[...].astype(c2_ref.dtype)

    @pl.when(f == pl.num_programs(1) - 1)
    def _():
        o_ref[...] = load_x() + 0.5 * o_ref[...]


def _ffn_row_tile(rows):
    g = math.gcd(*rows)
    for tm in (768, 512, 256, 128):
        if g % tm == 0:
            return tm
    raise ValueError(rows)


def _chunk_rows(total, steps):
    tile = 2 * SUBLANE
    for r in range(tile, total + 1, tile):
        if total % r == 0 and total // r <= steps:
            return r
    raise ValueError((total, steps))


def _ffn(xs, nw, w, nxt=None, tf=512):
    w1, w3, w2 = w
    d = xs[0].shape[1]
    m = sum(x.shape[0] for x in xs)
    dff = w1.shape[1]
    nf = dff // tf
    tm = _ffn_row_tile([x.shape[0] for x in xs])
    if len(xs) == 1:
        na, x_specs = None, [pl.BlockSpec((tm, d), lambda i, f: (i, 0))]
    else:
        na, spec_a, spec_b = _split_rows(tm, xs[0].shape[0], xs[1].shape[0])
        x_specs = [spec_a(d), spec_b(d)]
    in_specs = x_specs + [
        pl.BlockSpec((1, d), lambda i, f: (0, 0)),
        pl.BlockSpec((d, tf), lambda i, f: (0, f)),
        pl.BlockSpec((d, tf), lambda i, f: (0, f)),
        pl.BlockSpec((tf, d), lambda i, f: (f, 0)),
    ]
    out_specs = [pl.BlockSpec((tm, d), lambda i, f: (i, 0))]
    out_shape = [jax.ShapeDtypeStruct((m, d), F32)]
    args = [*xs, nw, w1, w3, w2]
    if nxt is not None:
        s1, s3, s2, layer = nxt
        steps = (m // tm) * nf
        r1, r2 = _chunk_rows(d, steps), _chunk_rows(dff, steps)
        n1, n2 = d // r1, dff // r2

        def chunk(n, base):
            return lambda i, f: (base + jnp.minimum(i * nf + f, n - 1), 0)

        in_specs += [pl.BlockSpec((r1, dff), chunk(n1, layer * n1)),
                     pl.BlockSpec((r1, dff), chunk(n1, layer * n1)),
                     pl.BlockSpec((r2, d), chunk(n2, layer * n2))]
        out_specs += [pl.BlockSpec((r1, dff), chunk(n1, 0)), pl.BlockSpec((r1, dff), chunk(n1, 0)),
                      pl.BlockSpec((r2, d), chunk(n2, 0))]
        out_shape += [jax.ShapeDtypeStruct((d, dff), MXU_DTYPE),
                      jax.ShapeDtypeStruct((d, dff), MXU_DTYPE),
                      jax.ShapeDtypeStruct((dff, d), MXU_DTYPE)]
        args += [s1, s3, s2]
    res = pl.pallas_call(
        functools.partial(_ffn_kernel, na=na, side=nxt is not None),
        grid=(m // tm, nf),
        in_specs=in_specs,
        out_specs=out_specs,
        out_shape=out_shape,
        scratch_shapes=[pltpu.VMEM((tm, d), MXU_DTYPE)],
        compiler_params=_params(("arbitrary", "arbitrary")),
        name="ffn",
    )(*args)
    return res[0], tuple(res[1:])


def _in_proj_kernel(x_ref, nw_ref, w_ref, o_ref, h_scr):
    @pl.when(pl.program_id(1) == 0)
    def _():
        x = x_ref[...]
        ms = jnp.mean(x * x, axis=-1, keepdims=True)
        h_scr[...] = (x * lax.rsqrt(ms + EPS) * nw_ref[...]).astype(h_scr.dtype)

    o_ref[...] = _mm_nt(h_scr[...], w_ref[...])


def _in_proj(x, nw, w, layer, tm=1024, tn=768):
    m, d = x.shape
    n = PROJ_COLS
    nn = n // tn
    tm = _row_tile(m, tm)
    return pl.pallas_call(
        _in_proj_kernel,
        grid=(m // tm, nn),
        in_specs=[
            pl.BlockSpec((tm, d), lambda i, j: (i, 0)),
            pl.BlockSpec((1, d), lambda i, j: (0, 0)),
            pl.BlockSpec((tn, d), lambda i, j: (layer * nn + j, 0)),
        ],
        out_specs=pl.BlockSpec((tm, tn), lambda i, j: (i, j)),
        out_shape=jax.ShapeDtypeStruct((m, n), F32),
        scratch_shapes=[pltpu.VMEM((tm, d), MXU_DTYPE)],
        compiler_params=_params(("parallel", "arbitrary")),
        name="in_proj",
    )(x, nw, w)


def _out_proj_kernel(x_ref, ya_ref, yb_ref, aa_ref, ab_ref, w_ref, o_ref, *, na):
    i = pl.program_id(0)
    half = ya_ref.shape[1]
    o_ref[...] = (x_ref[...] + _mm(_pick(i, na, ya_ref, yb_ref), w_ref[:half, :])
                  + _mm(_pick(i, na, aa_ref, ab_ref), w_ref[half:, :]))


def _out_proj(x, y_pair, att_pair, w, layer, tm=512):
    m, d = x.shape
    half = y_pair[0].shape[1]
    tm = _row_tile(math.gcd(y_pair[0].shape[0], y_pair[1].shape[0]), tm)
    na, spec_a, spec_b = _split_rows(tm, y_pair[0].shape[0], y_pair[1].shape[0])
    return pl.pallas_call(
        functools.partial(_out_proj_kernel, na=na),
        grid=(m // tm,),
        in_specs=[
            pl.BlockSpec((tm, d), lambda i: (i, 0)),
            spec_a(half), spec_b(half), spec_a(half), spec_b(half),
            pl.BlockSpec((2 * half, d), lambda i: (layer, 0)),
        ],
        out_specs=pl.BlockSpec((tm, d), lambda i: (i, 0)),
        out_shape=jax.ShapeDtypeStruct((m, d), F32),
        compiler_params=_params(("arbitrary",)),
        name="out_proj",
    )(x, *y_pair, *att_pair, w)


def _final_norm_kernel(x_ref, nw_ref, oa_ref, ob_ref, *, na):
    i = pl.program_id(0)
    x = x_ref[...]
    ms = jnp.mean(x * x, axis=-1, keepdims=True)
    y = x * lax.rsqrt(ms + EPS) * nw_ref[...]

    @pl.when(i < na)
    def _():
        oa_ref[...] = y

    @pl.when(i >= na)
    def _():
        ob_ref[...] = y


def _final_norm(x, nw, rows_a, rows_b, tm=512):
    d = x.shape[1]
    tm = _row_tile(math.gcd(rows_a, rows_b), tm)
    na, spec_a, spec_b = _split_rows(tm, rows_a, rows_b)
    return pl.pallas_call(
        functools.partial(_final_norm_kernel, na=na),
        grid=((rows_a + rows_b) // tm,),
        in_specs=[pl.BlockSpec((tm, d), lambda i: (i, 0)),
                  pl.BlockSpec((1, d), lambda i: (0, 0))],
        out_specs=[spec_a(d), spec_b(d)],
        out_shape=[jax.ShapeDtypeStruct((rows_a, d), F32),
                   jax.ShapeDtypeStruct((rows_b, d), F32)],
        compiler_params=_params(("arbitrary",)),
        name="final_norm",
    )(x, nw)


def _cast_kernel(x_ref, o_ref):
    o_ref[...] = x_ref[...].astype(o_ref.dtype)


def _cast_weights(w2, n_rows, row0=0, rows=512):
    c = w2.shape[1]
    rows = _row_tile(math.gcd(n_rows, row0) if row0 else n_rows, rows)
    blk0 = row0 // rows
    return pl.pallas_call(
        _cast_kernel,
        grid=(n_rows // rows,),
        in_specs=[pl.BlockSpec((rows, c), lambda i: (blk0 + i, 0))],
        out_specs=pl.BlockSpec((rows, c), lambda i: (i, 0)),
        out_shape=jax.ShapeDtypeStruct((n_rows, c), MXU_DTYPE),
        compiler_params=_params(("parallel",)),
        name="cast_w",
    )(w2)


_W_IN_PARTS = (
    (COL_Z, 0, D_SSM),
    (COL_XBC, D_SSM, CONV_DIM),
    (COL_DT, D_SSM + CONV_DIM, SSM_HEADS),
    (COL_Q, D_SSM + CONV_DIM + SSM_HEADS, N_HEADS * HEAD_DIM),
    (COL_K, D_SSM + CONV_DIM + SSM_HEADS + N_HEADS * HEAD_DIM, N_KV * HEAD_DIM),
    (COL_V, D_SSM + CONV_DIM + SSM_HEADS + (N_HEADS + N_KV) * HEAD_DIM, N_KV * HEAD_DIM),
    (COL_IQ, D_SSM + CONV_DIM + SSM_HEADS + (N_HEADS + 2 * N_KV) * HEAD_DIM, IDX_HEADS * IDX_DIM),
    (COL_IKW, D_SSM + CONV_DIM + SSM_HEADS + (N_HEADS + 2 * N_KV) * HEAD_DIM + IDX_HEADS * IDX_DIM,
     IDX_DIM + IDX_HEADS),
)


def _pack_w_in_kernel(w_ref, o_ref):
    cols = o_ref.shape[1]
    for dst, src, width in _W_IN_PARTS:
        o_ref[dst:dst + width, :] = w_ref[src:src + width, :].astype(o_ref.dtype)
        pad = -width % LANE
        if pad:
            o_ref[dst + width:dst + width + pad, :] = jnp.zeros((pad, cols), o_ref.dtype)


def _pack_w_in(w, kc=256):
    depth, d, c = w.shape
    wt = jnp.transpose(w, (0, 2, 1)).reshape(depth * c, d)
    return pl.pallas_call(
        _pack_w_in_kernel,
        grid=(depth, d // kc),
        in_specs=[pl.BlockSpec((c, kc), lambda l, k: (l, k))],
        out_specs=pl.BlockSpec((PROJ_COLS, kc), lambda l, k: (l, k)),
        out_shape=jax.ShapeDtypeStruct((depth * PROJ_COLS, d), MXU_DTYPE),
        compiler_params=_params(("parallel", "parallel")),
        name="pack_w_in",
    )(wt)


def _rope_lanes(x, c, s1, s2, half):
    outs = []
    for t in range(x.shape[1] // LANE):
        xt = x[:, t * LANE:(t + 1) * LANE]
        outs.append(xt * c + pltpu.roll(xt, half, 1) * s1
                    + pltpu.roll(xt, LANE - half, 1) * s2)
    return outs[0] if len(outs) == 1 else jnp.concatenate(outs, axis=1)


def _prep_kernel(q_ref, iq_ref, k_ref, v_ref, ikw_ref, tab_ref,
                 qo_ref, iqo_ref, ko_ref, kbo_ref, vo_ref, vbo_ref, ikwo_ref):
    tab = [tab_ref[:, i * LANE:(i + 1) * LANE] for i in range(9)]
    qo_ref[...] = _rope_lanes(q_ref[...], tab[0], tab[1], tab[2], ROPE_DIM // 2).astype(qo_ref.dtype)
    k = _rope_lanes(k_ref[...], tab[0], tab[1], tab[2], ROPE_DIM // 2)
    ko_ref[...] = k
    kbo_ref[...] = k.astype(kbo_ref.dtype)
    iqo_ref[...] = _rope_lanes(iq_ref[...], tab[3], tab[4], tab[5], IDX_ROPE_DIM // 2).astype(iqo_ref.dtype)
    ikwo_ref[...] = _rope_lanes(ikw_ref[...], tab[6], tab[7], tab[8], IDX_ROPE_DIM // 2)
    v = v_ref[...]
    vo_ref[...] = v
    vbo_ref[...] = v.astype(vbo_ref.dtype)


def _prep_tile(tp, ms, ts, tm=512):
    tm = _row_tile(math.gcd(tp, ms), tm)
    assert tm % ts == 0
    return tm


def _prep(proj, tab, mp, tp, tm):
    m = proj.shape[0]
    wq = N_HEADS * HEAD_DIM
    wkv = N_KV * HEAD_DIM
    row = lambda w, c: pl.BlockSpec((tm, w), lambda i: (i, c))
    tab_spec = pl.BlockSpec(
        (tm, 9 * LANE), lambda i: (jnp.where(i < mp // tm, i % (tp // tm), tp // tm), 0))
    return pl.pallas_call(
        _prep_kernel,
        grid=(m // tm,),
        in_specs=[row(wq, COL_Q // wq), row(wq, COL_IQ // wq), row(wkv, COL_K // wkv),
                  row(wkv, COL_V // wkv), row(LANE, COL_IKW // LANE), tab_spec],
        out_specs=[row(wq, 0), row(wq, 0), row(wkv, 0), row(wkv, 0), row(wkv, 0), row(wkv, 0),
                   row(LANE, 0)],
        out_shape=[jax.ShapeDtypeStruct((m, wq), MXU_DTYPE),
                   jax.ShapeDtypeStruct((m, wq), MXU_DTYPE),
                   jax.ShapeDtypeStruct((m, wkv), F32),
                   jax.ShapeDtypeStruct((m, wkv), MXU_DTYPE),
                   jax.ShapeDtypeStruct((m, wkv), F32),
                   jax.ShapeDtypeStruct((m, wkv), MXU_DTYPE),
                   jax.ShapeDtypeStruct((m, LANE), F32)],
        compiler_params=_params(("parallel",)),
        name="prep",
    )(proj, proj, proj, proj, proj, tab)


def _rope_table(pos, rot_dim, period, tail=None):
    half = rot_dim // 2
    inv_freq = jnp.float32(ROPE_THETA) ** (-jnp.arange(half, dtype=F32) * 2.0 / rot_dim)
    ang = pos.astype(F32)[:, None] * inv_freq[None, :]
    cos, sin = jnp.cos(ang), jnp.sin(ang)
    n = pos.shape[0]
    ones = jnp.ones((n, period - rot_dim), F32)
    zeros = jnp.zeros((n, period - rot_dim), F32)
    zh = jnp.zeros((n, half), F32)
    c = jnp.concatenate([cos, cos, ones], axis=1)
    s1 = jnp.concatenate([zh, sin, zeros], axis=1)
    s2 = jnp.concatenate([-sin, zh, zeros], axis=1)
    if tail is None:
        reps = LANE // period
        return [jnp.tile(a, (1, reps)) for a in (c, s1, s2)]
    zt = jnp.zeros((n, LANE - period), F32)
    return [jnp.concatenate([c, jnp.broadcast_to(tail[None, :], (n, LANE - period))], axis=1),
            jnp.concatenate([s1, zt], axis=1), jnp.concatenate([s2, zt], axis=1)]


def _ssd_kernel(*refs, qv, qp, has_state):
    if has_state:
        (z_ref, xbc_ref, dt_ref, ssm0_ref, conv0_ref, cw_ref, cb_ref, dtb_ref, aneg_ref,
         dsk_ref, nw_ref, y_ref, ssm_ref, conv_ref, ext_scr, st_scr) = refs
    else:
        (z_ref, xbc_ref, dt_ref, cw_ref, cb_ref, dtb_ref, aneg_ref,
         dsk_ref, nw_ref, y_ref, ssm_ref, conv_ref, ext_scr, st_scr) = refs
    c = pl.program_id(1)
    hp = SSM_HEADS * SSM_HEADDIM
    gw = SSM_STATE
    hpg = hp // SSM_GROUPS

    @pl.when(c == 0)
    def _():
        if has_state:
            ext_scr[0:SUBLANE, :] = conv0_ref[0]
            st_scr[...] = ssm0_ref[0].T
        else:
            ext_scr[0:SUBLANE, :] = jnp.zeros((SUBLANE, CONV_DIM), F32)
            st_scr[...] = jnp.zeros_like(st_scr)

    xbc_raw = xbc_ref[...]
    ext_scr[SUBLANE:SUBLANE + qv, :] = xbc_raw
    conv = cb_ref[...] + xbc_raw * cw_ref[CONV_W - 1:CONV_W, :]
    for j in range(CONV_W - 1):
        off = SUBLANE - (CONV_W - 1) + j
        conv = conv + ext_scr[off:off + qv, :] * cw_ref[j:j + 1, :]
    xc = _silu(conv)
    dtr = dt_ref[...]
    if qv < qp:
        xc = jnp.concatenate([xc, jnp.zeros((qp - qv, CONV_DIM), F32)], axis=0)
        dtr = jnp.concatenate([dtr, jnp.zeros((qp - qv, LANE), F32)], axis=0)
    xs = xc[:, :hp]
    bm = xc[:, hp:hp + SSM_GROUPS * gw]
    cm = xc[:, hp + SSM_GROUPS * gw:]

    xdt_in = dtr + dtb_ref[...]
    dt = jnp.maximum(xdt_in, 0.0) + jnp.log1p(jnp.exp(-jnp.abs(xdt_in)))
    row = lax.broadcasted_iota(jnp.int32, (qp, LANE), 0)
    dt = jnp.where(row < qv, dt, 0.0)
    la = dt * aneg_ref[...]
    ti = lax.broadcasted_iota(jnp.int32, (qp, qp), 0)
    si = lax.broadcasted_iota(jnp.int32, (qp, qp), 1)
    tri = ti >= si
    cum = _mm_exact(tri.astype(F32), la)
    cum_t = cum.T

    eh = lax.broadcasted_iota(jnp.int32, (LANE, hp), 0)
    ec = lax.broadcasted_iota(jnp.int32, (LANE, hp), 1)
    expand = (ec // SSM_HEADDIM == eh).astype(F32)
    dt_x = _mm_exact(dt, expand)
    cum_x = _mm_exact(cum, expand)
    xdt = xs * dt_x

    cb = [_mm_nt(cm[:, g * gw:(g + 1) * gw].astype(MXU_DTYPE),
                 bm[:, g * gw:(g + 1) * gw].astype(MXU_DTYPE)) for g in range(SSM_GROUPS)]
    lane = lax.broadcasted_iota(jnp.int32, (qp, LANE), 1)
    y_parts = []
    for j in range(SSM_HEADS // 2):
        ms = []
        for h in (2 * j, 2 * j + 1):
            d = cum[:, h:h + 1] - cum_t[h:h + 1, :]
            seg = jnp.exp(jnp.where(tri, d, -jnp.inf))
            ms.append((cb[h // (SSM_HEADS // SSM_GROUPS)] * seg).astype(MXU_DTYPE))
        xp = xdt[:, j * LANE:(j + 1) * LANE]
        rhs = jnp.concatenate([jnp.where(lane < SSM_HEADDIM, xp, 0.0),
                               jnp.where(lane >= SSM_HEADDIM, xp, 0.0)], axis=0)
        y_parts.append(_mm(jnp.concatenate(ms, axis=1), rhs.astype(MXU_DTYPE)))
    y = jnp.concatenate(y_parts, axis=1)

    st = st_scr[...]
    y_st = jnp.concatenate(
        [_mm(cm[:, g * gw:(g + 1) * gw].astype(MXU_DTYPE),
             st[:, g * hpg:(g + 1) * hpg].astype(MXU_DTYPE)) for g in range(SSM_GROUPS)], axis=1)
    y = y + y_st * jnp.exp(cum_x) + dsk_ref[...] * xs

    cum_last = cum_x[qp - 1:qp, :]
    xt = (xdt * jnp.exp(cum_last - cum_x)).astype(MXU_DTYPE)
    upd = jnp.concatenate(
        [_mm(bm[:, g * gw:(g + 1) * gw].T.astype(MXU_DTYPE), xt[:, g * hpg:(g + 1) * hpg])
         for g in range(SSM_GROUPS)], axis=1)
    st_new = st * jnp.exp(cum_last) + upd
    st_scr[...] = st_new

    zz = z_ref[...]
    gated = y[:qv, :] * _silu(zz)
    ms2 = jnp.mean(gated * gated, axis=-1, keepdims=True)
    y_ref[...] = (gated * lax.rsqrt(ms2 + EPS) * nw_ref[...]).astype(y_ref.dtype)

    tail_rows = ext_scr[qv:qv + SUBLANE, :]
    ext_scr[0:SUBLANE, :] = tail_rows

    @pl.when(c == pl.num_programs(1) - 1)
    def _():
        ssm_ref[0] = st_new.T
        conv_ref[0] = tail_rows


def _ssd(proj, row0, nb, t, qv, qp, state, consts):
    nc = t // qv
    hp = SSM_HEADS * SSM_HEADDIM
    blk0 = row0 // qv
    rowspec = lambda w, cb_: pl.BlockSpec((qv, w), lambda b, c: (blk0 + b * nc + c, cb_))
    const = lambda a: pl.BlockSpec(a.shape, lambda b, c: (0,) * a.ndim)
    in_specs = [rowspec(hp, COL_Z // hp), rowspec(CONV_DIM, COL_XBC // CONV_DIM),
                rowspec(LANE, COL_DT // LANE)]
    args = [proj, proj, proj]
    if state is not None:
        ssm0, conv0, layer = state
        in_specs += [pl.BlockSpec((1, hp, SSM_STATE), lambda b, c: (layer * nb + b, 0, 0)),
                     pl.BlockSpec((1, SUBLANE, CONV_DIM), lambda b, c: (layer * nb + b, 0, 0))]
        args += [ssm0, conv0]
    in_specs += [const(a) for a in consts]
    args += list(consts)
    return pl.pallas_call(
        functools.partial(_ssd_kernel, qv=qv, qp=qp, has_state=state is not None),
        grid=(nb, nc),
        in_specs=in_specs,
        out_specs=[pl.BlockSpec((qv, hp), lambda b, c: (b * nc + c, 0)),
                   pl.BlockSpec((1, hp, SSM_STATE), lambda b, c: (b, 0, 0)),
                   pl.BlockSpec((1, SUBLANE, CONV_DIM), lambda b, c: (b, 0, 0))],
        out_shape=[jax.ShapeDtypeStruct((nb * t, hp), MXU_DTYPE),
                   jax.ShapeDtypeStruct((nb, hp, SSM_STATE), F32),
                   jax.ShapeDtypeStruct((nb, SUBLANE, CONV_DIM), F32)],
        scratch_shapes=[pltpu.VMEM((qv + SUBLANE, CONV_DIM), F32),
                        pltpu.VMEM((SSM_STATE, hp), F32)],
        compiler_params=_params(("parallel", "arbitrary")),
        name="ssd_state" if state is not None else "ssd_prompt",
    )(*args)


def _row_tiles(n_rows):
    assert n_rows % COUNT_TILE == 0
    return [(r0, r0 + COUNT_TILE) for r0 in range(0, n_rows, COUNT_TILE)]


def _row_iota(r0, r1):
    return lax.broadcasted_iota(jnp.int32, (r1 - r0, LANE), 0) + r0


def _count_rows(fn, n_rows):
    acc = None
    for r0, r1 in _row_tiles(n_rows):
        part = fn(r0, r1).astype(F32)
        acc = part if acc is None else acc + part
    return jnp.sum(acc, axis=0, keepdims=True)


def _select_bias(sc_ref, key_ref, bias_ref, limit, n_keys, topk):
    s_pad = sc_ref.shape[0]
    tiles = _row_tiles(s_pad)
    for r0, r1 in tiles:
        s_io = _row_iota(r0, r1)
        adm = (s_io < limit) & (s_io < n_keys)
        bits = pltpu.bitcast(jnp.where(adm, sc_ref[r0:r1, :], -jnp.inf), jnp.int32)
        key_ref[r0:r1, :] = bits ^ ((bits >> 31) & jnp.int32(0x7FFFFFFF))
    kf = jnp.float32(topk)

    cnt0 = _count_rows(lambda r0, r1: key_ref[r0:r1, :] >= 0, s_pad)
    prefix = jnp.where(cnt0 >= kf, jnp.int32(0), jnp.int32(INT_MIN))

    def body(i, prefix):
        cand = prefix | jnp.left_shift(jnp.int32(1), 30 - i)
        cnt = _count_rows(lambda r0, r1: key_ref[r0:r1, :] >= cand, s_pad)
        return jnp.where(cnt >= kf, cand, prefix)

    thr = lax.fori_loop(0, 31, body, prefix)

    def finite(key):
        return (key > KEY_NEG_INF) & (key < KEY_POS_INF)

    for r0, r1 in tiles:
        key = key_ref[r0:r1, :]
        bias_ref[r0:r1, :] = jnp.where((key >= thr) & finite(key), 0.0, -jnp.inf)

    need = kf - _count_rows(lambda r0, r1: key_ref[r0:r1, :] > thr, s_pad)
    n_eq = _count_rows(lambda r0, r1: key_ref[r0:r1, :] == thr, s_pad)
    tie = (n_eq > need) & (thr > KEY_NEG_INF)
    any_tie = jnp.max(tie.astype(F32)) > 0.0

    @pl.when(any_tie)
    def _():
        nbits = max(1, (s_pad - 1).bit_length())

        def ibody(i, v):
            cand = v | jnp.left_shift(jnp.int32(1), nbits - 1 - i)
            below = _count_rows(
                lambda r0, r1: (key_ref[r0:r1, :] == thr) & (_row_iota(r0, r1) < cand), s_pad)
            return jnp.where(below < need, cand, v)

        last = lax.fori_loop(0, nbits, ibody, jnp.zeros((1, LANE), jnp.int32))
        for r0, r1 in tiles:
            key = key_ref[r0:r1, :]
            sel = (key > thr) | ((key == thr) & (_row_iota(r0, r1) <= last))
            bias_ref[r0:r1, :] = jnp.where(sel & finite(key), 0.0, -jnp.inf)


def _sel_prompt_kernel(iq_ref, ikw_q_ref, ikw_all_ref, bias_out_ref, sc_scr, key_scr, bias_scr,
                       *, topk, tk):
    j = pl.program_id(1)
    t_len = ikw_all_ref.shape[0]
    seg = _causal_seg(t_len)
    pos = j * LANE + lax.broadcasted_iota(jnp.int32, (1, LANE), 1)
    limit = (pos // CHUNK + 1) * CHUNK

    def run(s_eff):
        w_t = ikw_q_ref[...].T
        iq = iq_ref[...]
        for kt in range(s_eff // tk):
            ik = ikw_all_ref[kt * tk:(kt + 1) * tk, 0:IDX_DIM].astype(MXU_DTYPE)
            acc = jnp.zeros((tk, LANE), F32)
            for h in range(IDX_HEADS):
                lt = _mm_nt(ik, iq[:, h * IDX_DIM:(h + 1) * IDX_DIM])
                acc = acc + jnp.maximum(lt, 0.0) * w_t[IDX_DIM + h:IDX_DIM + h + 1, :]
            sc_scr[kt * tk:(kt + 1) * tk, :] = acc
        _select_bias(sc_scr.at[0:s_eff], key_scr.at[0:s_eff], bias_scr.at[0:s_eff],
                     limit, t_len, topk)
        for kt in range(s_eff // LANE):
            bias_out_ref[:, kt * LANE:(kt + 1) * LANE] = (
                bias_scr[kt * LANE:(kt + 1) * LANE, :].T.astype(bias_out_ref.dtype))
        if s_eff < t_len:
            bias_out_ref[:, s_eff:t_len] = jnp.full((LANE, t_len - s_eff), -jnp.inf,
                                                    bias_out_ref.dtype)

    for v in range(t_len // seg):
        pl.when((j * LANE) // seg == v)(functools.partial(run, (v + 1) * seg))


def _causal_seg(t_len):
    return min(4 * LANE, t_len)


def _sel_prompt(iq_rot, ikw_rot, nb, t):
    topk = min(TOPK_MAX, t // 4)
    nq = t // LANE
    wi = IDX_HEADS * IDX_DIM
    return pl.pallas_call(
        functools.partial(_sel_prompt_kernel, topk=topk, tk=min(256, t)),
        grid=(nb, nq),
        in_specs=[pl.BlockSpec((LANE, wi), lambda b, j: (b * nq + j, 0)),
                  pl.BlockSpec((LANE, LANE), lambda b, j: (b * nq + j, 0)),
                  pl.BlockSpec((t, LANE), lambda b, j: (b, 0))],
        out_specs=pl.BlockSpec((LANE, t), lambda b, j: (b * nq + j, 0)),
        out_shape=jax.ShapeDtypeStruct((nb * t, t), MXU_DTYPE),
        scratch_shapes=[pltpu.VMEM((t, LANE), F32), pltpu.VMEM((t, LANE), jnp.int32),
                        pltpu.VMEM((t, LANE), F32)],
        compiler_params=_params(("parallel", "arbitrary")),
        name="sel_prompt",
    )(iq_rot, ikw_rot, ikw_rot)


def _sel_sample_kernel(iqp_ref, w_ref, ikp_ref, ikn_ref, bias_out_ref,
                       sc_scr, key_scr, bias_scr, *, ts, past, topk, n_keys, tk):
    s_pad = sc_scr.shape[0]
    nbp = LANE // ts
    tiles = [(r0, r0 + tk) for r0 in range(0, past, tk)] + [(past, s_pad)]
    for b in range(nbp):
        iqp = iqp_ref[b]
        w = w_ref[b]
        for r0, r1 in tiles:
            if r0 < past:
                ik = ikp_ref[b, :, r0:r1].T
            else:
                ik = jnp.concatenate([ikn_ref[b * ts:(b + 1) * ts, 0:IDX_DIM],
                                      jnp.zeros((r1 - r0 - ts, IDX_DIM), F32)], axis=0)
            lt = _mm_nt(ik.astype(MXU_DTYPE), iqp)
            r = jnp.maximum(lt, 0.0) * w
            y = r[:, 0:LANE]
            for c in range(1, (IDX_HEADS * ts) // LANE):
                y = y + r[:, c * LANE:(c + 1) * LANE]
            sh = LANE // 2
            while sh >= ts:
                y = y + pltpu.roll(y, sh, 1)
                sh //= 2
            if b == 0:
                sc_scr[r0:r1, :] = y
            else:
                lane = lax.broadcasted_iota(jnp.int32, (r1 - r0, LANE), 1)
                sc_scr[r0:r1, :] = jnp.where(lane // ts == b, y, sc_scr[r0:r1, :])
    q = lax.broadcasted_iota(jnp.int32, (1, LANE), 1) % ts
    limit = ((past + q) // CHUNK + 1) * CHUNK
    _select_bias(sc_scr, key_scr, bias_scr, limit, n_keys, topk)
    for kt in range(s_pad // LANE):
        bias_out_ref[:, kt * LANE:(kt + 1) * LANE] = (
            bias_scr[kt * LANE:(kt + 1) * LANE, :].T.astype(bias_out_ref.dtype))


def _sel_sample(iqp, wrow, ik_past, layer, ikw_rot, row0, nb, ts, past, s_pad):
    n_keys = past + ts
    topk = min(TOPK_MAX, n_keys // 4)
    nbp = LANE // ts
    blk0 = row0 // LANE
    ng = nb // nbp
    return pl.pallas_call(
        functools.partial(_sel_sample_kernel, ts=ts, past=past, topk=topk, n_keys=n_keys,
                          tk=math.gcd(past, 1024)),
        grid=(nb // nbp,),
        in_specs=[pl.BlockSpec((nbp, IDX_HEADS * ts, IDX_DIM), lambda g: (g, 0, 0)),
                  pl.BlockSpec((nbp, 1, IDX_HEADS * ts), lambda g: (g, 0, 0)),
                  pl.BlockSpec((nbp, IDX_DIM, past), lambda g: (layer * ng + g, 0, 0)),
                  pl.BlockSpec((LANE, LANE), lambda g: (blk0 + g, 0))],
        out_specs=pl.BlockSpec((LANE, s_pad), lambda g: (g, 0)),
        out_shape=jax.ShapeDtypeStruct((nb * ts, s_pad), MXU_DTYPE),
        scratch_shapes=[pltpu.VMEM((s_pad, LANE), F32),
                        pltpu.VMEM((s_pad, LANE), jnp.int32), pltpu.VMEM((s_pad, LANE), F32)],
        compiler_params=_params(("parallel",)),
        name="sel_sample",
    )(iqp, wrow, ik_past, ikw_rot)


def _attend(q_rows, k_all, v_all, bias, o_ref):
    nq = bias.shape[0]
    scale = HEAD_DIM ** -0.5
    bias4 = jnp.concatenate([bias] * KV_GROUP, axis=0)
    for g in range(N_KV):
        qg = jnp.concatenate([q_rows(g * KV_GROUP + i) for i in range(KV_GROUP)], axis=0)
        lg = _mm_nt(qg, k_all[:, g * HEAD_DIM:(g + 1) * HEAD_DIM]) * scale + bias4
        m = jnp.max(lg, axis=-1, keepdims=True)
        p = jnp.exp(lg - m)
        s = jnp.sum(p, axis=-1, keepdims=True)
        o = _mm(p.astype(MXU_DTYPE), v_all[:, g * HEAD_DIM:(g + 1) * HEAD_DIM]) / s
        for i in range(KV_GROUP):
            h = g * KV_GROUP + i
            o_ref[:, h * HEAD_DIM:(h + 1) * HEAD_DIM] = o[i * nq:(i + 1) * nq, :].astype(o_ref.dtype)


def _attn_prompt_kernel(q_ref, k_ref, v_ref, bias_ref, o_ref):
    j = pl.program_id(1)
    t_len = k_ref.shape[0]
    seg = _causal_seg(t_len)

    def run(s_eff):
        _attend(lambda h: q_ref[:, h * HEAD_DIM:(h + 1) * HEAD_DIM], k_ref[0:s_eff, :],
                v_ref[0:s_eff, :], bias_ref[:, 0:s_eff].astype(F32), o_ref)

    for v in range(t_len // seg):
        pl.when((j * LANE) // seg == v)(functools.partial(run, (v + 1) * seg))


def _attn_prompt(q_rot, k_bf, v_bf, bias, nb, t):
    nq = t // LANE
    wq = N_HEADS * HEAD_DIM
    wkv = N_KV * HEAD_DIM
    return pl.pallas_call(
        _attn_prompt_kernel,
        grid=(nb, nq),
        in_specs=[pl.BlockSpec((LANE, wq), lambda b, j: (b * nq + j, 0)),
                  pl.BlockSpec((t, wkv), lambda b, j: (b, 0)),
                  pl.BlockSpec((t, wkv), lambda b, j: (b, 0)),
                  pl.BlockSpec((LANE, t), lambda b, j: (b * nq + j, 0))],
        out_specs=pl.BlockSpec((LANE, wq), lambda b, j: (b * nq + j, 0)),
        out_shape=jax.ShapeDtypeStruct((nb * t, wq), MXU_DTYPE),
        compiler_params=_params(("parallel", "arbitrary")),
        name="attn_prompt",
    )(q_rot, k_bf, v_bf, bias)


def _attn_sample_kernel(q_ref, kp_ref, vp_ref, kn_ref, vn_ref, bias_ref, o_ref, k_scr, v_scr,
                        *, ts, past):
    s_pad = k_scr.shape[0]
    wkv = N_KV * HEAD_DIM
    for g in range(N_KV):
        cols = slice(g * HEAD_DIM, (g + 1) * HEAD_DIM)
        k_scr[0:past, cols] = kp_ref[0, pl.ds(g, past, stride=N_KV), :].astype(MXU_DTYPE)
        v_scr[0:past, cols] = vp_ref[0, pl.ds(g, past, stride=N_KV), :].astype(MXU_DTYPE)
    k_scr[past:past + ts, :] = kn_ref[...]
    v_scr[past:past + ts, :] = vn_ref[...]
    k_scr[past + ts:s_pad, :] = jnp.zeros((s_pad - past - ts, wkv), MXU_DTYPE)
    v_scr[past + ts:s_pad, :] = jnp.zeros((s_pad - past - ts, wkv), MXU_DTYPE)
    _attend(lambda h: q_ref[:, h * HEAD_DIM:(h + 1) * HEAD_DIM], k_scr[...], v_scr[...],
            bias_ref[...].astype(F32), o_ref)


def _attn_sample(q_rot, k_past, v_past, layer, k_bf, v_bf, bias, row0, nb, ts, past, s_pad):
    wq = N_HEADS * HEAD_DIM
    wkv = N_KV * HEAD_DIM
    blk0 = row0 // ts
    return pl.pallas_call(
        functools.partial(_attn_sample_kernel, ts=ts, past=past),
        grid=(nb,),
        in_specs=[pl.BlockSpec((ts, wq), lambda b: (blk0 + b, 0)),
                  pl.BlockSpec((1, past * N_KV, HEAD_DIM), lambda b: (layer * nb + b, 0, 0)),
                  pl.BlockSpec((1, past * N_KV, HEAD_DIM), lambda b: (layer * nb + b, 0, 0)),
                  pl.BlockSpec((ts, wkv), lambda b: (blk0 + b, 0)),
                  pl.BlockSpec((ts, wkv), lambda b: (blk0 + b, 0)),
                  pl.BlockSpec((ts, s_pad), lambda b: (b, 0))],
        out_specs=pl.BlockSpec((ts, wq), lambda b: (b, 0)),
        out_shape=jax.ShapeDtypeStruct((nb * ts, wq), MXU_DTYPE),
        scratch_shapes=[pltpu.VMEM((s_pad, wkv), MXU_DTYPE), pltpu.VMEM((s_pad, wkv), MXU_DTYPE)],
        compiler_params=_params(("parallel",)),
        name="attn_sample",
    )(q_rot, k_past, v_past, k_bf, v_bf, bias)


def kernel(x_prompt, x_sample, cache_k, cache_v, cache_idx_k, state_ssm, state_conv, norm_ffn1, ffn1_w1, ffn1_w3, ffn1_w2, norm_mix, w_in, conv_w, conv_b, dt_bias, a_log, d_skip, ssm_norm_w, w_out, norm_ffn2, ffn2_w1, ffn2_w3, ffn2_w2, final_norm):
    bp, tp, d = x_prompt.shape
    bs, ts = x_sample.shape[:2]
    depth, _, past = cache_k.shape[:3]
    mp, ms = bp * tp, bs * ts
    n_keys_s = past + ts
    s_pad = -(-n_keys_s // LANE) * LANE
    hp = SSM_HEADS * SSM_HEADDIM
    wkv = N_KV * HEAD_DIM

    xs = (x_prompt.reshape(mp, d), x_sample.reshape(ms, d))

    tm_prep = _prep_tile(tp, ms, ts)
    pos = jnp.concatenate([jnp.arange(tp, dtype=jnp.int32),
                           jnp.tile(past + jnp.arange(ts, dtype=jnp.int32), tm_prep // ts)])
    iw_scale = jnp.concatenate([jnp.full((IDX_HEADS,), IDX_HEADS ** -0.5, F32),
                                jnp.ones((LANE - IDX_DIM - IDX_HEADS,), F32)])
    tab = jnp.concatenate(_rope_table(pos, ROPE_DIM, HEAD_DIM)
                          + _rope_table(pos, IDX_ROPE_DIM, IDX_DIM)
                          + _rope_table(pos, IDX_ROPE_DIM, IDX_DIM, tail=iw_scale), axis=1)

    w_in_p = _pack_w_in(w_in)
    stack2d = lambda w: w.reshape(w.shape[0] * w.shape[1], w.shape[2])
    ffn_f32 = [tuple(stack2d(w) for w in ws)
               for ws in ((ffn1_w1, ffn1_w3, ffn1_w2), (ffn2_w1, ffn2_w3, ffn2_w2))]
    dff = ffn1_w1.shape[2]
    w_ffn = (_cast_weights(ffn_f32[0][0], d), _cast_weights(ffn_f32[0][1], d),
             _cast_weights(ffn_f32[0][2], dff))
    w_out_b = _cast_weights(stack2d(w_out), depth * w_out.shape[1])
    k_cache = cache_k.reshape(depth * bs, past * N_KV, HEAD_DIM)
    v_cache = cache_v.reshape(depth * bs, past * N_KV, HEAD_DIM)
    ik_cache = jnp.swapaxes(cache_idx_k, 2, 3).reshape(depth * bs, IDX_DIM, past)
    ssm0 = state_ssm.reshape(depth * bs, hp, SSM_STATE)
    conv0_pad = jnp.pad(state_conv, ((0, 0), (0, 0), (SUBLANE - (CONV_W - 1), 0), (0, 0)))
    conv0_pad = conv0_pad.reshape(depth * bs, SUBLANE, CONV_DIM)
    lane_pad = lambda a: jnp.pad(a, (0, LANE - a.shape[0]))[None, :]

    outs = {k: [] for k in ("kp", "vp", "ikp", "sp", "cp", "ks", "vs", "iks", "ss", "cs")}
    for l in range(depth):
        x, w_ffn = _ffn(xs if l == 0 else (x,), norm_ffn1[l][None, :], w_ffn, (*ffn_f32[1], l))
        proj = _in_proj(x, norm_mix[l][None, :], w_in_p, l)
        q_rot, iq_rot, k_rot, k_bf, v_f, v_bf, ikw_rot = _prep(proj, tab, mp, tp, tm_prep)

        consts = (conv_w[l], conv_b[l][None, :], lane_pad(dt_bias[l]),
                  lane_pad(-jnp.exp(a_log[l])), jnp.repeat(d_skip[l], SSM_HEADDIM)[None, :],
                  ssm_norm_w[l][None, :])
        y_p, ssm_p, conv_p = _ssd(proj, 0, bp, tp, LANE, LANE, None, consts)
        y_s, ssm_s, conv_s = _ssd(proj, mp, bs, ts, ts, LANE, (ssm0, conv0_pad, l), consts)

        bias_p = _sel_prompt(iq_rot, ikw_rot, bp, tp)
        att_p = _attn_prompt(q_rot, k_bf, v_bf, bias_p, bp, tp)

        iqp = iq_rot[mp:].reshape(bs, ts, IDX_HEADS, IDX_DIM).transpose(0, 2, 1, 3)
        iqp = iqp.reshape(bs, IDX_HEADS * ts, IDX_DIM)
        wrow = ikw_rot[mp:, IDX_DIM:IDX_DIM + IDX_HEADS].reshape(bs, ts, IDX_HEADS)
        wrow = wrow.transpose(0, 2, 1).reshape(bs, 1, IDX_HEADS * ts)
        bias_s = _sel_sample(iqp, wrow, ik_cache, l, ikw_rot, mp, bs, ts, past, s_pad)
        att_s = _attn_sample(q_rot, k_cache, v_cache, l, k_bf, v_bf, bias_s,
                             mp, bs, ts, past, s_pad)

        x = _out_proj(x, (y_p, y_s), (att_p, att_s), w_out_b, l)
        x, w_ffn = _ffn((x,), norm_ffn2[l][None, :], w_ffn,
                        (*ffn_f32[0], l + 1) if l + 1 < depth else None)

        outs["kp"].append(k_rot[:mp].reshape(bp, tp, N_KV, HEAD_DIM))
        outs["vp"].append(v_f[:mp].reshape(bp, tp, N_KV, HEAD_DIM))
        outs["ikp"].append(ikw_rot[:mp, :IDX_DIM].reshape(bp, tp, IDX_DIM))
        outs["sp"].append(ssm_p.reshape(bp, SSM_HEADS, SSM_HEADDIM, SSM_STATE))
        outs["cp"].append(conv_p[:, SUBLANE - (CONV_W - 1):, :])
        outs["ks"].append(k_rot[mp:].reshape(bs, ts, N_KV, HEAD_DIM))
        outs["vs"].append(v_f[mp:].reshape(bs, ts, N_KV, HEAD_DIM))
        outs["iks"].append(ikw_rot[mp:, :IDX_DIM].reshape(bs, ts, IDX_DIM))
        outs["ss"].append(ssm_s.reshape(bs, SSM_HEADS, SSM_HEADDIM, SSM_STATE))
        outs["cs"].append(conv_s[:, SUBLANE - (CONV_W - 1):, :])

    y_p, y_s = _final_norm(x, final_norm[None, :], mp, ms)
    st = {k: jnp.stack(v) for k, v in outs.items()}
    return (y_p.reshape(bp, tp, d), y_s.reshape(bs, ts, d),
            st["kp"], st["vp"], st["ikp"], st["sp"], st["cp"],
            st["ks"], st["vs"], st["iks"], st["ss"], st["cs"])
```

```python
import functools
import math

import jax
import jax.numpy as jnp
from jax import lax
from jax.experimental import pallas as pl
from jax.experimental.pallas import tpu as pltpu

F32 = jnp.float32
MXU_DTYPE = jnp.bfloat16

D_MODEL = 2048
CHUNK = 64
D_SSM = 1024
SSM_HEADDIM = 64
SSM_HEADS = 16
SSM_GROUPS = 2
SSM_STATE = 128
CONV_W = 4
CONV_DIM = D_SSM + 2 * SSM_GROUPS * SSM_STATE
HEAD_DIM = 128
N_HEADS = 8
N_KV = 2
KV_GROUP = N_HEADS // N_KV
ROPE_DIM = HEAD_DIM // 4
IDX_HEADS = 16
IDX_DIM = 64
IDX_ROPE_DIM = IDX_DIM // 4
TOPK_MAX = 256
ROPE_THETA = 500000.0
D_FF = 5632
EPS = 1e-6

LANE = 128
SUBLANE = 8
COUNT_TILE = 128
ATTN_KEY_TILE = 512
VMEM_LIMIT = 56 * 1024 * 1024

COL_Z = 0
COL_Q = 1024
COL_IQ = 2048
COL_XBC = 3072
COL_K = 4608
COL_V = 4864
COL_DT = 5120
COL_IKW = 5248
PROJ_COLS = 5376

KEY_NEG_INF = -2139095041
KEY_POS_INF = 2139095040
INT_MIN = -2147483648

NT_DIMS = (((1,), (1,)), ((), ()))


def _mm(a, b):
    return jnp.dot(a, b, preferred_element_type=F32)


def _mm_nt(a, b):
    return lax.dot_general(a, b, NT_DIMS, preferred_element_type=F32)


def _mm_exact(a, b):
    return jnp.dot(a, b, preferred_element_type=F32, precision=lax.Precision.HIGHEST)


def _silu(x):
    return x * jax.nn.sigmoid(x)


def _params(sem, vmem=VMEM_LIMIT):
    return pltpu.CompilerParams(dimension_semantics=sem, vmem_limit_bytes=vmem)


def _row_tile(m, pref):
    t = pref
    while m % t:
        t //= 2
    return t


def _split_rows(tm, rows_a, rows_b):
    assert rows_a % tm == 0 and rows_b % tm == 0
    na = rows_a // tm

    def spec_a(width, *grid_rest):
        return pl.BlockSpec((tm, width), lambda i, *_: (jnp.minimum(i, na - 1), 0))

    def spec_b(width, *grid_rest):
        return pl.BlockSpec((tm, width), lambda i, *_: (jnp.maximum(i - na, 0), 0))

    return na, spec_a, spec_b


def _pick(i, na, a_ref, b_ref):
    return jnp.where(i < na, a_ref[...], b_ref[...])


def _ffn_kernel(*refs, na, side):
    refs = list(refs)
    if na is None:
        x_ref = refs.pop(0)
        load_x = lambda: x_ref[...]
    else:
        xa_ref, xb_ref = refs.pop(0), refs.pop(0)
        load_x = lambda: _pick(pl.program_id(0), na, xa_ref, xb_ref)
    nw_ref, w1_ref, w3_ref, w2_ref = refs[:4]
    refs = refs[4:]
    if side:
        s1_ref, s3_ref, s2_ref, o_ref, c1_ref, c3_ref, c2_ref, h_scr = refs
    else:
        o_ref, h_scr = refs
    f = pl.program_id(1)

    @pl.when(f == 0)
    def _():
        x = load_x()
        ms = jnp.mean(x * x, axis=-1, keepdims=True)
        h_scr[...] = (x * lax.rsqrt(ms + EPS) * nw_ref[...]).astype(h_scr.dtype)
        o_ref[...] = jnp.zeros_like(o_ref)

    h = h_scr[...]
    a = _mm(h, w1_ref[...])
    b = _mm(h, w3_ref[...])
    g = (_silu(a) * b).astype(h_scr.dtype)
    o_ref[...] += _mm(g, w2_ref[...])

    if side:
        c1_ref[...] = s1_ref[...].astype(c1_ref.dtype)
        c3_ref[...] = s3_ref[...].astype(c3_ref.dtype)
        c2_ref[...] = s2_ref[...].astype(c2_ref.dtype)

    @pl.when(f == pl.num_programs(1) - 1)
    def _():
        o_ref[...] = load_x() + 0.5 * o_ref[...]


def _ffn_row_tile(rows):
    g = math.gcd(*rows)
    for tm in (768, 512, 256, 128):
        if g % tm == 0:
            return tm
    raise ValueError(rows)


def _chunk_rows(total, steps):
    tile = 2 * SUBLANE
    for r in range(tile, total + 1, tile):
        if total % r == 0 and total // r <= steps:
            return r
    raise ValueError((total, steps))


def _ffn(xs, nw, w, nxt=None, tf=512):
    w1, w3, w2 = w
    d = xs[0].shape[1]
    m = sum(x.shape[0] for x in xs)
    dff = w1.shape[1]
    nf = dff // tf
    tm = _ffn_row_tile([x.shape[0] for x in xs])
    if len(xs) == 1:
        na, x_specs = None, [pl.BlockSpec((tm, d), lambda i, f: (i, 0))]
    else:
        na, spec_a, spec_b = _split_rows(tm, xs[0].shape[0], xs[1].shape[0])
        x_specs = [spec_a(d), spec_b(d)]
    in_specs = x_specs + [
        pl.BlockSpec((1, d), lambda i, f: (0, 0)),
        pl.BlockSpec((d, tf), lambda i, f: (0, f)),
        pl.BlockSpec((d, tf), lambda i, f: (0, f)),
        pl.BlockSpec((tf, d), lambda i, f: (f, 0)),
    ]
    out_specs = [pl.BlockSpec((tm, d), lambda i, f: (i, 0))]
    out_shape = [jax.ShapeDtypeStruct((m, d), F32)]
    args = [*xs, nw, w1, w3, w2]
    if nxt is not None:
        s1, s3, s2, layer = nxt
        steps = (m // tm) * nf
        r1, r2 = _chunk_rows(d, steps), _chunk_rows(dff, steps)
        n1, n2 = d // r1, dff // r2

        def chunk(n, base):
            return lambda i, f: (base + jnp.minimum(i * nf + f, n - 1), 0)

        in_specs += [pl.BlockSpec((r1, dff), chunk(n1, layer * n1)),
                     pl.BlockSpec((r1, dff), chunk(n1, layer * n1)),
                     pl.BlockSpec((r2, d), chunk(n2, layer * n2))]
        out_specs += [pl.BlockSpec((r1, dff), chunk(n1, 0)), pl.BlockSpec((r1, dff), chunk(n1, 0)),
                      pl.BlockSpec((r2, d), chunk(n2, 0))]
        out_shape += [jax.ShapeDtypeStruct((d, dff), MXU_DTYPE),
                      jax.ShapeDtypeStruct((d, dff), MXU_DTYPE),
                      jax.ShapeDtypeStruct((dff, d), MXU_DTYPE)]
        args += [s1, s3, s2]
    res = pl.pallas_call(
        functools.partial(_ffn_kernel, na=na, side=nxt is not None),
        grid=(m // tm, nf),
        in_specs=in_specs,
        out_specs=out_specs,
        out_shape=out_shape,
        scratch_shapes=[pltpu.VMEM((tm, d), MXU_DTYPE)],
        compiler_params=_params(("arbitrary", "arbitrary")),
        name="ffn",
    )(*args)
    return res[0], tuple(res[1:])


def _in_proj_kernel(x_ref, nw_ref, w_ref, o_ref, h_scr):
    @pl.when(pl.program_id(1) == 0)
    def _():
        x = x_ref[...]
        ms = jnp.mean(x * x, axis=-1, keepdims=True)
        h_scr[...] = (x * lax.rsqrt(ms + EPS) * nw_ref[...]).astype(h_scr.dtype)

    o_ref[...] = _mm_nt(h_scr[...], w_ref[...])


def _in_proj(x, nw, w, layer, tm=1024, tn=768):
    m, d = x.shape
    n = PROJ_COLS
    nn = n // tn
    tm = _row_tile(m, tm)
    return pl.pallas_call(
        _in_proj_kernel,
        grid=(m // tm, nn),
        in_specs=[
            pl.BlockSpec((tm, d), lambda i, j: (i, 0)),
            pl.BlockSpec((1, d), lambda i, j: (0, 0)),
            pl.BlockSpec((tn, d), lambda i, j: (layer * nn + j, 0)),
        ],
        out_specs=pl.BlockSpec((tm, tn), lambda i, j: (i, j)),
        out_shape=jax.ShapeDtypeStruct((m, n), F32),
        scratch_shapes=[pltpu.VMEM((tm, d), MXU_DTYPE)],
        compiler_params=_params(("parallel", "arbitrary")),
        name="in_proj",
    )(x, nw, w)


def _out_proj_kernel(x_ref, ya_ref, yb_ref, aa_ref, ab_ref, w_ref, o_ref, *, na):
    i = pl.program_id(0)
    half = ya_ref.shape[1]
    o_ref[...] = (x_ref[...] + _mm(_pick(i, na, ya_ref, yb_ref), w_ref[:half, :])
                  + _mm(_pick(i, na, aa_ref, ab_ref), w_ref[half:, :]))


def _out_proj(x, y_pair, att_pair, w, layer, tm=512):
    m, d = x.shape
    half = y_pair[0].shape[1]
    tm = _row_tile(math.gcd(y_pair[0].shape[0], y_pair[1].shape[0]), tm)
    na, spec_a, spec_b = _split_rows(tm, y_pair[0].shape[0], y_pair[1].shape[0])
    return pl.pallas_call(
        functools.partial(_out_proj_kernel, na=na),
        grid=(m // tm,),
        in_specs=[
            pl.BlockSpec((tm, d), lambda i: (i, 0)),
            spec_a(half), spec_b(half), spec_a(half), spec_b(half),
            pl.BlockSpec((2 * half, d), lambda i: (layer, 0)),
        ],
        out_specs=pl.BlockSpec((tm, d), lambda i: (i, 0)),
        out_shape=jax.ShapeDtypeStruct((m, d), F32),
        compiler_params=_params(("arbitrary",)),
        name="out_proj",
    )(x, *y_pair, *att_pair, w)


def _final_norm_kernel(x_ref, nw_ref, oa_ref, ob_ref, *, na):
    i = pl.program_id(0)
    x = x_ref[...]
    ms = jnp.mean(x * x, axis=-1, keepdims=True)
    y = x * lax.rsqrt(ms + EPS) * nw_ref[...]

    @pl.when(i < na)
    def _():
        oa_ref[...] = y

    @pl.when(i >= na)
    def _():
        ob_ref[...] = y


def _final_norm(x, nw, rows_a, rows_b, tm=512):
    d = x.shape[1]
    tm = _row_tile(math.gcd(rows_a, rows_b), tm)
    na, spec_a, spec_b = _split_rows(tm, rows_a, rows_b)
    return pl.pallas_call(
        functools.partial(_final_norm_kernel, na=na),
        grid=((rows_a + rows_b) // tm,),
        in_specs=[pl.BlockSpec((tm, d), lambda i: (i, 0)),
                  pl.BlockSpec((1, d), lambda i: (0, 0))],
        out_specs=[spec_a(d), spec_b(d)],
        out_shape=[jax.ShapeDtypeStruct((rows_a, d), F32),
                   jax.ShapeDtypeStruct((rows_b, d), F32)],
        compiler_params=_params(("arbitrary",)),
        name="final_norm",
    )(x, nw)


def _cast_kernel(x_ref, o_ref):
    o_ref[...] = x_ref[...].astype(o_ref.dtype)


def _cast_weights(w2, n_rows, row0=0, rows=512):
    c = w2.shape[1]
    rows = _row_tile(math.gcd(n_rows, row0) if row0 else n_rows, rows)
    blk0 = row0 // rows
    return pl.pallas_call(
        _cast_kernel,
        grid=(n_rows // rows,),
        in_specs=[pl.BlockSpec((rows, c), lambda i: (blk0 + i, 0))],
        out_specs=pl.BlockSpec((rows, c), lambda i: (i, 0)),
        out_shape=jax.ShapeDtypeStruct((n_rows, c), MXU_DTYPE),
        compiler_params=_params(("parallel",)),
        name="cast_w",
    )(w2)


_W_IN_PARTS = (
    (COL_Z, 0, D_SSM),
    (COL_XBC, D_SSM, CONV_DIM),
    (COL_DT, D_SSM + CONV_DIM, SSM_HEADS),
    (COL_Q, D_SSM + CONV_DIM + SSM_HEADS, N_HEADS * HEAD_DIM),
    (COL_K, D_SSM + CONV_DIM + SSM_HEADS + N_HEADS * HEAD_DIM, N_KV * HEAD_DIM),
    (COL_V, D_SSM + CONV_DIM + SSM_HEADS + (N_HEADS + N_KV) * HEAD_DIM, N_KV * HEAD_DIM),
    (COL_IQ, D_SSM + CONV_DIM + SSM_HEADS + (N_HEADS + 2 * N_KV) * HEAD_DIM, IDX_HEADS * IDX_DIM),
    (COL_IKW, D_SSM + CONV_DIM + SSM_HEADS + (N_HEADS + 2 * N_KV) * HEAD_DIM + IDX_HEADS * IDX_DIM,
     IDX_DIM + IDX_HEADS),
)


def _pack_w_in_kernel(w_ref, o_ref):
    cols = o_ref.shape[1]
    for dst, src, width in _W_IN_PARTS:
        o_ref[dst:dst + width, :] = w_ref[src:src + width, :].astype(o_ref.dtype)
        pad = -width % LANE
        if pad:
            o_ref[dst + width:dst + width + pad, :] = jnp.zeros((pad, cols), o_ref.dtype)


def _pack_w_in(w, kc=256):
    depth, d, c = w.shape
    wt = jnp.transpose(w, (0, 2, 1)).reshape(depth * c, d)
    return pl.pallas_call(
        _pack_w_in_kernel,
        grid=(depth, d // kc),
        in_specs=[pl.BlockSpec((c, kc), lambda l, k: (l, k))],
        out_specs=pl.BlockSpec((PROJ_COLS, kc), lambda l, k: (l, k)),
        out_shape=jax.ShapeDtypeStruct((depth * PROJ_COLS, d), MXU_DTYPE),
        compiler_params=_params(("parallel", "parallel")),
        name="pack_w_in",
    )(wt)


def _rope_lanes(x, c, s1, s2, half):
    outs = []
    for t in range(x.shape[1] // LANE):
        xt = x[:, t * LANE:(t + 1) * LANE]
        outs.append(xt * c + pltpu.roll(xt, half, 1) * s1
                    + pltpu.roll(xt, LANE - half, 1) * s2)
    return outs[0] if len(outs) == 1 else jnp.concatenate(outs, axis=1)


def _prep_kernel(q_ref, iq_ref, k_ref, v_ref, ikw_ref, tab_ref,
                 qo_ref, iqo_ref, ko_ref, kbo_ref, vo_ref, vbo_ref, ikwo_ref):
    tab = [tab_ref[:, i * LANE:(i + 1) * LANE] for i in range(9)]
    qo_ref[...] = _rope_lanes(q_ref[...], tab[0], tab[1], tab[2], ROPE_DIM // 2).astype(qo_ref.dtype)
    k = _rope_lanes(k_ref[...], tab[0], tab[1], tab[2], ROPE_DIM // 2)
    ko_ref[...] = k
    kbo_ref[...] = k.astype(kbo_ref.dtype)
    iqo_ref[...] = _rope_lanes(iq_ref[...], tab[3], tab[4], tab[5], IDX_ROPE_DIM // 2).astype(iqo_ref.dtype)
    ikwo_ref[...] = _rope_lanes(ikw_ref[...], tab[6], tab[7], tab[8], IDX_ROPE_DIM // 2)
    v = v_ref[...]
    vo_ref[...] = v
    vbo_ref[...] = v.astype(vbo_ref.dtype)


def _prep_tile(tp, ms, ts, tm=512):
    tm = _row_tile(math.gcd(tp, ms), tm)
    assert tm % ts == 0
    return tm


def _prep(proj, tab, mp, tp, tm):
    m = proj.shape[0]
    wq = N_HEADS * HEAD_DIM
    wkv = N_KV * HEAD_DIM
    row = lambda w, c: pl.BlockSpec((tm, w), lambda i: (i, c))
    tab_spec = pl.BlockSpec(
        (tm, 9 * LANE), lambda i: (jnp.where(i < mp // tm, i % (tp // tm), tp // tm), 0))
    return pl.pallas_call(
        _prep_kernel,
        grid=(m // tm,),
        in_specs=[row(wq, COL_Q // wq), row(wq, COL_IQ // wq), row(wkv, COL_K // wkv),
                  row(wkv, COL_V // wkv), row(LANE, COL_IKW // LANE), tab_spec],
        out_specs=[row(wq, 0), row(wq, 0), row(wkv, 0), row(wkv, 0), row(wkv, 0), row(wkv, 0),
                   row(LANE, 0)],
        out_shape=[jax.ShapeDtypeStruct((m, wq), MXU_DTYPE),
                   jax.ShapeDtypeStruct((m, wq), MXU_DTYPE),
                   jax.ShapeDtypeStruct((m, wkv), F32),
                   jax.ShapeDtypeStruct((m, wkv), MXU_DTYPE),
                   jax.ShapeDtypeStruct((m, wkv), F32),
                   jax.ShapeDtypeStruct((m, wkv), MXU_DTYPE),
                   jax.ShapeDtypeStruct((m, LANE), F32)],
        compiler_params=_params(("parallel",)),
        name="prep",
    )(proj, proj, proj, proj, proj, tab)


def _rope_table(pos, rot_dim, period, tail=None):
    half = rot_dim // 2
    inv_freq = jnp.float32(ROPE_THETA) ** (-jnp.arange(half, dtype=F32) * 2.0 / rot_dim)
    ang = pos.astype(F32)[:, None] * inv_freq[None, :]
    cos, sin = jnp.cos(ang), jnp.sin(ang)
    n = pos.shape[0]
    ones = jnp.ones((n, period - rot_dim), F32)
    zeros = jnp.zeros((n, period - rot_dim), F32)
    zh = jnp.zeros((n, half), F32)
    c = jnp.concatenate([cos, cos, ones], axis=1)
    s1 = jnp.concatenate([zh, sin, zeros], axis=1)
    s2 = jnp.concatenate([-sin, zh, zeros], axis=1)
    if tail is None:
        reps = LANE // period
        return [jnp.tile(a, (1, reps)) for a in (c, s1, s2)]
    zt = jnp.zeros((n, LANE - period), F32)
    return [jnp.concatenate([c, jnp.broadcast_to(tail[None, :], (n, LANE - period))], axis=1),
            jnp.concatenate([s1, zt], axis=1), jnp.concatenate([s2, zt], axis=1)]


def _ssd_kernel(*refs, qv, qp, has_state):
    if has_state:
        (z_ref, xbc_ref, dt_ref, ssm0_ref, conv0_ref, cw_ref, cb_ref, dtb_ref, aneg_ref,
         dsk_ref, nw_ref, y_ref, ssm_ref, conv_ref, ext_scr, st_scr) = refs
    else:
        (z_ref, xbc_ref, dt_ref, cw_ref, cb_ref, dtb_ref, aneg_ref,
         dsk_ref, nw_ref, y_ref, ssm_ref, conv_ref, ext_scr, st_scr) = refs
    c = pl.program_id(1)
    hp = SSM_HEADS * SSM_HEADDIM
    gw = SSM_STATE
    hpg = hp // SSM_GROUPS

    @pl.when(c == 0)
    def _():
        if has_state:
            ext_scr[0:SUBLANE, :] = conv0_ref[0]
            st_scr[...] = ssm0_ref[0].T
        else:
            ext_scr[0:SUBLANE, :] = jnp.zeros((SUBLANE, CONV_DIM), F32)
            st_scr[...] = jnp.zeros_like(st_scr)

    xbc_raw = xbc_ref[...]
    ext_scr[SUBLANE:SUBLANE + qv, :] = xbc_raw
    conv = cb_ref[...] + xbc_raw * cw_ref[CONV_W - 1:CONV_W, :]
    for j in range(CONV_W - 1):
        off = SUBLANE - (CONV_W - 1) + j
        conv = conv + ext_scr[off:off + qv, :] * cw_ref[j:j + 1, :]
    xc = _silu(conv)
    dtr = dt_ref[...]
    if qv < qp:
        xc = jnp.concatenate([xc, jnp.zeros((qp - qv, CONV_DIM), F32)], axis=0)
        dtr = jnp.concatenate([dtr, jnp.zeros((qp - qv, LANE), F32)], axis=0)
    xs = xc[:, :hp]
    bm = xc[:, hp:hp + SSM_GROUPS * gw]
    cm = xc[:, hp + SSM_GROUPS * gw:]

    xdt_in = dtr + dtb_ref[...]
    dt = jnp.maximum(xdt_in, 0.0) + jnp.log1p(jnp.exp(-jnp.abs(xdt_in)))
    row = lax.broadcasted_iota(jnp.int32, (qp, LANE), 0)
    dt = jnp.where(row < qv, dt, 0.0)
    la = dt * aneg_ref[...]
    ti = lax.broadcasted_iota(jnp.int32, (qp, qp), 0)
    si = lax.broadcasted_iota(jnp.int32, (qp, qp), 1)
    tri = ti >= si
    cum = _mm_exact(tri.astype(F32), la)
    cum_t = cum.T

    eh = lax.broadcasted_iota(jnp.int32, (LANE, hp), 0)
    ec = lax.broadcasted_iota(jnp.int32, (LANE, hp), 1)
    expand = (ec // SSM_HEADDIM == eh).astype(F32)
    dt_x = _mm_exact(dt, expand)
    cum_x = _mm_exact(cum, expand)
    xdt = xs * dt_x

    cb = [_mm_nt(cm[:, g * gw:(g + 1) * gw].astype(MXU_DTYPE),
                 bm[:, g * gw:(g + 1) * gw].astype(MXU_DTYPE)) for g in range(SSM_GROUPS)]
    lane = lax.broadcasted_iota(jnp.int32, (qp, LANE), 1)
    y_parts = []
    for j in range(SSM_HEADS // 2):
        ms = []
        for h in (2 * j, 2 * j + 1):
            d = cum[:, h:h + 1] - cum_t[h:h + 1, :]
            seg = jnp.exp(jnp.where(tri, d, -jnp.inf))
            ms.append((cb[h // (SSM_HEADS // SSM_GROUPS)] * seg).astype(MXU_DTYPE))
        xp = xdt[:, j * LANE:(j + 1) * LANE]
        rhs = jnp.concatenate([jnp.where(lane < SSM_HEADDIM, xp, 0.0),
                               jnp.where(lane >= SSM_HEADDIM, xp, 0.0)], axis=0)
        y_parts.append(_mm(jnp.concatenate(ms, axis=1), rhs.astype(MXU_DTYPE)))
    y = jnp.concatenate(y_parts, axis=1)

    st = st_scr[...]
    y_st = jnp.concatenate(
        [_mm(cm[:, g * gw:(g + 1) * gw].astype(MXU_DTYPE),
             st[:, g * hpg:(g + 1) * hpg].astype(MXU_DTYPE)) for g in range(SSM_GROUPS)], axis=1)
    y = y + y_st * jnp.exp(cum_x) + dsk_ref[...] * xs

    cum_last = cum_x[qp - 1:qp, :]
    xt = (xdt * jnp.exp(cum_last - cum_x)).astype(MXU_DTYPE)
    upd = jnp.concatenate(
        [_mm(bm[:, g * gw:(g + 1) * gw].T.astype(MXU_DTYPE), xt[:, g * hpg:(g + 1) * hpg])
         for g in range(SSM_GROUPS)], axis=1)
    st_new = st * jnp.exp(cum_last) + upd
    st_scr[...] = st_new

    zz = z_ref[...]
    gated = y[:qv, :] * _silu(zz)
    ms2 = jnp.mean(gated * gated, axis=-1, keepdims=True)
    y_ref[...] = (gated * lax.rsqrt(ms2 + EPS) * nw_ref[...]).astype(y_ref.dtype)

    tail_rows = ext_scr[qv:qv + SUBLANE, :]
    ext_scr[0:SUBLANE, :] = tail_rows

    @pl.when(c == pl.num_programs(1) - 1)
    def _():
        ssm_ref[0] = st_new.T
        conv_ref[0] = tail_rows


def _ssd(proj, row0, nb, t, qv, qp, state, consts):
    nc = t // qv
    hp = SSM_HEADS * SSM_HEADDIM
    blk0 = row0 // qv
    rowspec = lambda w, cb_: pl.BlockSpec((qv, w), lambda b, c: (blk0 + b * nc + c, cb_))
    const = lambda a: pl.BlockSpec(a.shape, lambda b, c: (0,) * a.ndim)
    in_specs = [rowspec(hp, COL_Z // hp), rowspec(CONV_DIM, COL_XBC // CONV_DIM),
                rowspec(LANE, COL_DT // LANE)]
    args = [proj, proj, proj]
    if state is not None:
        ssm0, conv0, layer = state
        in_specs += [pl.BlockSpec((1, hp, SSM_STATE), lambda b, c: (layer * nb + b, 0, 0)),
                     pl.BlockSpec((1, SUBLANE, CONV_DIM), lambda b, c: (layer * nb + b, 0, 0))]
        args += [ssm0, conv0]
    in_specs += [const(a) for a in consts]
    args += list(consts)
    return pl.pallas_call(
        functools.partial(_ssd_kernel, qv=qv, qp=qp, has_state=state is not None),
        grid=(nb, nc),
        in_specs=in_specs,
        out_specs=[pl.BlockSpec((qv, hp), lambda b, c: (b * nc + c, 0)),
                   pl.BlockSpec((1, hp, SSM_STATE), lambda b, c: (b, 0, 0)),
                   pl.BlockSpec((1, SUBLANE, CONV_DIM), lambda b, c: (b, 0, 0))],
        out_shape=[jax.ShapeDtypeStruct((nb * t, hp), MXU_DTYPE),
                   jax.ShapeDtypeStruct((nb, hp, SSM_STATE), F32),
                   jax.ShapeDtypeStruct((nb, SUBLANE, CONV_DIM), F32)],
        scratch_shapes=[pltpu.VMEM((qv + SUBLANE, CONV_DIM), F32),
                        pltpu.VMEM((SSM_STATE, hp), F32)],
        compiler_params=_params(("parallel", "arbitrary")),
        name="ssd_state" if state is not None else "ssd_prompt",
    )(*args)


def _row_tiles(n_rows):
    assert n_rows % COUNT_TILE == 0
    return [(r0, r0 + COUNT_TILE) for r0 in range(0, n_rows, COUNT_TILE)]


def _row_iota(r0, r1):
    return lax.broadcasted_iota(jnp.int32, (r1 - r0, LANE), 0) + r0


def _count_rows(fn, n_rows):
    acc = None
    for r0, r1 in _row_tiles(n_rows):
        part = fn(r0, r1).astype(F32)
        acc = part if acc is None else acc + part
    return jnp.sum(acc, axis=0, keepdims=True)


def _select_bias(sc_ref, key_ref, bias_ref, limit, n_keys, topk):
    s_pad = sc_ref.shape[0]
    tiles = _row_tiles(s_pad)
    for r0, r1 in tiles:
        s_io = _row_iota(r0, r1)
        adm = (s_io < limit) & (s_io < n_keys)
        bits = pltpu.bitcast(jnp.where(adm, sc_ref[r0:r1, :], -jnp.inf), jnp.int32)
        key_ref[r0:r1, :] = bits ^ ((bits >> 31) & jnp.int32(0x7FFFFFFF))
    kf = jnp.float32(topk)

    cnt0 = _count_rows(lambda r0, r1: key_ref[r0:r1, :] >= 0, s_pad)
    prefix = jnp.where(cnt0 >= kf, jnp.int32(0), jnp.int32(INT_MIN))

    def body(i, prefix):
        cand = prefix | jnp.left_shift(jnp.int32(1), 30 - i)
        cnt = _count_rows(lambda r0, r1: key_ref[r0:r1, :] >= cand, s_pad)
        return jnp.where(cnt >= kf, cand, prefix)

    thr = lax.fori_loop(0, 31, body, prefix)

    def finite(key):
        return (key > KEY_NEG_INF) & (key < KEY_POS_INF)

    for r0, r1 in tiles:
        key = key_ref[r0:r1, :]
        bias_ref[r0:r1, :] = jnp.where((key >= thr) & finite(key), 0.0, -jnp.inf)

    need = kf - _count_rows(lambda r0, r1: key_ref[r0:r1, :] > thr, s_pad)
    n_eq = _count_rows(lambda r0, r1: key_ref[r0:r1, :] == thr, s_pad)
    tie = (n_eq > need) & (thr > KEY_NEG_INF)
    any_tie = jnp.max(tie.astype(F32)) > 0.0

    @pl.when(any_tie)
    def _():
        nbits = max(1, (s_pad - 1).bit_length())

        def ibody(i, v):
            cand = v | jnp.left_shift(jnp.int32(1), nbits - 1 - i)
            below = _count_rows(
                lambda r0, r1: (key_ref[r0:r1, :] == thr) & (_row_iota(r0, r1) < cand), s_pad)
            return jnp.where(below < need, cand, v)

        last = lax.fori_loop(0, nbits, ibody, jnp.zeros((1, LANE), jnp.int32))
        for r0, r1 in tiles:
            key = key_ref[r0:r1, :]
            sel = (key > thr) | ((key == thr) & (_row_iota(r0, r1) <= last))
            bias_ref[r0:r1, :] = jnp.where(sel & finite(key), 0.0, -jnp.inf)


def _sel_prompt_kernel(iq_ref, ikw_q_ref, ikw_all_ref, bias_out_ref, sc_scr, key_scr, bias_scr,
                       *, topk, tk):
    j = pl.program_id(1)
    t_len = ikw_all_ref.shape[0]
    seg = _causal_seg(t_len)
    pos = j * LANE + lax.broadcasted_iota(jnp.int32, (1, LANE), 1)
    limit = (pos // CHUNK + 1) * CHUNK

    def run(s_eff):
        w_t = ikw_q_ref[...].T
        iq = iq_ref[...]
        for kt in range(s_eff // tk):
            ik = ikw_all_ref[kt * tk:(kt + 1) * tk, 0:IDX_DIM].astype(MXU_DTYPE)
            acc = jnp.zeros((tk, LANE), F32)
            for h in range(IDX_HEADS):
                lt = _mm_nt(ik, iq[:, h * IDX_DIM:(h + 1) * IDX_DIM])
                acc = acc + jnp.maximum(lt, 0.0) * w_t[IDX_DIM + h:IDX_DIM + h + 1, :]
            sc_scr[kt * tk:(kt + 1) * tk, :] = acc
        _select_bias(sc_scr.at[0:s_eff], key_scr.at[0:s_eff], bias_scr.at[0:s_eff],
                     limit, t_len, topk)
        for kt in range(s_eff // LANE):
            bias_out_ref[:, kt * LANE:(kt + 1) * LANE] = (
                bias_scr[kt * LANE:(kt + 1) * LANE, :].T.astype(bias_out_ref.dtype))
        if s_eff < t_len:
            bias_out_ref[:, s_eff:t_len] = jnp.full((LANE, t_len - s_eff), -jnp.inf,
                                                    bias_out_ref.dtype)

    for v in range(t_len // seg):
        pl.when((j * LANE) // seg == v)(functools.partial(run, (v + 1) * seg))


def _causal_seg(t_len):
    return min(4 * LANE, t_len)


def _sel_prompt(iq_rot, ikw_rot, nb, t):
    topk = min(TOPK_MAX, t // 4)
    nq = t // LANE
    wi = IDX_HEADS * IDX_DIM
    return pl.pallas_call(
        functools.partial(_sel_prompt_kernel, topk=topk, tk=min(256, t)),
        grid=(nb, nq),
        in_specs=[pl.BlockSpec((LANE, wi), lambda b, j: (b * nq + j, 0)),
                  pl.BlockSpec((LANE, LANE), lambda b, j: (b * nq + j, 0)),
                  pl.BlockSpec((t, LANE), lambda b, j: (b, 0))],
        out_specs=pl.BlockSpec((LANE, t), lambda b, j: (b * nq + j, 0)),
        out_shape=jax.ShapeDtypeStruct((nb * t, t), MXU_DTYPE),
        scratch_shapes=[pltpu.VMEM((t, LANE), F32), pltpu.VMEM((t, LANE), jnp.int32),
                        pltpu.VMEM((t, LANE), F32)],
        compiler_params=_params(("parallel", "arbitrary")),
        name="sel_prompt",
    )(iq_rot, ikw_rot, ikw_rot)


def _sel_sample_kernel(iqp_ref, w_ref, ikp_ref, ikn_ref, bias_out_ref,
                       sc_scr, key_scr, bias_scr, *, ts, past, topk, n_keys, tk):
    s_pad = sc_scr.shape[0]
    nbp = LANE // ts
    tiles = [(r0, r0 + tk) for r0 in range(0, past, tk)] + [(past, s_pad)]
    for b in range(nbp):
        iqp = iqp_ref[b]
        w = w_ref[b]
        for r0, r1 in tiles:
            if r0 < past:
                ik = ikp_ref[b, :, r0:r1].T
            else:
                ik = jnp.concatenate([ikn_ref[b * ts:(b + 1) * ts, 0:IDX_DIM],
                                      jnp.zeros((r1 - r0 - ts, IDX_DIM), F32)], axis=0)
            lt = _mm_nt(ik.astype(MXU_DTYPE), iqp)
            r = jnp.maximum(lt, 0.0) * w
            y = r[:, 0:LANE]
            for c in range(1, (IDX_HEADS * ts) // LANE):
                y = y + r[:, c * LANE:(c + 1) * LANE]
            sh = LANE // 2
            while sh >= ts:
                y = y + pltpu.roll(y, sh, 1)
                sh //= 2
            if b == 0:
                sc_scr[r0:r1, :] = y
            else:
                lane = lax.broadcasted_iota(jnp.int32, (r1 - r0, LANE), 1)
                sc_scr[r0:r1, :] = jnp.where(lane // ts == b, y, sc_scr[r0:r1, :])
    q = lax.broadcasted_iota(jnp.int32, (1, LANE), 1) % ts
    limit = ((past + q) // CHUNK + 1) * CHUNK
    _select_bias(sc_scr, key_scr, bias_scr, limit, n_keys, topk)
    for kt in range(s_pad // LANE):
        bias_out_ref[:, kt * LANE:(kt + 1) * LANE] = (
            bias_scr[kt * LANE:(kt + 1) * LANE, :].T.astype(bias_out_ref.dtype))


def _sel_sample(iqp, wrow, ik_past, layer, ikw_rot, row0, nb, ts, past, s_pad):
    n_keys = past + ts
    topk = min(TOPK_MAX, n_keys // 4)
    nbp = LANE // ts
    blk0 = row0 // LANE
    ng = nb // nbp
    return pl.pallas_call(
        functools.partial(_sel_sample_kernel, ts=ts, past=past, topk=topk, n_keys=n_keys,
                          tk=math.gcd(past, 1024)),
        grid=(nb // nbp,),
        in_specs=[pl.BlockSpec((nbp, IDX_HEADS * ts, IDX_DIM), lambda g: (g, 0, 0)),
                  pl.BlockSpec((nbp, 1, IDX_HEADS * ts), lambda g: (g, 0, 0)),
                  pl.BlockSpec((nbp, IDX_DIM, past), lambda g: (layer * ng + g, 0, 0)),
                  pl.BlockSpec((LANE, LANE), lambda g: (blk0 + g, 0))],
        out_specs=pl.BlockSpec((LANE, s_pad), lambda g: (g, 0)),
        out_shape=jax.ShapeDtypeStruct((nb * ts, s_pad), MXU_DTYPE),
        scratch_shapes=[pltpu.VMEM((s_pad, LANE), F32),
                        pltpu.VMEM((s_pad, LANE), jnp.int32), pltpu.VMEM((s_pad, LANE), F32)],
        compiler_params=_params(("parallel",)),
        name="sel_sample",
    )(iqp, wrow, ik_past, ikw_rot)


def _attend(q_ref, k_ref, v_ref, bias_ref, o_ref, s_eff, tk):
    nq = q_ref.shape[0]
    scale = HEAD_DIM ** -0.5 * math.log2(math.e)
    rows = KV_GROUP * nq
    for g in range(N_KV):
        cols = slice(g * HEAD_DIM, (g + 1) * HEAD_DIM)
        qg = jnp.concatenate(
            [q_ref[:, (g * KV_GROUP + i) * HEAD_DIM:(g * KV_GROUP + i + 1) * HEAD_DIM]
             for i in range(KV_GROUP)], axis=0)
        m = jnp.full((rows, 1), -jnp.inf, F32)
        l = jnp.zeros((rows, 1), F32)
        o = jnp.zeros((rows, HEAD_DIM), F32)
        for r0 in range(0, s_eff, tk):
            r1 = min(r0 + tk, s_eff)
            b = bias_ref[:, r0:r1].astype(F32)
            s = _mm_nt(qg, k_ref[r0:r1, cols]) * scale + jnp.concatenate([b] * KV_GROUP, axis=0)
            m_new = jnp.maximum(m, jnp.max(s, axis=-1, keepdims=True))
            m_safe = jnp.where(m_new == -jnp.inf, 0.0, m_new)
            alpha = jnp.exp2(m - m_safe)
            p = jnp.exp2(s - m_safe)
            l = alpha * l + jnp.sum(p, axis=-1, keepdims=True)
            o = alpha * o + _mm(p.astype(MXU_DTYPE), v_ref[r0:r1, cols])
            m = m_new
        o = o / l
        for i in range(KV_GROUP):
            h = g * KV_GROUP + i
            o_ref[:, h * HEAD_DIM:(h + 1) * HEAD_DIM] = o[i * nq:(i + 1) * nq, :].astype(o_ref.dtype)


def _attn_prompt_kernel(q_ref, k_ref, v_ref, bias_ref, o_ref):
    j = pl.program_id(1)
    t_len = k_ref.shape[0]
    seg = _causal_seg(t_len)

    def run(s_eff):
        _attend(q_ref, k_ref, v_ref, bias_ref, o_ref, s_eff, ATTN_KEY_TILE)

    for v in range(t_len // seg):
        pl.when((j * LANE) // seg == v)(functools.partial(run, (v + 1) * seg))


def _attn_prompt(q_rot, k_bf, v_bf, bias, nb, t):
    nq = t // LANE
    wq = N_HEADS * HEAD_DIM
    wkv = N_KV * HEAD_DIM
    return pl.pallas_call(
        _attn_prompt_kernel,
        grid=(nb, nq),
        in_specs=[pl.BlockSpec((LANE, wq), lambda b, j: (b * nq + j, 0)),
                  pl.BlockSpec((t, wkv), lambda b, j: (b, 0)),
                  pl.BlockSpec((t, wkv), lambda b, j: (b, 0)),
                  pl.BlockSpec((LANE, t), lambda b, j: (b * nq + j, 0))],
        out_specs=pl.BlockSpec((LANE, wq), lambda b, j: (b * nq + j, 0)),
        out_shape=jax.ShapeDtypeStruct((nb * t, wq), MXU_DTYPE),
        compiler_params=_params(("parallel", "arbitrary")),
        name="attn_prompt",
    )(q_rot, k_bf, v_bf, bias)


def _attn_sample_kernel(q_ref, kp_ref, vp_ref, kn_ref, vn_ref, bias_ref, o_ref, k_scr, v_scr,
                        *, ts, past):
    s_pad = k_scr.shape[0]
    wkv = N_KV * HEAD_DIM
    for g in range(N_KV):
        cols = slice(g * HEAD_DIM, (g + 1) * HEAD_DIM)
        k_scr[0:past, cols] = kp_ref[0, pl.ds(g, past, stride=N_KV), :].astype(MXU_DTYPE)
        v_scr[0:past, cols] = vp_ref[0, pl.ds(g, past, stride=N_KV), :].astype(MXU_DTYPE)
    k_scr[past:past + ts, :] = kn_ref[...]
    v_scr[past:past + ts, :] = vn_ref[...]
    k_scr[past + ts:s_pad, :] = jnp.zeros((s_pad - past - ts, wkv), MXU_DTYPE)
    v_scr[past + ts:s_pad, :] = jnp.zeros((s_pad - past - ts, wkv), MXU_DTYPE)
    _attend(q_ref, k_scr, v_scr, bias_ref, o_ref, s_pad, -(-s_pad // (2 * LANE)) * LANE)


def _attn_sample(q_rot, k_past, v_past, layer, k_bf, v_bf, bias, row0, nb, ts, past, s_pad):
    wq = N_HEADS * HEAD_DIM
    wkv = N_KV * HEAD_DIM
    blk0 = row0 // ts
    return pl.pallas_call(
        functools.partial(_attn_sample_kernel, ts=ts, past=past),
        grid=(nb,),
        in_specs=[pl.BlockSpec((ts, wq), lambda b: (blk0 + b, 0)),
                  pl.BlockSpec((1, past * N_KV, HEAD_DIM), lambda b: (layer * nb + b, 0, 0)),
                  pl.BlockSpec((1, past * N_KV, HEAD_DIM), lambda b: (layer * nb + b, 0, 0)),
                  pl.BlockSpec((ts, wkv), lambda b: (blk0 + b, 0)),
                  pl.BlockSpec((ts, wkv), lambda b: (blk0 + b, 0)),
                  pl.BlockSpec((ts, s_pad), lambda b: (b, 0))],
        out_specs=pl.BlockSpec((ts, wq), lambda b: (b, 0)),
        out_shape=jax.ShapeDtypeStruct((nb * ts, wq), MXU_DTYPE),
        scratch_shapes=[pltpu.VMEM((s_pad, wkv), MXU_DTYPE), pltpu.VMEM((s_pad, wkv), MXU_DTYPE)],
        compiler_params=_params(("parallel",)),
        name="attn_sample",
    )(q_rot, k_past, v_past, k_bf, v_bf, bias)


def kernel(x_prompt, x_sample, cache_k, cache_v, cache_idx_k, state_ssm, state_conv, norm_ffn1, ffn1_w1, ffn1_w3, ffn1_w2, norm_mix, w_in, conv_w, conv_b, dt_bias, a_log, d_skip, ssm_norm_w, w_out, norm_ffn2, ffn2_w1, ffn2_w3, ffn2_w2, final_norm):
    bp, tp, d = x_prompt.shape
    bs, ts = x_sample.shape[:2]
    depth, _, past = cache_k.shape[:3]
    mp, ms = bp * tp, bs * ts
    n_keys_s = past + ts
    s_pad = -(-n_keys_s // LANE) * LANE
    hp = SSM_HEADS * SSM_HEADDIM
    wkv = N_KV * HEAD_DIM

    xs = (x_prompt.reshape(mp, d), x_sample.reshape(ms, d))

    tm_prep = _prep_tile(tp, ms, ts)
    pos = jnp.concatenate([jnp.arange(tp, dtype=jnp.int32),
                           jnp.tile(past + jnp.arange(ts, dtype=jnp.int32), tm_prep // ts)])
    iw_scale = jnp.concatenate([jnp.full((IDX_HEADS,), IDX_HEADS ** -0.5, F32),
                                jnp.ones((LANE - IDX_DIM - IDX_HEADS,), F32)])
    tab = jnp.concatenate(_rope_table(pos, ROPE_DIM, HEAD_DIM)
                          + _rope_table(pos, IDX_ROPE_DIM, IDX_DIM)
                          + _rope_table(pos, IDX_ROPE_DIM, IDX_DIM, tail=iw_scale), axis=1)

    w_in_p = _pack_w_in(w_in)
    stack2d = lambda w: w.reshape(w.shape[0] * w.shape[1], w.shape[2])
    ffn_f32 = [tuple(stack2d(w) for w in ws)
               for ws in ((ffn1_w1, ffn1_w3, ffn1_w2), (ffn2_w1, ffn2_w3, ffn2_w2))]
    dff = ffn1_w1.shape[2]
    w_ffn = (_cast_weights(ffn_f32[0][0], d), _cast_weights(ffn_f32[0][1], d),
             _cast_weights(ffn_f32[0][2], dff))
    w_out_b = _cast_weights(stack2d(w_out), depth * w_out.shape[1])
    k_cache = cache_k.reshape(depth * bs, past * N_KV, HEAD_DIM)
    v_cache = cache_v.reshape(depth * bs, past * N_KV, HEAD_DIM)
    ik_cache = jnp.swapaxes(cache_idx_k, 2, 3).reshape(depth * bs, IDX_DIM, past)
    ssm0 = state_ssm.reshape(depth * bs, hp, SSM_STATE)
    conv0_pad = jnp.pad(state_conv, ((0, 0), (0, 0), (SUBLANE - (CONV_W - 1), 0), (0, 0)))
    conv0_pad = conv0_pad.reshape(depth * bs, SUBLANE, CONV_DIM)
    lane_pad = lambda a: jnp.pad(a, (0, LANE - a.shape[0]))[None, :]

    outs = {k: [] for k in ("kp", "vp", "ikp", "sp", "cp", "ks", "vs", "iks", "ss", "cs")}
    for l in range(depth):
        x, w_ffn = _ffn(xs if l == 0 else (x,), norm_ffn1[l][None, :], w_ffn, (*ffn_f32[1], l))
        proj = _in_proj(x, norm_mix[l][None, :], w_in_p, l)
        q_rot, iq_rot, k_rot, k_bf, v_f, v_bf, ikw_rot = _prep(proj, tab, mp, tp, tm_prep)

        consts = (conv_w[l], conv_b[l][None, :], lane_pad(dt_bias[l]),
                  lane_pad(-jnp.exp(a_log[l])), jnp.repeat(d_skip[l], SSM_HEADDIM)[None, :],
                  ssm_norm_w[l][None, :])
        y_p, ssm_p, conv_p = _ssd(proj, 0, bp, tp, LANE, LANE, None, consts)
        y_s, ssm_s, conv_s = _ssd(proj, mp, bs, ts, ts, LANE, (ssm0, conv0_pad, l), consts)

        bias_p = _sel_prompt(iq_rot, ikw_rot, bp, tp)
        att_p = _attn_prompt(q_rot, k_bf, v_bf, bias_p, bp, tp)

        iqp = iq_rot[mp:].reshape(bs, ts, IDX_HEADS, IDX_DIM).transpose(0, 2, 1, 3)
        iqp = iqp.reshape(bs, IDX_HEADS * ts, IDX_DIM)
        wrow = ikw_rot[mp:, IDX_DIM:IDX_DIM + IDX_HEADS].reshape(bs, ts, IDX_HEADS)
        wrow = wrow.transpose(0, 2, 1).reshape(bs, 1, IDX_HEADS * ts)
        bias_s = _sel_sample(iqp, wrow, ik_cache, l, ikw_rot, mp, bs, ts, past, s_pad)
        att_s = _attn_sample(q_rot, k_cache, v_cache, l, k_bf, v_bf, bias_s,
                             mp, bs, ts, past, s_pad)

        x = _out_proj(x, (y_p, y_s), (att_p, att_s), w_out_b, l)
        x, w_ffn = _ffn((x,), norm_ffn2[l][None, :], w_ffn,
                        (*ffn_f32[0], l + 1) if l + 1 < depth else None)

        outs["kp"].append(k_rot[:mp].reshape(bp, tp, N_KV, HEAD_DIM))
        outs["vp"].append(v_f[:mp].reshape(bp, tp, N_KV, HEAD_DIM))
        outs["ikp"].append(ikw_rot[:mp, :IDX_DIM].reshape(bp, tp, IDX_DIM))
        outs["sp"].append(ssm_p.reshape(bp, SSM_HEADS, SSM_HEADDIM, SSM_STATE))
        outs["cp"].append(conv_p[:, SUBLANE - (CONV_W - 1):, :])
        outs["ks"].append(k_rot[mp:].reshape(bs, ts, N_KV, HEAD_DIM))
        outs["vs"].append(v_f[mp:].reshape(bs, ts, N_KV, HEAD_DIM))
        outs["iks"].append(ikw_rot[mp:, :IDX_DIM].reshape(bs, ts, IDX_DIM))
        outs["ss"].append(ssm_s.reshape(bs, SSM_HEADS, SSM_HEADDIM, SSM_STATE))
        outs["cs"].append(conv_s[:, SUBLANE - (CONV_W - 1):, :])

    y_p, y_s = _final_norm(x, final_norm[None, :], mp, ms)
    st = {k: jnp.stack(v) for k, v in outs.items()}
    return (y_p.reshape(bp, tp, d), y_s.reshape(bs, ts, d),
            st["kp"], st["vp"], st["ikp"], st["sp"], st["cp"],
            st["ks"], st["vs"], st["iks"], st["ss"], st["cs"])
```

```python
import functools
import math

import jax
import jax.numpy as jnp
from jax import lax
from jax.experimental import pallas as pl
from jax.experimental.pallas import tpu as pltpu

F32 = jnp.float32
MXU_DTYPE = jnp.bfloat16

D_MODEL = 2048
CHUNK = 64
D_SSM = 1024
SSM_HEADDIM = 64
SSM_HEADS = 16
SSM_GROUPS = 2
SSM_STATE = 128
CONV_W = 4
CONV_DIM = D_SSM + 2 * SSM_GROUPS * SSM_STATE
HEAD_DIM = 128
N_HEADS = 8
N_KV = 2
KV_GROUP = N_HEADS // N_KV
ROPE_DIM = HEAD_DIM // 4
IDX_HEADS = 16
IDX_DIM = 64
IDX_ROPE_DIM = IDX_DIM // 4
TOPK_MAX = 256
ROPE_THETA = 500000.0
D_FF = 5632
EPS = 1e-6

LANE = 128
SUBLANE = 8
COUNT_TILE = 128
ATTN_KEY_TILE = 512
VMEM_LIMIT = 56 * 1024 * 1024

COL_Z = 0
COL_Q = 1024
COL_IQ = 2048
COL_XBC = 3072
COL_K = 4608
COL_V = 4864
COL_DT = 5120
COL_IKW = 5248
PROJ_COLS = 5376

KEY_NEG_INF = -2139095041
KEY_POS_INF = 2139095040
INT_MIN = -2147483648

NT_DIMS = (((1,), (1,)), ((), ()))


def _mm(a, b):
    return jnp.dot(a, b, preferred_element_type=F32)


def _mm_nt(a, b):
    return lax.dot_general(a, b, NT_DIMS, preferred_element_type=F32)


def _mm_exact(a, b):
    return jnp.dot(a, b, preferred_element_type=F32, precision=lax.Precision.HIGHEST)


def _silu(x):
    return x * jax.nn.sigmoid(x)


def _params(sem, vmem=VMEM_LIMIT):
    return pltpu.CompilerParams(dimension_semantics=sem, vmem_limit_bytes=vmem)


def _row_tile(m, pref):
    t = pref
    while m % t:
        t //= 2
    return t


def _split_rows(tm, rows_a, rows_b):
    assert rows_a % tm == 0 and rows_b % tm == 0
    na = rows_a // tm

    def spec_a(width, *grid_rest):
        return pl.BlockSpec((tm, width), lambda i, *_: (jnp.minimum(i, na - 1), 0))

    def spec_b(width, *grid_rest):
        return pl.BlockSpec((tm, width), lambda i, *_: (jnp.maximum(i - na, 0), 0))

    return na, spec_a, spec_b


def _pick(i, na, a_ref, b_ref):
    return jnp.where(i < na, a_ref[...], b_ref[...])


def _ffn_kernel(*refs, na, side):
    refs = list(refs)
    if na is None:
        x_ref = refs.pop(0)
        load_x = lambda: x_ref[...]
    else:
        xa_ref, xb_ref = refs.pop(0), refs.pop(0)
        load_x = lambda: _pick(pl.program_id(0), na, xa_ref, xb_ref)
    nw_ref, w1_ref, w3_ref, w2_ref = refs[:4]
    refs = refs[4:]
    if side:
        s1_ref, s3_ref, s2_ref, o_ref, c1_ref, c3_ref, c2_ref, h_scr = refs
    else:
        o_ref, h_scr = refs
    f = pl.program_id(1)

    @pl.when(f == 0)
    def _():
        x = load_x()
        ms = jnp.mean(x * x, axis=-1, keepdims=True)
        h_scr[...] = (x * lax.rsqrt(ms + EPS) * nw_ref[...]).astype(h_scr.dtype)
        o_ref[...] = jnp.zeros_like(o_ref)

    h = h_scr[...]
    a = _mm(h, w1_ref[...])
    b = _mm(h, w3_ref[...])
    g = (_silu(a) * b).astype(h_scr.dtype)
    o_ref[...] += _mm(g, w2_ref[...])

    if side:
        c1_ref[...] = s1_ref[...].astype(c1_ref.dtype)
        c3_ref[...] = s3_ref[...].astype(c3_ref.dtype)
        c2_ref[...] = s2_ref[...].astype(c2_ref.dtype)

    @pl.when(f == pl.num_programs(1) - 1)
    def _():
        o_ref[...] = load_x() + 0.5 * o_ref[...]


def _ffn_row_tile(rows):
    g = math.gcd(*rows)
    for tm in (768, 512, 256, 128):
        if g % tm == 0:
            return tm
    raise ValueError(rows)


def _chunk_rows(total, steps):
    tile = 2 * SUBLANE
    for r in range(tile, total + 1, tile):
        if total % r == 0 and total // r <= steps:
            return r
    raise ValueError((total, steps))


def _ffn(xs, nw, w, nxt=None, tf=512):
    w1, w3, w2 = w
    d = xs[0].shape[1]
    m = sum(x.shape[0] for x in xs)
    dff = w1.shape[1]
    nf = dff // tf
    tm = _ffn_row_tile([x.shape[0] for x in xs])
    if len(xs) == 1:
        na, x_specs = None, [pl.BlockSpec((tm, d), lambda i, f: (i, 0))]
    else:
        na, spec_a, spec_b = _split_rows(tm, xs[0].shape[0], xs[1].shape[0])
        x_specs = [spec_a(d), spec_b(d)]
    in_specs = x_specs + [
        pl.BlockSpec((1, d), lambda i, f: (0, 0)),
        pl.BlockSpec((d, tf), lambda i, f: (0, f)),
        pl.BlockSpec((d, tf), lambda i, f: (0, f)),
        pl.BlockSpec((tf, d), lambda i, f: (f, 0)),
    ]
    out_specs = [pl.BlockSpec((tm, d), lambda i, f: (i, 0))]
    out_shape = [jax.ShapeDtypeStruct((m, d), F32)]
    args = [*xs, nw, w1, w3, w2]
    if nxt is not None:
        s1, s3, s2, layer = nxt
        steps = (m // tm) * nf
        r1, r2 = _chunk_rows(d, steps), _chunk_rows(dff, steps)
        n1, n2 = d // r1, dff // r2

        def chunk(n, base):
            return lambda i, f: (base + jnp.minimum(i * nf + f, n - 1), 0)

        in_specs += [pl.BlockSpec((r1, dff), chunk(n1, layer * n1)),
                     pl.BlockSpec((r1, dff), chunk(n1, layer * n1)),
                     pl.BlockSpec((r2, d), chunk(n2, layer * n2))]
        out_specs += [pl.BlockSpec((r1, dff), chunk(n1, 0)), pl.BlockSpec((r1, dff), chunk(n1, 0)),
                      pl.BlockSpec((r2, d), chunk(n2, 0))]
        out_shape += [jax.ShapeDtypeStruct((d, dff), MXU_DTYPE),
                      jax.ShapeDtypeStruct((d, dff), MXU_DTYPE),
                      jax.ShapeDtypeStruct((dff, d), MXU_DTYPE)]
        args += [s1, s3, s2]
    res = pl.pallas_call(
        functools.partial(_ffn_kernel, na=na, side=nxt is not None),
        grid=(m // tm, nf),
        in_specs=in_specs,
        out_specs=out_specs,
        out_shape=out_shape,
        scratch_shapes=[pltpu.VMEM((tm, d), MXU_DTYPE)],
        compiler_params=_params(("arbitrary", "arbitrary")),
        name="ffn",
    )(*args)
    return res[0], tuple(res[1:])


def _in_proj_kernel(x_ref, nw_ref, w_ref, o_ref, h_scr):
    @pl.when(pl.program_id(1) == 0)
    def _():
        x = x_ref[...]
        ms = jnp.mean(x * x, axis=-1, keepdims=True)
        h_scr[...] = (x * lax.rsqrt(ms + EPS) * nw_ref[...]).astype(h_scr.dtype)

    o_ref[...] = _mm_nt(h_scr[...], w_ref[...])


def _in_proj(x, nw, w, layer, tm=1024, tn=768):
    m, d = x.shape
    n = PROJ_COLS
    nn = n // tn
    tm = _row_tile(m, tm)
    return pl.pallas_call(
        _in_proj_kernel,
        grid=(m // tm, nn),
        in_specs=[
            pl.BlockSpec((tm, d), lambda i, j: (i, 0)),
            pl.BlockSpec((1, d), lambda i, j: (0, 0)),
            pl.BlockSpec((tn, d), lambda i, j: (layer * nn + j, 0)),
        ],
        out_specs=pl.BlockSpec((tm, tn), lambda i, j: (i, j)),
        out_shape=jax.ShapeDtypeStruct((m, n), F32),
        scratch_shapes=[pltpu.VMEM((tm, d), MXU_DTYPE)],
        compiler_params=_params(("parallel", "arbitrary")),
        name="in_proj",
    )(x, nw, w)


def _out_proj_kernel(x_ref, ya_ref, yb_ref, aa_ref, ab_ref, w_ref, o_ref, *, na):
    i = pl.program_id(0)
    half = ya_ref.shape[1]
    o_ref[...] = (x_ref[...] + _mm(_pick(i, na, ya_ref, yb_ref), w_ref[:half, :])
                  + _mm(_pick(i, na, aa_ref, ab_ref), w_ref[half:, :]))


def _out_proj(x, y_pair, att_pair, w, layer, tm=512):
    m, d = x.shape
    half = y_pair[0].shape[1]
    tm = _row_tile(math.gcd(y_pair[0].shape[0], y_pair[1].shape[0]), tm)
    na, spec_a, spec_b = _split_rows(tm, y_pair[0].shape[0], y_pair[1].shape[0])
    return pl.pallas_call(
        functools.partial(_out_proj_kernel, na=na),
        grid=(m // tm,),
        in_specs=[
            pl.BlockSpec((tm, d), lambda i: (i, 0)),
            spec_a(half), spec_b(half), spec_a(half), spec_b(half),
            pl.BlockSpec((2 * half, d), lambda i: (layer, 0)),
        ],
        out_specs=pl.BlockSpec((tm, d), lambda i: (i, 0)),
        out_shape=jax.ShapeDtypeStruct((m, d), F32),
        compiler_params=_params(("arbitrary",)),
        name="out_proj",
    )(x, *y_pair, *att_pair, w)


def _final_norm_kernel(x_ref, nw_ref, oa_ref, ob_ref, *, na):
    i = pl.program_id(0)
    x = x_ref[...]
    ms = jnp.mean(x * x, axis=-1, keepdims=True)
    y = x * lax.rsqrt(ms + EPS) * nw_ref[...]

    @pl.when(i < na)
    def _():
        oa_ref[...] = y

    @pl.when(i >= na)
    def _():
        ob_ref[...] = y


def _final_norm(x, nw, rows_a, rows_b, tm=512):
    d = x.shape[1]
    tm = _row_tile(math.gcd(rows_a, rows_b), tm)
    na, spec_a, spec_b = _split_rows(tm, rows_a, rows_b)
    return pl.pallas_call(
        functools.partial(_final_norm_kernel, na=na),
        grid=((rows_a + rows_b) // tm,),
        in_specs=[pl.BlockSpec((tm, d), lambda i: (i, 0)),
                  pl.BlockSpec((1, d), lambda i: (0, 0))],
        out_specs=[spec_a(d), spec_b(d)],
        out_shape=[jax.ShapeDtypeStruct((rows_a, d), F32),
                   jax.ShapeDtypeStruct((rows_b, d), F32)],
        compiler_params=_params(("arbitrary",)),
        name="final_norm",
    )(x, nw)


def _cast_kernel(x_ref, o_ref):
    o_ref[...] = x_ref[...].astype(o_ref.dtype)


def _cast_weights(w2, n_rows, row0=0, rows=512):
    c = w2.shape[1]
    rows = _row_tile(math.gcd(n_rows, row0) if row0 else n_rows, rows)
    blk0 = row0 // rows
    return pl.pallas_call(
        _cast_kernel,
        grid=(n_rows // rows,),
        in_specs=[pl.BlockSpec((rows, c), lambda i: (blk0 + i, 0))],
        out_specs=pl.BlockSpec((rows, c), lambda i: (i, 0)),
        out_shape=jax.ShapeDtypeStruct((n_rows, c), MXU_DTYPE),
        compiler_params=_params(("parallel",)),
        name="cast_w",
    )(w2)


_W_IN_PARTS = (
    (COL_Z, 0, D_SSM),
    (COL_XBC, D_SSM, CONV_DIM),
    (COL_DT, D_SSM + CONV_DIM, SSM_HEADS),
    (COL_Q, D_SSM + CONV_DIM + SSM_HEADS, N_HEADS * HEAD_DIM),
    (COL_K, D_SSM + CONV_DIM + SSM_HEADS + N_HEADS * HEAD_DIM, N_KV * HEAD_DIM),
    (COL_V, D_SSM + CONV_DIM + SSM_HEADS + (N_HEADS + N_KV) * HEAD_DIM, N_KV * HEAD_DIM),
    (COL_IQ, D_SSM + CONV_DIM + SSM_HEADS + (N_HEADS + 2 * N_KV) * HEAD_DIM, IDX_HEADS * IDX_DIM),
    (COL_IKW, D_SSM + CONV_DIM + SSM_HEADS + (N_HEADS + 2 * N_KV) * HEAD_DIM + IDX_HEADS * IDX_DIM,
     IDX_DIM + IDX_HEADS),
)


def _pack_w_in_kernel(w_ref, o_ref):
    cols = o_ref.shape[1]
    for dst, src, width in _W_IN_PARTS:
        o_ref[dst:dst + width, :] = w_ref[src:src + width, :].astype(o_ref.dtype)
        pad = -width % LANE
        if pad:
            o_ref[dst + width:dst + width + pad, :] = jnp.zeros((pad, cols), o_ref.dtype)


def _pack_w_in(w, kc=256):
    depth, d, c = w.shape
    wt = jnp.transpose(w, (0, 2, 1)).reshape(depth * c, d)
    return pl.pallas_call(
        _pack_w_in_kernel,
        grid=(depth, d // kc),
        in_specs=[pl.BlockSpec((c, kc), lambda l, k: (l, k))],
        out_specs=pl.BlockSpec((PROJ_COLS, kc), lambda l, k: (l, k)),
        out_shape=jax.ShapeDtypeStruct((depth * PROJ_COLS, d), MXU_DTYPE),
        compiler_params=_params(("parallel", "parallel")),
        name="pack_w_in",
    )(wt)


def _rope_lanes(x, c, s1, s2, half):
    outs = []
    for t in range(x.shape[1] // LANE):
        xt = x[:, t * LANE:(t + 1) * LANE]
        outs.append(xt * c + pltpu.roll(xt, half, 1) * s1
                    + pltpu.roll(xt, LANE - half, 1) * s2)
    return outs[0] if len(outs) == 1 else jnp.concatenate(outs, axis=1)


def _prep_kernel(*refs, na, aliased):
    q_ref, iq_ref, k_ref, v_ref, ikw_ref, tab_ref = refs[:6]
    refs = refs[6 + (4 if aliased else 0):]
    (qo_ref, iqo_ref, kbo_ref, vbo_ref, ikwo_ref, kp_ref, ks_ref, vp_ref, vs_ref) = refs
    i = pl.program_id(0)
    tm = k_ref.shape[0]
    tab = [tab_ref[:, j * LANE:(j + 1) * LANE] for j in range(9)]
    qo_ref[...] = _rope_lanes(q_ref[...], tab[0], tab[1], tab[2], ROPE_DIM // 2).astype(qo_ref.dtype)
    k = _rope_lanes(k_ref[...], tab[0], tab[1], tab[2], ROPE_DIM // 2)
    kbo_ref[...] = k.astype(kbo_ref.dtype)
    iqo_ref[...] = _rope_lanes(iq_ref[...], tab[3], tab[4], tab[5], IDX_ROPE_DIM // 2).astype(iqo_ref.dtype)
    ikwo_ref[...] = _rope_lanes(ikw_ref[...], tab[6], tab[7], tab[8], IDX_ROPE_DIM // 2)
    v = v_ref[...]
    vbo_ref[...] = v.astype(vbo_ref.dtype)

    def put(dst_k, dst_v):
        for g in range(N_KV):
            dst_k[pl.ds(g, tm, stride=N_KV), :] = k[:, g * HEAD_DIM:(g + 1) * HEAD_DIM]
            dst_v[pl.ds(g, tm, stride=N_KV), :] = v[:, g * HEAD_DIM:(g + 1) * HEAD_DIM]

    pl.when(i < na)(functools.partial(put, kp_ref, vp_ref))
    pl.when(i >= na)(functools.partial(put, ks_ref, vs_ref))


def _prep_tile(tp, ms, ts, tm=512):
    tm = _row_tile(math.gcd(tp, ms), tm)
    assert tm % ts == 0
    return tm


def _prep(proj, tab, mp, tp, tm, layer, depth, caches):
    m = proj.shape[0]
    ms = m - mp
    wq = N_HEADS * HEAD_DIM
    wkv = N_KV * HEAD_DIM
    na = mp // tm
    row = lambda w, c: pl.BlockSpec((tm, w), lambda i: (i, c))
    tab_spec = pl.BlockSpec(
        (tm, 9 * LANE), lambda i: (jnp.where(i < na, i % (tp // tm), tp // tm), 0))
    p_spec = pl.BlockSpec((N_KV * tm, HEAD_DIM), lambda i: (layer * na + jnp.minimum(i, na - 1), 0))
    s_spec = pl.BlockSpec((N_KV * tm, HEAD_DIM),
                          lambda i: (layer * (ms // tm) + jnp.maximum(i - na, 0), 0))
    p_shape = jax.ShapeDtypeStruct((depth * mp * N_KV, HEAD_DIM), F32)
    s_shape = jax.ShapeDtypeStruct((depth * ms * N_KV, HEAD_DIM), F32)
    in_specs = [row(wq, COL_Q // wq), row(wq, COL_IQ // wq), row(wkv, COL_K // wkv),
                row(wkv, COL_V // wkv), row(LANE, COL_IKW // LANE), tab_spec]
    args = [proj, proj, proj, proj, proj, tab]
    aliases = {}
    if caches is not None:
        in_specs += [pl.BlockSpec(memory_space=pl.ANY)] * 4
        aliases = {len(args) + j: 5 + j for j in range(4)}
        args += list(caches)
    res = pl.pallas_call(
        functools.partial(_prep_kernel, na=na, aliased=caches is not None),
        grid=(m // tm,),
        in_specs=in_specs,
        out_specs=[row(wq, 0), row(wq, 0), row(wkv, 0), row(wkv, 0), row(LANE, 0),
                   p_spec, s_spec, p_spec, s_spec],
        out_shape=[jax.ShapeDtypeStruct((m, wq), MXU_DTYPE),
                   jax.ShapeDtypeStruct((m, wq), MXU_DTYPE),
                   jax.ShapeDtypeStruct((m, wkv), MXU_DTYPE),
                   jax.ShapeDtypeStruct((m, wkv), MXU_DTYPE),
                   jax.ShapeDtypeStruct((m, LANE), F32),
                   p_shape, s_shape, p_shape, s_shape],
        input_output_aliases=aliases,
        compiler_params=_params(("arbitrary",)),
        name="prep",
    )(*args)
    return res[:5], tuple(res[5:])


def _rope_table(pos, rot_dim, period, tail=None):
    half = rot_dim // 2
    inv_freq = jnp.float32(ROPE_THETA) ** (-jnp.arange(half, dtype=F32) * 2.0 / rot_dim)
    ang = pos.astype(F32)[:, None] * inv_freq[None, :]
    cos, sin = jnp.cos(ang), jnp.sin(ang)
    n = pos.shape[0]
    ones = jnp.ones((n, period - rot_dim), F32)
    zeros = jnp.zeros((n, period - rot_dim), F32)
    zh = jnp.zeros((n, half), F32)
    c = jnp.concatenate([cos, cos, ones], axis=1)
    s1 = jnp.concatenate([zh, sin, zeros], axis=1)
    s2 = jnp.concatenate([-sin, zh, zeros], axis=1)
    if tail is None:
        reps = LANE // period
        return [jnp.tile(a, (1, reps)) for a in (c, s1, s2)]
    zt = jnp.zeros((n, LANE - period), F32)
    return [jnp.concatenate([c, jnp.broadcast_to(tail[None, :], (n, LANE - period))], axis=1),
            jnp.concatenate([s1, zt], axis=1), jnp.concatenate([s2, zt], axis=1)]


def _ssd_kernel(*refs, qv, qp, has_state, aliased):
    refs = list(refs)
    z_ref, xbc_ref, dt_ref = refs[:3]
    refs = refs[3:]
    if has_state:
        ssm0_ref, conv0_ref = refs[:2]
        refs = refs[2:]
    cw_ref, cb_ref, dtb_ref, aneg_ref, dsk_ref, nw_ref = refs[:6]
    refs = refs[6 + (1 if aliased else 0):]
    y_ref, ssm_ref, conv_ref, ext_scr, st_scr = refs
    c = pl.program_id(1)
    hp = SSM_HEADS * SSM_HEADDIM
    gw = SSM_STATE
    hpg = hp // SSM_GROUPS

    @pl.when(c == 0)
    def _():
        if has_state:
            ext_scr[0:SUBLANE, :] = conv0_ref[0]
            st_scr[...] = ssm0_ref[0].T
        else:
            ext_scr[0:SUBLANE, :] = jnp.zeros((SUBLANE, CONV_DIM), F32)
            st_scr[...] = jnp.zeros_like(st_scr)

    xbc_raw = xbc_ref[...]
    ext_scr[SUBLANE:SUBLANE + qv, :] = xbc_raw
    conv = cb_ref[...] + xbc_raw * cw_ref[CONV_W - 1:CONV_W, :]
    for j in range(CONV_W - 1):
        off = SUBLANE - (CONV_W - 1) + j
        conv = conv + ext_scr[off:off + qv, :] * cw_ref[j:j + 1, :]
    xc = _silu(conv)
    dtr = dt_ref[...]
    if qv < qp:
        xc = jnp.concatenate([xc, jnp.zeros((qp - qv, CONV_DIM), F32)], axis=0)
        dtr = jnp.concatenate([dtr, jnp.zeros((qp - qv, LANE), F32)], axis=0)
    xs = xc[:, :hp]
    bm = xc[:, hp:hp + SSM_GROUPS * gw]
    cm = xc[:, hp + SSM_GROUPS * gw:]

    xdt_in = dtr + dtb_ref[...]
    dt = jnp.maximum(xdt_in, 0.0) + jnp.log1p(jnp.exp(-jnp.abs(xdt_in)))
    row = lax.broadcasted_iota(jnp.int32, (qp, LANE), 0)
    dt = jnp.where(row < qv, dt, 0.0)
    la = dt * aneg_ref[...]
    ti = lax.broadcasted_iota(jnp.int32, (qp, qp), 0)
    si = lax.broadcasted_iota(jnp.int32, (qp, qp), 1)
    tri = ti >= si
    cum = _mm_exact(tri.astype(F32), la)
    cum_t = cum.T

    eh = lax.broadcasted_iota(jnp.int32, (LANE, hp), 0)
    ec = lax.broadcasted_iota(jnp.int32, (LANE, hp), 1)
    expand = (ec // SSM_HEADDIM == eh).astype(F32)
    dt_x = _mm_exact(dt, expand)
    cum_x = _mm_exact(cum, expand)
    xdt = xs * dt_x

    cb = [_mm_nt(cm[:, g * gw:(g + 1) * gw].astype(MXU_DTYPE),
                 bm[:, g * gw:(g + 1) * gw].astype(MXU_DTYPE)) for g in range(SSM_GROUPS)]
    lane = lax.broadcasted_iota(jnp.int32, (qp, LANE), 1)
    y_parts = []
    for j in range(SSM_HEADS // 2):
        ms = []
        for h in (2 * j, 2 * j + 1):
            d = cum[:, h:h + 1] - cum_t[h:h + 1, :]
            seg = jnp.exp(jnp.where(tri, d, -jnp.inf))
            ms.append((cb[h // (SSM_HEADS // SSM_GROUPS)] * seg).astype(MXU_DTYPE))
        xp = xdt[:, j * LANE:(j + 1) * LANE]
        rhs = jnp.concatenate([jnp.where(lane < SSM_HEADDIM, xp, 0.0),
                               jnp.where(lane >= SSM_HEADDIM, xp, 0.0)], axis=0)
        y_parts.append(_mm(jnp.concatenate(ms, axis=1), rhs.astype(MXU_DTYPE)))
    y = jnp.concatenate(y_parts, axis=1)

    st = st_scr[...]
    y_st = jnp.concatenate(
        [_mm(cm[:, g * gw:(g + 1) * gw].astype(MXU_DTYPE),
             st[:, g * hpg:(g + 1) * hpg].astype(MXU_DTYPE)) for g in range(SSM_GROUPS)], axis=1)
    y = y + y_st * jnp.exp(cum_x) + dsk_ref[...] * xs

    cum_last = cum_x[qp - 1:qp, :]
    xt = (xdt * jnp.exp(cum_last - cum_x)).astype(MXU_DTYPE)
    upd = jnp.concatenate(
        [_mm(bm[:, g * gw:(g + 1) * gw].T.astype(MXU_DTYPE), xt[:, g * hpg:(g + 1) * hpg])
         for g in range(SSM_GROUPS)], axis=1)
    st_new = st * jnp.exp(cum_last) + upd
    st_scr[...] = st_new

    zz = z_ref[...]
    gated = y[:qv, :] * _silu(zz)
    ms2 = jnp.mean(gated * gated, axis=-1, keepdims=True)
    y_ref[...] = (gated * lax.rsqrt(ms2 + EPS) * nw_ref[...]).astype(y_ref.dtype)

    tail_rows = ext_scr[qv:qv + SUBLANE, :]
    ext_scr[0:SUBLANE, :] = tail_rows

    @pl.when(c == pl.num_programs(1) - 1)
    def _():
        ssm_ref[0] = st_new.T
        conv_ref[0] = tail_rows


def _ssd(proj, row0, nb, t, qv, qp, state, consts, layer, depth, ssm_buf):
    nc = t // qv
    hp = SSM_HEADS * SSM_HEADDIM
    blk0 = row0 // qv
    rowspec = lambda w, cb_: pl.BlockSpec((qv, w), lambda b, c: (blk0 + b * nc + c, cb_))
    const = lambda a: pl.BlockSpec(a.shape, lambda b, c: (0,) * a.ndim)
    in_specs = [rowspec(hp, COL_Z // hp), rowspec(CONV_DIM, COL_XBC // CONV_DIM),
                rowspec(LANE, COL_DT // LANE)]
    args = [proj, proj, proj]
    if state is not None:
        ssm0, conv0 = state
        in_specs += [pl.BlockSpec((1, hp, SSM_STATE), lambda b, c: (layer * nb + b, 0, 0)),
                     pl.BlockSpec((1, SUBLANE, CONV_DIM), lambda b, c: (layer * nb + b, 0, 0))]
        args += [ssm0, conv0]
    in_specs += [const(a) for a in consts]
    args += list(consts)
    aliases = {}
    if ssm_buf is not None:
        in_specs.append(pl.BlockSpec(memory_space=pl.ANY))
        aliases = {len(args): 1}
        args.append(ssm_buf)
    return pl.pallas_call(
        functools.partial(_ssd_kernel, qv=qv, qp=qp, has_state=state is not None,
                          aliased=ssm_buf is not None),
        grid=(nb, nc),
        in_specs=in_specs,
        out_specs=[pl.BlockSpec((qv, hp), lambda b, c: (b * nc + c, 0)),
                   pl.BlockSpec((1, hp, SSM_STATE), lambda b, c: (layer * nb + b, 0, 0)),
                   pl.BlockSpec((1, SUBLANE, CONV_DIM), lambda b, c: (b, 0, 0))],
        out_shape=[jax.ShapeDtypeStruct((nb * t, hp), MXU_DTYPE),
                   jax.ShapeDtypeStruct((depth * nb, hp, SSM_STATE), F32),
                   jax.ShapeDtypeStruct((nb, SUBLANE, CONV_DIM), F32)],
        scratch_shapes=[pltpu.VMEM((qv + SUBLANE, CONV_DIM), F32),
                        pltpu.VMEM((SSM_STATE, hp), F32)],
        input_output_aliases=aliases,
        compiler_params=_params(("arbitrary", "arbitrary")),
        name="ssd_state" if state is not None else "ssd_prompt",
    )(*args)


def _row_tiles(n_rows):
    assert n_rows % COUNT_TILE == 0
    return [(r0, r0 + COUNT_TILE) for r0 in range(0, n_rows, COUNT_TILE)]


def _row_iota(r0, r1):
    return lax.broadcasted_iota(jnp.int32, (r1 - r0, LANE), 0) + r0


def _count_rows(fn, n_rows):
    acc = None
    for r0, r1 in _row_tiles(n_rows):
        part = fn(r0, r1).astype(F32)
        acc = part if acc is None else acc + part
    return jnp.sum(acc, axis=0, keepdims=True)


def _select_bias(sc_ref, key_ref, bias_ref, limit, n_keys, topk):
    s_pad = sc_ref.shape[0]
    tiles = _row_tiles(s_pad)
    for r0, r1 in tiles:
        s_io = _row_iota(r0, r1)
        adm = (s_io < limit) & (s_io < n_keys)
        bits = pltpu.bitcast(jnp.where(adm, sc_ref[r0:r1, :], -jnp.inf), jnp.int32)
        key_ref[r0:r1, :] = bits ^ ((bits >> 31) & jnp.int32(0x7FFFFFFF))
    kf = jnp.float32(topk)

    cnt0 = _count_rows(lambda r0, r1: key_ref[r0:r1, :] >= 0, s_pad)
    prefix = jnp.where(cnt0 >= kf, jnp.int32(0), jnp.int32(INT_MIN))

    def body(i, prefix):
        cand = prefix | jnp.left_shift(jnp.int32(1), 30 - i)
        cnt = _count_rows(lambda r0, r1: key_ref[r0:r1, :] >= cand, s_pad)
        return jnp.where(cnt >= kf, cand, prefix)

    thr = lax.fori_loop(0, 31, body, prefix)

    def finite(key):
        return (key > KEY_NEG_INF) & (key < KEY_POS_INF)

    for r0, r1 in tiles:
        key = key_ref[r0:r1, :]
        bias_ref[r0:r1, :] = jnp.where((key >= thr) & finite(key), 0.0, -jnp.inf)

    need = kf - _count_rows(lambda r0, r1: key_ref[r0:r1, :] > thr, s_pad)
    n_eq = _count_rows(lambda r0, r1: key_ref[r0:r1, :] == thr, s_pad)
    tie = (n_eq > need) & (thr > KEY_NEG_INF)
    any_tie = jnp.max(tie.astype(F32)) > 0.0

    @pl.when(any_tie)
    def _():
        nbits = max(1, (s_pad - 1).bit_length())

        def ibody(i, v):
            cand = v | jnp.left_shift(jnp.int32(1), nbits - 1 - i)
            below = _count_rows(
                lambda r0, r1: (key_ref[r0:r1, :] == thr) & (_row_iota(r0, r1) < cand), s_pad)
            return jnp.where(below < need, cand, v)

        last = lax.fori_loop(0, nbits, ibody, jnp.zeros((1, LANE), jnp.int32))
        for r0, r1 in tiles:
            key = key_ref[r0:r1, :]
            sel = (key > thr) | ((key == thr) & (_row_iota(r0, r1) <= last))
            bias_ref[r0:r1, :] = jnp.where(sel & finite(key), 0.0, -jnp.inf)


def _sel_prompt_kernel(iq_ref, ikw_q_ref, ikw_all_ref, bias_out_ref, sc_scr, key_scr, bias_scr,
                       *, topk, tk):
    j = pl.program_id(1)
    t_len = ikw_all_ref.shape[0]
    seg = _causal_seg(t_len)
    pos = j * LANE + lax.broadcasted_iota(jnp.int32, (1, LANE), 1)
    limit = (pos // CHUNK + 1) * CHUNK

    def run(s_eff):
        w_t = ikw_q_ref[...].T
        iq = iq_ref[...]
        for kt in range(s_eff // tk):
            ik = ikw_all_ref[kt * tk:(kt + 1) * tk, 0:IDX_DIM].astype(MXU_DTYPE)
            acc = jnp.zeros((tk, LANE), F32)
            for h in range(IDX_HEADS):
                lt = _mm_nt(ik, iq[:, h * IDX_DIM:(h + 1) * IDX_DIM])
                acc = acc + jnp.maximum(lt, 0.0) * w_t[IDX_DIM + h:IDX_DIM + h + 1, :]
            sc_scr[kt * tk:(kt + 1) * tk, :] = acc
        _select_bias(sc_scr.at[0:s_eff], key_scr.at[0:s_eff], bias_scr.at[0:s_eff],
                     limit, t_len, topk)
        for kt in range(s_eff // LANE):
            bias_out_ref[:, kt * LANE:(kt + 1) * LANE] = (
                bias_scr[kt * LANE:(kt + 1) * LANE, :].T.astype(bias_out_ref.dtype))
        if s_eff < t_len:
            bias_out_ref[:, s_eff:t_len] = jnp.full((LANE, t_len - s_eff), -jnp.inf,
                                                    bias_out_ref.dtype)

    for v in range(t_len // seg):
        pl.when((j * LANE) // seg == v)(functools.partial(run, (v + 1) * seg))


def _causal_seg(t_len):
    return min(4 * LANE, t_len)


def _sel_prompt(iq_rot, ikw_rot, nb, t):
    topk = min(TOPK_MAX, t // 4)
    nq = t // LANE
    wi = IDX_HEADS * IDX_DIM
    return pl.pallas_call(
        functools.partial(_sel_prompt_kernel, topk=topk, tk=min(256, t)),
        grid=(nb, nq),
        in_specs=[pl.BlockSpec((LANE, wi), lambda b, j: (b * nq + j, 0)),
                  pl.BlockSpec((LANE, LANE), lambda b, j: (b * nq + j, 0)),
                  pl.BlockSpec((t, LANE), lambda b, j: (b, 0))],
        out_specs=pl.BlockSpec((LANE, t), lambda b, j: (b * nq + j, 0)),
        out_shape=jax.ShapeDtypeStruct((nb * t, t), MXU_DTYPE),
        scratch_shapes=[pltpu.VMEM((t, LANE), F32), pltpu.VMEM((t, LANE), jnp.int32),
                        pltpu.VMEM((t, LANE), F32)],
        compiler_params=_params(("parallel", "arbitrary")),
        name="sel_prompt",
    )(iq_rot, ikw_rot, ikw_rot)


def _sel_sample_kernel(iqp_ref, w_ref, ikp_ref, ikn_ref, bias_out_ref,
                       sc_scr, key_scr, bias_scr, *, ts, past, topk, n_keys, tk):
    s_pad = sc_scr.shape[0]
    nbp = LANE // ts
    tiles = [(r0, r0 + tk) for r0 in range(0, past, tk)] + [(past, s_pad)]
    for b in range(nbp):
        iqp = iqp_ref[b]
        w = w_ref[b]
        for r0, r1 in tiles:
            if r0 < past:
                ik = ikp_ref[b, :, r0:r1].T
            else:
                ik = jnp.concatenate([ikn_ref[b * ts:(b + 1) * ts, 0:IDX_DIM],
                                      jnp.zeros((r1 - r0 - ts, IDX_DIM), F32)], axis=0)
            lt = _mm_nt(ik.astype(MXU_DTYPE), iqp)
            r = jnp.maximum(lt, 0.0) * w
            y = r[:, 0:LANE]
            for c in range(1, (IDX_HEADS * ts) // LANE):
                y = y + r[:, c * LANE:(c + 1) * LANE]
            sh = LANE // 2
            while sh >= ts:
                y = y + pltpu.roll(y, sh, 1)
                sh //= 2
            if b == 0:
                sc_scr[r0:r1, :] = y
            else:
                lane = lax.broadcasted_iota(jnp.int32, (r1 - r0, LANE), 1)
                sc_scr[r0:r1, :] = jnp.where(lane // ts == b, y, sc_scr[r0:r1, :])
    q = lax.broadcasted_iota(jnp.int32, (1, LANE), 1) % ts
    limit = ((past + q) // CHUNK + 1) * CHUNK
    _select_bias(sc_scr, key_scr, bias_scr, limit, n_keys, topk)
    for kt in range(s_pad // LANE):
        bias_out_ref[:, kt * LANE:(kt + 1) * LANE] = (
            bias_scr[kt * LANE:(kt + 1) * LANE, :].T.astype(bias_out_ref.dtype))


def _sel_sample(iqp, wrow, ik_past, layer, ikw_rot, row0, nb, ts, past, s_pad):
    n_keys = past + ts
    topk = min(TOPK_MAX, n_keys // 4)
    nbp = LANE // ts
    blk0 = row0 // LANE
    ng = nb // nbp
    return pl.pallas_call(
        functools.partial(_sel_sample_kernel, ts=ts, past=past, topk=topk, n_keys=n_keys,
                          tk=math.gcd(past, 1024)),
        grid=(nb // nbp,),
        in_specs=[pl.BlockSpec((nbp, IDX_HEADS * ts, IDX_DIM), lambda g: (g, 0, 0)),
                  pl.BlockSpec((nbp, 1, IDX_HEADS * ts), lambda g: (g, 0, 0)),
                  pl.BlockSpec((nbp, IDX_DIM, past), lambda g: (layer * ng + g, 0, 0)),
                  pl.BlockSpec((LANE, LANE), lambda g: (blk0 + g, 0))],
        out_specs=pl.BlockSpec((LANE, s_pad), lambda g: (g, 0)),
        out_shape=jax.ShapeDtypeStruct((nb * ts, s_pad), MXU_DTYPE),
        scratch_shapes=[pltpu.VMEM((s_pad, LANE), F32),
                        pltpu.VMEM((s_pad, LANE), jnp.int32), pltpu.VMEM((s_pad, LANE), F32)],
        compiler_params=_params(("parallel",)),
        name="sel_sample",
    )(iqp, wrow, ik_past, ikw_rot)


def _attend(q_ref, k_ref, v_ref, bias_ref, o_ref, s_eff, tk):
    nq = q_ref.shape[0]
    scale = HEAD_DIM ** -0.5 * math.log2(math.e)
    rows = KV_GROUP * nq
    for g in range(N_KV):
        cols = slice(g * HEAD_DIM, (g + 1) * HEAD_DIM)
        qg = jnp.concatenate(
            [q_ref[:, (g * KV_GROUP + i) * HEAD_DIM:(g * KV_GROUP + i + 1) * HEAD_DIM]
             for i in range(KV_GROUP)], axis=0)
        m = jnp.full((rows, 1), -jnp.inf, F32)
        l = jnp.zeros((rows, 1), F32)
        o = jnp.zeros((rows, HEAD_DIM), F32)
        for r0 in range(0, s_eff, tk):
            r1 = min(r0 + tk, s_eff)
            b = bias_ref[:, r0:r1].astype(F32)
            s = _mm_nt(qg, k_ref[r0:r1, cols]) * scale + jnp.concatenate([b] * KV_GROUP, axis=0)
            m_new = jnp.maximum(m, jnp.max(s, axis=-1, keepdims=True))
            m_safe = jnp.where(m_new == -jnp.inf, 0.0, m_new)
            alpha = jnp.exp2(m - m_safe)
            p = jnp.exp2(s - m_safe)
            l = alpha * l + jnp.sum(p, axis=-1, keepdims=True)
            o = alpha * o + _mm(p.astype(MXU_DTYPE), v_ref[r0:r1, cols])
            m = m_new
        o = o / l
        for i in range(KV_GROUP):
            h = g * KV_GROUP + i
            o_ref[:, h * HEAD_DIM:(h + 1) * HEAD_DIM] = o[i * nq:(i + 1) * nq, :].astype(o_ref.dtype)


def _attn_prompt_kernel(q_ref, k_ref, v_ref, bias_ref, o_ref):
    j = pl.program_id(1)
    t_len = k_ref.shape[0]
    seg = _causal_seg(t_len)

    def run(s_eff):
        _attend(q_ref, k_ref, v_ref, bias_ref, o_ref, s_eff, ATTN_KEY_TILE)

    for v in range(t_len // seg):
        pl.when((j * LANE) // seg == v)(functools.partial(run, (v + 1) * seg))


def _attn_prompt(q_rot, k_bf, v_bf, bias, nb, t):
    nq = t // LANE
    wq = N_HEADS * HEAD_DIM
    wkv = N_KV * HEAD_DIM
    return pl.pallas_call(
        _attn_prompt_kernel,
        grid=(nb, nq),
        in_specs=[pl.BlockSpec((LANE, wq), lambda b, j: (b * nq + j, 0)),
                  pl.BlockSpec((t, wkv), lambda b, j: (b, 0)),
                  pl.BlockSpec((t, wkv), lambda b, j: (b, 0)),
                  pl.BlockSpec((LANE, t), lambda b, j: (b * nq + j, 0))],
        out_specs=pl.BlockSpec((LANE, wq), lambda b, j: (b * nq + j, 0)),
        out_shape=jax.ShapeDtypeStruct((nb * t, wq), MXU_DTYPE),
        compiler_params=_params(("parallel", "arbitrary")),
        name="attn_prompt",
    )(q_rot, k_bf, v_bf, bias)


def _attn_sample_kernel(q_ref, kp_ref, vp_ref, kn_ref, vn_ref, bias_ref, o_ref, k_scr, v_scr,
                        *, ts, past):
    s_pad = k_scr.shape[0]
    wkv = N_KV * HEAD_DIM
    for g in range(N_KV):
        cols = slice(g * HEAD_DIM, (g + 1) * HEAD_DIM)
        k_scr[0:past, cols] = kp_ref[0, pl.ds(g, past, stride=N_KV), :].astype(MXU_DTYPE)
        v_scr[0:past, cols] = vp_ref[0, pl.ds(g, past, stride=N_KV), :].astype(MXU_DTYPE)
    k_scr[past:past + ts, :] = kn_ref[...]
    v_scr[past:past + ts, :] = vn_ref[...]
    k_scr[past + ts:s_pad, :] = jnp.zeros((s_pad - past - ts, wkv), MXU_DTYPE)
    v_scr[past + ts:s_pad, :] = jnp.zeros((s_pad - past - ts, wkv), MXU_DTYPE)
    _attend(q_ref, k_scr, v_scr, bias_ref, o_ref, s_pad, -(-s_pad // (2 * LANE)) * LANE)


def _attn_sample(q_rot, k_past, v_past, layer, k_bf, v_bf, bias, row0, nb, ts, past, s_pad):
    wq = N_HEADS * HEAD_DIM
    wkv = N_KV * HEAD_DIM
    blk0 = row0 // ts
    return pl.pallas_call(
        functools.partial(_attn_sample_kernel, ts=ts, past=past),
        grid=(nb,),
        in_specs=[pl.BlockSpec((ts, wq), lambda b: (blk0 + b, 0)),
                  pl.BlockSpec((1, past * N_KV, HEAD_DIM), lambda b: (layer * nb + b, 0, 0)),
                  pl.BlockSpec((1, past * N_KV, HEAD_DIM), lambda b: (layer * nb + b, 0, 0)),
                  pl.BlockSpec((ts, wkv), lambda b: (blk0 + b, 0)),
                  pl.BlockSpec((ts, wkv), lambda b: (blk0 + b, 0)),
                  pl.BlockSpec((ts, s_pad), lambda b: (b, 0))],
        out_specs=pl.BlockSpec((ts, wq), lambda b: (b, 0)),
        out_shape=jax.ShapeDtypeStruct((nb * ts, wq), MXU_DTYPE),
        scratch_shapes=[pltpu.VMEM((s_pad, wkv), MXU_DTYPE), pltpu.VMEM((s_pad, wkv), MXU_DTYPE)],
        compiler_params=_params(("parallel",)),
        name="attn_sample",
    )(q_rot, k_past, v_past, k_bf, v_bf, bias)


def kernel(x_prompt, x_sample, cache_k, cache_v, cache_idx_k, state_ssm, state_conv, norm_ffn1, ffn1_w1, ffn1_w3, ffn1_w2, norm_mix, w_in, conv_w, conv_b, dt_bias, a_log, d_skip, ssm_norm_w, w_out, norm_ffn2, ffn2_w1, ffn2_w3, ffn2_w2, final_norm):
    bp, tp, d = x_prompt.shape
    bs, ts = x_sample.shape[:2]
    depth, _, past = cache_k.shape[:3]
    mp, ms = bp * tp, bs * ts
    n_keys_s = past + ts
    s_pad = -(-n_keys_s // LANE) * LANE
    hp = SSM_HEADS * SSM_HEADDIM
    wkv = N_KV * HEAD_DIM

    xs = (x_prompt.reshape(mp, d), x_sample.reshape(ms, d))

    tm_prep = _prep_tile(tp, ms, ts)
    pos = jnp.concatenate([jnp.arange(tp, dtype=jnp.int32),
                           jnp.tile(past + jnp.arange(ts, dtype=jnp.int32), tm_prep // ts)])
    iw_scale = jnp.concatenate([jnp.full((IDX_HEADS,), IDX_HEADS ** -0.5, F32),
                                jnp.ones((LANE - IDX_DIM - IDX_HEADS,), F32)])
    tab = jnp.concatenate(_rope_table(pos, ROPE_DIM, HEAD_DIM)
                          + _rope_table(pos, IDX_ROPE_DIM, IDX_DIM)
                          + _rope_table(pos, IDX_ROPE_DIM, IDX_DIM, tail=iw_scale), axis=1)

    w_in_p = _pack_w_in(w_in)
    stack2d = lambda w: w.reshape(w.shape[0] * w.shape[1], w.shape[2])
    ffn_f32 = [tuple(stack2d(w) for w in ws)
               for ws in ((ffn1_w1, ffn1_w3, ffn1_w2), (ffn2_w1, ffn2_w3, ffn2_w2))]
    dff = ffn1_w1.shape[2]
    w_ffn = (_cast_weights(ffn_f32[0][0], d), _cast_weights(ffn_f32[0][1], d),
             _cast_weights(ffn_f32[0][2], dff))
    w_out_b = _cast_weights(stack2d(w_out), depth * w_out.shape[1])
    k_cache = cache_k.reshape(depth * bs, past * N_KV, HEAD_DIM)
    v_cache = cache_v.reshape(depth * bs, past * N_KV, HEAD_DIM)
    ik_cache = jnp.swapaxes(cache_idx_k, 2, 3).reshape(depth * bs, IDX_DIM, past)
    ssm0 = state_ssm.reshape(depth * bs, hp, SSM_STATE)
    conv0_pad = jnp.pad(state_conv, ((0, 0), (0, 0), (SUBLANE - (CONV_W - 1), 0), (0, 0)))
    conv0_pad = conv0_pad.reshape(depth * bs, SUBLANE, CONV_DIM)
    lane_pad = lambda a: jnp.pad(a, (0, LANE - a.shape[0]))[None, :]

    outs = {k: [] for k in ("ikp", "cp", "iks", "cs")}
    kv_out = ssm_p = ssm_s = None
    for l in range(depth):
        x, w_ffn = _ffn(xs if l == 0 else (x,), norm_ffn1[l][None, :], w_ffn, (*ffn_f32[1], l))
        proj = _in_proj(x, norm_mix[l][None, :], w_in_p, l)
        (q_rot, iq_rot, k_bf, v_bf, ikw_rot), kv_out = _prep(
            proj, tab, mp, tp, tm_prep, l, depth, kv_out)

        consts = (conv_w[l], conv_b[l][None, :], lane_pad(dt_bias[l]),
                  lane_pad(-jnp.exp(a_log[l])), jnp.repeat(d_skip[l], SSM_HEADDIM)[None, :],
                  ssm_norm_w[l][None, :])
        y_p, ssm_p, conv_p = _ssd(proj, 0, bp, tp, LANE, LANE, None, consts, l, depth, ssm_p)
        y_s, ssm_s, conv_s = _ssd(proj, mp, bs, ts, ts, LANE, (ssm0, conv0_pad), consts,
                                  l, depth, ssm_s)

        bias_p = _sel_prompt(iq_rot, ikw_rot, bp, tp)
        att_p = _attn_prompt(q_rot, k_bf, v_bf, bias_p, bp, tp)

        iqp = iq_rot[mp:].reshape(bs, ts, IDX_HEADS, IDX_DIM).transpose(0, 2, 1, 3)
        iqp = iqp.reshape(bs, IDX_HEADS * ts, IDX_DIM)
        wrow = ikw_rot[mp:, IDX_DIM:IDX_DIM + IDX_HEADS].reshape(bs, ts, IDX_HEADS)
        wrow = wrow.transpose(0, 2, 1).reshape(bs, 1, IDX_HEADS * ts)
        bias_s = _sel_sample(iqp, wrow, ik_cache, l, ikw_rot, mp, bs, ts, past, s_pad)
        att_s = _attn_sample(q_rot, k_cache, v_cache, l, k_bf, v_bf, bias_s,
                             mp, bs, ts, past, s_pad)

        x = _out_proj(x, (y_p, y_s), (att_p, att_s), w_out_b, l)
        x, w_ffn = _ffn((x,), norm_ffn2[l][None, :], w_ffn,
                        (*ffn_f32[0], l + 1) if l + 1 < depth else None)

        outs["ikp"].append(ikw_rot[:mp, :IDX_DIM].reshape(bp, tp, IDX_DIM))
        outs["cp"].append(conv_p[:, SUBLANE - (CONV_W - 1):, :])
        outs["iks"].append(ikw_rot[mp:, :IDX_DIM].reshape(bs, ts, IDX_DIM))
        outs["cs"].append(conv_s[:, SUBLANE - (CONV_W - 1):, :])

    y_p, y_s = _final_norm(x, final_norm[None, :], mp, ms)
    st = {k: jnp.stack(v) for k, v in outs.items()}
    kp, ks, vp, vs = kv_out
    kv_p = lambda a: a.reshape(depth, bp, tp, N_KV, HEAD_DIM)
    kv_s = lambda a: a.reshape(depth, bs, ts, N_KV, HEAD_DIM)
    return (y_p.reshape(bp, tp, d), y_s.reshape(bs, ts, d),
            kv_p(kp), kv_p(vp), st["ikp"],
            ssm_p.reshape(depth, bp, SSM_HEADS, SSM_HEADDIM, SSM_STATE), st["cp"],
            kv_s(ks), kv_s(vs), st["iks"],
            ssm_s.reshape(depth, bs, SSM_HEADS, SSM_HEADDIM, SSM_STATE), st["cs"])
```

```python
import functools
import math

import jax
import jax.numpy as jnp
from jax import lax
from jax.experimental import pallas as pl
from jax.experimental.pallas import tpu as pltpu

F32 = jnp.float32
MXU_DTYPE = jnp.bfloat16

D_MODEL = 2048
CHUNK = 64
D_SSM = 1024
SSM_HEADDIM = 64
SSM_HEADS = 16
SSM_GROUPS = 2
SSM_STATE = 128
CONV_W = 4
CONV_DIM = D_SSM + 2 * SSM_GROUPS * SSM_STATE
HEAD_DIM = 128
N_HEADS = 8
N_KV = 2
KV_GROUP = N_HEADS // N_KV
ROPE_DIM = HEAD_DIM // 4
IDX_HEADS = 16
IDX_DIM = 64
IDX_ROPE_DIM = IDX_DIM // 4
TOPK_MAX = 256
ROPE_THETA = 500000.0
D_FF = 5632
EPS = 1e-6

LANE = 128
SUBLANE = 8
COUNT_TILE = 128
ATTN_KEY_TILE = 512
VMEM_LIMIT = 56 * 1024 * 1024

COL_Z = 0
COL_Q = 1024
COL_IQ = 2048
COL_XBC = 3072
COL_K = 4608
COL_V = 4864
COL_DT = 5120
COL_IKW = 5248
PROJ_COLS = 5376

KEY_NEG_INF = -2139095041
KEY_POS_INF = 2139095040
INT_MIN = -2147483648

NT_DIMS = (((1,), (1,)), ((), ()))


def _mm(a, b):
    return jnp.dot(a, b, preferred_element_type=F32)


def _mm_nt(a, b):
    return lax.dot_general(a, b, NT_DIMS, preferred_element_type=F32)


def _split3(x):
    hi = x.astype(MXU_DTYPE)
    r = x - hi.astype(F32)
    mid = r.astype(MXU_DTYPE)
    lo = (r - mid.astype(F32)).astype(MXU_DTYPE)
    return hi, mid, lo


def _mm_exact(a, b, exact):
    if exact == "a":
        a01 = a.astype(MXU_DTYPE)
        return sum(_mm(a01, p) for p in _split3(b))
    b01 = b.astype(MXU_DTYPE)
    return sum(_mm(p, b01) for p in _split3(a))


def _silu(x):
    return x * jax.nn.sigmoid(x)


def _params(sem, vmem=VMEM_LIMIT):
    return pltpu.CompilerParams(dimension_semantics=sem, vmem_limit_bytes=vmem)


def _row_tile(m, pref):
    t = pref
    while m % t:
        t //= 2
    return t


def _split_rows(tm, rows_a, rows_b):
    assert rows_a % tm == 0 and rows_b % tm == 0
    na = rows_a // tm

    def spec_a(width, *grid_rest):
        return pl.BlockSpec((tm, width), lambda i, *_: (jnp.minimum(i, na - 1), 0))

    def spec_b(width, *grid_rest):
        return pl.BlockSpec((tm, width), lambda i, *_: (jnp.maximum(i - na, 0), 0))

    return na, spec_a, spec_b


def _pick(i, na, a_ref, b_ref):
    return jnp.where(i < na, a_ref[...], b_ref[...])


def _ffn_kernel(*refs, na, side):
    refs = list(refs)
    if na is None:
        x_ref = refs.pop(0)
        load_x = lambda: x_ref[...]
    else:
        xa_ref, xb_ref = refs.pop(0), refs.pop(0)
        load_x = lambda: _pick(pl.program_id(0), na, xa_ref, xb_ref)
    nw_ref, w1_ref, w3_ref, w2_ref = refs[:4]
    refs = refs[4:]
    if side:
        s1_ref, s3_ref, s2_ref, o_ref, c1_ref, c3_ref, c2_ref, h_scr = refs
    else:
        o_ref, h_scr = refs
    f = pl.program_id(1)

    @pl.when(f == 0)
    def _():
        x = load_x()
        ms = jnp.mean(x * x, axis=-1, keepdims=True)
        h_scr[...] = (x * lax.rsqrt(ms + EPS) * nw_ref[...]).astype(h_scr.dtype)
        o_ref[...] = jnp.zeros_like(o_ref)

    h = h_scr[...]
    a = _mm(h, w1_ref[...])
    b = _mm(h, w3_ref[...])
    g = (_silu(a) * b).astype(h_scr.dtype)
    o_ref[...] += _mm(g, w2_ref[...])

    if side:
        c1_ref[...] = s1_ref[...].astype(c1_ref.dtype)
        c3_ref[...] = s3_ref[...].astype(c3_ref.dtype)
        c2_ref[...] = s2_ref[...].astype(c2_ref.dtype)

    @pl.when(f == pl.num_programs(1) - 1)
    def _():
        o_ref[...] = load_x() + 0.5 * o_ref[...]


def _ffn_row_tile(rows):
    g = math.gcd(*rows)
    for tm in (768, 512, 256, 128):
        if g % tm == 0:
            return tm
    raise ValueError(rows)


def _chunk_rows(total, steps):
    tile = 2 * SUBLANE
    for r in range(tile, total + 1, tile):
        if total % r == 0 and total // r <= steps:
            return r
    raise ValueError((total, steps))


def _ffn(xs, nw, w, nxt=None, tf=512):
    w1, w3, w2 = w
    d = xs[0].shape[1]
    m = sum(x.shape[0] for x in xs)
    dff = w1.shape[1]
    nf = dff // tf
    tm = _ffn_row_tile([x.shape[0] for x in xs])
    if len(xs) == 1:
        na, x_specs = None, [pl.BlockSpec((tm, d), lambda i, f: (i, 0))]
    else:
        na, spec_a, spec_b = _split_rows(tm, xs[0].shape[0], xs[1].shape[0])
        x_specs = [spec_a(d), spec_b(d)]
    in_specs = x_specs + [
        pl.BlockSpec((1, d), lambda i, f: (0, 0)),
        pl.BlockSpec((d, tf), lambda i, f: (0, f)),
        pl.BlockSpec((d, tf), lambda i, f: (0, f)),
        pl.BlockSpec((tf, d), lambda i, f: (f, 0)),
    ]
    out_specs = [pl.BlockSpec((tm, d), lambda i, f: (i, 0))]
    out_shape = [jax.ShapeDtypeStruct((m, d), F32)]
    args = [*xs, nw, w1, w3, w2]
    if nxt is not None:
        s1, s3, s2, layer = nxt
        steps = (m // tm) * nf
        r1, r2 = _chunk_rows(d, steps), _chunk_rows(dff, steps)
        n1, n2 = d // r1, dff // r2

        def chunk(n, base):
            return lambda i, f: (base + jnp.minimum(i * nf + f, n - 1), 0)

        in_specs += [pl.BlockSpec((r1, dff), chunk(n1, layer * n1)),
                     pl.BlockSpec((r1, dff), chunk(n1, layer * n1)),
                     pl.BlockSpec((r2, d), chunk(n2, layer * n2))]
        out_specs += [pl.BlockSpec((r1, dff), chunk(n1, 0)), pl.BlockSpec((r1, dff), chunk(n1, 0)),
                      pl.BlockSpec((r2, d), chunk(n2, 0))]
        out_shape += [jax.ShapeDtypeStruct((d, dff), MXU_DTYPE),
                      jax.ShapeDtypeStruct((d, dff), MXU_DTYPE),
                      jax.ShapeDtypeStruct((dff, d), MXU_DTYPE)]
        args += [s1, s3, s2]
    res = pl.pallas_call(
        functools.partial(_ffn_kernel, na=na, side=nxt is not None),
        grid=(m // tm, nf),
        in_specs=in_specs,
        out_specs=out_specs,
        out_shape=out_shape,
        scratch_shapes=[pltpu.VMEM((tm, d), MXU_DTYPE)],
        compiler_params=_params(("arbitrary", "arbitrary")),
        name="ffn",
    )(*args)
    return res[0], tuple(res[1:])


def _in_proj_kernel(x_ref, nw_ref, w_ref, o_ref, h_scr):
    @pl.when(pl.program_id(1) == 0)
    def _():
        x = x_ref[...]
        ms = jnp.mean(x * x, axis=-1, keepdims=True)
        h_scr[...] = (x * lax.rsqrt(ms + EPS) * nw_ref[...]).astype(h_scr.dtype)

    o_ref[...] = _mm_nt(h_scr[...], w_ref[...])


def _in_proj(x, nw, w, layer, tm=1024, tn=768):
    m, d = x.shape
    n = PROJ_COLS
    nn = n // tn
    tm = _row_tile(m, tm)
    return pl.pallas_call(
        _in_proj_kernel,
        grid=(m // tm, nn),
        in_specs=[
            pl.BlockSpec((tm, d), lambda i, j: (i, 0)),
            pl.BlockSpec((1, d), lambda i, j: (0, 0)),
            pl.BlockSpec((tn, d), lambda i, j: (layer * nn + j, 0)),
        ],
        out_specs=pl.BlockSpec((tm, tn), lambda i, j: (i, j)),
        out_shape=jax.ShapeDtypeStruct((m, n), F32),
        scratch_shapes=[pltpu.VMEM((tm, d), MXU_DTYPE)],
        compiler_params=_params(("parallel", "arbitrary")),
        name="in_proj",
    )(x, nw, w)


def _out_proj_kernel(x_ref, ya_ref, yb_ref, aa_ref, ab_ref, w_ref, o_ref, *, na):
    i = pl.program_id(0)
    half = ya_ref.shape[1]
    o_ref[...] = (x_ref[...] + _mm(_pick(i, na, ya_ref, yb_ref), w_ref[:half, :])
                  + _mm(_pick(i, na, aa_ref, ab_ref), w_ref[half:, :]))


def _out_proj(x, y_pair, att_pair, w, layer, tm=512):
    m, d = x.shape
    half = y_pair[0].shape[1]
    tm = _row_tile(math.gcd(y_pair[0].shape[0], y_pair[1].shape[0]), tm)
    na, spec_a, spec_b = _split_rows(tm, y_pair[0].shape[0], y_pair[1].shape[0])
    return pl.pallas_call(
        functools.partial(_out_proj_kernel, na=na),
        grid=(m // tm,),
        in_specs=[
            pl.BlockSpec((tm, d), lambda i: (i, 0)),
            spec_a(half), spec_b(half), spec_a(half), spec_b(half),
            pl.BlockSpec((2 * half, d), lambda i: (layer, 0)),
        ],
        out_specs=pl.BlockSpec((tm, d), lambda i: (i, 0)),
        out_shape=jax.ShapeDtypeStruct((m, d), F32),
        compiler_params=_params(("arbitrary",)),
        name="out_proj",
    )(x, *y_pair, *att_pair, w)


def _final_norm_kernel(x_ref, nw_ref, oa_ref, ob_ref, *, na):
    i = pl.program_id(0)
    x = x_ref[...]
    ms = jnp.mean(x * x, axis=-1, keepdims=True)
    y = x * lax.rsqrt(ms + EPS) * nw_ref[...]

    @pl.when(i < na)
    def _():
        oa_ref[...] = y

    @pl.when(i >= na)
    def _():
        ob_ref[...] = y


def _final_norm(x, nw, rows_a, rows_b, tm=512):
    d = x.shape[1]
    tm = _row_tile(math.gcd(rows_a, rows_b), tm)
    na, spec_a, spec_b = _split_rows(tm, rows_a, rows_b)
    return pl.pallas_call(
        functools.partial(_final_norm_kernel, na=na),
        grid=((rows_a + rows_b) // tm,),
        in_specs=[pl.BlockSpec((tm, d), lambda i: (i, 0)),
                  pl.BlockSpec((1, d), lambda i: (0, 0))],
        out_specs=[spec_a(d), spec_b(d)],
        out_shape=[jax.ShapeDtypeStruct((rows_a, d), F32),
                   jax.ShapeDtypeStruct((rows_b, d), F32)],
        compiler_params=_params(("arbitrary",)),
        name="final_norm",
    )(x, nw)


def _cast_kernel(x_ref, o_ref):
    o_ref[...] = x_ref[...].astype(o_ref.dtype)


def _cast_weights(w2, n_rows, row0=0, rows=512):
    c = w2.shape[1]
    rows = _row_tile(math.gcd(n_rows, row0) if row0 else n_rows, rows)
    blk0 = row0 // rows
    return pl.pallas_call(
        _cast_kernel,
        grid=(n_rows // rows,),
        in_specs=[pl.BlockSpec((rows, c), lambda i: (blk0 + i, 0))],
        out_specs=pl.BlockSpec((rows, c), lambda i: (i, 0)),
        out_shape=jax.ShapeDtypeStruct((n_rows, c), MXU_DTYPE),
        compiler_params=_params(("parallel",)),
        name="cast_w",
    )(w2)


_W_IN_PARTS = (
    (COL_Z, 0, D_SSM),
    (COL_XBC, D_SSM, CONV_DIM),
    (COL_DT, D_SSM + CONV_DIM, SSM_HEADS),
    (COL_Q, D_SSM + CONV_DIM + SSM_HEADS, N_HEADS * HEAD_DIM),
    (COL_K, D_SSM + CONV_DIM + SSM_HEADS + N_HEADS * HEAD_DIM, N_KV * HEAD_DIM),
    (COL_V, D_SSM + CONV_DIM + SSM_HEADS + (N_HEADS + N_KV) * HEAD_DIM, N_KV * HEAD_DIM),
    (COL_IQ, D_SSM + CONV_DIM + SSM_HEADS + (N_HEADS + 2 * N_KV) * HEAD_DIM, IDX_HEADS * IDX_DIM),
    (COL_IKW, D_SSM + CONV_DIM + SSM_HEADS + (N_HEADS + 2 * N_KV) * HEAD_DIM + IDX_HEADS * IDX_DIM,
     IDX_DIM + IDX_HEADS),
)


def _pack_w_in_kernel(w_ref, o_ref):
    cols = o_ref.shape[1]
    for dst, src, width in _W_IN_PARTS:
        o_ref[dst:dst + width, :] = w_ref[src:src + width, :].astype(o_ref.dtype)
        pad = -width % LANE
        if pad:
            o_ref[dst + width:dst + width + pad, :] = jnp.zeros((pad, cols), o_ref.dtype)


def _pack_w_in(w, kc=256):
    depth, d, c = w.shape
    wt = jnp.transpose(w, (0, 2, 1)).reshape(depth * c, d)
    return pl.pallas_call(
        _pack_w_in_kernel,
        grid=(depth, d // kc),
        in_specs=[pl.BlockSpec((c, kc), lambda l, k: (l, k))],
        out_specs=pl.BlockSpec((PROJ_COLS, kc), lambda l, k: (l, k)),
        out_shape=jax.ShapeDtypeStruct((depth * PROJ_COLS, d), MXU_DTYPE),
        compiler_params=_params(("parallel", "parallel")),
        name="pack_w_in",
    )(wt)


def _rope_lanes(x, c, s1, s2, half):
    outs = []
    for t in range(x.shape[1] // LANE):
        xt = x[:, t * LANE:(t + 1) * LANE]
        outs.append(xt * c + pltpu.roll(xt, half, 1) * s1
                    + pltpu.roll(xt, LANE - half, 1) * s2)
    return outs[0] if len(outs) == 1 else jnp.concatenate(outs, axis=1)


def _prep_kernel(*refs, na, aliased):
    q_ref, iq_ref, k_ref, v_ref, ikw_ref, tab_ref = refs[:6]
    refs = refs[6 + (4 if aliased else 0):]
    (qo_ref, iqo_ref, kbo_ref, vbo_ref, ikwo_ref, kp_ref, ks_ref, vp_ref, vs_ref) = refs
    i = pl.program_id(0)
    tm = k_ref.shape[0]
    tab = [tab_ref[:, j * LANE:(j + 1) * LANE] for j in range(9)]
    qo_ref[...] = _rope_lanes(q_ref[...], tab[0], tab[1], tab[2], ROPE_DIM // 2).astype(qo_ref.dtype)
    k = _rope_lanes(k_ref[...], tab[0], tab[1], tab[2], ROPE_DIM // 2)
    kbo_ref[...] = k.astype(kbo_ref.dtype)
    iqo_ref[...] = _rope_lanes(iq_ref[...], tab[3], tab[4], tab[5], IDX_ROPE_DIM // 2).astype(iqo_ref.dtype)
    ikwo_ref[...] = _rope_lanes(ikw_ref[...], tab[6], tab[7], tab[8], IDX_ROPE_DIM // 2)
    v = v_ref[...]
    vbo_ref[...] = v.astype(vbo_ref.dtype)

    def put(dst_k, dst_v):
        for g in range(N_KV):
            dst_k[pl.ds(g, tm, stride=N_KV), :] = k[:, g * HEAD_DIM:(g + 1) * HEAD_DIM]
            dst_v[pl.ds(g, tm, stride=N_KV), :] = v[:, g * HEAD_DIM:(g + 1) * HEAD_DIM]

    pl.when(i < na)(functools.partial(put, kp_ref, vp_ref))
    pl.when(i >= na)(functools.partial(put, ks_ref, vs_ref))


def _prep_tile(tp, ms, ts, tm=512):
    tm = _row_tile(math.gcd(tp, ms), tm)
    assert tm % ts == 0
    return tm


def _prep(proj, tab, mp, tp, tm, layer, depth, caches):
    m = proj.shape[0]
    ms = m - mp
    wq = N_HEADS * HEAD_DIM
    wkv = N_KV * HEAD_DIM
    na = mp // tm
    row = lambda w, c: pl.BlockSpec((tm, w), lambda i: (i, c))
    tab_spec = pl.BlockSpec(
        (tm, 9 * LANE), lambda i: (jnp.where(i < na, i % (tp // tm), tp // tm), 0))
    p_spec = pl.BlockSpec((N_KV * tm, HEAD_DIM), lambda i: (layer * na + jnp.minimum(i, na - 1), 0))
    s_spec = pl.BlockSpec((N_KV * tm, HEAD_DIM),
                          lambda i: (layer * (ms // tm) + jnp.maximum(i - na, 0), 0))
    p_shape = jax.ShapeDtypeStruct((depth * mp * N_KV, HEAD_DIM), F32)
    s_shape = jax.ShapeDtypeStruct((depth * ms * N_KV, HEAD_DIM), F32)
    in_specs = [row(wq, COL_Q // wq), row(wq, COL_IQ // wq), row(wkv, COL_K // wkv),
                row(wkv, COL_V // wkv), row(LANE, COL_IKW // LANE), tab_spec]
    args = [proj, proj, proj, proj, proj, tab]
    aliases = {}
    if caches is not None:
        in_specs += [pl.BlockSpec(memory_space=pl.ANY)] * 4
        aliases = {len(args) + j: 5 + j for j in range(4)}
        args += list(caches)
    res = pl.pallas_call(
        functools.partial(_prep_kernel, na=na, aliased=caches is not None),
        grid=(m // tm,),
        in_specs=in_specs,
        out_specs=[row(wq, 0), row(wq, 0), row(wkv, 0), row(wkv, 0), row(LANE, 0),
                   p_spec, s_spec, p_spec, s_spec],
        out_shape=[jax.ShapeDtypeStruct((m, wq), MXU_DTYPE),
                   jax.ShapeDtypeStruct((m, wq), MXU_DTYPE),
                   jax.ShapeDtypeStruct((m, wkv), MXU_DTYPE),
                   jax.ShapeDtypeStruct((m, wkv), MXU_DTYPE),
                   jax.ShapeDtypeStruct((m, LANE), F32),
                   p_shape, s_shape, p_shape, s_shape],
        input_output_aliases=aliases,
        compiler_params=_params(("arbitrary",)),
        name="prep",
    )(*args)
    return res[:5], tuple(res[5:])


def _rope_table(pos, rot_dim, period, tail=None):
    half = rot_dim // 2
    inv_freq = jnp.float32(ROPE_THETA) ** (-jnp.arange(half, dtype=F32) * 2.0 / rot_dim)
    ang = pos.astype(F32)[:, None] * inv_freq[None, :]
    cos, sin = jnp.cos(ang), jnp.sin(ang)
    n = pos.shape[0]
    ones = jnp.ones((n, period - rot_dim), F32)
    zeros = jnp.zeros((n, period - rot_dim), F32)
    zh = jnp.zeros((n, half), F32)
    c = jnp.concatenate([cos, cos, ones], axis=1)
    s1 = jnp.concatenate([zh, sin, zeros], axis=1)
    s2 = jnp.concatenate([-sin, zh, zeros], axis=1)
    if tail is None:
        reps = LANE // period
        return [jnp.tile(a, (1, reps)) for a in (c, s1, s2)]
    zt = jnp.zeros((n, LANE - period), F32)
    return [jnp.concatenate([c, jnp.broadcast_to(tail[None, :], (n, LANE - period))], axis=1),
            jnp.concatenate([s1, zt], axis=1), jnp.concatenate([s2, zt], axis=1)]


def _ssd_kernel(*refs, qv, qp, has_state, aliased):
    refs = list(refs)
    z_ref, xbc_ref, dt_ref = refs[:3]
    refs = refs[3:]
    if has_state:
        ssm0_ref, conv0_ref = refs[:2]
        refs = refs[2:]
    cw_ref, cb_ref, dtb_ref, aneg_ref, dsk_ref, nw_ref = refs[:6]
    refs = refs[6 + (1 if aliased else 0):]
    y_ref, ssm_ref, conv_ref, ext_scr, st_scr = refs
    c = pl.program_id(1)
    hp = SSM_HEADS * SSM_HEADDIM
    gw = SSM_STATE
    hpg = hp // SSM_GROUPS

    @pl.when(c == 0)
    def _():
        if has_state:
            ext_scr[0:SUBLANE, :] = conv0_ref[0]
            st_scr[...] = ssm0_ref[0].T
        else:
            ext_scr[0:SUBLANE, :] = jnp.zeros((SUBLANE, CONV_DIM), F32)
            st_scr[...] = jnp.zeros_like(st_scr)

    xbc_raw = xbc_ref[...]
    ext_scr[SUBLANE:SUBLANE + qv, :] = xbc_raw
    conv = cb_ref[...] + xbc_raw * cw_ref[CONV_W - 1:CONV_W, :]
    for j in range(CONV_W - 1):
        off = SUBLANE - (CONV_W - 1) + j
        conv = conv + ext_scr[off:off + qv, :] * cw_ref[j:j + 1, :]
    xc = _silu(conv)
    dtr = dt_ref[...]
    if qv < qp:
        xc = jnp.concatenate([xc, jnp.zeros((qp - qv, CONV_DIM), F32)], axis=0)
        dtr = jnp.concatenate([dtr, jnp.zeros((qp - qv, LANE), F32)], axis=0)
    xs = xc[:, :hp]
    bm = xc[:, hp:hp + SSM_GROUPS * gw]
    cm = xc[:, hp + SSM_GROUPS * gw:]

    xdt_in = dtr + dtb_ref[...]
    dt = jnp.maximum(xdt_in, 0.0) + jnp.log1p(jnp.exp(-jnp.abs(xdt_in)))
    row = lax.broadcasted_iota(jnp.int32, (qp, LANE), 0)
    dt = jnp.where(row < qv, dt, 0.0)
    la = dt * aneg_ref[...]
    ti = lax.broadcasted_iota(jnp.int32, (qp, qp), 0)
    si = lax.broadcasted_iota(jnp.int32, (qp, qp), 1)
    tri = ti >= si
    cum = _mm_exact(tri.astype(F32), la, "a")
    cum_t = cum.T

    eh = lax.broadcasted_iota(jnp.int32, (LANE, hp), 0)
    ec = lax.broadcasted_iota(jnp.int32, (LANE, hp), 1)
    expand = (ec // SSM_HEADDIM == eh).astype(F32)
    dt_x = _mm_exact(dt, expand, "b")
    cum_x = _mm_exact(cum, expand, "b")
    xdt = xs * dt_x

    cb = [_mm_nt(cm[:, g * gw:(g + 1) * gw].astype(MXU_DTYPE),
                 bm[:, g * gw:(g + 1) * gw].astype(MXU_DTYPE)) for g in range(SSM_GROUPS)]
    lane = lax.broadcasted_iota(jnp.int32, (qp, LANE), 1)
    y_parts = []
    for j in range(SSM_HEADS // 2):
        ms = []
        for h in (2 * j, 2 * j + 1):
            d = cum[:, h:h + 1] - cum_t[h:h + 1, :]
            seg = jnp.exp(jnp.where(tri, d, -jnp.inf))
            ms.append((cb[h // (SSM_HEADS // SSM_GROUPS)] * seg).astype(MXU_DTYPE))
        xp = xdt[:, j * LANE:(j + 1) * LANE]
        rhs = jnp.concatenate([jnp.where(lane < SSM_HEADDIM, xp, 0.0),
                               jnp.where(lane >= SSM_HEADDIM, xp, 0.0)], axis=0)
        y_parts.append(_mm(jnp.concatenate(ms, axis=1), rhs.astype(MXU_DTYPE)))
    y = jnp.concatenate(y_parts, axis=1)

    st = st_scr[...]
    y_st = jnp.concatenate(
        [_mm(cm[:, g * gw:(g + 1) * gw].astype(MXU_DTYPE),
             st[:, g * hpg:(g + 1) * hpg].astype(MXU_DTYPE)) for g in range(SSM_GROUPS)], axis=1)
    y = y + y_st * jnp.exp(cum_x) + dsk_ref[...] * xs

    cum_last = cum_x[qp - 1:qp, :]
    xt = (xdt * jnp.exp(cum_last - cum_x)).astype(MXU_DTYPE)
    upd = jnp.concatenate(
        [_mm(bm[:, g * gw:(g + 1) * gw].T.astype(MXU_DTYPE), xt[:, g * hpg:(g + 1) * hpg])
         for g in range(SSM_GROUPS)], axis=1)
    st_new = st * jnp.exp(cum_last) + upd
    st_scr[...] = st_new

    zz = z_ref[...]
    gated = y[:qv, :] * _silu(zz)
    ms2 = jnp.mean(gated * gated, axis=-1, keepdims=True)
    y_ref[...] = (gated * lax.rsqrt(ms2 + EPS) * nw_ref[...]).astype(y_ref.dtype)

    tail_rows = ext_scr[qv:qv + SUBLANE, :]
    ext_scr[0:SUBLANE, :] = tail_rows

    @pl.when(c == pl.num_programs(1) - 1)
    def _():
        ssm_ref[0] = st_new.T
        conv_ref[0] = tail_rows


def _ssd(proj, row0, nb, t, qv, qp, state, consts, layer, depth, ssm_buf):
    nc = t // qv
    hp = SSM_HEADS * SSM_HEADDIM
    blk0 = row0 // qv
    rowspec = lambda w, cb_: pl.BlockSpec((qv, w), lambda b, c: (blk0 + b * nc + c, cb_))
    const = lambda a: pl.BlockSpec(a.shape, lambda b, c: (0,) * a.ndim)
    in_specs = [rowspec(hp, COL_Z // hp), rowspec(CONV_DIM, COL_XBC // CONV_DIM),
                rowspec(LANE, COL_DT // LANE)]
    args = [proj, proj, proj]
    if state is not None:
        ssm0, conv0 = state
        in_specs += [pl.BlockSpec((1, hp, SSM_STATE), lambda b, c: (layer * nb + b, 0, 0)),
                     pl.BlockSpec((1, SUBLANE, CONV_DIM), lambda b, c: (layer * nb + b, 0, 0))]
        args += [ssm0, conv0]
    in_specs += [const(a) for a in consts]
    args += list(consts)
    aliases = {}
    if ssm_buf is not None:
        in_specs.append(pl.BlockSpec(memory_space=pl.ANY))
        aliases = {len(args): 1}
        args.append(ssm_buf)
    return pl.pallas_call(
        functools.partial(_ssd_kernel, qv=qv, qp=qp, has_state=state is not None,
                          aliased=ssm_buf is not None),
        grid=(nb, nc),
        in_specs=in_specs,
        out_specs=[pl.BlockSpec((qv, hp), lambda b, c: (b * nc + c, 0)),
                   pl.BlockSpec((1, hp, SSM_STATE), lambda b, c: (layer * nb + b, 0, 0)),
                   pl.BlockSpec((1, SUBLANE, CONV_DIM), lambda b, c: (b, 0, 0))],
        out_shape=[jax.ShapeDtypeStruct((nb * t, hp), MXU_DTYPE),
                   jax.ShapeDtypeStruct((depth * nb, hp, SSM_STATE), F32),
                   jax.ShapeDtypeStruct((nb, SUBLANE, CONV_DIM), F32)],
        scratch_shapes=[pltpu.VMEM((qv + SUBLANE, CONV_DIM), F32),
                        pltpu.VMEM((SSM_STATE, hp), F32)],
        input_output_aliases=aliases,
        compiler_params=_params(("arbitrary", "arbitrary")),
        name="ssd_state" if state is not None else "ssd_prompt",
    )(*args)


def _row_tiles(n_rows):
    assert n_rows % COUNT_TILE == 0
    return [(r0, r0 + COUNT_TILE) for r0 in range(0, n_rows, COUNT_TILE)]


def _row_iota(r0, r1):
    return lax.broadcasted_iota(jnp.int32, (r1 - r0, LANE), 0) + r0


def _count_rows(fn, n_rows):
    acc = None
    for r0, r1 in _row_tiles(n_rows):
        part = fn(r0, r1).astype(F32)
        acc = part if acc is None else acc + part
    return jnp.sum(acc, axis=0, keepdims=True)


def _select_bias(sc_ref, key_ref, bias_ref, limit, n_keys, topk):
    s_pad = sc_ref.shape[0]
    tiles = _row_tiles(s_pad)
    for r0, r1 in tiles:
        s_io = _row_iota(r0, r1)
        adm = (s_io < limit) & (s_io < n_keys)
        bits = pltpu.bitcast(jnp.where(adm, sc_ref[r0:r1, :], -jnp.inf), jnp.int32)
        key_ref[r0:r1, :] = bits ^ ((bits >> 31) & jnp.int32(0x7FFFFFFF))
    kf = jnp.float32(topk)

    cnt0 = _count_rows(lambda r0, r1: key_ref[r0:r1, :] >= 0, s_pad)
    prefix = jnp.where(cnt0 >= kf, jnp.int32(0), jnp.int32(INT_MIN))

    def body(i, prefix):
        cand = prefix | jnp.left_shift(jnp.int32(1), 30 - i)
        cnt = _count_rows(lambda r0, r1: key_ref[r0:r1, :] >= cand, s_pad)
        return jnp.where(cnt >= kf, cand, prefix)

    thr = lax.fori_loop(0, 31, body, prefix)

    def finite(key):
        return (key > KEY_NEG_INF) & (key < KEY_POS_INF)

    for r0, r1 in tiles:
        key = key_ref[r0:r1, :]
        bias_ref[r0:r1, :] = jnp.where((key >= thr) & finite(key), 0.0, -jnp.inf)

    need = kf - _count_rows(lambda r0, r1: key_ref[r0:r1, :] > thr, s_pad)
    n_eq = _count_rows(lambda r0, r1: key_ref[r0:r1, :] == thr, s_pad)
    tie = (n_eq > need) & (thr > KEY_NEG_INF)
    any_tie = jnp.max(tie.astype(F32)) > 0.0

    @pl.when(any_tie)
    def _():
        nbits = max(1, (s_pad - 1).bit_length())

        def ibody(i, v):
            cand = v | jnp.left_shift(jnp.int32(1), nbits - 1 - i)
            below = _count_rows(
                lambda r0, r1: (key_ref[r0:r1, :] == thr) & (_row_iota(r0, r1) < cand), s_pad)
            return jnp.where(below < need, cand, v)

        last = lax.fori_loop(0, nbits, ibody, jnp.zeros((1, LANE), jnp.int32))
        for r0, r1 in tiles:
            key = key_ref[r0:r1, :]
            sel = (key > thr) | ((key == thr) & (_row_iota(r0, r1) <= last))
            bias_ref[r0:r1, :] = jnp.where(sel & finite(key), 0.0, -jnp.inf)


def _sel_prompt_kernel(iq_ref, ikw_q_ref, ikw_all_ref, bias_out_ref, sc_scr, key_scr, bias_scr,
                       *, topk, tk):
    j = pl.program_id(1)
    t_len = ikw_all_ref.shape[0]
    seg = _causal_seg(t_len)
    pos = j * LANE + lax.broadcasted_iota(jnp.int32, (1, LANE), 1)
    limit = (pos // CHUNK + 1) * CHUNK

    def run(s_eff):
        w_t = ikw_q_ref[...].T
        iq = iq_ref[...]
        for kt in range(s_eff // tk):
            ik = ikw_all_ref[kt * tk:(kt + 1) * tk, 0:IDX_DIM].astype(MXU_DTYPE)
            acc = jnp.zeros((tk, LANE), F32)
            for h in range(IDX_HEADS):
                lt = _mm_nt(ik, iq[:, h * IDX_DIM:(h + 1) * IDX_DIM])
                acc = acc + jnp.maximum(lt, 0.0) * w_t[IDX_DIM + h:IDX_DIM + h + 1, :]
            sc_scr[kt * tk:(kt + 1) * tk, :] = acc
        if s_eff == seg:
            few = (j + 1) * LANE <= topk

            @pl.when(few)
            def _():
                for r0, r1 in _row_tiles(s_eff):
                    sc = sc_scr[r0:r1, :]
                    ok = (_row_iota(r0, r1) < limit) & (jnp.abs(sc) < jnp.inf)
                    bias_scr[r0:r1, :] = jnp.where(ok, 0.0, -jnp.inf)

            @pl.when(jnp.logical_not(few))
            def _():
                _select_bias(sc_scr.at[0:s_eff], key_scr.at[0:s_eff], bias_scr.at[0:s_eff],
                             limit, t_len, topk)
        else:
            _select_bias(sc_scr.at[0:s_eff], key_scr.at[0:s_eff], bias_scr.at[0:s_eff],
                         limit, t_len, topk)
        for kt in range(s_eff // LANE):
            bias_out_ref[:, kt * LANE:(kt + 1) * LANE] = (
                bias_scr[kt * LANE:(kt + 1) * LANE, :].T.astype(bias_out_ref.dtype))
        if s_eff < t_len:
            bias_out_ref[:, s_eff:t_len] = jnp.full((LANE, t_len - s_eff), -jnp.inf,
                                                    bias_out_ref.dtype)

    for v in range(t_len // seg):
        pl.when((j * LANE) // seg == v)(functools.partial(run, (v + 1) * seg))


def _causal_seg(t_len):
    return min(4 * LANE, t_len)


def _sel_prompt(iq_rot, ikw_rot, nb, t):
    topk = min(TOPK_MAX, t // 4)
    nq = t // LANE
    wi = IDX_HEADS * IDX_DIM
    return pl.pallas_call(
        functools.partial(_sel_prompt_kernel, topk=topk, tk=min(256, t)),
        grid=(nb, nq),
        in_specs=[pl.BlockSpec((LANE, wi), lambda b, j: (b * nq + j, 0)),
                  pl.BlockSpec((LANE, LANE), lambda b, j: (b * nq + j, 0)),
                  pl.BlockSpec((t, LANE), lambda b, j: (b, 0))],
        out_specs=pl.BlockSpec((LANE, t), lambda b, j: (b * nq + j, 0)),
        out_shape=jax.ShapeDtypeStruct((nb * t, t), MXU_DTYPE),
        scratch_shapes=[pltpu.VMEM((t, LANE), F32), pltpu.VMEM((t, LANE), jnp.int32),
                        pltpu.VMEM((t, LANE), F32)],
        compiler_params=_params(("parallel", "arbitrary")),
        name="sel_prompt",
    )(iq_rot, ikw_rot, ikw_rot)


def _sel_sample_kernel(iqp_ref, w_ref, ikp_ref, ikn_ref, bias_out_ref,
                       sct_scr, sc_scr, key_scr, bias_scr, *, ts, past, topk, n_keys, tk):
    s_pad = sc_scr.shape[0]
    nbp = LANE // ts
    tiles = [(r0, r0 + tk) for r0 in range(0, past, tk)] + [(past, s_pad)]
    for b in range(nbp):
        iqp = iqp_ref[b]
        wb = jnp.broadcast_to(w_ref[b], (LANE, IDX_HEADS * ts)).T
        for r0, r1 in tiles:
            if r0 < past:
                ikt = ikp_ref[b, :, r0:r1]
            else:
                new = jnp.concatenate([ikn_ref[b * ts:(b + 1) * ts, :],
                                       jnp.zeros((LANE - ts, LANE), F32)], axis=0)
                ikt = new.T[0:IDX_DIM, :]
            lg = _mm(iqp, ikt.astype(MXU_DTYPE))
            for c in range((r1 - r0) // LANE):
                acc = None
                for h in range(IDX_HEADS):
                    t = (jnp.maximum(lg[h * ts:(h + 1) * ts, c * LANE:(c + 1) * LANE], 0.0)
                         * wb[h * ts:(h + 1) * ts, :])
                    acc = t if acc is None else acc + t
                sct_scr[b * ts:(b + 1) * ts, r0 + c * LANE:r0 + (c + 1) * LANE] = acc
    for kt in range(s_pad // LANE):
        sc_scr[kt * LANE:(kt + 1) * LANE, :] = sct_scr[:, kt * LANE:(kt + 1) * LANE].T
    q = lax.broadcasted_iota(jnp.int32, (1, LANE), 1) % ts
    limit = ((past + q) // CHUNK + 1) * CHUNK
    _select_bias(sc_scr, key_scr, bias_scr, limit, n_keys, topk)
    for kt in range(s_pad // LANE):
        bias_out_ref[:, kt * LANE:(kt + 1) * LANE] = (
            bias_scr[kt * LANE:(kt + 1) * LANE, :].T.astype(bias_out_ref.dtype))


def _sel_sample(iqp, wrow, ik_past, layer, ikw_rot, row0, nb, ts, past, s_pad):
    n_keys = past + ts
    topk = min(TOPK_MAX, n_keys // 4)
    nbp = LANE // ts
    blk0 = row0 // LANE
    ng = nb // nbp
    return pl.pallas_call(
        functools.partial(_sel_sample_kernel, ts=ts, past=past, topk=topk, n_keys=n_keys,
                          tk=math.gcd(past, 1024)),
        grid=(nb // nbp,),
        in_specs=[pl.BlockSpec((nbp, IDX_HEADS * ts, IDX_DIM), lambda g: (g, 0, 0)),
                  pl.BlockSpec((nbp, 1, IDX_HEADS * ts), lambda g: (g, 0, 0)),
                  pl.BlockSpec((nbp, IDX_DIM, past), lambda g: (layer * ng + g, 0, 0)),
                  pl.BlockSpec((LANE, LANE), lambda g: (blk0 + g, 0))],
        out_specs=pl.BlockSpec((LANE, s_pad), lambda g: (g, 0)),
        out_shape=jax.ShapeDtypeStruct((nb * ts, s_pad), MXU_DTYPE),
        scratch_shapes=[pltpu.VMEM((LANE, s_pad), F32), pltpu.VMEM((s_pad, LANE), F32),
                        pltpu.VMEM((s_pad, LANE), jnp.int32), pltpu.VMEM((s_pad, LANE), F32)],
        compiler_params=_params(("parallel",)),
        name="sel_sample",
    )(iqp, wrow, ik_past, ikw_rot)


def _attend(q_ref, k_ref, v_ref, bias_ref, o_ref, s_eff, tk):
    nq = q_ref.shape[0]
    scale = HEAD_DIM ** -0.5 * math.log2(math.e)
    rows = KV_GROUP * nq
    for g in range(N_KV):
        cols = slice(g * HEAD_DIM, (g + 1) * HEAD_DIM)
        qg = jnp.concatenate(
            [q_ref[:, (g * KV_GROUP + i) * HEAD_DIM:(g * KV_GROUP + i + 1) * HEAD_DIM]
             for i in range(KV_GROUP)], axis=0)
        m = jnp.full((rows, 1), -jnp.inf, F32)
        l = jnp.zeros((rows, 1), F32)
        o = jnp.zeros((rows, HEAD_DIM), F32)
        for r0 in range(0, s_eff, tk):
            r1 = min(r0 + tk, s_eff)
            b = bias_ref[:, r0:r1].astype(F32)
            s = _mm_nt(qg, k_ref[r0:r1, cols]) * scale + jnp.concatenate([b] * KV_GROUP, axis=0)
            m_new = jnp.maximum(m, jnp.max(s, axis=-1, keepdims=True))
            m_safe = jnp.where(m_new == -jnp.inf, 0.0, m_new)
            alpha = jnp.exp2(m - m_safe)
            p = jnp.exp2(s - m_safe)
            l = alpha * l + jnp.sum(p, axis=-1, keepdims=True)
            o = alpha * o + _mm(p.astype(MXU_DTYPE), v_ref[r0:r1, cols])
            m = m_new
        o = o / l
        for i in range(KV_GROUP):
            h = g * KV_GROUP + i
            o_ref[:, h * HEAD_DIM:(h + 1) * HEAD_DIM] = o[i * nq:(i + 1) * nq, :].astype(o_ref.dtype)


def _attn_prompt_kernel(q_ref, k_ref, v_ref, bias_ref, o_ref):
    j = pl.program_id(1)
    t_len = k_ref.shape[0]
    seg = _causal_seg(t_len)

    def run(s_eff):
        _attend(q_ref, k_ref, v_ref, bias_ref, o_ref, s_eff, ATTN_KEY_TILE)

    for v in range(t_len // seg):
        pl.when((j * LANE) // seg == v)(functools.partial(run, (v + 1) * seg))


def _attn_prompt(q_rot, k_bf, v_bf, bias, nb, t):
    nq = t // LANE
    wq = N_HEADS * HEAD_DIM
    wkv = N_KV * HEAD_DIM
    return pl.pallas_call(
        _attn_prompt_kernel,
        grid=(nb, nq),
        in_specs=[pl.BlockSpec((LANE, wq), lambda b, j: (b * nq + j, 0)),
                  pl.BlockSpec((t, wkv), lambda b, j: (b, 0)),
                  pl.BlockSpec((t, wkv), lambda b, j: (b, 0)),
                  pl.BlockSpec((LANE, t), lambda b, j: (b * nq + j, 0))],
        out_specs=pl.BlockSpec((LANE, wq), lambda b, j: (b * nq + j, 0)),
        out_shape=jax.ShapeDtypeStruct((nb * t, wq), MXU_DTYPE),
        compiler_params=_params(("parallel", "arbitrary")),
        name="attn_prompt",
    )(q_rot, k_bf, v_bf, bias)


def _attn_sample_kernel(q_ref, kp_ref, vp_ref, kn_ref, vn_ref, bias_ref, o_ref, k_scr, v_scr,
                        *, ts, past):
    s_pad = k_scr.shape[0]
    wkv = N_KV * HEAD_DIM
    for g in range(N_KV):
        cols = slice(g * HEAD_DIM, (g + 1) * HEAD_DIM)
        k_scr[0:past, cols] = kp_ref[0, pl.ds(g, past, stride=N_KV), :].astype(MXU_DTYPE)
        v_scr[0:past, cols] = vp_ref[0, pl.ds(g, past, stride=N_KV), :].astype(MXU_DTYPE)
    k_scr[past:past + ts, :] = kn_ref[...]
    v_scr[past:past + ts, :] = vn_ref[...]
    k_scr[past + ts:s_pad, :] = jnp.zeros((s_pad - past - ts, wkv), MXU_DTYPE)
    v_scr[past + ts:s_pad, :] = jnp.zeros((s_pad - past - ts, wkv), MXU_DTYPE)
    _attend(q_ref, k_scr, v_scr, bias_ref, o_ref, s_pad, -(-s_pad // (2 * LANE)) * LANE)


def _attn_sample(q_rot, k_past, v_past, layer, k_bf, v_bf, bias, row0, nb, ts, past, s_pad):
    wq = N_HEADS * HEAD_DIM
    wkv = N_KV * HEAD_DIM
    blk0 = row0 // ts
    return pl.pallas_call(
        functools.partial(_attn_sample_kernel, ts=ts, past=past),
        grid=(nb,),
        in_specs=[pl.BlockSpec((ts, wq), lambda b: (blk0 + b, 0)),
                  pl.BlockSpec((1, past * N_KV, HEAD_DIM), lambda b: (layer * nb + b, 0, 0)),
                  pl.BlockSpec((1, past * N_KV, HEAD_DIM), lambda b: (layer * nb + b, 0, 0)),
                  pl.BlockSpec((ts, wkv), lambda b: (blk0 + b, 0)),
                  pl.BlockSpec((ts, wkv), lambda b: (blk0 + b, 0)),
                  pl.BlockSpec((ts, s_pad), lambda b: (b, 0))],
        out_specs=pl.BlockSpec((ts, wq), lambda b: (b, 0)),
        out_shape=jax.ShapeDtypeStruct((nb * ts, wq), MXU_DTYPE),
        scratch_shapes=[pltpu.VMEM((s_pad, wkv), MXU_DTYPE), pltpu.VMEM((s_pad, wkv), MXU_DTYPE)],
        compiler_params=_params(("parallel",)),
        name="attn_sample",
    )(q_rot, k_past, v_past, k_bf, v_bf, bias)


def kernel(x_prompt, x_sample, cache_k, cache_v, cache_idx_k, state_ssm, state_conv, norm_ffn1, ffn1_w1, ffn1_w3, ffn1_w2, norm_mix, w_in, conv_w, conv_b, dt_bias, a_log, d_skip, ssm_norm_w, w_out, norm_ffn2, ffn2_w1, ffn2_w3, ffn2_w2, final_norm):
    bp, tp, d = x_prompt.shape
    bs, ts = x_sample.shape[:2]
    depth, _, past = cache_k.shape[:3]
    mp, ms = bp * tp, bs * ts
    n_keys_s = past + ts
    s_pad = -(-n_keys_s // LANE) * LANE
    hp = SSM_HEADS * SSM_HEADDIM
    wkv = N_KV * HEAD_DIM

    xs = (x_prompt.reshape(mp, d), x_sample.reshape(ms, d))

    tm_prep = _prep_tile(tp, ms, ts)
    pos = jnp.concatenate([jnp.arange(tp, dtype=jnp.int32),
                           jnp.tile(past + jnp.arange(ts, dtype=jnp.int32), tm_prep // ts)])
    iw_scale = jnp.concatenate([jnp.full((IDX_HEADS,), IDX_HEADS ** -0.5, F32),
                                jnp.ones((LANE - IDX_DIM - IDX_HEADS,), F32)])
    tab = jnp.concatenate(_rope_table(pos, ROPE_DIM, HEAD_DIM)
                          + _rope_table(pos, IDX_ROPE_DIM, IDX_DIM)
                          + _rope_table(pos, IDX_ROPE_DIM, IDX_DIM, tail=iw_scale), axis=1)

    w_in_p = _pack_w_in(w_in)
    stack2d = lambda w: w.reshape(w.shape[0] * w.shape[1], w.shape[2])
    ffn_f32 = [tuple(stack2d(w) for w in ws)
               for ws in ((ffn1_w1, ffn1_w3, ffn1_w2), (ffn2_w1, ffn2_w3, ffn2_w2))]
    dff = ffn1_w1.shape[2]
    w_ffn = (_cast_weights(ffn_f32[0][0], d), _cast_weights(ffn_f32[0][1], d),
             _cast_weights(ffn_f32[0][2], dff))
    w_out_b = _cast_weights(stack2d(w_out), depth * w_out.shape[1])
    k_cache = cache_k.reshape(depth * bs, past * N_KV, HEAD_DIM)
    v_cache = cache_v.reshape(depth * bs, past * N_KV, HEAD_DIM)
    ik_cache = jnp.swapaxes(cache_idx_k, 2, 3).reshape(depth * bs, IDX_DIM, past)
    ssm0 = state_ssm.reshape(depth * bs, hp, SSM_STATE)
    conv0_pad = jnp.pad(state_conv, ((0, 0), (0, 0), (SUBLANE - (CONV_W - 1), 0), (0, 0)))
    conv0_pad = conv0_pad.reshape(depth * bs, SUBLANE, CONV_DIM)
    lane_pad = lambda a: jnp.pad(a, (0, LANE - a.shape[0]))[None, :]

    outs = {k: [] for k in ("ikp", "cp", "iks", "cs")}
    kv_out = ssm_p = ssm_s = None
    for l in range(depth):
        x, w_ffn = _ffn(xs if l == 0 else (x,), norm_ffn1[l][None, :], w_ffn, (*ffn_f32[1], l))
        proj = _in_proj(x, norm_mix[l][None, :], w_in_p, l)
        (q_rot, iq_rot, k_bf, v_bf, ikw_rot), kv_out = _prep(
            proj, tab, mp, tp, tm_prep, l, depth, kv_out)

        consts = (conv_w[l], conv_b[l][None, :], lane_pad(dt_bias[l]),
                  lane_pad(-jnp.exp(a_log[l])), jnp.repeat(d_skip[l], SSM_HEADDIM)[None, :],
                  ssm_norm_w[l][None, :])
        y_p, ssm_p, conv_p = _ssd(proj, 0, bp, tp, LANE, LANE, None, consts, l, depth, ssm_p)
        y_s, ssm_s, conv_s = _ssd(proj, mp, bs, ts, ts, LANE, (ssm0, conv0_pad), consts,
                                  l, depth, ssm_s)

        bias_p = _sel_prompt(iq_rot, ikw_rot, bp, tp)
        att_p = _attn_prompt(q_rot, k_bf, v_bf, bias_p, bp, tp)

        iqp = iq_rot[mp:].reshape(bs, ts, IDX_HEADS, IDX_DIM).transpose(0, 2, 1, 3)
        iqp = iqp.reshape(bs, IDX_HEADS * ts, IDX_DIM)
        wrow = ikw_rot[mp:, IDX_DIM:IDX_DIM + IDX_HEADS].reshape(bs, ts, IDX_HEADS)
        wrow = wrow.transpose(0, 2, 1).reshape(bs, 1, IDX_HEADS * ts)
        bias_s = _sel_sample(iqp, wrow, ik_cache, l, ikw_rot, mp, bs, ts, past, s_pad)
        att_s = _attn_sample(q_rot, k_cache, v_cache, l, k_bf, v_bf, bias_s,
                             mp, bs, ts, past, s_pad)

        x = _out_proj(x, (y_p, y_s), (att_p, att_s), w_out_b, l)
        x, w_ffn = _ffn((x,), norm_ffn2[l][None, :], w_ffn,
                        (*ffn_f32[0], l + 1) if l + 1 < depth else None)

        outs["ikp"].append(ikw_rot[:mp, :IDX_DIM].reshape(bp, tp, IDX_DIM))
        outs["cp"].append(conv_p[:, SUBLANE - (CONV_W - 1):, :])
        outs["iks"].append(ikw_rot[mp:, :IDX_DIM].reshape(bs, ts, IDX_DIM))
        outs["cs"].append(conv_s[:, SUBLANE - (CONV_W - 1):, :])

    y_p, y_s = _final_norm(x, final_norm[None, :], mp, ms)
    st = {k: jnp.stack(v) for k, v in outs.items()}
    kp, ks, vp, vs = kv_out
    kv_p = lambda a: a.reshape(depth, bp, tp, N_KV, HEAD_DIM)
    kv_s = lambda a: a.reshape(depth, bs, ts, N_KV, HEAD_DIM)
    return (y_p.reshape(bp, tp, d), y_s.reshape(bs, ts, d),
            kv_p(kp), kv_p(vp), st["ikp"],
            ssm_p.reshape(depth, bp, SSM_HEADS, SSM_HEADDIM, SSM_STATE), st["cp"],
            kv_s(ks), kv_s(vs), st["iks"],
            ssm_s.reshape(depth, bs, SSM_HEADS, SSM_HEADDIM, SSM_STATE), st["cs"])
```

```python
import functools
import math

import jax
import jax.numpy as jnp
from jax import lax
from jax.experimental import pallas as pl
from jax.experimental.pallas import tpu as pltpu

F32 = jnp.float32
MXU_DTYPE = jnp.bfloat16

D_MODEL = 2048
CHUNK = 64
D_SSM = 1024
SSM_HEADDIM = 64
SSM_HEADS = 16
SSM_GROUPS = 2
SSM_STATE = 128
CONV_W = 4
CONV_DIM = D_SSM + 2 * SSM_GROUPS * SSM_STATE
HEAD_DIM = 128
N_HEADS = 8
N_KV = 2
KV_GROUP = N_HEADS // N_KV
ROPE_DIM = HEAD_DIM // 4
IDX_HEADS = 16
IDX_DIM = 64
IDX_ROPE_DIM = IDX_DIM // 4
TOPK_MAX = 256
ROPE_THETA = 500000.0
D_FF = 5632
EPS = 1e-6

LANE = 128
SUBLANE = 8
COUNT_TILE = 128
ATTN_KEY_TILE = 512
VMEM_LIMIT = 56 * 1024 * 1024

COL_Z = 0
COL_Q = 1024
COL_IQ = 2048
COL_XBC = 3072
COL_K = 4608
COL_V = 4864
COL_DT = 5120
COL_IKW = 5248
PROJ_COLS = 5376

KEY_NEG_INF = -2139095041
KEY_POS_INF = 2139095040
INT_MIN = -2147483648

NT_DIMS = (((1,), (1,)), ((), ()))


def _mm(a, b):
    return jnp.dot(a, b, preferred_element_type=F32)


def _mm_nt(a, b):
    return lax.dot_general(a, b, NT_DIMS, preferred_element_type=F32)


def _split3(x):
    hi = x.astype(MXU_DTYPE)
    r = x - hi.astype(F32)
    mid = r.astype(MXU_DTYPE)
    lo = (r - mid.astype(F32)).astype(MXU_DTYPE)
    return hi, mid, lo


def _mm_exact(a, b, exact):
    if exact == "a":
        a01 = a.astype(MXU_DTYPE)
        return sum(_mm(a01, p) for p in _split3(b))
    b01 = b.astype(MXU_DTYPE)
    return sum(_mm(p, b01) for p in _split3(a))


def _silu(x):
    return x * jax.nn.sigmoid(x)


def _params(sem, vmem=VMEM_LIMIT):
    return pltpu.CompilerParams(dimension_semantics=sem, vmem_limit_bytes=vmem)


def _row_tile(m, pref):
    t = pref
    while m % t:
        t //= 2
    return t


def _split_rows(tm, rows_a, rows_b):
    assert rows_a % tm == 0 and rows_b % tm == 0
    na = rows_a // tm

    def spec_a(width, *grid_rest):
        return pl.BlockSpec((tm, width), lambda i, *_: (jnp.minimum(i, na - 1), 0))

    def spec_b(width, *grid_rest):
        return pl.BlockSpec((tm, width), lambda i, *_: (jnp.maximum(i - na, 0), 0))

    return na, spec_a, spec_b


def _pick(i, na, a_ref, b_ref):
    return jnp.where(i < na, a_ref[...], b_ref[...])


def _ffn_kernel(*refs, na, side):
    refs = list(refs)
    if na is None:
        x_ref = refs.pop(0)
        load_x = lambda: x_ref[...]
    else:
        xa_ref, xb_ref = refs.pop(0), refs.pop(0)
        load_x = lambda: _pick(pl.program_id(0), na, xa_ref, xb_ref)
    nw_ref, w1_ref, w3_ref, w2_ref = refs[:4]
    refs = refs[4:]
    if side:
        s1_ref, s3_ref, s2_ref, o_ref, c1_ref, c3_ref, c2_ref, h_scr = refs
    else:
        o_ref, h_scr = refs
    f = pl.program_id(1)

    @pl.when(f == 0)
    def _():
        x = load_x()
        ms = jnp.mean(x * x, axis=-1, keepdims=True)
        h_scr[...] = (x * lax.rsqrt(ms + EPS) * nw_ref[...]).astype(h_scr.dtype)
        o_ref[...] = jnp.zeros_like(o_ref)

    h = h_scr[...]
    a = _mm(h, w1_ref[...])
    b = _mm(h, w3_ref[...])
    g = (_silu(a) * b).astype(h_scr.dtype)
    o_ref[...] += _mm(g, w2_ref[...])

    if side:
        c1_ref[...] = s1_ref[...].astype(c1_ref.dtype)
        c3_ref[...] = s3_ref[...].astype(c3_ref.dtype)
        c2_ref[...] = s2_ref[...].astype(c2_ref.dtype)

    @pl.when(f == pl.num_programs(1) - 1)
    def _():
        o_ref[...] = load_x() + 0.5 * o_ref[...]


def _ffn_row_tile(rows):
    g = math.gcd(*rows)
    for tm in (768, 512, 256, 128):
        if g % tm == 0:
            return tm
    raise ValueError(rows)


def _chunk_rows(total, steps):
    tile = 2 * SUBLANE
    for r in range(tile, total + 1, tile):
        if total % r == 0 and total // r <= steps:
            return r
    raise ValueError((total, steps))


def _ffn(xs, nw, w, nxt=None, tf=512):
    w1, w3, w2 = w
    d = xs[0].shape[1]
    m = sum(x.shape[0] for x in xs)
    dff = w1.shape[1]
    nf = dff // tf
    tm = _ffn_row_tile([x.shape[0] for x in xs])
    if len(xs) == 1:
        na, x_specs = None, [pl.BlockSpec((tm, d), lambda i, f: (i, 0))]
    else:
        na, spec_a, spec_b = _split_rows(tm, xs[0].shape[0], xs[1].shape[0])
        x_specs = [spec_a(d), spec_b(d)]
    in_specs = x_specs + [
        pl.BlockSpec((1, d), lambda i, f: (0, 0)),
        pl.BlockSpec((d, tf), lambda i, f: (0, f)),
        pl.BlockSpec((d, tf), lambda i, f: (0, f)),
        pl.BlockSpec((tf, d), lambda i, f: (f, 0)),
    ]
    out_specs = [pl.BlockSpec((tm, d), lambda i, f: (i, 0))]
    out_shape = [jax.ShapeDtypeStruct((m, d), F32)]
    args = [*xs, nw, w1, w3, w2]
    if nxt is not None:
        s1, s3, s2, layer = nxt
        steps = (m // tm) * nf
        r1, r2 = _chunk_rows(d, steps), _chunk_rows(dff, steps)
        n1, n2 = d // r1, dff // r2

        def chunk(n, base):
            return lambda i, f: (base + jnp.minimum(i * nf + f, n - 1), 0)

        in_specs += [pl.BlockSpec((r1, dff), chunk(n1, layer * n1)),
                     pl.BlockSpec((r1, dff), chunk(n1, layer * n1)),
                     pl.BlockSpec((r2, d), chunk(n2, layer * n2))]
        out_specs += [pl.BlockSpec((r1, dff), chunk(n1, 0)), pl.BlockSpec((r1, dff), chunk(n1, 0)),
                      pl.BlockSpec((r2, d), chunk(n2, 0))]
        out_shape += [jax.ShapeDtypeStruct((d, dff), MXU_DTYPE),
                      jax.ShapeDtypeStruct((d, dff), MXU_DTYPE),
                      jax.ShapeDtypeStruct((dff, d), MXU_DTYPE)]
        args += [s1, s3, s2]
    res = pl.pallas_call(
        functools.partial(_ffn_kernel, na=na, side=nxt is not None),
        grid=(m // tm, nf),
        in_specs=in_specs,
        out_specs=out_specs,
        out_shape=out_shape,
        scratch_shapes=[pltpu.VMEM((tm, d), MXU_DTYPE)],
        compiler_params=_params(("arbitrary", "arbitrary")),
        name="ffn",
    )(*args)
    return res[0], tuple(res[1:])


def _in_proj_kernel(x_ref, nw_ref, w_ref, o_ref, h_scr):
    @pl.when(pl.program_id(1) == 0)
    def _():
        x = x_ref[...]
        ms = jnp.mean(x * x, axis=-1, keepdims=True)
        h_scr[...] = (x * lax.rsqrt(ms + EPS) * nw_ref[...]).astype(h_scr.dtype)

    o_ref[...] = _mm_nt(h_scr[...], w_ref[...])


def _in_proj(x, nw, w, layer, tm=1024, tn=1792):
    m, d = x.shape
    n = PROJ_COLS
    nn = n // tn
    tm = _row_tile(m, tm)
    return pl.pallas_call(
        _in_proj_kernel,
        grid=(m // tm, nn),
        in_specs=[
            pl.BlockSpec((tm, d), lambda i, j: (i, 0)),
            pl.BlockSpec((1, d), lambda i, j: (0, 0)),
            pl.BlockSpec((tn, d), lambda i, j: (layer * nn + j, 0)),
        ],
        out_specs=pl.BlockSpec((tm, tn), lambda i, j: (i, j)),
        out_shape=jax.ShapeDtypeStruct((m, n), F32),
        scratch_shapes=[pltpu.VMEM((tm, d), MXU_DTYPE)],
        compiler_params=_params(("parallel", "arbitrary")),
        name="in_proj",
    )(x, nw, w)


def _out_proj_kernel(x_ref, ya_ref, yb_ref, aa_ref, ab_ref, w_ref, o_ref, *, na):
    i = pl.program_id(0)
    half = ya_ref.shape[1]
    o_ref[...] = (x_ref[...] + _mm(_pick(i, na, ya_ref, yb_ref), w_ref[:half, :])
                  + _mm(_pick(i, na, aa_ref, ab_ref), w_ref[half:, :]))


def _out_proj(x, y_pair, att_pair, w, layer, tm=512):
    m, d = x.shape
    half = y_pair[0].shape[1]
    tm = _row_tile(math.gcd(y_pair[0].shape[0], y_pair[1].shape[0]), tm)
    na, spec_a, spec_b = _split_rows(tm, y_pair[0].shape[0], y_pair[1].shape[0])
    return pl.pallas_call(
        functools.partial(_out_proj_kernel, na=na),
        grid=(m // tm,),
        in_specs=[
            pl.BlockSpec((tm, d), lambda i: (i, 0)),
            spec_a(half), spec_b(half), spec_a(half), spec_b(half),
            pl.BlockSpec((2 * half, d), lambda i: (layer, 0)),
        ],
        out_specs=pl.BlockSpec((tm, d), lambda i: (i, 0)),
        out_shape=jax.ShapeDtypeStruct((m, d), F32),
        compiler_params=_params(("arbitrary",)),
        name="out_proj",
    )(x, *y_pair, *att_pair, w)


def _final_norm_kernel(x_ref, nw_ref, oa_ref, ob_ref, *, na):
    i = pl.program_id(0)
    x = x_ref[...]
    ms = jnp.mean(x * x, axis=-1, keepdims=True)
    y = x * lax.rsqrt(ms + EPS) * nw_ref[...]

    @pl.when(i < na)
    def _():
        oa_ref[...] = y

    @pl.when(i >= na)
    def _():
        ob_ref[...] = y


def _final_norm(x, nw, rows_a, rows_b, tm=512):
    d = x.shape[1]
    tm = _row_tile(math.gcd(rows_a, rows_b), tm)
    na, spec_a, spec_b = _split_rows(tm, rows_a, rows_b)
    return pl.pallas_call(
        functools.partial(_final_norm_kernel, na=na),
        grid=((rows_a + rows_b) // tm,),
        in_specs=[pl.BlockSpec((tm, d), lambda i: (i, 0)),
                  pl.BlockSpec((1, d), lambda i: (0, 0))],
        out_specs=[spec_a(d), spec_b(d)],
        out_shape=[jax.ShapeDtypeStruct((rows_a, d), F32),
                   jax.ShapeDtypeStruct((rows_b, d), F32)],
        compiler_params=_params(("arbitrary",)),
        name="final_norm",
    )(x, nw)


def _cast_kernel(x_ref, o_ref):
    o_ref[...] = x_ref[...].astype(o_ref.dtype)


def _cast_weights(w2, n_rows, row0=0, rows=512):
    c = w2.shape[1]
    rows = _row_tile(math.gcd(n_rows, row0) if row0 else n_rows, rows)
    blk0 = row0 // rows
    return pl.pallas_call(
        _cast_kernel,
        grid=(n_rows // rows,),
        in_specs=[pl.BlockSpec((rows, c), lambda i: (blk0 + i, 0))],
        out_specs=pl.BlockSpec((rows, c), lambda i: (i, 0)),
        out_shape=jax.ShapeDtypeStruct((n_rows, c), MXU_DTYPE),
        compiler_params=_params(("parallel",)),
        name="cast_w",
    )(w2)


_W_IN_PARTS = (
    (COL_Z, 0, D_SSM),
    (COL_XBC, D_SSM, CONV_DIM),
    (COL_DT, D_SSM + CONV_DIM, SSM_HEADS),
    (COL_Q, D_SSM + CONV_DIM + SSM_HEADS, N_HEADS * HEAD_DIM),
    (COL_K, D_SSM + CONV_DIM + SSM_HEADS + N_HEADS * HEAD_DIM, N_KV * HEAD_DIM),
    (COL_V, D_SSM + CONV_DIM + SSM_HEADS + (N_HEADS + N_KV) * HEAD_DIM, N_KV * HEAD_DIM),
    (COL_IQ, D_SSM + CONV_DIM + SSM_HEADS + (N_HEADS + 2 * N_KV) * HEAD_DIM, IDX_HEADS * IDX_DIM),
    (COL_IKW, D_SSM + CONV_DIM + SSM_HEADS + (N_HEADS + 2 * N_KV) * HEAD_DIM + IDX_HEADS * IDX_DIM,
     IDX_DIM + IDX_HEADS),
)


def _pack_w_in_kernel(w_ref, o_ref):
    cols = o_ref.shape[1]
    for dst, src, width in _W_IN_PARTS:
        o_ref[dst:dst + width, :] = w_ref[src:src + width, :].astype(o_ref.dtype)
        pad = -width % LANE
        if pad:
            o_ref[dst + width:dst + width + pad, :] = jnp.zeros((pad, cols), o_ref.dtype)


def _pack_w_in(w, kc=256):
    depth, d, c = w.shape
    wt = jnp.transpose(w, (0, 2, 1)).reshape(depth * c, d)
    return pl.pallas_call(
        _pack_w_in_kernel,
        grid=(depth, d // kc),
        in_specs=[pl.BlockSpec((c, kc), lambda l, k: (l, k))],
        out_specs=pl.BlockSpec((PROJ_COLS, kc), lambda l, k: (l, k)),
        out_shape=jax.ShapeDtypeStruct((depth * PROJ_COLS, d), MXU_DTYPE),
        compiler_params=_params(("parallel", "parallel")),
        name="pack_w_in",
    )(wt)


def _rope_lanes(x, c, s1, s2, half):
    outs = []
    for t in range(x.shape[1] // LANE):
        xt = x[:, t * LANE:(t + 1) * LANE]
        outs.append(xt * c + pltpu.roll(xt, half, 1) * s1
                    + pltpu.roll(xt, LANE - half, 1) * s2)
    return outs[0] if len(outs) == 1 else jnp.concatenate(outs, axis=1)


def _prep_kernel(*refs, na, aliased):
    q_ref, iq_ref, k_ref, v_ref, ikw_ref, tab_ref = refs[:6]
    refs = refs[6 + (4 if aliased else 0):]
    (qo_ref, iqo_ref, kbo_ref, vbo_ref, ikwo_ref, kp_ref, ks_ref, vp_ref, vs_ref) = refs
    i = pl.program_id(0)
    tm = k_ref.shape[0]
    tab = [tab_ref[:, j * LANE:(j + 1) * LANE] for j in range(9)]
    qo_ref[...] = _rope_lanes(q_ref[...], tab[0], tab[1], tab[2], ROPE_DIM // 2).astype(qo_ref.dtype)
    k = _rope_lanes(k_ref[...], tab[0], tab[1], tab[2], ROPE_DIM // 2)
    kbo_ref[...] = k.astype(kbo_ref.dtype)
    iqo_ref[...] = _rope_lanes(iq_ref[...], tab[3], tab[4], tab[5], IDX_ROPE_DIM // 2).astype(iqo_ref.dtype)
    ikwo_ref[...] = _rope_lanes(ikw_ref[...], tab[6], tab[7], tab[8], IDX_ROPE_DIM // 2)
    v = v_ref[...]
    vbo_ref[...] = v.astype(vbo_ref.dtype)

    def put(dst_k, dst_v):
        for g in range(N_KV):
            dst_k[pl.ds(g, tm, stride=N_KV), :] = k[:, g * HEAD_DIM:(g + 1) * HEAD_DIM]
            dst_v[pl.ds(g, tm, stride=N_KV), :] = v[:, g * HEAD_DIM:(g + 1) * HEAD_DIM]

    pl.when(i < na)(functools.partial(put, kp_ref, vp_ref))
    pl.when(i >= na)(functools.partial(put, ks_ref, vs_ref))


def _prep_tile(tp, ms, ts, tm=512):
    tm = _row_tile(math.gcd(tp, ms), tm)
    assert tm % ts == 0
    return tm


def _prep(proj, tab, mp, tp, tm, layer, depth, caches):
    m = proj.shape[0]
    ms = m - mp
    wq = N_HEADS * HEAD_DIM
    wkv = N_KV * HEAD_DIM
    na = mp // tm
    row = lambda w, c: pl.BlockSpec((tm, w), lambda i: (i, c))
    tab_spec = pl.BlockSpec(
        (tm, 9 * LANE), lambda i: (jnp.where(i < na, i % (tp // tm), tp // tm), 0))
    p_spec = pl.BlockSpec((N_KV * tm, HEAD_DIM), lambda i: (layer * na + jnp.minimum(i, na - 1), 0))
    s_spec = pl.BlockSpec((N_KV * tm, HEAD_DIM),
                          lambda i: (layer * (ms // tm) + jnp.maximum(i - na, 0), 0))
    p_shape = jax.ShapeDtypeStruct((depth * mp * N_KV, HEAD_DIM), F32)
    s_shape = jax.ShapeDtypeStruct((depth * ms * N_KV, HEAD_DIM), F32)
    in_specs = [row(wq, COL_Q // wq), row(wq, COL_IQ // wq), row(wkv, COL_K // wkv),
                row(wkv, COL_V // wkv), row(LANE, COL_IKW // LANE), tab_spec]
    args = [proj, proj, proj, proj, proj, tab]
    aliases = {}
    if caches is not None:
        in_specs += [pl.BlockSpec(memory_space=pl.ANY)] * 4
        aliases = {len(args) + j: 5 + j for j in range(4)}
        args += list(caches)
    res = pl.pallas_call(
        functools.partial(_prep_kernel, na=na, aliased=caches is not None),
        grid=(m // tm,),
        in_specs=in_specs,
        out_specs=[row(wq, 0), row(wq, 0), row(wkv, 0), row(wkv, 0), row(LANE, 0),
                   p_spec, s_spec, p_spec, s_spec],
        out_shape=[jax.ShapeDtypeStruct((m, wq), MXU_DTYPE),
                   jax.ShapeDtypeStruct((m, wq), MXU_DTYPE),
                   jax.ShapeDtypeStruct((m, wkv), MXU_DTYPE),
                   jax.ShapeDtypeStruct((m, wkv), MXU_DTYPE),
                   jax.ShapeDtypeStruct((m, LANE), F32),
                   p_shape, s_shape, p_shape, s_shape],
        input_output_aliases=aliases,
        compiler_params=_params(("arbitrary",)),
        name="prep",
    )(*args)
    return res[:5], tuple(res[5:])


def _rope_table(pos, rot_dim, period, tail=None):
    half = rot_dim // 2
    inv_freq = jnp.float32(ROPE_THETA) ** (-jnp.arange(half, dtype=F32) * 2.0 / rot_dim)
    ang = pos.astype(F32)[:, None] * inv_freq[None, :]
    cos, sin = jnp.cos(ang), jnp.sin(ang)
    n = pos.shape[0]
    ones = jnp.ones((n, period - rot_dim), F32)
    zeros = jnp.zeros((n, period - rot_dim), F32)
    zh = jnp.zeros((n, half), F32)
    c = jnp.concatenate([cos, cos, ones], axis=1)
    s1 = jnp.concatenate([zh, sin, zeros], axis=1)
    s2 = jnp.concatenate([-sin, zh, zeros], axis=1)
    if tail is None:
        reps = LANE // period
        return [jnp.tile(a, (1, reps)) for a in (c, s1, s2)]
    zt = jnp.zeros((n, LANE - period), F32)
    return [jnp.concatenate([c, jnp.broadcast_to(tail[None, :], (n, LANE - period))], axis=1),
            jnp.concatenate([s1, zt], axis=1), jnp.concatenate([s2, zt], axis=1)]


def _ssd_kernel(*refs, qv, qp, has_state, aliased):
    refs = list(refs)
    z_ref, xbc_ref, dt_ref = refs[:3]
    refs = refs[3:]
    if has_state:
        ssm0_ref, conv0_ref = refs[:2]
        refs = refs[2:]
    cw_ref, cb_ref, dtb_ref, aneg_ref, dsk_ref, nw_ref = refs[:6]
    refs = refs[6 + (1 if aliased else 0):]
    y_ref, ssm_ref, conv_ref, ext_scr, st_scr = refs
    c = pl.program_id(1)
    hp = SSM_HEADS * SSM_HEADDIM
    gw = SSM_STATE
    hpg = hp // SSM_GROUPS

    @pl.when(c == 0)
    def _():
        if has_state:
            ext_scr[0:SUBLANE, :] = conv0_ref[0]
            st_scr[...] = ssm0_ref[0].T
        else:
            ext_scr[0:SUBLANE, :] = jnp.zeros((SUBLANE, CONV_DIM), F32)
            st_scr[...] = jnp.zeros_like(st_scr)

    xbc_raw = xbc_ref[...]
    ext_scr[SUBLANE:SUBLANE + qv, :] = xbc_raw
    conv = cb_ref[...] + xbc_raw * cw_ref[CONV_W - 1:CONV_W, :]
    for j in range(CONV_W - 1):
        off = SUBLANE - (CONV_W - 1) + j
        conv = conv + ext_scr[off:off + qv, :] * cw_ref[j:j + 1, :]
    xc = _silu(conv)
    dtr = dt_ref[...]
    if qv < qp:
        xc = jnp.concatenate([xc, jnp.zeros((qp - qv, CONV_DIM), F32)], axis=0)
        dtr = jnp.concatenate([dtr, jnp.zeros((qp - qv, LANE), F32)], axis=0)
    xs = xc[:, :hp]
    bm = xc[:, hp:hp + SSM_GROUPS * gw]
    cm = xc[:, hp + SSM_GROUPS * gw:]

    xdt_in = dtr + dtb_ref[...]
    dt = jnp.maximum(xdt_in, 0.0) + jnp.log1p(jnp.exp(-jnp.abs(xdt_in)))
    row = lax.broadcasted_iota(jnp.int32, (qp, LANE), 0)
    dt = jnp.where(row < qv, dt, 0.0)
    la = dt * aneg_ref[...]
    ti = lax.broadcasted_iota(jnp.int32, (qp, qp), 0)
    si = lax.broadcasted_iota(jnp.int32, (qp, qp), 1)
    tri = ti >= si
    cum = _mm_exact(tri.astype(F32), la, "a")
    cum_t = cum.T

    eh = lax.broadcasted_iota(jnp.int32, (LANE, hp), 0)
    ec = lax.broadcasted_iota(jnp.int32, (LANE, hp), 1)
    expand = (ec // SSM_HEADDIM == eh).astype(F32)
    dt_x = _mm_exact(dt, expand, "b")
    cum_x = _mm_exact(cum, expand, "b")
    xdt = xs * dt_x

    cb = [_mm_nt(cm[:, g * gw:(g + 1) * gw].astype(MXU_DTYPE),
                 bm[:, g * gw:(g + 1) * gw].astype(MXU_DTYPE)) for g in range(SSM_GROUPS)]
    lane = lax.broadcasted_iota(jnp.int32, (qp, LANE), 1)
    y_parts = []
    for j in range(SSM_HEADS // 2):
        ms = []
        for h in (2 * j, 2 * j + 1):
            d = cum[:, h:h + 1] - cum_t[h:h + 1, :]
            seg = jnp.exp(jnp.where(tri, d, -jnp.inf))
            ms.append((cb[h // (SSM_HEADS // SSM_GROUPS)] * seg).astype(MXU_DTYPE))
        xp = xdt[:, j * LANE:(j + 1) * LANE]
        rhs = jnp.concatenate([jnp.where(lane < SSM_HEADDIM, xp, 0.0),
                               jnp.where(lane >= SSM_HEADDIM, xp, 0.0)], axis=0)
        y_parts.append(_mm(jnp.concatenate(ms, axis=1), rhs.astype(MXU_DTYPE)))
    y = jnp.concatenate(y_parts, axis=1)

    st = st_scr[...]
    y_st = jnp.concatenate(
        [_mm(cm[:, g * gw:(g + 1) * gw].astype(MXU_DTYPE),
             st[:, g * hpg:(g + 1) * hpg].astype(MXU_DTYPE)) for g in range(SSM_GROUPS)], axis=1)
    y = y + y_st * jnp.exp(cum_x) + dsk_ref[...] * xs

    cum_last = cum_x[qp - 1:qp, :]
    xt = (xdt * jnp.exp(cum_last - cum_x)).astype(MXU_DTYPE)
    upd = jnp.concatenate(
        [_mm(bm[:, g * gw:(g + 1) * gw].T.astype(MXU_DTYPE), xt[:, g * hpg:(g + 1) * hpg])
         for g in range(SSM_GROUPS)], axis=1)
    st_new = st * jnp.exp(cum_last) + upd
    st_scr[...] = st_new

    zz = z_ref[...]
    gated = y[:qv, :] * _silu(zz)
    ms2 = jnp.mean(gated * gated, axis=-1, keepdims=True)
    y_ref[...] = (gated * lax.rsqrt(ms2 + EPS) * nw_ref[...]).astype(y_ref.dtype)

    tail_rows = ext_scr[qv:qv + SUBLANE, :]
    ext_scr[0:SUBLANE, :] = tail_rows

    @pl.when(c == pl.num_programs(1) - 1)
    def _():
        ssm_ref[0] = st_new.T
        conv_ref[0] = tail_rows


def _ssd(proj, row0, nb, t, qv, qp, state, consts, layer, depth, ssm_buf):
    nc = t // qv
    hp = SSM_HEADS * SSM_HEADDIM
    blk0 = row0 // qv
    rowspec = lambda w, cb_: pl.BlockSpec((qv, w), lambda b, c: (blk0 + b * nc + c, cb_))
    const = lambda a: pl.BlockSpec(a.shape, lambda b, c: (0,) * a.ndim)
    in_specs = [rowspec(hp, COL_Z // hp), rowspec(CONV_DIM, COL_XBC // CONV_DIM),
                rowspec(LANE, COL_DT // LANE)]
    args = [proj, proj, proj]
    if state is not None:
        ssm0, conv0 = state
        in_specs += [pl.BlockSpec((1, hp, SSM_STATE), lambda b, c: (layer * nb + b, 0, 0)),
                     pl.BlockSpec((1, SUBLANE, CONV_DIM), lambda b, c: (layer * nb + b, 0, 0))]
        args += [ssm0, conv0]
    in_specs += [const(a) for a in consts]
    args += list(consts)
    aliases = {}
    if ssm_buf is not None:
        in_specs.append(pl.BlockSpec(memory_space=pl.ANY))
        aliases = {len(args): 1}
        args.append(ssm_buf)
    return pl.pallas_call(
        functools.partial(_ssd_kernel, qv=qv, qp=qp, has_state=state is not None,
                          aliased=ssm_buf is not None),
        grid=(nb, nc),
        in_specs=in_specs,
        out_specs=[pl.BlockSpec((qv, hp), lambda b, c: (b * nc + c, 0)),
                   pl.BlockSpec((1, hp, SSM_STATE), lambda b, c: (layer * nb + b, 0, 0)),
                   pl.BlockSpec((1, SUBLANE, CONV_DIM), lambda b, c: (b, 0, 0))],
        out_shape=[jax.ShapeDtypeStruct((nb * t, hp), MXU_DTYPE),
                   jax.ShapeDtypeStruct((depth * nb, hp, SSM_STATE), F32),
                   jax.ShapeDtypeStruct((nb, SUBLANE, CONV_DIM), F32)],
        scratch_shapes=[pltpu.VMEM((qv + SUBLANE, CONV_DIM), F32),
                        pltpu.VMEM((SSM_STATE, hp), F32)],
        input_output_aliases=aliases,
        compiler_params=_params(("arbitrary", "arbitrary")),
        name="ssd_state" if state is not None else "ssd_prompt",
    )(*args)


def _row_tiles(n_rows):
    assert n_rows % COUNT_TILE == 0
    return [(r0, r0 + COUNT_TILE) for r0 in range(0, n_rows, COUNT_TILE)]


def _row_iota(r0, r1):
    return lax.broadcasted_iota(jnp.int32, (r1 - r0, LANE), 0) + r0


def _count_rows(fn, n_rows):
    acc = None
    for r0, r1 in _row_tiles(n_rows):
        part = fn(r0, r1).astype(F32)
        acc = part if acc is None else acc + part
    return jnp.sum(acc, axis=0, keepdims=True)


def _select_bias(sc_ref, key_ref, bias_ref, limit, n_keys, topk):
    s_pad = sc_ref.shape[0]
    tiles = _row_tiles(s_pad)
    for r0, r1 in tiles:
        s_io = _row_iota(r0, r1)
        adm = (s_io < limit) & (s_io < n_keys)
        bits = pltpu.bitcast(jnp.where(adm, sc_ref[r0:r1, :], -jnp.inf), jnp.int32)
        key_ref[r0:r1, :] = bits ^ ((bits >> 31) & jnp.int32(0x7FFFFFFF))
    kf = jnp.float32(topk)

    cnt0 = _count_rows(lambda r0, r1: key_ref[r0:r1, :] >= 0, s_pad)
    pos = cnt0 >= kf
    prefix = jnp.where(pos, jnp.int32(0), jnp.int32(INT_MIN))
    n_ge = jnp.where(pos, cnt0, jnp.float32(s_pad))

    def body(i, carry):
        prefix, n_ge = carry
        cand = prefix | jnp.left_shift(jnp.int32(1), 30 - i)
        cnt = _count_rows(lambda r0, r1: key_ref[r0:r1, :] >= cand, s_pad)
        take = cnt >= kf
        return jnp.where(take, cand, prefix), jnp.where(take, cnt, n_ge)

    thr, n_ge = lax.fori_loop(0, 31, body, (prefix, n_ge))

    def finite(key):
        return (key > KEY_NEG_INF) & (key < KEY_POS_INF)

    for r0, r1 in tiles:
        key = key_ref[r0:r1, :]
        bias_ref[r0:r1, :] = jnp.where((key >= thr) & finite(key), 0.0, -jnp.inf)

    tie = (n_ge > kf) & (thr > KEY_NEG_INF)
    any_tie = jnp.max(tie.astype(F32)) > 0.0

    @pl.when(any_tie)
    def _():
        need = kf - _count_rows(lambda r0, r1: key_ref[r0:r1, :] > thr, s_pad)
        nbits = max(1, (s_pad - 1).bit_length())

        def ibody(i, v):
            cand = v | jnp.left_shift(jnp.int32(1), nbits - 1 - i)
            below = _count_rows(
                lambda r0, r1: (key_ref[r0:r1, :] == thr) & (_row_iota(r0, r1) < cand), s_pad)
            return jnp.where(below < need, cand, v)

        last = lax.fori_loop(0, nbits, ibody, jnp.zeros((1, LANE), jnp.int32))
        for r0, r1 in tiles:
            key = key_ref[r0:r1, :]
            sel = (key > thr) | ((key == thr) & (_row_iota(r0, r1) <= last))
            bias_ref[r0:r1, :] = jnp.where(sel & finite(key), 0.0, -jnp.inf)


def _sel_prompt_kernel(iq_ref, ikw_q_ref, ikw_all_ref, bias_out_ref, sc_scr, key_scr, bias_scr,
                       *, topk, tk):
    j = pl.program_id(1)
    t_len = ikw_all_ref.shape[0]
    seg = _causal_seg(t_len)
    pos = j * LANE + lax.broadcasted_iota(jnp.int32, (1, LANE), 1)
    limit = (pos // CHUNK + 1) * CHUNK

    def run(s_eff):
        w_t = ikw_q_ref[...].T
        iq = iq_ref[...]
        for kt in range(s_eff // tk):
            ik = ikw_all_ref[kt * tk:(kt + 1) * tk, 0:IDX_DIM].astype(MXU_DTYPE)
            acc = jnp.zeros((tk, LANE), F32)
            for h in range(IDX_HEADS):
                lt = _mm_nt(ik, iq[:, h * IDX_DIM:(h + 1) * IDX_DIM])
                acc = acc + jnp.maximum(lt, 0.0) * w_t[IDX_DIM + h:IDX_DIM + h + 1, :]
            sc_scr[kt * tk:(kt + 1) * tk, :] = acc
        if s_eff == seg:
            few = (j + 1) * LANE <= topk

            @pl.when(few)
            def _():
                for r0, r1 in _row_tiles(s_eff):
                    sc = sc_scr[r0:r1, :]
                    ok = (_row_iota(r0, r1) < limit) & (jnp.abs(sc) < jnp.inf)
                    bias_scr[r0:r1, :] = jnp.where(ok, 0.0, -jnp.inf)

            @pl.when(jnp.logical_not(few))
            def _():
                _select_bias(sc_scr.at[0:s_eff], key_scr.at[0:s_eff], bias_scr.at[0:s_eff],
                             limit, t_len, topk)
        else:
            _select_bias(sc_scr.at[0:s_eff], key_scr.at[0:s_eff], bias_scr.at[0:s_eff],
                         limit, t_len, topk)
        for kt in range(s_eff // LANE):
            bias_out_ref[:, kt * LANE:(kt + 1) * LANE] = (
                bias_scr[kt * LANE:(kt + 1) * LANE, :].T.astype(bias_out_ref.dtype))
        if s_eff < t_len:
            bias_out_ref[:, s_eff:t_len] = jnp.full((LANE, t_len - s_eff), -jnp.inf,
                                                    bias_out_ref.dtype)

    for v in range(t_len // seg):
        pl.when((j * LANE) // seg == v)(functools.partial(run, (v + 1) * seg))


def _causal_seg(t_len):
    return min(4 * LANE, t_len)


def _sel_prompt(iq_rot, ikw_rot, nb, t):
    topk = min(TOPK_MAX, t // 4)
    nq = t // LANE
    wi = IDX_HEADS * IDX_DIM
    return pl.pallas_call(
        functools.partial(_sel_prompt_kernel, topk=topk, tk=min(256, t)),
        grid=(nb, nq),
        in_specs=[pl.BlockSpec((LANE, wi), lambda b, j: (b * nq + j, 0)),
                  pl.BlockSpec((LANE, LANE), lambda b, j: (b * nq + j, 0)),
                  pl.BlockSpec((t, LANE), lambda b, j: (b, 0))],
        out_specs=pl.BlockSpec((LANE, t), lambda b, j: (b * nq + j, 0)),
        out_shape=jax.ShapeDtypeStruct((nb * t, t), MXU_DTYPE),
        scratch_shapes=[pltpu.VMEM((t, LANE), F32), pltpu.VMEM((t, LANE), jnp.int32),
                        pltpu.VMEM((t, LANE), F32)],
        compiler_params=_params(("parallel", "arbitrary")),
        name="sel_prompt",
    )(iq_rot, ikw_rot, ikw_rot)


def _sel_sample_kernel(iqp_ref, w_ref, ikp_ref, ikn_ref, bias_out_ref,
                       sct_scr, sc_scr, key_scr, bias_scr, *, ts, past, topk, n_keys, tk):
    s_pad = sc_scr.shape[0]
    nbp = LANE // ts
    tiles = [(r0, r0 + tk) for r0 in range(0, past, tk)] + [(past, s_pad)]
    for b in range(nbp):
        iqp = iqp_ref[b]
        wb = jnp.broadcast_to(w_ref[b], (LANE, IDX_HEADS * ts)).T
        for r0, r1 in tiles:
            if r0 < past:
                ikt = ikp_ref[b, :, r0:r1]
            else:
                new = jnp.concatenate([ikn_ref[b * ts:(b + 1) * ts, :],
                                       jnp.zeros((LANE - ts, LANE), F32)], axis=0)
                ikt = new.T[0:IDX_DIM, :]
            lg = _mm(iqp, ikt.astype(MXU_DTYPE))
            for c in range((r1 - r0) // LANE):
                acc = None
                for h in range(IDX_HEADS):
                    t = (jnp.maximum(lg[h * ts:(h + 1) * ts, c * LANE:(c + 1) * LANE], 0.0)
                         * wb[h * ts:(h + 1) * ts, :])
                    acc = t if acc is None else acc + t
                sct_scr[b * ts:(b + 1) * ts, r0 + c * LANE:r0 + (c + 1) * LANE] = acc
    for kt in range(s_pad // LANE):
        sc_scr[kt * LANE:(kt + 1) * LANE, :] = sct_scr[:, kt * LANE:(kt + 1) * LANE].T
    q = lax.broadcasted_iota(jnp.int32, (1, LANE), 1) % ts
    limit = ((past + q) // CHUNK + 1) * CHUNK
    _select_bias(sc_scr, key_scr, bias_scr, limit, n_keys, topk)
    for kt in range(s_pad // LANE):
        bias_out_ref[:, kt * LANE:(kt + 1) * LANE] = (
            bias_scr[kt * LANE:(kt + 1) * LANE, :].T.astype(bias_out_ref.dtype))


def _sel_sample(iqp, wrow, ik_past, layer, ikw_rot, row0, nb, ts, past, s_pad):
    n_keys = past + ts
    topk = min(TOPK_MAX, n_keys // 4)
    nbp = LANE // ts
    blk0 = row0 // LANE
    ng = nb // nbp
    return pl.pallas_call(
        functools.partial(_sel_sample_kernel, ts=ts, past=past, topk=topk, n_keys=n_keys,
                          tk=math.gcd(past, 1024)),
        grid=(nb // nbp,),
        in_specs=[pl.BlockSpec((nbp, IDX_HEADS * ts, IDX_DIM), lambda g: (g, 0, 0)),
                  pl.BlockSpec((nbp, 1, IDX_HEADS * ts), lambda g: (g, 0, 0)),
                  pl.BlockSpec((nbp, IDX_DIM, past), lambda g: (layer * ng + g, 0, 0)),
                  pl.BlockSpec((LANE, LANE), lambda g: (blk0 + g, 0))],
        out_specs=pl.BlockSpec((LANE, s_pad), lambda g: (g, 0)),
        out_shape=jax.ShapeDtypeStruct((nb * ts, s_pad), MXU_DTYPE),
        scratch_shapes=[pltpu.VMEM((LANE, s_pad), F32), pltpu.VMEM((s_pad, LANE), F32),
                        pltpu.VMEM((s_pad, LANE), jnp.int32), pltpu.VMEM((s_pad, LANE), F32)],
        compiler_params=_params(("parallel",)),
        name="sel_sample",
    )(iqp, wrow, ik_past, ikw_rot)


def _attend(q_ref, k_ref, v_ref, bias_ref, o_ref, s_eff, tk):
    nq = q_ref.shape[0]
    scale = HEAD_DIM ** -0.5 * math.log2(math.e)
    rows = KV_GROUP * nq
    for g in range(N_KV):
        cols = slice(g * HEAD_DIM, (g + 1) * HEAD_DIM)
        qg = jnp.concatenate(
            [q_ref[:, (g * KV_GROUP + i) * HEAD_DIM:(g * KV_GROUP + i + 1) * HEAD_DIM]
             for i in range(KV_GROUP)], axis=0)
        m = jnp.full((rows, 1), -jnp.inf, F32)
        l = jnp.zeros((rows, 1), F32)
        o = jnp.zeros((rows, HEAD_DIM), F32)
        for r0 in range(0, s_eff, tk):
            r1 = min(r0 + tk, s_eff)
            b = bias_ref[:, r0:r1].astype(F32)
            s = _mm_nt(qg, k_ref[r0:r1, cols]) * scale + jnp.concatenate([b] * KV_GROUP, axis=0)
            m_new = jnp.maximum(m, jnp.max(s, axis=-1, keepdims=True))
            m_safe = jnp.where(m_new == -jnp.inf, 0.0, m_new)
            alpha = jnp.exp2(m - m_safe)
            p = jnp.exp2(s - m_safe)
            l = alpha * l + jnp.sum(p, axis=-1, keepdims=True)
            o = alpha * o + _mm(p.astype(MXU_DTYPE), v_ref[r0:r1, cols])
            m = m_new
        o = o / l
        for i in range(KV_GROUP):
            h = g * KV_GROUP + i
            o_ref[:, h * HEAD_DIM:(h + 1) * HEAD_DIM] = o[i * nq:(i + 1) * nq, :].astype(o_ref.dtype)


def _attn_prompt_kernel(q_ref, k_ref, v_ref, bias_ref, o_ref):
    j = pl.program_id(1)
    t_len = k_ref.shape[0]
    seg = _causal_seg(t_len)

    def run(s_eff):
        _attend(q_ref, k_ref, v_ref, bias_ref, o_ref, s_eff, ATTN_KEY_TILE)

    for v in range(t_len // seg):
        pl.when((j * LANE) // seg == v)(functools.partial(run, (v + 1) * seg))


def _attn_prompt(q_rot, k_bf, v_bf, bias, nb, t):
    nq = t // LANE
    wq = N_HEADS * HEAD_DIM
    wkv = N_KV * HEAD_DIM
    return pl.pallas_call(
        _attn_prompt_kernel,
        grid=(nb, nq),
        in_specs=[pl.BlockSpec((LANE, wq), lambda b, j: (b * nq + j, 0)),
                  pl.BlockSpec((t, wkv), lambda b, j: (b, 0)),
                  pl.BlockSpec((t, wkv), lambda b, j: (b, 0)),
                  pl.BlockSpec((LANE, t), lambda b, j: (b * nq + j, 0))],
        out_specs=pl.BlockSpec((LANE, wq), lambda b, j: (b * nq + j, 0)),
        out_shape=jax.ShapeDtypeStruct((nb * t, wq), MXU_DTYPE),
        compiler_params=_params(("parallel", "arbitrary")),
        name="attn_prompt",
    )(q_rot, k_bf, v_bf, bias)


def _attn_sample_kernel(q_ref, kp_ref, vp_ref, kn_ref, vn_ref, bias_ref, o_ref, k_scr, v_scr,
                        *, ts, past):
    s_pad = k_scr.shape[0]
    wkv = N_KV * HEAD_DIM
    for g in range(N_KV):
        cols = slice(g * HEAD_DIM, (g + 1) * HEAD_DIM)
        k_scr[0:past, cols] = kp_ref[0, pl.ds(g, past, stride=N_KV), :].astype(MXU_DTYPE)
        v_scr[0:past, cols] = vp_ref[0, pl.ds(g, past, stride=N_KV), :].astype(MXU_DTYPE)
    k_scr[past:past + ts, :] = kn_ref[...]
    v_scr[past:past + ts, :] = vn_ref[...]
    k_scr[past + ts:s_pad, :] = jnp.zeros((s_pad - past - ts, wkv), MXU_DTYPE)
    v_scr[past + ts:s_pad, :] = jnp.zeros((s_pad - past - ts, wkv), MXU_DTYPE)
    _attend(q_ref, k_scr, v_scr, bias_ref, o_ref, s_pad, -(-s_pad // (2 * LANE)) * LANE)


def _attn_sample(q_rot, k_past, v_past, layer, k_bf, v_bf, bias, row0, nb, ts, past, s_pad):
    wq = N_HEADS * HEAD_DIM
    wkv = N_KV * HEAD_DIM
    blk0 = row0 // ts
    return pl.pallas_call(
        functools.partial(_attn_sample_kernel, ts=ts, past=past),
        grid=(nb,),
        in_specs=[pl.BlockSpec((ts, wq), lambda b: (blk0 + b, 0)),
                  pl.BlockSpec((1, past * N_KV, HEAD_DIM), lambda b: (layer * nb + b, 0, 0)),
                  pl.BlockSpec((1, past * N_KV, HEAD_DIM), lambda b: (layer * nb + b, 0, 0)),
                  pl.BlockSpec((ts, wkv), lambda b: (blk0 + b, 0)),
                  pl.BlockSpec((ts, wkv), lambda b: (blk0 + b, 0)),
                  pl.BlockSpec((ts, s_pad), lambda b: (b, 0))],
        out_specs=pl.BlockSpec((ts, wq), lambda b: (b, 0)),
        out_shape=jax.ShapeDtypeStruct((nb * ts, wq), MXU_DTYPE),
        scratch_shapes=[pltpu.VMEM((s_pad, wkv), MXU_DTYPE), pltpu.VMEM((s_pad, wkv), MXU_DTYPE)],
        compiler_params=_params(("parallel",)),
        name="attn_sample",
    )(q_rot, k_past, v_past, k_bf, v_bf, bias)


def kernel(x_prompt, x_sample, cache_k, cache_v, cache_idx_k, state_ssm, state_conv, norm_ffn1, ffn1_w1, ffn1_w3, ffn1_w2, norm_mix, w_in, conv_w, conv_b, dt_bias, a_log, d_skip, ssm_norm_w, w_out, norm_ffn2, ffn2_w1, ffn2_w3, ffn2_w2, final_norm):
    bp, tp, d = x_prompt.shape
    bs, ts = x_sample.shape[:2]
    depth, _, past = cache_k.shape[:3]
    mp, ms = bp * tp, bs * ts
    n_keys_s = past + ts
    s_pad = -(-n_keys_s // LANE) * LANE
    hp = SSM_HEADS * SSM_HEADDIM
    wkv = N_KV * HEAD_DIM

    xs = (x_prompt.reshape(mp, d), x_sample.reshape(ms, d))

    tm_prep = _prep_tile(tp, ms, ts)
    pos = jnp.concatenate([jnp.arange(tp, dtype=jnp.int32),
                           jnp.tile(past + jnp.arange(ts, dtype=jnp.int32), tm_prep // ts)])
    iw_scale = jnp.concatenate([jnp.full((IDX_HEADS,), IDX_HEADS ** -0.5, F32),
                                jnp.ones((LANE - IDX_DIM - IDX_HEADS,), F32)])
    tab = jnp.concatenate(_rope_table(pos, ROPE_DIM, HEAD_DIM)
                          + _rope_table(pos, IDX_ROPE_DIM, IDX_DIM)
                          + _rope_table(pos, IDX_ROPE_DIM, IDX_DIM, tail=iw_scale), axis=1)

    w_in_p = _pack_w_in(w_in)
    stack2d = lambda w: w.reshape(w.shape[0] * w.shape[1], w.shape[2])
    ffn_f32 = [tuple(stack2d(w) for w in ws)
               for ws in ((ffn1_w1, ffn1_w3, ffn1_w2), (ffn2_w1, ffn2_w3, ffn2_w2))]
    dff = ffn1_w1.shape[2]
    w_ffn = (_cast_weights(ffn_f32[0][0], d), _cast_weights(ffn_f32[0][1], d),
             _cast_weights(ffn_f32[0][2], dff))
    w_out_b = _cast_weights(stack2d(w_out), depth * w_out.shape[1])
    k_cache = cache_k.reshape(depth * bs, past * N_KV, HEAD_DIM)
    v_cache = cache_v.reshape(depth * bs, past * N_KV, HEAD_DIM)
    ik_cache = jnp.swapaxes(cache_idx_k, 2, 3).reshape(depth * bs, IDX_DIM, past)
    ssm0 = state_ssm.reshape(depth * bs, hp, SSM_STATE)
    conv0_pad = jnp.pad(state_conv, ((0, 0), (0, 0), (SUBLANE - (CONV_W - 1), 0), (0, 0)))
    conv0_pad = conv0_pad.reshape(depth * bs, SUBLANE, CONV_DIM)
    lane_pad = lambda a: jnp.pad(a, (0, LANE - a.shape[0]))[None, :]

    outs = {k: [] for k in ("ikp", "cp", "iks", "cs")}
    kv_out = ssm_p = ssm_s = None
    for l in range(depth):
        x, w_ffn = _ffn(xs if l == 0 else (x,), norm_ffn1[l][None, :], w_ffn, (*ffn_f32[1], l))
        proj = _in_proj(x, norm_mix[l][None, :], w_in_p, l)
        (q_rot, iq_rot, k_bf, v_bf, ikw_rot), kv_out = _prep(
            proj, tab, mp, tp, tm_prep, l, depth, kv_out)

        consts = (conv_w[l], conv_b[l][None, :], lane_pad(dt_bias[l]),
                  lane_pad(-jnp.exp(a_log[l])), jnp.repeat(d_skip[l], SSM_HEADDIM)[None, :],
                  ssm_norm_w[l][None, :])
        y_p, ssm_p, conv_p = _ssd(proj, 0, bp, tp, LANE, LANE, None, consts, l, depth, ssm_p)
        y_s, ssm_s, conv_s = _ssd(proj, mp, bs, ts, ts, LANE, (ssm0, conv0_pad), consts,
                                  l, depth, ssm_s)

        bias_p = _sel_prompt(iq_rot, ikw_rot, bp, tp)
        att_p = _attn_prompt(q_rot, k_bf, v_bf, bias_p, bp, tp)

        iqp = iq_rot[mp:].reshape(bs, ts, IDX_HEADS, IDX_DIM).transpose(0, 2, 1, 3)
        iqp = iqp.reshape(bs, IDX_HEADS * ts, IDX_DIM)
        wrow = ikw_rot[mp:, IDX_DIM:IDX_DIM + IDX_HEADS].reshape(bs, ts, IDX_HEADS)
        wrow = wrow.transpose(0, 2, 1).reshape(bs, 1, IDX_HEADS * ts)
        bias_s = _sel_sample(iqp, wrow, ik_cache, l, ikw_rot, mp, bs, ts, past, s_pad)
        att_s = _attn_sample(q_rot, k_cache, v_cache, l, k_bf, v_bf, bias_s,
                             mp, bs, ts, past, s_pad)

        x = _out_proj(x, (y_p, y_s), (att_p, att_s), w_out_b, l)
        x, w_ffn = _ffn((x,), norm_ffn2[l][None, :], w_ffn,
                        (*ffn_f32[0], l + 1) if l + 1 < depth else None)

        outs["ikp"].append(ikw_rot[:mp, :IDX_DIM].reshape(bp, tp, IDX_DIM))
        outs["cp"].append(conv_p[:, SUBLANE - (CONV_W - 1):, :])
        outs["iks"].append(ikw_rot[mp:, :IDX_DIM].reshape(bs, ts, IDX_DIM))
        outs["cs"].append(conv_s[:, SUBLANE - (CONV_W - 1):, :])

    y_p, y_s = _final_norm(x, final_norm[None, :], mp, ms)
    st = {k: jnp.stack(v) for k, v in outs.items()}
    kp, ks, vp, vs = kv_out
    kv_p = lambda a: a.reshape(depth, bp, tp, N_KV, HEAD_DIM)
    kv_s = lambda a: a.reshape(depth, bs, ts, N_KV, HEAD_DIM)
    return (y_p.reshape(bp, tp, d), y_s.reshape(bs, ts, d),
            kv_p(kp), kv_p(vp), st["ikp"],
            ssm_p.reshape(depth, bp, SSM_HEADS, SSM_HEADDIM, SSM_STATE), st["cp"],
            kv_s(ks), kv_s(vs), st["iks"],
            ssm_s.reshape(depth, bs, SSM_HEADS, SSM_HEADDIM, SSM_STATE), st["cs"])
```

```python
import functools
import math

import jax
import jax.numpy as jnp
from jax import lax
from jax.experimental import pallas as pl
from jax.experimental.pallas import tpu as pltpu

F32 = jnp.float32
MXU_DTYPE = jnp.bfloat16

D_MODEL = 2048
CHUNK = 64
D_SSM = 1024
SSM_HEADDIM = 64
SSM_HEADS = 16
SSM_GROUPS = 2
SSM_STATE = 128
CONV_W = 4
CONV_DIM = D_SSM + 2 * SSM_GROUPS * SSM_STATE
HEAD_DIM = 128
N_HEADS = 8
N_KV = 2
KV_GROUP = N_HEADS // N_KV
ROPE_DIM = HEAD_DIM // 4
IDX_HEADS = 16
IDX_DIM = 64
IDX_ROPE_DIM = IDX_DIM // 4
TOPK_MAX = 256
ROPE_THETA = 500000.0
D_FF = 5632
EPS = 1e-6

LANE = 128
SUBLANE = 8
COUNT_TILE = 128
ATTN_KEY_TILE = 512
VMEM_LIMIT = 56 * 1024 * 1024

COL_Z = 0
COL_Q = 1024
COL_IQ = 2048
COL_XBC = 3072
COL_K = 4608
COL_V = 4864
COL_DT = 5120
COL_IKW = 5248
PROJ_COLS = 5376

KEY_NEG_INF = -2139095041
KEY_POS_INF = 2139095040
INT_MIN = -2147483648

NT_DIMS = (((1,), (1,)), ((), ()))


def _mm(a, b):
    return jnp.dot(a, b, preferred_element_type=F32)


def _mm_nt(a, b):
    return lax.dot_general(a, b, NT_DIMS, preferred_element_type=F32)


def _split3(x):
    hi = x.astype(MXU_DTYPE)
    r = x - hi.astype(F32)
    mid = r.astype(MXU_DTYPE)
    lo = (r - mid.astype(F32)).astype(MXU_DTYPE)
    return hi, mid, lo


def _mm_exact(a, b, exact):
    if exact == "a":
        a01 = a.astype(MXU_DTYPE)
        return sum(_mm(a01, p) for p in _split3(b))
    b01 = b.astype(MXU_DTYPE)
    return sum(_mm(p, b01) for p in _split3(a))


def _silu(x):
    return x * jax.nn.sigmoid(x)


def _params(sem, vmem=VMEM_LIMIT):
    return pltpu.CompilerParams(dimension_semantics=sem, vmem_limit_bytes=vmem)


def _row_tile(m, pref):
    t = pref
    while m % t:
        t //= 2
    return t


def _split_rows(tm, rows_a, rows_b):
    assert rows_a % tm == 0 and rows_b % tm == 0
    na = rows_a // tm

    def spec_a(width, *grid_rest):
        return pl.BlockSpec((tm, width), lambda i, *_: (jnp.minimum(i, na - 1), 0))

    def spec_b(width, *grid_rest):
        return pl.BlockSpec((tm, width), lambda i, *_: (jnp.maximum(i - na, 0), 0))

    return na, spec_a, spec_b


def _pick(i, na, a_ref, b_ref):
    return jnp.where(i < na, a_ref[...], b_ref[...])


def _ffn_kernel(*refs, na, side):
    refs = list(refs)
    if na is None:
        x_ref = refs.pop(0)
        load_x = lambda: x_ref[...]
    else:
        xa_ref, xb_ref = refs.pop(0), refs.pop(0)
        load_x = lambda: _pick(pl.program_id(0), na, xa_ref, xb_ref)
    nw_ref, w1_ref, w3_ref, w2_ref = refs[:4]
    refs = refs[4:]
    if side:
        s1_ref, s3_ref, s2_ref, o_ref, c1_ref, c3_ref, c2_ref, h_scr = refs
    else:
        o_ref, h_scr = refs
    f = pl.program_id(1)

    @pl.when(f == 0)
    def _():
        x = load_x()
        ms = jnp.mean(x * x, axis=-1, keepdims=True)
        h_scr[...] = (x * lax.rsqrt(ms + EPS) * nw_ref[...]).astype(h_scr.dtype)
        o_ref[...] = jnp.zeros_like(o_ref)

    h = h_scr[...]
    a = _mm(h, w1_ref[...])
    b = _mm(h, w3_ref[...])
    g = (_silu(a) * b).astype(h_scr.dtype)
    o_ref[...] += _mm(g, w2_ref[...])

    if side:
        c1_ref[...] = s1_ref[...].astype(c1_ref.dtype)
        c3_ref[...] = s3_ref[...].astype(c3_ref.dtype)
        c2_ref[...] = s2_ref[...].astype(c2_ref.dtype)

    @pl.when(f == pl.num_programs(1) - 1)
    def _():
        o_ref[...] = load_x() + 0.5 * o_ref[...]


def _ffn_row_tile(rows):
    g = math.gcd(*rows)
    for tm in (768, 512, 256, 128):
        if g % tm == 0:
            return tm
    raise ValueError(rows)


def _chunk_rows(total, steps):
    tile = 2 * SUBLANE
    for r in range(tile, total + 1, tile):
        if total % r == 0 and total // r <= steps:
            return r
    raise ValueError((total, steps))


def _ffn(xs, nw, w, nxt=None, tf=512):
    w1, w3, w2 = w
    d = xs[0].shape[1]
    m = sum(x.shape[0] for x in xs)
    dff = w1.shape[1]
    nf = dff // tf
    tm = _ffn_row_tile([x.shape[0] for x in xs])
    if len(xs) == 1:
        na, x_specs = None, [pl.BlockSpec((tm, d), lambda i, f: (i, 0))]
    else:
        na, spec_a, spec_b = _split_rows(tm, xs[0].shape[0], xs[1].shape[0])
        x_specs = [spec_a(d), spec_b(d)]
    in_specs = x_specs + [
        pl.BlockSpec((1, d), lambda i, f: (0, 0)),
        pl.BlockSpec((d, tf), lambda i, f: (0, f)),
        pl.BlockSpec((d, tf), lambda i, f: (0, f)),
        pl.BlockSpec((tf, d), lambda i, f: (f, 0)),
    ]
    out_specs = [pl.BlockSpec((tm, d), lambda i, f: (i, 0))]
    out_shape = [jax.ShapeDtypeStruct((m, d), F32)]
    args = [*xs, nw, w1, w3, w2]
    if nxt is not None:
        s1, s3, s2, layer = nxt
        steps = (m // tm) * nf
        r1, r2 = _chunk_rows(d, steps), _chunk_rows(dff, steps)
        n1, n2 = d // r1, dff // r2

        def chunk(n, base):
            return lambda i, f: (base + jnp.minimum(i * nf + f, n - 1), 0)

        in_specs += [pl.BlockSpec((r1, dff), chunk(n1, layer * n1)),
                     pl.BlockSpec((r1, dff), chunk(n1, layer * n1)),
                     pl.BlockSpec((r2, d), chunk(n2, layer * n2))]
        out_specs += [pl.BlockSpec((r1, dff), chunk(n1, 0)), pl.BlockSpec((r1, dff), chunk(n1, 0)),
                      pl.BlockSpec((r2, d), chunk(n2, 0))]
        out_shape += [jax.ShapeDtypeStruct((d, dff), MXU_DTYPE),
                      jax.ShapeDtypeStruct((d, dff), MXU_DTYPE),
                      jax.ShapeDtypeStruct((dff, d), MXU_DTYPE)]
        args += [s1, s3, s2]
    res = pl.pallas_call(
        functools.partial(_ffn_kernel, na=na, side=nxt is not None),
        grid=(m // tm, nf),
        in_specs=in_specs,
        out_specs=out_specs,
        out_shape=out_shape,
        scratch_shapes=[pltpu.VMEM((tm, d), MXU_DTYPE)],
        compiler_params=_params(("arbitrary", "arbitrary")),
        name="ffn",
    )(*args)
    return res[0], tuple(res[1:])


def _in_proj_kernel(x_ref, nw_ref, w_ref, o_ref, h_scr):
    @pl.when(pl.program_id(1) == 0)
    def _():
        x = x_ref[...]
        ms = jnp.mean(x * x, axis=-1, keepdims=True)
        h_scr[...] = (x * lax.rsqrt(ms + EPS) * nw_ref[...]).astype(h_scr.dtype)

    o_ref[...] = _mm_nt(h_scr[...], w_ref[...])


def _in_proj(x, nw, w, layer, tm=1024, tn=1792):
    m, d = x.shape
    n = PROJ_COLS
    nn = n // tn
    tm = _row_tile(m, tm)
    return pl.pallas_call(
        _in_proj_kernel,
        grid=(m // tm, nn),
        in_specs=[
            pl.BlockSpec((tm, d), lambda i, j: (i, 0)),
            pl.BlockSpec((1, d), lambda i, j: (0, 0)),
            pl.BlockSpec((tn, d), lambda i, j: (layer * nn + j, 0)),
        ],
        out_specs=pl.BlockSpec((tm, tn), lambda i, j: (i, j)),
        out_shape=jax.ShapeDtypeStruct((m, n), F32),
        scratch_shapes=[pltpu.VMEM((tm, d), MXU_DTYPE)],
        compiler_params=_params(("parallel", "arbitrary")),
        name="in_proj",
    )(x, nw, w)


def _out_proj_kernel(x_ref, ya_ref, yb_ref, aa_ref, ab_ref, w_ref, o_ref, *, na):
    i = pl.program_id(0)
    half = ya_ref.shape[1]
    o_ref[...] = (x_ref[...] + _mm(_pick(i, na, ya_ref, yb_ref), w_ref[:half, :])
                  + _mm(_pick(i, na, aa_ref, ab_ref), w_ref[half:, :]))


def _out_proj(x, y_pair, att_pair, w, layer, tm=512):
    m, d = x.shape
    half = y_pair[0].shape[1]
    tm = _row_tile(math.gcd(y_pair[0].shape[0], y_pair[1].shape[0]), tm)
    na, spec_a, spec_b = _split_rows(tm, y_pair[0].shape[0], y_pair[1].shape[0])
    return pl.pallas_call(
        functools.partial(_out_proj_kernel, na=na),
        grid=(m // tm,),
        in_specs=[
            pl.BlockSpec((tm, d), lambda i: (i, 0)),
            spec_a(half), spec_b(half), spec_a(half), spec_b(half),
            pl.BlockSpec((2 * half, d), lambda i: (layer, 0)),
        ],
        out_specs=pl.BlockSpec((tm, d), lambda i: (i, 0)),
        out_shape=jax.ShapeDtypeStruct((m, d), F32),
        compiler_params=_params(("arbitrary",)),
        name="out_proj",
    )(x, *y_pair, *att_pair, w)


def _final_norm_kernel(x_ref, nw_ref, oa_ref, ob_ref, *, na):
    i = pl.program_id(0)
    x = x_ref[...]
    ms = jnp.mean(x * x, axis=-1, keepdims=True)
    y = x * lax.rsqrt(ms + EPS) * nw_ref[...]

    @pl.when(i < na)
    def _():
        oa_ref[...] = y

    @pl.when(i >= na)
    def _():
        ob_ref[...] = y


def _final_norm(x, nw, rows_a, rows_b, tm=512):
    d = x.shape[1]
    tm = _row_tile(math.gcd(rows_a, rows_b), tm)
    na, spec_a, spec_b = _split_rows(tm, rows_a, rows_b)
    return pl.pallas_call(
        functools.partial(_final_norm_kernel, na=na),
        grid=((rows_a + rows_b) // tm,),
        in_specs=[pl.BlockSpec((tm, d), lambda i: (i, 0)),
                  pl.BlockSpec((1, d), lambda i: (0, 0))],
        out_specs=[spec_a(d), spec_b(d)],
        out_shape=[jax.ShapeDtypeStruct((rows_a, d), F32),
                   jax.ShapeDtypeStruct((rows_b, d), F32)],
        compiler_params=_params(("arbitrary",)),
        name="final_norm",
    )(x, nw)


def _cast_kernel(x_ref, o_ref):
    o_ref[...] = x_ref[...].astype(o_ref.dtype)


def _cast_weights(w2, n_rows, row0=0, rows=512):
    c = w2.shape[1]
    rows = _row_tile(math.gcd(n_rows, row0) if row0 else n_rows, rows)
    blk0 = row0 // rows
    return pl.pallas_call(
        _cast_kernel,
        grid=(n_rows // rows,),
        in_specs=[pl.BlockSpec((rows, c), lambda i: (blk0 + i, 0))],
        out_specs=pl.BlockSpec((rows, c), lambda i: (i, 0)),
        out_shape=jax.ShapeDtypeStruct((n_rows, c), MXU_DTYPE),
        compiler_params=_params(("parallel",)),
        name="cast_w",
    )(w2)


_W_IN_PARTS = (
    (COL_Z, 0, D_SSM),
    (COL_XBC, D_SSM, CONV_DIM),
    (COL_DT, D_SSM + CONV_DIM, SSM_HEADS),
    (COL_Q, D_SSM + CONV_DIM + SSM_HEADS, N_HEADS * HEAD_DIM),
    (COL_K, D_SSM + CONV_DIM + SSM_HEADS + N_HEADS * HEAD_DIM, N_KV * HEAD_DIM),
    (COL_V, D_SSM + CONV_DIM + SSM_HEADS + (N_HEADS + N_KV) * HEAD_DIM, N_KV * HEAD_DIM),
    (COL_IQ, D_SSM + CONV_DIM + SSM_HEADS + (N_HEADS + 2 * N_KV) * HEAD_DIM, IDX_HEADS * IDX_DIM),
    (COL_IKW, D_SSM + CONV_DIM + SSM_HEADS + (N_HEADS + 2 * N_KV) * HEAD_DIM + IDX_HEADS * IDX_DIM,
     IDX_DIM + IDX_HEADS),
)


def _pack_w_in_kernel(w_ref, o_ref):
    cols = o_ref.shape[1]
    for dst, src, width in _W_IN_PARTS:
        o_ref[dst:dst + width, :] = w_ref[src:src + width, :].astype(o_ref.dtype)
        pad = -width % LANE
        if pad:
            o_ref[dst + width:dst + width + pad, :] = jnp.zeros((pad, cols), o_ref.dtype)


def _pack_w_in(w, kc=256):
    depth, d, c = w.shape
    wt = jnp.transpose(w, (0, 2, 1)).reshape(depth * c, d)
    return pl.pallas_call(
        _pack_w_in_kernel,
        grid=(depth, d // kc),
        in_specs=[pl.BlockSpec((c, kc), lambda l, k: (l, k))],
        out_specs=pl.BlockSpec((PROJ_COLS, kc), lambda l, k: (l, k)),
        out_shape=jax.ShapeDtypeStruct((depth * PROJ_COLS, d), MXU_DTYPE),
        compiler_params=_params(("parallel", "parallel")),
        name="pack_w_in",
    )(wt)


def _rope_lanes(x, c, s1, s2, half):
    outs = []
    for t in range(x.shape[1] // LANE):
        xt = x[:, t * LANE:(t + 1) * LANE]
        outs.append(xt * c + pltpu.roll(xt, half, 1) * s1
                    + pltpu.roll(xt, LANE - half, 1) * s2)
    return outs[0] if len(outs) == 1 else jnp.concatenate(outs, axis=1)


def _prep_kernel(*refs, na):
    q_ref, iq_ref, k_ref, v_ref, ikw_ref, tab_ref = refs[:6]
    (qo_ref, iqo_ref, kbo_ref, vbo_ref, ikwo_ref, kp_ref, ks_ref, vp_ref, vs_ref) = refs[10:]
    i = pl.program_id(0)
    tm = k_ref.shape[0]
    tab = [tab_ref[:, j * LANE:(j + 1) * LANE] for j in range(9)]
    qo_ref[...] = _rope_lanes(q_ref[...], tab[0], tab[1], tab[2], ROPE_DIM // 2).astype(qo_ref.dtype)
    k = _rope_lanes(k_ref[...], tab[0], tab[1], tab[2], ROPE_DIM // 2)
    kbo_ref[...] = k.astype(kbo_ref.dtype)
    iqo_ref[...] = _rope_lanes(iq_ref[...], tab[3], tab[4], tab[5], IDX_ROPE_DIM // 2).astype(iqo_ref.dtype)
    ikwo_ref[...] = _rope_lanes(ikw_ref[...], tab[6], tab[7], tab[8], IDX_ROPE_DIM // 2)
    v = v_ref[...]
    vbo_ref[...] = v.astype(vbo_ref.dtype)

    def put(dst_k, dst_v):
        for g in range(N_KV):
            dst_k[pl.ds(g, tm, stride=N_KV), :] = k[:, g * HEAD_DIM:(g + 1) * HEAD_DIM]
            dst_v[pl.ds(g, tm, stride=N_KV), :] = v[:, g * HEAD_DIM:(g + 1) * HEAD_DIM]

    pl.when(i < na)(functools.partial(put, kp_ref, vp_ref))
    pl.when(i >= na)(functools.partial(put, ks_ref, vs_ref))


def _prep_tile(tp, ms, ts, tm=512):
    tm = _row_tile(math.gcd(tp, ms), tm)
    assert tm % ts == 0
    return tm


def _prep(proj, tab, mp, tp, tm, layer, depth, caches):
    m = proj.shape[0]
    ms = m - mp
    wq = N_HEADS * HEAD_DIM
    wkv = N_KV * HEAD_DIM
    na = mp // tm
    row = lambda w, c: pl.BlockSpec((tm, w), lambda i: (i, c))
    tab_spec = pl.BlockSpec(
        (tm, 9 * LANE), lambda i: (jnp.where(i < na, i % (tp // tm), tp // tm), 0))
    p_spec = pl.BlockSpec((N_KV * tm, HEAD_DIM), lambda i: (layer * na + jnp.minimum(i, na - 1), 0))
    s_spec = pl.BlockSpec((N_KV * tm, HEAD_DIM),
                          lambda i: (layer * (ms // tm) + jnp.maximum(i - na, 0), 0))
    p_shape = jax.ShapeDtypeStruct((depth * mp * N_KV, HEAD_DIM), F32)
    s_shape = jax.ShapeDtypeStruct((depth * ms * N_KV, HEAD_DIM), F32)
    in_specs = [row(wq, COL_Q // wq), row(wq, COL_IQ // wq), row(wkv, COL_K // wkv),
                row(wkv, COL_V // wkv), row(LANE, COL_IKW // LANE), tab_spec]
    in_specs += [pl.BlockSpec(memory_space=pl.ANY)] * 4
    args = [proj, proj, proj, proj, proj, tab, *caches]
    aliases = {6 + j: 5 + j for j in range(4)}
    res = pl.pallas_call(
        functools.partial(_prep_kernel, na=na),
        grid=(m // tm,),
        in_specs=in_specs,
        out_specs=[row(wq, 0), row(wq, 0), row(wkv, 0), row(wkv, 0), row(LANE, 0),
                   p_spec, s_spec, p_spec, s_spec],
        out_shape=[jax.ShapeDtypeStruct((m, wq), MXU_DTYPE),
                   jax.ShapeDtypeStruct((m, wq), MXU_DTYPE),
                   jax.ShapeDtypeStruct((m, wkv), MXU_DTYPE),
                   jax.ShapeDtypeStruct((m, wkv), MXU_DTYPE),
                   jax.ShapeDtypeStruct((m, LANE), F32),
                   p_shape, s_shape, p_shape, s_shape],
        input_output_aliases=aliases,
        compiler_params=_params(("arbitrary",)),
        name="prep",
    )(*args)
    return res[:5], tuple(res[5:])


def _rope_table(pos, rot_dim, period, tail=None):
    half = rot_dim // 2
    inv_freq = jnp.float32(ROPE_THETA) ** (-jnp.arange(half, dtype=F32) * 2.0 / rot_dim)
    ang = pos.astype(F32)[:, None] * inv_freq[None, :]
    cos, sin = jnp.cos(ang), jnp.sin(ang)
    n = pos.shape[0]
    ones = jnp.ones((n, period - rot_dim), F32)
    zeros = jnp.zeros((n, period - rot_dim), F32)
    zh = jnp.zeros((n, half), F32)
    c = jnp.concatenate([cos, cos, ones], axis=1)
    s1 = jnp.concatenate([zh, sin, zeros], axis=1)
    s2 = jnp.concatenate([-sin, zh, zeros], axis=1)
    if tail is None:
        reps = LANE // period
        return [jnp.tile(a, (1, reps)) for a in (c, s1, s2)]
    zt = jnp.zeros((n, LANE - period), F32)
    return [jnp.concatenate([c, jnp.broadcast_to(tail[None, :], (n, LANE - period))], axis=1),
            jnp.concatenate([s1, zt], axis=1), jnp.concatenate([s2, zt], axis=1)]


def _ssd_kernel(*refs, qv, qp, has_state):
    refs = list(refs)
    z_ref, xbc_ref, dt_ref = refs[:3]
    refs = refs[3:]
    if has_state:
        ssm0_ref, conv0_ref = refs[:2]
        refs = refs[2:]
    cw_ref, cb_ref, dtb_ref, aneg_ref, dsk_ref, nw_ref = refs[:6]
    y_ref, ssm_ref, conv_ref, ext_scr, st_scr = refs[7:]
    c = pl.program_id(1)
    hp = SSM_HEADS * SSM_HEADDIM
    gw = SSM_STATE
    hpg = hp // SSM_GROUPS

    @pl.when(c == 0)
    def _():
        if has_state:
            ext_scr[0:SUBLANE, :] = conv0_ref[0]
            st_scr[...] = ssm0_ref[0].T
        else:
            ext_scr[0:SUBLANE, :] = jnp.zeros((SUBLANE, CONV_DIM), F32)
            st_scr[...] = jnp.zeros_like(st_scr)

    xbc_raw = xbc_ref[...]
    ext_scr[SUBLANE:SUBLANE + qv, :] = xbc_raw
    conv = cb_ref[...] + xbc_raw * cw_ref[CONV_W - 1:CONV_W, :]
    for j in range(CONV_W - 1):
        off = SUBLANE - (CONV_W - 1) + j
        conv = conv + ext_scr[off:off + qv, :] * cw_ref[j:j + 1, :]
    xc = _silu(conv)
    dtr = dt_ref[...]
    if qv < qp:
        xc = jnp.concatenate([xc, jnp.zeros((qp - qv, CONV_DIM), F32)], axis=0)
        dtr = jnp.concatenate([dtr, jnp.zeros((qp - qv, LANE), F32)], axis=0)
    xs = xc[:, :hp]
    bm = xc[:, hp:hp + SSM_GROUPS * gw]
    cm = xc[:, hp + SSM_GROUPS * gw:]

    xdt_in = dtr + dtb_ref[...]
    dt = jnp.maximum(xdt_in, 0.0) + jnp.log1p(jnp.exp(-jnp.abs(xdt_in)))
    row = lax.broadcasted_iota(jnp.int32, (qp, LANE), 0)
    dt = jnp.where(row < qv, dt, 0.0)
    la = dt * aneg_ref[...]
    ti = lax.broadcasted_iota(jnp.int32, (qp, qp), 0)
    si = lax.broadcasted_iota(jnp.int32, (qp, qp), 1)
    tri = ti >= si
    cum = _mm_exact(tri.astype(F32), la, "a")
    cum_t = cum.T

    eh = lax.broadcasted_iota(jnp.int32, (LANE, hp), 0)
    ec = lax.broadcasted_iota(jnp.int32, (LANE, hp), 1)
    expand = (ec // SSM_HEADDIM == eh).astype(F32)
    dt_x = _mm_exact(dt, expand, "b")
    cum_x = _mm_exact(cum, expand, "b")
    xdt = xs * dt_x

    cb = [_mm_nt(cm[:, g * gw:(g + 1) * gw].astype(MXU_DTYPE),
                 bm[:, g * gw:(g + 1) * gw].astype(MXU_DTYPE)) for g in range(SSM_GROUPS)]
    lane = lax.broadcasted_iota(jnp.int32, (qp, LANE), 1)
    y_parts = []
    for j in range(SSM_HEADS // 2):
        ms = []
        for h in (2 * j, 2 * j + 1):
            d = cum[:, h:h + 1] - cum_t[h:h + 1, :]
            seg = jnp.exp(jnp.where(tri, d, -jnp.inf))
            ms.append((cb[h // (SSM_HEADS // SSM_GROUPS)] * seg).astype(MXU_DTYPE))
        xp = xdt[:, j * LANE:(j + 1) * LANE]
        rhs = jnp.concatenate([jnp.where(lane < SSM_HEADDIM, xp, 0.0),
                               jnp.where(lane >= SSM_HEADDIM, xp, 0.0)], axis=0)
        y_parts.append(_mm(jnp.concatenate(ms, axis=1), rhs.astype(MXU_DTYPE)))
    y = jnp.concatenate(y_parts, axis=1)

    st = st_scr[...]
    y_st = jnp.concatenate(
        [_mm(cm[:, g * gw:(g + 1) * gw].astype(MXU_DTYPE),
             st[:, g * hpg:(g + 1) * hpg].astype(MXU_DTYPE)) for g in range(SSM_GROUPS)], axis=1)
    y = y + y_st * jnp.exp(cum_x) + dsk_ref[...] * xs

    cum_last = cum_x[qp - 1:qp, :]
    xt = (xdt * jnp.exp(cum_last - cum_x)).astype(MXU_DTYPE)
    upd = jnp.concatenate(
        [_mm(bm[:, g * gw:(g + 1) * gw].T.astype(MXU_DTYPE), xt[:, g * hpg:(g + 1) * hpg])
         for g in range(SSM_GROUPS)], axis=1)
    st_new = st * jnp.exp(cum_last) + upd
    st_scr[...] = st_new

    zz = z_ref[...]
    gated = y[:qv, :] * _silu(zz)
    ms2 = jnp.mean(gated * gated, axis=-1, keepdims=True)
    y_ref[...] = (gated * lax.rsqrt(ms2 + EPS) * nw_ref[...]).astype(y_ref.dtype)

    tail_rows = ext_scr[qv:qv + SUBLANE, :]
    ext_scr[0:SUBLANE, :] = tail_rows

    @pl.when(c == pl.num_programs(1) - 1)
    def _():
        ssm_ref[0] = st_new.T
        conv_ref[0] = tail_rows


def _ssd(proj, row0, nb, t, qv, qp, state, consts, layer, depth, ssm_buf):
    nc = t // qv
    hp = SSM_HEADS * SSM_HEADDIM
    blk0 = row0 // qv
    rowspec = lambda w, cb_: pl.BlockSpec((qv, w), lambda b, c: (blk0 + b * nc + c, cb_))
    const = lambda a: pl.BlockSpec(a.shape, lambda b, c: (0,) * a.ndim)
    in_specs = [rowspec(hp, COL_Z // hp), rowspec(CONV_DIM, COL_XBC // CONV_DIM),
                rowspec(LANE, COL_DT // LANE)]
    args = [proj, proj, proj]
    if state is not None:
        ssm0, conv0 = state
        in_specs += [pl.BlockSpec((1, hp, SSM_STATE), lambda b, c: (layer * nb + b, 0, 0)),
                     pl.BlockSpec((1, SUBLANE, CONV_DIM), lambda b, c: (layer * nb + b, 0, 0))]
        args += [ssm0, conv0]
    in_specs += [const(a) for a in consts]
    args += list(consts)
    in_specs.append(pl.BlockSpec(memory_space=pl.ANY))
    aliases = {len(args): 1}
    args.append(ssm_buf)
    return pl.pallas_call(
        functools.partial(_ssd_kernel, qv=qv, qp=qp, has_state=state is not None),
        grid=(nb, nc),
        in_specs=in_specs,
        out_specs=[pl.BlockSpec((qv, hp), lambda b, c: (b * nc + c, 0)),
                   pl.BlockSpec((1, hp, SSM_STATE), lambda b, c: (layer * nb + b, 0, 0)),
                   pl.BlockSpec((1, SUBLANE, CONV_DIM), lambda b, c: (b, 0, 0))],
        out_shape=[jax.ShapeDtypeStruct((nb * t, hp), MXU_DTYPE),
                   jax.ShapeDtypeStruct((depth * nb, hp, SSM_STATE), F32),
                   jax.ShapeDtypeStruct((nb, SUBLANE, CONV_DIM), F32)],
        scratch_shapes=[pltpu.VMEM((qv + SUBLANE, CONV_DIM), F32),
                        pltpu.VMEM((SSM_STATE, hp), F32)],
        input_output_aliases=aliases,
        compiler_params=_params(("arbitrary", "arbitrary")),
        name="ssd_state" if state is not None else "ssd_prompt",
    )(*args)


def _row_tiles(n_rows):
    assert n_rows % COUNT_TILE == 0
    return [(r0, r0 + COUNT_TILE) for r0 in range(0, n_rows, COUNT_TILE)]


def _row_iota(r0, r1):
    return lax.broadcasted_iota(jnp.int32, (r1 - r0, LANE), 0) + r0


def _count_rows(fn, n_rows):
    acc = None
    for r0, r1 in _row_tiles(n_rows):
        part = fn(r0, r1).astype(F32)
        acc = part if acc is None else acc + part
    return jnp.sum(acc, axis=0, keepdims=True)


def _select_bias(sc_ref, key_ref, bias_ref, limit, n_keys, topk):
    s_pad = sc_ref.shape[0]
    tiles = _row_tiles(s_pad)
    for r0, r1 in tiles:
        s_io = _row_iota(r0, r1)
        adm = (s_io < limit) & (s_io < n_keys)
        bits = pltpu.bitcast(jnp.where(adm, sc_ref[r0:r1, :], -jnp.inf), jnp.int32)
        key_ref[r0:r1, :] = bits ^ ((bits >> 31) & jnp.int32(0x7FFFFFFF))
    kf = jnp.float32(topk)

    cnt0 = _count_rows(lambda r0, r1: key_ref[r0:r1, :] >= 0, s_pad)
    pos = cnt0 >= kf
    prefix = jnp.where(pos, jnp.int32(0), jnp.int32(INT_MIN))
    n_ge = jnp.where(pos, cnt0, jnp.float32(s_pad))

    def body(i, carry):
        prefix, n_ge = carry
        cand = prefix | jnp.left_shift(jnp.int32(1), 30 - i)
        cnt = _count_rows(lambda r0, r1: key_ref[r0:r1, :] >= cand, s_pad)
        take = cnt >= kf
        return jnp.where(take, cand, prefix), jnp.where(take, cnt, n_ge)

    thr, n_ge = lax.fori_loop(0, 31, body, (prefix, n_ge))

    def finite(key):
        return (key > KEY_NEG_INF) & (key < KEY_POS_INF)

    for r0, r1 in tiles:
        key = key_ref[r0:r1, :]
        bias_ref[r0:r1, :] = jnp.where((key >= thr) & finite(key), 0.0, -jnp.inf)

    tie = (n_ge > kf) & (thr > KEY_NEG_INF)
    any_tie = jnp.max(tie.astype(F32)) > 0.0

    @pl.when(any_tie)
    def _():
        need = kf - _count_rows(lambda r0, r1: key_ref[r0:r1, :] > thr, s_pad)
        nbits = max(1, (s_pad - 1).bit_length())

        def ibody(i, v):
            cand = v | jnp.left_shift(jnp.int32(1), nbits - 1 - i)
            below = _count_rows(
                lambda r0, r1: (key_ref[r0:r1, :] == thr) & (_row_iota(r0, r1) < cand), s_pad)
            return jnp.where(below < need, cand, v)

        last = lax.fori_loop(0, nbits, ibody, jnp.zeros((1, LANE), jnp.int32))
        for r0, r1 in tiles:
            key = key_ref[r0:r1, :]
            sel = (key > thr) | ((key == thr) & (_row_iota(r0, r1) <= last))
            bias_ref[r0:r1, :] = jnp.where(sel & finite(key), 0.0, -jnp.inf)


def _sel_prompt_kernel(iq_ref, ikw_q_ref, ikw_all_ref, bias_out_ref, sc_scr, key_scr, bias_scr,
                       *, topk, tk):
    j = pl.program_id(1)
    t_len = ikw_all_ref.shape[0]
    seg = _causal_seg(t_len)
    pos = j * LANE + lax.broadcasted_iota(jnp.int32, (1, LANE), 1)
    limit = (pos // CHUNK + 1) * CHUNK

    def run(s_eff):
        w_t = ikw_q_ref[...].T
        iq = iq_ref[...]
        for kt in range(s_eff // tk):
            ik = ikw_all_ref[kt * tk:(kt + 1) * tk, 0:IDX_DIM].astype(MXU_DTYPE)
            acc = jnp.zeros((tk, LANE), F32)
            for h in range(IDX_HEADS):
                lt = _mm_nt(ik, iq[:, h * IDX_DIM:(h + 1) * IDX_DIM])
                acc = acc + jnp.maximum(lt, 0.0) * w_t[IDX_DIM + h:IDX_DIM + h + 1, :]
            sc_scr[kt * tk:(kt + 1) * tk, :] = acc
        if s_eff == seg:
            few = (j + 1) * LANE <= topk

            @pl.when(few)
            def _():
                for r0, r1 in _row_tiles(s_eff):
                    sc = sc_scr[r0:r1, :]
                    ok = (_row_iota(r0, r1) < limit) & (jnp.abs(sc) < jnp.inf)
                    bias_scr[r0:r1, :] = jnp.where(ok, 0.0, -jnp.inf)

            @pl.when(jnp.logical_not(few))
            def _():
                _select_bias(sc_scr.at[0:s_eff], key_scr.at[0:s_eff], bias_scr.at[0:s_eff],
                             limit, t_len, topk)
        else:
            _select_bias(sc_scr.at[0:s_eff], key_scr.at[0:s_eff], bias_scr.at[0:s_eff],
                         limit, t_len, topk)
        for kt in range(s_eff // LANE):
            bias_out_ref[:, kt * LANE:(kt + 1) * LANE] = (
                bias_scr[kt * LANE:(kt + 1) * LANE, :].T.astype(bias_out_ref.dtype))
        if s_eff < t_len:
            bias_out_ref[:, s_eff:t_len] = jnp.full((LANE, t_len - s_eff), -jnp.inf,
                                                    bias_out_ref.dtype)

    for v in range(t_len // seg):
        pl.when((j * LANE) // seg == v)(functools.partial(run, (v + 1) * seg))


def _causal_seg(t_len):
    return min(4 * LANE, t_len)


def _sel_prompt(iq_rot, ikw_rot, nb, t):
    topk = min(TOPK_MAX, t // 4)
    nq = t // LANE
    wi = IDX_HEADS * IDX_DIM
    return pl.pallas_call(
        functools.partial(_sel_prompt_kernel, topk=topk, tk=min(256, t)),
        grid=(nb, nq),
        in_specs=[pl.BlockSpec((LANE, wi), lambda b, j: (b * nq + j, 0)),
                  pl.BlockSpec((LANE, LANE), lambda b, j: (b * nq + j, 0)),
                  pl.BlockSpec((t, LANE), lambda b, j: (b, 0))],
        out_specs=pl.BlockSpec((LANE, t), lambda b, j: (b * nq + j, 0)),
        out_shape=jax.ShapeDtypeStruct((nb * t, t), MXU_DTYPE),
        scratch_shapes=[pltpu.VMEM((t, LANE), F32), pltpu.VMEM((t, LANE), jnp.int32),
                        pltpu.VMEM((t, LANE), F32)],
        compiler_params=_params(("parallel", "arbitrary")),
        name="sel_prompt",
    )(iq_rot, ikw_rot, ikw_rot)


def _sel_sample_kernel(iqp_ref, w_ref, ikp_ref, ikn_ref, bias_out_ref,
                       sct_scr, sc_scr, key_scr, bias_scr, *, ts, past, topk, n_keys, tk):
    s_pad = sc_scr.shape[0]
    nbp = LANE // ts
    tiles = [(r0, r0 + tk) for r0 in range(0, past, tk)] + [(past, s_pad)]
    for b in range(nbp):
        iqp = iqp_ref[b]
        wb = jnp.broadcast_to(w_ref[b], (LANE, IDX_HEADS * ts)).T
        for r0, r1 in tiles:
            if r0 < past:
                ikt = ikp_ref[b, :, r0:r1]
            else:
                new = jnp.concatenate([ikn_ref[b * ts:(b + 1) * ts, :],
                                       jnp.zeros((LANE - ts, LANE), F32)], axis=0)
                ikt = new.T[0:IDX_DIM, :]
            lg = _mm(iqp, ikt.astype(MXU_DTYPE))
            for c in range((r1 - r0) // LANE):
                acc = None
                for h in range(IDX_HEADS):
                    t = (jnp.maximum(lg[h * ts:(h + 1) * ts, c * LANE:(c + 1) * LANE], 0.0)
                         * wb[h * ts:(h + 1) * ts, :])
                    acc = t if acc is None else acc + t
                sct_scr[b * ts:(b + 1) * ts, r0 + c * LANE:r0 + (c + 1) * LANE] = acc
    for kt in range(s_pad // LANE):
        sc_scr[kt * LANE:(kt + 1) * LANE, :] = sct_scr[:, kt * LANE:(kt + 1) * LANE].T
    q = lax.broadcasted_iota(jnp.int32, (1, LANE), 1) % ts
    limit = ((past + q) // CHUNK + 1) * CHUNK
    _select_bias(sc_scr, key_scr, bias_scr, limit, n_keys, topk)
    for kt in range(s_pad // LANE):
        bias_out_ref[:, kt * LANE:(kt + 1) * LANE] = (
            bias_scr[kt * LANE:(kt + 1) * LANE, :].T.astype(bias_out_ref.dtype))


def _sel_sample(iqp, wrow, ik_past, layer, ikw_rot, row0, nb, ts, past, s_pad):
    n_keys = past + ts
    topk = min(TOPK_MAX, n_keys // 4)
    nbp = LANE // ts
    blk0 = row0 // LANE
    ng = nb // nbp
    return pl.pallas_call(
        functools.partial(_sel_sample_kernel, ts=ts, past=past, topk=topk, n_keys=n_keys,
                          tk=math.gcd(past, 1024)),
        grid=(nb // nbp,),
        in_specs=[pl.BlockSpec((nbp, IDX_HEADS * ts, IDX_DIM), lambda g: (g, 0, 0)),
                  pl.BlockSpec((nbp, 1, IDX_HEADS * ts), lambda g: (g, 0, 0)),
                  pl.BlockSpec((nbp, IDX_DIM, past), lambda g: (layer * ng + g, 0, 0)),
                  pl.BlockSpec((LANE, LANE), lambda g: (blk0 + g, 0))],
        out_specs=pl.BlockSpec((LANE, s_pad), lambda g: (g, 0)),
        out_shape=jax.ShapeDtypeStruct((nb * ts, s_pad), MXU_DTYPE),
        scratch_shapes=[pltpu.VMEM((LANE, s_pad), F32), pltpu.VMEM((s_pad, LANE), F32),
                        pltpu.VMEM((s_pad, LANE), jnp.int32), pltpu.VMEM((s_pad, LANE), F32)],
        compiler_params=_params(("parallel",)),
        name="sel_sample",
    )(iqp, wrow, ik_past, ikw_rot)


def _attend(q_ref, k_ref, v_ref, bias_ref, o_ref, s_eff, tk):
    nq = q_ref.shape[0]
    scale = HEAD_DIM ** -0.5 * math.log2(math.e)
    rows = KV_GROUP * nq
    for g in range(N_KV):
        cols = slice(g * HEAD_DIM, (g + 1) * HEAD_DIM)
        qg = jnp.concatenate(
            [q_ref[:, (g * KV_GROUP + i) * HEAD_DIM:(g * KV_GROUP + i + 1) * HEAD_DIM]
             for i in range(KV_GROUP)], axis=0)
        m = jnp.full((rows, 1), -jnp.inf, F32)
        l = jnp.zeros((rows, 1), F32)
        o = jnp.zeros((rows, HEAD_DIM), F32)
        for r0 in range(0, s_eff, tk):
            r1 = min(r0 + tk, s_eff)
            b = bias_ref[:, r0:r1].astype(F32)
            s = _mm_nt(qg, k_ref[r0:r1, cols]) * scale + jnp.concatenate([b] * KV_GROUP, axis=0)
            m_new = jnp.maximum(m, jnp.max(s, axis=-1, keepdims=True))
            m_safe = jnp.where(m_new == -jnp.inf, 0.0, m_new)
            alpha = jnp.exp2(m - m_safe)
            p = jnp.exp2(s - m_safe)
            l = alpha * l + jnp.sum(p, axis=-1, keepdims=True)
            o = alpha * o + _mm(p.astype(MXU_DTYPE), v_ref[r0:r1, cols])
            m = m_new
        o = o / l
        for i in range(KV_GROUP):
            h = g * KV_GROUP + i
            o_ref[:, h * HEAD_DIM:(h + 1) * HEAD_DIM] = o[i * nq:(i + 1) * nq, :].astype(o_ref.dtype)


def _attn_prompt_kernel(q_ref, k_ref, v_ref, bias_ref, o_ref):
    j = pl.program_id(1)
    t_len = k_ref.shape[0]
    seg = _causal_seg(t_len)

    def run(s_eff):
        _attend(q_ref, k_ref, v_ref, bias_ref, o_ref, s_eff, ATTN_KEY_TILE)

    for v in range(t_len // seg):
        pl.when((j * LANE) // seg == v)(functools.partial(run, (v + 1) * seg))


def _attn_prompt(q_rot, k_bf, v_bf, bias, nb, t):
    nq = t // LANE
    wq = N_HEADS * HEAD_DIM
    wkv = N_KV * HEAD_DIM
    return pl.pallas_call(
        _attn_prompt_kernel,
        grid=(nb, nq),
        in_specs=[pl.BlockSpec((LANE, wq), lambda b, j: (b * nq + j, 0)),
                  pl.BlockSpec((t, wkv), lambda b, j: (b, 0)),
                  pl.BlockSpec((t, wkv), lambda b, j: (b, 0)),
                  pl.BlockSpec((LANE, t), lambda b, j: (b * nq + j, 0))],
        out_specs=pl.BlockSpec((LANE, wq), lambda b, j: (b * nq + j, 0)),
        out_shape=jax.ShapeDtypeStruct((nb * t, wq), MXU_DTYPE),
        compiler_params=_params(("parallel", "arbitrary")),
        name="attn_prompt",
    )(q_rot, k_bf, v_bf, bias)


def _attn_sample_kernel(q_ref, kp_ref, vp_ref, kn_ref, vn_ref, bias_ref, o_ref, k_scr, v_scr,
                        *, ts, past):
    s_pad = k_scr.shape[0]
    wkv = N_KV * HEAD_DIM
    for g in range(N_KV):
        cols = slice(g * HEAD_DIM, (g + 1) * HEAD_DIM)
        k_scr[0:past, cols] = kp_ref[0, pl.ds(g, past, stride=N_KV), :].astype(MXU_DTYPE)
        v_scr[0:past, cols] = vp_ref[0, pl.ds(g, past, stride=N_KV), :].astype(MXU_DTYPE)
    k_scr[past:past + ts, :] = kn_ref[...]
    v_scr[past:past + ts, :] = vn_ref[...]
    k_scr[past + ts:s_pad, :] = jnp.zeros((s_pad - past - ts, wkv), MXU_DTYPE)
    v_scr[past + ts:s_pad, :] = jnp.zeros((s_pad - past - ts, wkv), MXU_DTYPE)
    _attend(q_ref, k_scr, v_scr, bias_ref, o_ref, s_pad, -(-s_pad // (2 * LANE)) * LANE)


def _attn_sample(q_rot, k_past, v_past, layer, k_bf, v_bf, bias, row0, nb, ts, past, s_pad):
    wq = N_HEADS * HEAD_DIM
    wkv = N_KV * HEAD_DIM
    blk0 = row0 // ts
    return pl.pallas_call(
        functools.partial(_attn_sample_kernel, ts=ts, past=past),
        grid=(nb,),
        in_specs=[pl.BlockSpec((ts, wq), lambda b: (blk0 + b, 0)),
                  pl.BlockSpec((1, past * N_KV, HEAD_DIM), lambda b: (layer * nb + b, 0, 0)),
                  pl.BlockSpec((1, past * N_KV, HEAD_DIM), lambda b: (layer * nb + b, 0, 0)),
                  pl.BlockSpec((ts, wkv), lambda b: (blk0 + b, 0)),
                  pl.BlockSpec((ts, wkv), lambda b: (blk0 + b, 0)),
                  pl.BlockSpec((ts, s_pad), lambda b: (b, 0))],
        out_specs=pl.BlockSpec((ts, wq), lambda b: (b, 0)),
        out_shape=jax.ShapeDtypeStruct((nb * ts, wq), MXU_DTYPE),
        scratch_shapes=[pltpu.VMEM((s_pad, wkv), MXU_DTYPE), pltpu.VMEM((s_pad, wkv), MXU_DTYPE)],
        compiler_params=_params(("parallel",)),
        name="attn_sample",
    )(q_rot, k_past, v_past, k_bf, v_bf, bias)


def kernel(x_prompt, x_sample, cache_k, cache_v, cache_idx_k, state_ssm, state_conv, norm_ffn1, ffn1_w1, ffn1_w3, ffn1_w2, norm_mix, w_in, conv_w, conv_b, dt_bias, a_log, d_skip, ssm_norm_w, w_out, norm_ffn2, ffn2_w1, ffn2_w3, ffn2_w2, final_norm):
    bp, tp, d = x_prompt.shape
    bs, ts = x_sample.shape[:2]
    depth, _, past = cache_k.shape[:3]
    mp, ms = bp * tp, bs * ts
    n_keys_s = past + ts
    s_pad = -(-n_keys_s // LANE) * LANE
    hp = SSM_HEADS * SSM_HEADDIM
    wkv = N_KV * HEAD_DIM

    xs = (x_prompt.reshape(mp, d), x_sample.reshape(ms, d))

    tm_prep = _prep_tile(tp, ms, ts)
    pos = jnp.concatenate([jnp.arange(tp, dtype=jnp.int32),
                           jnp.tile(past + jnp.arange(ts, dtype=jnp.int32), tm_prep // ts)])
    iw_scale = jnp.concatenate([jnp.full((IDX_HEADS,), IDX_HEADS ** -0.5, F32),
                                jnp.ones((LANE - IDX_DIM - IDX_HEADS,), F32)])
    tab = jnp.concatenate(_rope_table(pos, ROPE_DIM, HEAD_DIM)
                          + _rope_table(pos, IDX_ROPE_DIM, IDX_DIM)
                          + _rope_table(pos, IDX_ROPE_DIM, IDX_DIM, tail=iw_scale), axis=1)

    w_in_p = _pack_w_in(w_in)
    stack2d = lambda w: w.reshape(w.shape[0] * w.shape[1], w.shape[2])
    ffn_f32 = [tuple(stack2d(w) for w in ws)
               for ws in ((ffn1_w1, ffn1_w3, ffn1_w2), (ffn2_w1, ffn2_w3, ffn2_w2))]
    dff = ffn1_w1.shape[2]
    w_ffn = (_cast_weights(ffn_f32[0][0], d), _cast_weights(ffn_f32[0][1], d),
             _cast_weights(ffn_f32[0][2], dff))
    w_out_b = _cast_weights(stack2d(w_out), depth * w_out.shape[1])
    k_cache = cache_k.reshape(depth * bs, past * N_KV, HEAD_DIM)
    v_cache = cache_v.reshape(depth * bs, past * N_KV, HEAD_DIM)
    ik_cache = jnp.swapaxes(cache_idx_k, 2, 3).reshape(depth * bs, IDX_DIM, past)
    ssm0 = state_ssm.reshape(depth * bs, hp, SSM_STATE)
    conv0_pad = jnp.pad(state_conv, ((0, 0), (0, 0), (SUBLANE - (CONV_W - 1), 0), (0, 0)))
    conv0_pad = conv0_pad.reshape(depth * bs, SUBLANE, CONV_DIM)
    lane_pad = lambda a: jnp.pad(a, (0, LANE - a.shape[0]))[None, :]

    outs = {k: [] for k in ("ikp", "cp", "iks", "cs")}
    kv_out = tuple(jnp.zeros((depth * rows * N_KV, HEAD_DIM), F32) for rows in (mp, ms, mp, ms))
    ssm_p = jnp.zeros((depth * bp, hp, SSM_STATE), F32)
    ssm_s = jnp.zeros((depth * bs, hp, SSM_STATE), F32)
    for l in range(depth):
        x, w_ffn = _ffn(xs if l == 0 else (x,), norm_ffn1[l][None, :], w_ffn, (*ffn_f32[1], l))
        proj = _in_proj(x, norm_mix[l][None, :], w_in_p, l)
        (q_rot, iq_rot, k_bf, v_bf, ikw_rot), kv_out = _prep(
            proj, tab, mp, tp, tm_prep, l, depth, kv_out)

        consts = (conv_w[l], conv_b[l][None, :], lane_pad(dt_bias[l]),
                  lane_pad(-jnp.exp(a_log[l])), jnp.repeat(d_skip[l], SSM_HEADDIM)[None, :],
                  ssm_norm_w[l][None, :])
        y_p, ssm_p, conv_p = _ssd(proj, 0, bp, tp, LANE, LANE, None, consts, l, depth, ssm_p)
        y_s, ssm_s, conv_s = _ssd(proj, mp, bs, ts, ts, LANE, (ssm0, conv0_pad), consts,
                                  l, depth, ssm_s)

        bias_p = _sel_prompt(iq_rot, ikw_rot, bp, tp)
        att_p = _attn_prompt(q_rot, k_bf, v_bf, bias_p, bp, tp)

        iqp = iq_rot[mp:].reshape(bs, ts, IDX_HEADS, IDX_DIM).transpose(0, 2, 1, 3)
        iqp = iqp.reshape(bs, IDX_HEADS * ts, IDX_DIM)
        wrow = ikw_rot[mp:, IDX_DIM:IDX_DIM + IDX_HEADS].reshape(bs, ts, IDX_HEADS)
        wrow = wrow.transpose(0, 2, 1).reshape(bs, 1, IDX_HEADS * ts)
        bias_s = _sel_sample(iqp, wrow, ik_cache, l, ikw_rot, mp, bs, ts, past, s_pad)
        att_s = _attn_sample(q_rot, k_cache, v_cache, l, k_bf, v_bf, bias_s,
                             mp, bs, ts, past, s_pad)

        x = _out_proj(x, (y_p, y_s), (att_p, att_s), w_out_b, l)
        x, w_ffn = _ffn((x,), norm_ffn2[l][None, :], w_ffn,
                        (*ffn_f32[0], l + 1) if l + 1 < depth else None)

        outs["ikp"].append(ikw_rot[:mp, :IDX_DIM].reshape(bp, tp, IDX_DIM))
        outs["cp"].append(conv_p[:, SUBLANE - (CONV_W - 1):, :])
        outs["iks"].append(ikw_rot[mp:, :IDX_DIM].reshape(bs, ts, IDX_DIM))
        outs["cs"].append(conv_s[:, SUBLANE - (CONV_W - 1):, :])

    y_p, y_s = _final_norm(x, final_norm[None, :], mp, ms)
    st = {k: jnp.stack(v) for k, v in outs.items()}
    kp, ks, vp, vs = kv_out
    kv_p = lambda a: a.reshape(depth, bp, tp, N_KV, HEAD_DIM)
    kv_s = lambda a: a.reshape(depth, bs, ts, N_KV, HEAD_DIM)
    return (y_p.reshape(bp, tp, d), y_s.reshape(bs, ts, d),
            kv_p(kp), kv_p(vp), st["ikp"],
            ssm_p.reshape(depth, bp, SSM_HEADS, SSM_HEADDIM, SSM_STATE), st["cp"],
            kv_s(ks), kv_s(vs), st["iks"],
            ssm_s.reshape(depth, bs, SSM_HEADS, SSM_HEADDIM, SSM_STATE), st["cs"])
```

```python
import functools
import math

import jax
import jax.numpy as jnp
from jax import lax
from jax.experimental import pallas as pl
from jax.experimental.pallas import tpu as pltpu

F32 = jnp.float32
MXU_DTYPE = jnp.bfloat16

D_MODEL = 2048
CHUNK = 64
D_SSM = 1024
SSM_HEADDIM = 64
SSM_HEADS = 16
SSM_GROUPS = 2
SSM_STATE = 128
CONV_W = 4
CONV_DIM = D_SSM + 2 * SSM_GROUPS * SSM_STATE
HEAD_DIM = 128
N_HEADS = 8
N_KV = 2
KV_GROUP = N_HEADS // N_KV
ROPE_DIM = HEAD_DIM // 4
IDX_HEADS = 16
IDX_DIM = 64
IDX_ROPE_DIM = IDX_DIM // 4
TOPK_MAX = 256
ROPE_THETA = 500000.0
D_FF = 5632
EPS = 1e-6

LANE = 128
SUBLANE = 8
COUNT_TILE = 128
ATTN_KEY_TILE = 512
VMEM_LIMIT = 56 * 1024 * 1024

COL_Z = 0
COL_Q = 1024
COL_IQ = 2048
COL_XBC = 3072
COL_K = 4608
COL_V = 4864
COL_DT = 5120
COL_IKW = 5248
PROJ_COLS = 5376

KEY_NEG_INF = -2139095041
KEY_POS_INF = 2139095040
INT_MIN = -2147483648

NT_DIMS = (((1,), (1,)), ((), ()))


def _mm(a, b):
    return jnp.dot(a, b, preferred_element_type=F32)


def _mm_nt(a, b):
    return lax.dot_general(a, b, NT_DIMS, preferred_element_type=F32)


def _split3(x):
    hi = x.astype(MXU_DTYPE)
    r = x - hi.astype(F32)
    mid = r.astype(MXU_DTYPE)
    lo = (r - mid.astype(F32)).astype(MXU_DTYPE)
    return hi, mid, lo


def _mm_exact(a, b, exact):
    if exact == "a":
        a01 = a.astype(MXU_DTYPE)
        return sum(_mm(a01, p) for p in _split3(b))
    b01 = b.astype(MXU_DTYPE)
    return sum(_mm(p, b01) for p in _split3(a))


def _silu(x):
    return x * jax.nn.sigmoid(x)


def _params(sem, vmem=VMEM_LIMIT):
    return pltpu.CompilerParams(dimension_semantics=sem, vmem_limit_bytes=vmem)


def _row_tile(m, pref):
    t = pref
    while m % t:
        t //= 2
    return t


def _split_rows(tm, rows_a, rows_b):
    assert rows_a % tm == 0 and rows_b % tm == 0
    na = rows_a // tm

    def spec_a(width, *grid_rest):
        return pl.BlockSpec((tm, width), lambda i, *_: (jnp.minimum(i, na - 1), 0))

    def spec_b(width, *grid_rest):
        return pl.BlockSpec((tm, width), lambda i, *_: (jnp.maximum(i - na, 0), 0))

    return na, spec_a, spec_b


def _pick(i, na, a_ref, b_ref):
    return jnp.where(i < na, a_ref[...], b_ref[...])


def _ffn_kernel(*refs, na, side):
    refs = list(refs)
    if na is None:
        x_ref = refs.pop(0)
        load_x = lambda: x_ref[...]
    else:
        xa_ref, xb_ref = refs.pop(0), refs.pop(0)
        load_x = lambda: _pick(pl.program_id(0), na, xa_ref, xb_ref)
    nw_ref, w1_ref, w3_ref, w2_ref = refs[:4]
    refs = refs[4:]
    if side:
        s1_ref, s3_ref, s2_ref, o_ref, c1_ref, c3_ref, c2_ref, h_scr = refs
    else:
        o_ref, h_scr = refs
    f = pl.program_id(1)

    @pl.when(f == 0)
    def _():
        x = load_x()
        ms = jnp.mean(x * x, axis=-1, keepdims=True)
        h_scr[...] = (x * lax.rsqrt(ms + EPS) * nw_ref[...]).astype(h_scr.dtype)
        o_ref[...] = jnp.zeros_like(o_ref)

    h = h_scr[...]
    a = _mm(h, w1_ref[...])
    b = _mm(h, w3_ref[...])
    g = (_silu(a) * b).astype(h_scr.dtype)
    o_ref[...] += _mm(g, w2_ref[...])

    if side:
        c1_ref[...] = s1_ref[...].astype(c1_ref.dtype)
        c3_ref[...] = s3_ref[...].astype(c3_ref.dtype)
        c2_ref[...] = s2_ref[...].astype(c2_ref.dtype)

    @pl.when(f == pl.num_programs(1) - 1)
    def _():
        o_ref[...] = load_x() + 0.5 * o_ref[...]


def _ffn_row_tile(rows):
    g = math.gcd(*rows)
    for tm in (768, 512, 256, 128):
        if g % tm == 0:
            return tm
    raise ValueError(rows)


def _chunk_rows(total, steps):
    tile = 2 * SUBLANE
    for r in range(tile, total + 1, tile):
        if total % r == 0 and total // r <= steps:
            return r
    raise ValueError((total, steps))


def _ffn(xs, nw, w, nxt=None, tf=512):
    w1, w3, w2 = w
    d = xs[0].shape[1]
    m = sum(x.shape[0] for x in xs)
    dff = w1.shape[1]
    nf = dff // tf
    tm = _ffn_row_tile([x.shape[0] for x in xs])
    if len(xs) == 1:
        na, x_specs = None, [pl.BlockSpec((tm, d), lambda i, f: (i, 0))]
    else:
        na, spec_a, spec_b = _split_rows(tm, xs[0].shape[0], xs[1].shape[0])
        x_specs = [spec_a(d), spec_b(d)]
    in_specs = x_specs + [
        pl.BlockSpec((1, d), lambda i, f: (0, 0)),
        pl.BlockSpec((d, tf), lambda i, f: (0, f)),
        pl.BlockSpec((d, tf), lambda i, f: (0, f)),
        pl.BlockSpec((tf, d), lambda i, f: (f, 0)),
    ]
    out_specs = [pl.BlockSpec((tm, d), lambda i, f: (i, 0))]
    out_shape = [jax.ShapeDtypeStruct((m, d), F32)]
    args = [*xs, nw, w1, w3, w2]
    if nxt is not None:
        s1, s3, s2, layer = nxt
        steps = (m // tm) * nf
        r1, r2 = _chunk_rows(d, steps), _chunk_rows(dff, steps)
        n1, n2 = d // r1, dff // r2

        def chunk(n, base):
            return lambda i, f: (base + jnp.minimum(i * nf + f, n - 1), 0)

        in_specs += [pl.BlockSpec((r1, dff), chunk(n1, layer * n1)),
                     pl.BlockSpec((r1, dff), chunk(n1, layer * n1)),
                     pl.BlockSpec((r2, d), chunk(n2, layer * n2))]
        out_specs += [pl.BlockSpec((r1, dff), chunk(n1, 0)), pl.BlockSpec((r1, dff), chunk(n1, 0)),
                      pl.BlockSpec((r2, d), chunk(n2, 0))]
        out_shape += [jax.ShapeDtypeStruct((d, dff), MXU_DTYPE),
                      jax.ShapeDtypeStruct((d, dff), MXU_DTYPE),
                      jax.ShapeDtypeStruct((dff, d), MXU_DTYPE)]
        args += [s1, s3, s2]
    res = pl.pallas_call(
        functools.partial(_ffn_kernel, na=na, side=nxt is not None),
        grid=(m // tm, nf),
        in_specs=in_specs,
        out_specs=out_specs,
        out_shape=out_shape,
        scratch_shapes=[pltpu.VMEM((tm, d), MXU_DTYPE)],
        compiler_params=_params(("arbitrary", "arbitrary")),
        name="ffn",
    )(*args)
    return res[0], tuple(res[1:])


def _in_proj_kernel(x_ref, nw_ref, w_ref, o_ref, h_scr):
    @pl.when(pl.program_id(1) == 0)
    def _():
        x = x_ref[...]
        ms = jnp.mean(x * x, axis=-1, keepdims=True)
        h_scr[...] = (x * lax.rsqrt(ms + EPS) * nw_ref[...]).astype(h_scr.dtype)

    o_ref[...] = _mm_nt(h_scr[...], w_ref[...])


def _in_proj(x, nw, w, layer, tm=1024, tn=1792):
    m, d = x.shape
    n = PROJ_COLS
    nn = n // tn
    tm = _row_tile(m, tm)
    return pl.pallas_call(
        _in_proj_kernel,
        grid=(m // tm, nn),
        in_specs=[
            pl.BlockSpec((tm, d), lambda i, j: (i, 0)),
            pl.BlockSpec((1, d), lambda i, j: (0, 0)),
            pl.BlockSpec((tn, d), lambda i, j: (layer * nn + j, 0)),
        ],
        out_specs=pl.BlockSpec((tm, tn), lambda i, j: (i, j)),
        out_shape=jax.ShapeDtypeStruct((m, n), F32),
        scratch_shapes=[pltpu.VMEM((tm, d), MXU_DTYPE)],
        compiler_params=_params(("parallel", "arbitrary")),
        name="in_proj",
    )(x, nw, w)


def _out_proj_kernel(x_ref, ya_ref, yb_ref, aa_ref, ab_ref, w_ref, o_ref, *, na):
    i = pl.program_id(0)
    half = ya_ref.shape[1]
    o_ref[...] = (x_ref[...] + _mm(_pick(i, na, ya_ref, yb_ref), w_ref[:half, :])
                  + _mm(_pick(i, na, aa_ref, ab_ref), w_ref[half:, :]))


def _out_proj(x, y_pair, att_pair, w, layer, tm=512):
    m, d = x.shape
    half = y_pair[0].shape[1]
    tm = _row_tile(math.gcd(y_pair[0].shape[0], y_pair[1].shape[0]), tm)
    na, spec_a, spec_b = _split_rows(tm, y_pair[0].shape[0], y_pair[1].shape[0])
    return pl.pallas_call(
        functools.partial(_out_proj_kernel, na=na),
        grid=(m // tm,),
        in_specs=[
            pl.BlockSpec((tm, d), lambda i: (i, 0)),
            spec_a(half), spec_b(half), spec_a(half), spec_b(half),
            pl.BlockSpec((2 * half, d), lambda i: (layer, 0)),
        ],
        out_specs=pl.BlockSpec((tm, d), lambda i: (i, 0)),
        out_shape=jax.ShapeDtypeStruct((m, d), F32),
        compiler_params=_params(("arbitrary",)),
        name="out_proj",
    )(x, *y_pair, *att_pair, w)


def _final_norm_kernel(x_ref, nw_ref, oa_ref, ob_ref, *, na):
    i = pl.program_id(0)
    x = x_ref[...]
    ms = jnp.mean(x * x, axis=-1, keepdims=True)
    y = x * lax.rsqrt(ms + EPS) * nw_ref[...]

    @pl.when(i < na)
    def _():
        oa_ref[...] = y

    @pl.when(i >= na)
    def _():
        ob_ref[...] = y


def _final_norm(x, nw, rows_a, rows_b, tm=512):
    d = x.shape[1]
    tm = _row_tile(math.gcd(rows_a, rows_b), tm)
    na, spec_a, spec_b = _split_rows(tm, rows_a, rows_b)
    return pl.pallas_call(
        functools.partial(_final_norm_kernel, na=na),
        grid=((rows_a + rows_b) // tm,),
        in_specs=[pl.BlockSpec((tm, d), lambda i: (i, 0)),
                  pl.BlockSpec((1, d), lambda i: (0, 0))],
        out_specs=[spec_a(d), spec_b(d)],
        out_shape=[jax.ShapeDtypeStruct((rows_a, d), F32),
                   jax.ShapeDtypeStruct((rows_b, d), F32)],
        compiler_params=_params(("arbitrary",)),
        name="final_norm",
    )(x, nw)


def _cast_kernel(x_ref, o_ref):
    o_ref[...] = x_ref[...].astype(o_ref.dtype)


def _cast_weights(w2, n_rows, row0=0, rows=512):
    c = w2.shape[1]
    rows = _row_tile(math.gcd(n_rows, row0) if row0 else n_rows, rows)
    blk0 = row0 // rows
    return pl.pallas_call(
        _cast_kernel,
        grid=(n_rows // rows,),
        in_specs=[pl.BlockSpec((rows, c), lambda i: (blk0 + i, 0))],
        out_specs=pl.BlockSpec((rows, c), lambda i: (i, 0)),
        out_shape=jax.ShapeDtypeStruct((n_rows, c), MXU_DTYPE),
        compiler_params=_params(("parallel",)),
        name="cast_w",
    )(w2)


_W_IN_PARTS = (
    (COL_Z, 0, D_SSM),
    (COL_XBC, D_SSM, CONV_DIM),
    (COL_DT, D_SSM + CONV_DIM, SSM_HEADS),
    (COL_Q, D_SSM + CONV_DIM + SSM_HEADS, N_HEADS * HEAD_DIM),
    (COL_K, D_SSM + CONV_DIM + SSM_HEADS + N_HEADS * HEAD_DIM, N_KV * HEAD_DIM),
    (COL_V, D_SSM + CONV_DIM + SSM_HEADS + (N_HEADS + N_KV) * HEAD_DIM, N_KV * HEAD_DIM),
    (COL_IQ, D_SSM + CONV_DIM + SSM_HEADS + (N_HEADS + 2 * N_KV) * HEAD_DIM, IDX_HEADS * IDX_DIM),
    (COL_IKW, D_SSM + CONV_DIM + SSM_HEADS + (N_HEADS + 2 * N_KV) * HEAD_DIM + IDX_HEADS * IDX_DIM,
     IDX_DIM + IDX_HEADS),
)


def _pack_w_in_kernel(w_ref, o_ref):
    cols = o_ref.shape[1]
    for dst, src, width in _W_IN_PARTS:
        o_ref[dst:dst + width, :] = w_ref[src:src + width, :].astype(o_ref.dtype)
        pad = -width % LANE
        if pad:
            o_ref[dst + width:dst + width + pad, :] = jnp.zeros((pad, cols), o_ref.dtype)


def _pack_w_in(w, kc=256):
    depth, d, c = w.shape
    wt = jnp.transpose(w, (0, 2, 1)).reshape(depth * c, d)
    return pl.pallas_call(
        _pack_w_in_kernel,
        grid=(depth, d // kc),
        in_specs=[pl.BlockSpec((c, kc), lambda l, k: (l, k))],
        out_specs=pl.BlockSpec((PROJ_COLS, kc), lambda l, k: (l, k)),
        out_shape=jax.ShapeDtypeStruct((depth * PROJ_COLS, d), MXU_DTYPE),
        compiler_params=_params(("parallel", "parallel")),
        name="pack_w_in",
    )(wt)


def _rope_lanes(x, c, s1, s2, half):
    outs = []
    for t in range(x.shape[1] // LANE):
        xt = x[:, t * LANE:(t + 1) * LANE]
        outs.append(xt * c + pltpu.roll(xt, half, 1) * s1
                    + pltpu.roll(xt, LANE - half, 1) * s2)
    return outs[0] if len(outs) == 1 else jnp.concatenate(outs, axis=1)


def _prep_kernel(*refs, na):
    q_ref, iq_ref, k_ref, v_ref, ikw_ref, tab_ref = refs[:6]
    (qo_ref, iqo_ref, kbo_ref, vbo_ref, ikwo_ref, kp_ref, ks_ref, vp_ref, vs_ref) = refs[10:]
    i = pl.program_id(0)
    tm = k_ref.shape[0]
    tab = [tab_ref[:, j * LANE:(j + 1) * LANE] for j in range(9)]
    qo_ref[...] = _rope_lanes(q_ref[...], tab[0], tab[1], tab[2], ROPE_DIM // 2).astype(qo_ref.dtype)
    k = _rope_lanes(k_ref[...], tab[0], tab[1], tab[2], ROPE_DIM // 2)
    kbo_ref[...] = k.astype(kbo_ref.dtype)
    iqo_ref[...] = _rope_lanes(iq_ref[...], tab[3], tab[4], tab[5], IDX_ROPE_DIM // 2).astype(iqo_ref.dtype)
    ikwo_ref[...] = _rope_lanes(ikw_ref[...], tab[6], tab[7], tab[8], IDX_ROPE_DIM // 2)
    v = v_ref[...]
    vbo_ref[...] = v.astype(vbo_ref.dtype)

    def put(dst_k, dst_v):
        for g in range(N_KV):
            dst_k[pl.ds(g, tm, stride=N_KV), :] = k[:, g * HEAD_DIM:(g + 1) * HEAD_DIM]
            dst_v[pl.ds(g, tm, stride=N_KV), :] = v[:, g * HEAD_DIM:(g + 1) * HEAD_DIM]

    pl.when(i < na)(functools.partial(put, kp_ref, vp_ref))
    pl.when(i >= na)(functools.partial(put, ks_ref, vs_ref))


def _prep_tile(tp, ms, ts, tm=512):
    tm = _row_tile(math.gcd(tp, ms), tm)
    assert tm % ts == 0
    return tm


def _prep(proj, tab, mp, tp, tm, layer, depth, caches):
    m = proj.shape[0]
    ms = m - mp
    wq = N_HEADS * HEAD_DIM
    wkv = N_KV * HEAD_DIM
    na = mp // tm
    nt = tp // tm
    bp = na // nt
    rt = lambda i: jnp.where(i < na, (i % bp) * nt + i // bp, i)
    row = lambda w, c: pl.BlockSpec((tm, w), lambda i: (rt(i), c))
    tab_spec = pl.BlockSpec((tm, 9 * LANE), lambda i: (jnp.where(i < na, i // bp, nt), 0))
    p_spec = pl.BlockSpec((N_KV * tm, HEAD_DIM),
                          lambda i: (layer * na + rt(jnp.minimum(i, na - 1)), 0))
    s_spec = pl.BlockSpec((N_KV * tm, HEAD_DIM),
                          lambda i: (layer * (ms // tm) + jnp.maximum(i - na, 0), 0))
    p_shape = jax.ShapeDtypeStruct((depth * mp * N_KV, HEAD_DIM), F32)
    s_shape = jax.ShapeDtypeStruct((depth * ms * N_KV, HEAD_DIM), F32)
    in_specs = [row(wq, COL_Q // wq), row(wq, COL_IQ // wq), row(wkv, COL_K // wkv),
                row(wkv, COL_V // wkv), row(LANE, COL_IKW // LANE), tab_spec]
    in_specs += [pl.BlockSpec(memory_space=pl.ANY)] * 4
    args = [proj, proj, proj, proj, proj, tab, *caches]
    aliases = {6 + j: 5 + j for j in range(4)}
    res = pl.pallas_call(
        functools.partial(_prep_kernel, na=na),
        grid=(m // tm,),
        in_specs=in_specs,
        out_specs=[row(wq, 0), row(wq, 0), row(wkv, 0), row(wkv, 0), row(LANE, 0),
                   p_spec, s_spec, p_spec, s_spec],
        out_shape=[jax.ShapeDtypeStruct((m, wq), MXU_DTYPE),
                   jax.ShapeDtypeStruct((m, wq), MXU_DTYPE),
                   jax.ShapeDtypeStruct((m, wkv), MXU_DTYPE),
                   jax.ShapeDtypeStruct((m, wkv), MXU_DTYPE),
                   jax.ShapeDtypeStruct((m, LANE), F32),
                   p_shape, s_shape, p_shape, s_shape],
        input_output_aliases=aliases,
        compiler_params=_params(("arbitrary",)),
        name="prep",
    )(*args)
    return res[:5], tuple(res[5:])


def _rope_table(pos, rot_dim, period, tail=None):
    half = rot_dim // 2
    inv_freq = jnp.float32(ROPE_THETA) ** (-jnp.arange(half, dtype=F32) * 2.0 / rot_dim)
    ang = pos.astype(F32)[:, None] * inv_freq[None, :]
    cos, sin = jnp.cos(ang), jnp.sin(ang)
    n = pos.shape[0]
    ones = jnp.ones((n, period - rot_dim), F32)
    zeros = jnp.zeros((n, period - rot_dim), F32)
    zh = jnp.zeros((n, half), F32)
    c = jnp.concatenate([cos, cos, ones], axis=1)
    s1 = jnp.concatenate([zh, sin, zeros], axis=1)
    s2 = jnp.concatenate([-sin, zh, zeros], axis=1)
    if tail is None:
        reps = LANE // period
        return [jnp.tile(a, (1, reps)) for a in (c, s1, s2)]
    zt = jnp.zeros((n, LANE - period), F32)
    return [jnp.concatenate([c, jnp.broadcast_to(tail[None, :], (n, LANE - period))], axis=1),
            jnp.concatenate([s1, zt], axis=1), jnp.concatenate([s2, zt], axis=1)]


def _ssd_kernel(*refs, qv, qp, has_state):
    refs = list(refs)
    z_ref, xbc_ref, dt_ref = refs[:3]
    refs = refs[3:]
    if has_state:
        ssm0_ref, conv0_ref = refs[:2]
        refs = refs[2:]
    cw_ref, cb_ref, dtb_ref, aneg_ref, dsk_ref, nw_ref = refs[:6]
    y_ref, ssm_ref, conv_ref, ext_scr, st_scr = refs[7:]
    c = pl.program_id(1)
    hp = SSM_HEADS * SSM_HEADDIM
    gw = SSM_STATE
    hpg = hp // SSM_GROUPS

    @pl.when(c == 0)
    def _():
        if has_state:
            ext_scr[0:SUBLANE, :] = conv0_ref[0]
            st_scr[...] = ssm0_ref[0].T
        else:
            ext_scr[0:SUBLANE, :] = jnp.zeros((SUBLANE, CONV_DIM), F32)
            st_scr[...] = jnp.zeros_like(st_scr)

    xbc_raw = xbc_ref[...]
    ext_scr[SUBLANE:SUBLANE + qv, :] = xbc_raw
    conv = cb_ref[...] + xbc_raw * cw_ref[CONV_W - 1:CONV_W, :]
    for j in range(CONV_W - 1):
        off = SUBLANE - (CONV_W - 1) + j
        conv = conv + ext_scr[off:off + qv, :] * cw_ref[j:j + 1, :]
    xc = _silu(conv)
    dtr = dt_ref[...]
    if qv < qp:
        xc = jnp.concatenate([xc, jnp.zeros((qp - qv, CONV_DIM), F32)], axis=0)
        dtr = jnp.concatenate([dtr, jnp.zeros((qp - qv, LANE), F32)], axis=0)
    xs = xc[:, :hp]
    bm = xc[:, hp:hp + SSM_GROUPS * gw]
    cm = xc[:, hp + SSM_GROUPS * gw:]

    xdt_in = dtr + dtb_ref[...]
    dt = jnp.maximum(xdt_in, 0.0) + jnp.log1p(jnp.exp(-jnp.abs(xdt_in)))
    row = lax.broadcasted_iota(jnp.int32, (qp, LANE), 0)
    dt = jnp.where(row < qv, dt, 0.0)
    la = dt * aneg_ref[...]
    ti = lax.broadcasted_iota(jnp.int32, (qp, qp), 0)
    si = lax.broadcasted_iota(jnp.int32, (qp, qp), 1)
    tri = ti >= si
    cum = _mm_exact(tri.astype(F32), la, "a")
    cum_t = cum.T

    eh = lax.broadcasted_iota(jnp.int32, (LANE, hp), 0)
    ec = lax.broadcasted_iota(jnp.int32, (LANE, hp), 1)
    expand = (ec // SSM_HEADDIM == eh).astype(F32)
    dt_x = _mm_exact(dt, expand, "b")
    cum_x = _mm_exact(cum, expand, "b")
    xdt = xs * dt_x

    cb = [_mm_nt(cm[:, g * gw:(g + 1) * gw].astype(MXU_DTYPE),
                 bm[:, g * gw:(g + 1) * gw].astype(MXU_DTYPE)) for g in range(SSM_GROUPS)]
    lane = lax.broadcasted_iota(jnp.int32, (qp, LANE), 1)
    y_parts = []
    for j in range(SSM_HEADS // 2):
        ms = []
        for h in (2 * j, 2 * j + 1):
            d = cum[:, h:h + 1] - cum_t[h:h + 1, :]
            seg = jnp.exp(jnp.where(tri, d, -jnp.inf))
            ms.append((cb[h // (SSM_HEADS // SSM_GROUPS)] * seg).astype(MXU_DTYPE))
        xp = xdt[:, j * LANE:(j + 1) * LANE]
        rhs = jnp.concatenate([jnp.where(lane < SSM_HEADDIM, xp, 0.0),
                               jnp.where(lane >= SSM_HEADDIM, xp, 0.0)], axis=0)
        y_parts.append(_mm(jnp.concatenate(ms, axis=1), rhs.astype(MXU_DTYPE)))
    y = jnp.concatenate(y_parts, axis=1)

    st = st_scr[...]
    y_st = jnp.concatenate(
        [_mm(cm[:, g * gw:(g + 1) * gw].astype(MXU_DTYPE),
             st[:, g * hpg:(g + 1) * hpg].astype(MXU_DTYPE)) for g in range(SSM_GROUPS)], axis=1)
    y = y + y_st * jnp.exp(cum_x) + dsk_ref[...] * xs

    cum_last = cum_x[qp - 1:qp, :]
    xt = (xdt * jnp.exp(cum_last - cum_x)).astype(MXU_DTYPE)
    upd = jnp.concatenate(
        [_mm(bm[:, g * gw:(g + 1) * gw].T.astype(MXU_DTYPE), xt[:, g * hpg:(g + 1) * hpg])
         for g in range(SSM_GROUPS)], axis=1)
    st_new = st * jnp.exp(cum_last) + upd
    st_scr[...] = st_new

    zz = z_ref[...]
    gated = y[:qv, :] * _silu(zz)
    ms2 = jnp.mean(gated * gated, axis=-1, keepdims=True)
    y_ref[...] = (gated * lax.rsqrt(ms2 + EPS) * nw_ref[...]).astype(y_ref.dtype)

    tail_rows = ext_scr[qv:qv + SUBLANE, :]
    ext_scr[0:SUBLANE, :] = tail_rows

    @pl.when(c == pl.num_programs(1) - 1)
    def _():
        ssm_ref[0] = st_new.T
        conv_ref[0] = tail_rows


def _ssd(proj, row0, nb, t, qv, qp, state, consts, layer, depth, ssm_buf):
    nc = t // qv
    hp = SSM_HEADS * SSM_HEADDIM
    blk0 = row0 // qv
    rowspec = lambda w, cb_: pl.BlockSpec((qv, w), lambda b, c: (blk0 + b * nc + c, cb_))
    const = lambda a: pl.BlockSpec(a.shape, lambda b, c: (0,) * a.ndim)
    in_specs = [rowspec(hp, COL_Z // hp), rowspec(CONV_DIM, COL_XBC // CONV_DIM),
                rowspec(LANE, COL_DT // LANE)]
    args = [proj, proj, proj]
    if state is not None:
        ssm0, conv0 = state
        in_specs += [pl.BlockSpec((1, hp, SSM_STATE), lambda b, c: (layer * nb + b, 0, 0)),
                     pl.BlockSpec((1, SUBLANE, CONV_DIM), lambda b, c: (layer * nb + b, 0, 0))]
        args += [ssm0, conv0]
    in_specs += [const(a) for a in consts]
    args += list(consts)
    in_specs.append(pl.BlockSpec(memory_space=pl.ANY))
    aliases = {len(args): 1}
    args.append(ssm_buf)
    return pl.pallas_call(
        functools.partial(_ssd_kernel, qv=qv, qp=qp, has_state=state is not None),
        grid=(nb, nc),
        in_specs=in_specs,
        out_specs=[pl.BlockSpec((qv, hp), lambda b, c: (b * nc + c, 0)),
                   pl.BlockSpec((1, hp, SSM_STATE), lambda b, c: (layer * nb + b, 0, 0)),
                   pl.BlockSpec((1, SUBLANE, CONV_DIM), lambda b, c: (b, 0, 0))],
        out_shape=[jax.ShapeDtypeStruct((nb * t, hp), MXU_DTYPE),
                   jax.ShapeDtypeStruct((depth * nb, hp, SSM_STATE), F32),
                   jax.ShapeDtypeStruct((nb, SUBLANE, CONV_DIM), F32)],
        scratch_shapes=[pltpu.VMEM((qv + SUBLANE, CONV_DIM), F32),
                        pltpu.VMEM((SSM_STATE, hp), F32)],
        input_output_aliases=aliases,
        compiler_params=_params(("arbitrary", "arbitrary")),
        name="ssd_state" if state is not None else "ssd_prompt",
    )(*args)


def _row_tiles(n_rows):
    assert n_rows % COUNT_TILE == 0
    return [(r0, r0 + COUNT_TILE) for r0 in range(0, n_rows, COUNT_TILE)]


def _row_iota(r0, r1):
    return lax.broadcasted_iota(jnp.int32, (r1 - r0, LANE), 0) + r0


def _count_rows(fn, n_rows):
    acc = None
    for r0, r1 in _row_tiles(n_rows):
        part = fn(r0, r1).astype(F32)
        acc = part if acc is None else acc + part
    return jnp.sum(acc, axis=0, keepdims=True)


def _select_bias(sc_ref, key_ref, bias_ref, limit, n_keys, topk):
    s_pad = sc_ref.shape[0]
    tiles = _row_tiles(s_pad)
    for r0, r1 in tiles:
        s_io = _row_iota(r0, r1)
        adm = (s_io < limit) & (s_io < n_keys)
        bits = pltpu.bitcast(jnp.where(adm, sc_ref[r0:r1, :], -jnp.inf), jnp.int32)
        key_ref[r0:r1, :] = bits ^ ((bits >> 31) & jnp.int32(0x7FFFFFFF))
    kf = jnp.float32(topk)

    cnt0 = _count_rows(lambda r0, r1: key_ref[r0:r1, :] >= 0, s_pad)
    pos = cnt0 >= kf
    prefix = jnp.where(pos, jnp.int32(0), jnp.int32(INT_MIN))
    n_ge = jnp.where(pos, cnt0, jnp.float32(s_pad))

    def body(i, carry):
        prefix, n_ge = carry
        cand = prefix | jnp.left_shift(jnp.int32(1), 30 - i)
        cnt = _count_rows(lambda r0, r1: key_ref[r0:r1, :] >= cand, s_pad)
        take = cnt >= kf
        return jnp.where(take, cand, prefix), jnp.where(take, cnt, n_ge)

    thr, n_ge = lax.fori_loop(0, 31, body, (prefix, n_ge))

    def finite(key):
        return (key > KEY_NEG_INF) & (key < KEY_POS_INF)

    for r0, r1 in tiles:
        key = key_ref[r0:r1, :]
        bias_ref[r0:r1, :] = jnp.where((key >= thr) & finite(key), 0.0, -jnp.inf)

    tie = (n_ge > kf) & (thr > KEY_NEG_INF)
    any_tie = jnp.max(tie.astype(F32)) > 0.0

    @pl.when(any_tie)
    def _():
        need = kf - _count_rows(lambda r0, r1: key_ref[r0:r1, :] > thr, s_pad)
        nbits = max(1, (s_pad - 1).bit_length())

        def ibody(i, v):
            cand = v | jnp.left_shift(jnp.int32(1), nbits - 1 - i)
            below = _count_rows(
                lambda r0, r1: (key_ref[r0:r1, :] == thr) & (_row_iota(r0, r1) < cand), s_pad)
            return jnp.where(below < need, cand, v)

        last = lax.fori_loop(0, nbits, ibody, jnp.zeros((1, LANE), jnp.int32))
        for r0, r1 in tiles:
            key = key_ref[r0:r1, :]
            sel = (key > thr) | ((key == thr) & (_row_iota(r0, r1) <= last))
            bias_ref[r0:r1, :] = jnp.where(sel & finite(key), 0.0, -jnp.inf)


def _sel_prompt_kernel(iq_ref, ikw_q_ref, ikw_all_ref, bias_out_ref, sc_scr, key_scr, bias_scr,
                       *, topk, tk):
    j = pl.program_id(1)
    t_len = ikw_all_ref.shape[0]
    seg = _causal_seg(t_len)
    pos = j * LANE + lax.broadcasted_iota(jnp.int32, (1, LANE), 1)
    limit = (pos // CHUNK + 1) * CHUNK

    def run(s_eff):
        w_t = ikw_q_ref[...].T
        iq = iq_ref[...]
        for kt in range(s_eff // tk):
            ik = ikw_all_ref[kt * tk:(kt + 1) * tk, 0:IDX_DIM].astype(MXU_DTYPE)
            acc = jnp.zeros((tk, LANE), F32)
            for h in range(IDX_HEADS):
                lt = _mm_nt(ik, iq[:, h * IDX_DIM:(h + 1) * IDX_DIM])
                acc = acc + jnp.maximum(lt, 0.0) * w_t[IDX_DIM + h:IDX_DIM + h + 1, :]
            sc_scr[kt * tk:(kt + 1) * tk, :] = acc
        if s_eff == seg:
            few = (j + 1) * LANE <= topk

            @pl.when(few)
            def _():
                for r0, r1 in _row_tiles(s_eff):
                    sc = sc_scr[r0:r1, :]
                    ok = (_row_iota(r0, r1) < limit) & (jnp.abs(sc) < jnp.inf)
                    bias_scr[r0:r1, :] = jnp.where(ok, 0.0, -jnp.inf)

            @pl.when(jnp.logical_not(few))
            def _():
                _select_bias(sc_scr.at[0:s_eff], key_scr.at[0:s_eff], bias_scr.at[0:s_eff],
                             limit, t_len, topk)
        else:
            _select_bias(sc_scr.at[0:s_eff], key_scr.at[0:s_eff], bias_scr.at[0:s_eff],
                         limit, t_len, topk)
        for kt in range(s_eff // LANE):
            bias_out_ref[:, kt * LANE:(kt + 1) * LANE] = (
                bias_scr[kt * LANE:(kt + 1) * LANE, :].T.astype(bias_out_ref.dtype))
        if s_eff < t_len:
            bias_out_ref[:, s_eff:t_len] = jnp.full((LANE, t_len - s_eff), -jnp.inf,
                                                    bias_out_ref.dtype)

    for v in range(t_len // seg):
        pl.when((j * LANE) // seg == v)(functools.partial(run, (v + 1) * seg))


def _causal_seg(t_len):
    return min(4 * LANE, t_len)


def _sel_prompt(iq_rot, ikw_rot, nb, t):
    topk = min(TOPK_MAX, t // 4)
    nq = t // LANE
    wi = IDX_HEADS * IDX_DIM
    return pl.pallas_call(
        functools.partial(_sel_prompt_kernel, topk=topk, tk=min(256, t)),
        grid=(nb, nq),
        in_specs=[pl.BlockSpec((LANE, wi), lambda b, j: (b * nq + j, 0)),
                  pl.BlockSpec((LANE, LANE), lambda b, j: (b * nq + j, 0)),
                  pl.BlockSpec((t, LANE), lambda b, j: (b, 0))],
        out_specs=pl.BlockSpec((LANE, t), lambda b, j: (b * nq + j, 0)),
        out_shape=jax.ShapeDtypeStruct((nb * t, t), MXU_DTYPE),
        scratch_shapes=[pltpu.VMEM((t, LANE), F32), pltpu.VMEM((t, LANE), jnp.int32),
                        pltpu.VMEM((t, LANE), F32)],
        compiler_params=_params(("parallel", "arbitrary")),
        name="sel_prompt",
    )(iq_rot, ikw_rot, ikw_rot)


def _sel_sample_kernel(iqp_ref, w_ref, ikp_ref, ikn_ref, bias_out_ref,
                       sct_scr, sc_scr, key_scr, bias_scr, *, ts, past, topk, n_keys, tk):
    s_pad = sc_scr.shape[0]
    nbp = LANE // ts
    tiles = [(r0, r0 + tk) for r0 in range(0, past, tk)] + [(past, s_pad)]
    for b in range(nbp):
        iqp = iqp_ref[b]
        wb = jnp.broadcast_to(w_ref[b], (LANE, IDX_HEADS * ts)).T
        for r0, r1 in tiles:
            if r0 < past:
                ikt = ikp_ref[b, :, r0:r1]
            else:
                new = jnp.concatenate([ikn_ref[b * ts:(b + 1) * ts, :],
                                       jnp.zeros((LANE - ts, LANE), F32)], axis=0)
                ikt = new.T[0:IDX_DIM, :]
            lg = _mm(iqp, ikt.astype(MXU_DTYPE))
            for c in range((r1 - r0) // LANE):
                acc = None
                for h in range(IDX_HEADS):
                    t = (jnp.maximum(lg[h * ts:(h + 1) * ts, c * LANE:(c + 1) * LANE], 0.0)
                         * wb[h * ts:(h + 1) * ts, :])
                    acc = t if acc is None else acc + t
                sct_scr[b * ts:(b + 1) * ts, r0 + c * LANE:r0 + (c + 1) * LANE] = acc
    for kt in range(s_pad // LANE):
        sc_scr[kt * LANE:(kt + 1) * LANE, :] = sct_scr[:, kt * LANE:(kt + 1) * LANE].T
    q = lax.broadcasted_iota(jnp.int32, (1, LANE), 1) % ts
    limit = ((past + q) // CHUNK + 1) * CHUNK
    _select_bias(sc_scr, key_scr, bias_scr, limit, n_keys, topk)
    for kt in range(s_pad // LANE):
        bias_out_ref[:, kt * LANE:(kt + 1) * LANE] = (
            bias_scr[kt * LANE:(kt + 1) * LANE, :].T.astype(bias_out_ref.dtype))


def _sel_sample(iqp, wrow, ik_past, layer, ikw_rot, row0, nb, ts, past, s_pad):
    n_keys = past + ts
    topk = min(TOPK_MAX, n_keys // 4)
    nbp = LANE // ts
    blk0 = row0 // LANE
    ng = nb // nbp
    return pl.pallas_call(
        functools.partial(_sel_sample_kernel, ts=ts, past=past, topk=topk, n_keys=n_keys,
                          tk=math.gcd(past, 1024)),
        grid=(nb // nbp,),
        in_specs=[pl.BlockSpec((nbp, IDX_HEADS * ts, IDX_DIM), lambda g: (g, 0, 0)),
                  pl.BlockSpec((nbp, 1, IDX_HEADS * ts), lambda g: (g, 0, 0)),
                  pl.BlockSpec((nbp, IDX_DIM, past), lambda g: (layer * ng + g, 0, 0)),
                  pl.BlockSpec((LANE, LANE), lambda g: (blk0 + g, 0))],
        out_specs=pl.BlockSpec((LANE, s_pad), lambda g: (g, 0)),
        out_shape=jax.ShapeDtypeStruct((nb * ts, s_pad), MXU_DTYPE),
        scratch_shapes=[pltpu.VMEM((LANE, s_pad), F32), pltpu.VMEM((s_pad, LANE), F32),
                        pltpu.VMEM((s_pad, LANE), jnp.int32), pltpu.VMEM((s_pad, LANE), F32)],
        compiler_params=_params(("parallel",)),
        name="sel_sample",
    )(iqp, wrow, ik_past, ikw_rot)


def _attend(q_ref, k_ref, v_ref, bias_ref, o_ref, s_eff, tk):
    nq = q_ref.shape[0]
    scale = HEAD_DIM ** -0.5 * math.log2(math.e)
    rows = KV_GROUP * nq
    for g in range(N_KV):
        cols = slice(g * HEAD_DIM, (g + 1) * HEAD_DIM)
        qg = jnp.concatenate(
            [q_ref[:, (g * KV_GROUP + i) * HEAD_DIM:(g * KV_GROUP + i + 1) * HEAD_DIM]
             for i in range(KV_GROUP)], axis=0)
        m = jnp.full((rows, 1), -jnp.inf, F32)
        l = jnp.zeros((rows, 1), F32)
        o = jnp.zeros((rows, HEAD_DIM), F32)
        for r0 in range(0, s_eff, tk):
            r1 = min(r0 + tk, s_eff)
            b = bias_ref[:, r0:r1].astype(F32)
            s = _mm_nt(qg, k_ref[r0:r1, cols]) * scale + jnp.concatenate([b] * KV_GROUP, axis=0)
            m_new = jnp.maximum(m, jnp.max(s, axis=-1, keepdims=True))
            m_safe = jnp.where(m_new == -jnp.inf, 0.0, m_new)
            alpha = jnp.exp2(m - m_safe)
            p = jnp.exp2(s - m_safe)
            l = alpha * l + jnp.sum(p, axis=-1, keepdims=True)
            o = alpha * o + _mm(p.astype(MXU_DTYPE), v_ref[r0:r1, cols])
            m = m_new
        o = o / l
        for i in range(KV_GROUP):
            h = g * KV_GROUP + i
            o_ref[:, h * HEAD_DIM:(h + 1) * HEAD_DIM] = o[i * nq:(i + 1) * nq, :].astype(o_ref.dtype)


def _attn_prompt_kernel(q_ref, k_ref, v_ref, bias_ref, o_ref):
    j = pl.program_id(1)
    t_len = k_ref.shape[0]
    seg = _causal_seg(t_len)

    def run(s_eff):
        _attend(q_ref, k_ref, v_ref, bias_ref, o_ref, s_eff, ATTN_KEY_TILE)

    for v in range(t_len // seg):
        pl.when((j * LANE) // seg == v)(functools.partial(run, (v + 1) * seg))


def _attn_prompt(q_rot, k_bf, v_bf, bias, nb, t):
    nq = t // LANE
    wq = N_HEADS * HEAD_DIM
    wkv = N_KV * HEAD_DIM
    return pl.pallas_call(
        _attn_prompt_kernel,
        grid=(nb, nq),
        in_specs=[pl.BlockSpec((LANE, wq), lambda b, j: (b * nq + j, 0)),
                  pl.BlockSpec((t, wkv), lambda b, j: (b, 0)),
                  pl.BlockSpec((t, wkv), lambda b, j: (b, 0)),
                  pl.BlockSpec((LANE, t), lambda b, j: (b * nq + j, 0))],
        out_specs=pl.BlockSpec((LANE, wq), lambda b, j: (b * nq + j, 0)),
        out_shape=jax.ShapeDtypeStruct((nb * t, wq), MXU_DTYPE),
        compiler_params=_params(("parallel", "arbitrary")),
        name="attn_prompt",
    )(q_rot, k_bf, v_bf, bias)


def _attn_sample_kernel(q_ref, kp_ref, vp_ref, kn_ref, vn_ref, bias_ref, o_ref, k_scr, v_scr,
                        *, ts, past):
    s_pad = k_scr.shape[0]
    wkv = N_KV * HEAD_DIM
    for g in range(N_KV):
        cols = slice(g * HEAD_DIM, (g + 1) * HEAD_DIM)
        k_scr[0:past, cols] = kp_ref[0, pl.ds(g, past, stride=N_KV), :].astype(MXU_DTYPE)
        v_scr[0:past, cols] = vp_ref[0, pl.ds(g, past, stride=N_KV), :].astype(MXU_DTYPE)
    k_scr[past:past + ts, :] = kn_ref[...]
    v_scr[past:past + ts, :] = vn_ref[...]
    k_scr[past + ts:s_pad, :] = jnp.zeros((s_pad - past - ts, wkv), MXU_DTYPE)
    v_scr[past + ts:s_pad, :] = jnp.zeros((s_pad - past - ts, wkv), MXU_DTYPE)
    _attend(q_ref, k_scr, v_scr, bias_ref, o_ref, s_pad, -(-s_pad // (2 * LANE)) * LANE)


def _attn_sample(q_rot, k_past, v_past, layer, k_bf, v_bf, bias, row0, nb, ts, past, s_pad):
    wq = N_HEADS * HEAD_DIM
    wkv = N_KV * HEAD_DIM
    blk0 = row0 // ts
    return pl.pallas_call(
        functools.partial(_attn_sample_kernel, ts=ts, past=past),
        grid=(nb,),
        in_specs=[pl.BlockSpec((ts, wq), lambda b: (blk0 + b, 0)),
                  pl.BlockSpec((1, past * N_KV, HEAD_DIM), lambda b: (layer * nb + b, 0, 0)),
                  pl.BlockSpec((1, past * N_KV, HEAD_DIM), lambda b: (layer * nb + b, 0, 0)),
                  pl.BlockSpec((ts, wkv), lambda b: (blk0 + b, 0)),
                  pl.BlockSpec((ts, wkv), lambda b: (blk0 + b, 0)),
                  pl.BlockSpec((ts, s_pad), lambda b: (b, 0))],
        out_specs=pl.BlockSpec((ts, wq), lambda b: (b, 0)),
        out_shape=jax.ShapeDtypeStruct((nb * ts, wq), MXU_DTYPE),
        scratch_shapes=[pltpu.VMEM((s_pad, wkv), MXU_DTYPE), pltpu.VMEM((s_pad, wkv), MXU_DTYPE)],
        compiler_params=_params(("parallel",)),
        name="attn_sample",
    )(q_rot, k_past, v_past, k_bf, v_bf, bias)


def kernel(x_prompt, x_sample, cache_k, cache_v, cache_idx_k, state_ssm, state_conv, norm_ffn1, ffn1_w1, ffn1_w3, ffn1_w2, norm_mix, w_in, conv_w, conv_b, dt_bias, a_log, d_skip, ssm_norm_w, w_out, norm_ffn2, ffn2_w1, ffn2_w3, ffn2_w2, final_norm):
    bp, tp, d = x_prompt.shape
    bs, ts = x_sample.shape[:2]
    depth, _, past = cache_k.shape[:3]
    mp, ms = bp * tp, bs * ts
    n_keys_s = past + ts
    s_pad = -(-n_keys_s // LANE) * LANE
    hp = SSM_HEADS * SSM_HEADDIM
    wkv = N_KV * HEAD_DIM

    xs = (x_prompt.reshape(mp, d), x_sample.reshape(ms, d))

    tm_prep = _prep_tile(tp, ms, ts)
    pos = jnp.concatenate([jnp.arange(tp, dtype=jnp.int32),
                           jnp.tile(past + jnp.arange(ts, dtype=jnp.int32), tm_prep // ts)])
    iw_scale = jnp.concatenate([jnp.full((IDX_HEADS,), IDX_HEADS ** -0.5, F32),
                                jnp.ones((LANE - IDX_DIM - IDX_HEADS,), F32)])
    tab = jnp.concatenate(_rope_table(pos, ROPE_DIM, HEAD_DIM)
                          + _rope_table(pos, IDX_ROPE_DIM, IDX_DIM)
                          + _rope_table(pos, IDX_ROPE_DIM, IDX_DIM, tail=iw_scale), axis=1)

    w_in_p = _pack_w_in(w_in)
    stack2d = lambda w: w.reshape(w.shape[0] * w.shape[1], w.shape[2])
    ffn_f32 = [tuple(stack2d(w) for w in ws)
               for ws in ((ffn1_w1, ffn1_w3, ffn1_w2), (ffn2_w1, ffn2_w3, ffn2_w2))]
    dff = ffn1_w1.shape[2]
    w_ffn = (_cast_weights(ffn_f32[0][0], d), _cast_weights(ffn_f32[0][1], d),
             _cast_weights(ffn_f32[0][2], dff))
    w_out_b = _cast_weights(stack2d(w_out), depth * w_out.shape[1])
    k_cache = cache_k.reshape(depth * bs, past * N_KV, HEAD_DIM)
    v_cache = cache_v.reshape(depth * bs, past * N_KV, HEAD_DIM)
    ik_cache = jnp.swapaxes(cache_idx_k, 2, 3).reshape(depth * bs, IDX_DIM, past)
    ssm0 = state_ssm.reshape(depth * bs, hp, SSM_STATE)
    conv0_pad = jnp.pad(state_conv, ((0, 0), (0, 0), (SUBLANE - (CONV_W - 1), 0), (0, 0)))
    conv0_pad = conv0_pad.reshape(depth * bs, SUBLANE, CONV_DIM)
    lane_pad = lambda a: jnp.pad(a, (0, LANE - a.shape[0]))[None, :]

    outs = {k: [] for k in ("ikp", "cp", "iks", "cs")}
    kv_out = tuple(jnp.zeros((depth * rows * N_KV, HEAD_DIM), F32) for rows in (mp, ms, mp, ms))
    ssm_p = jnp.zeros((depth * bp, hp, SSM_STATE), F32)
    ssm_s = jnp.zeros((depth * bs, hp, SSM_STATE), F32)
    for l in range(depth):
        x, w_ffn = _ffn(xs if l == 0 else (x,), norm_ffn1[l][None, :], w_ffn, (*ffn_f32[1], l))
        proj = _in_proj(x, norm_mix[l][None, :], w_in_p, l)
        (q_rot, iq_rot, k_bf, v_bf, ikw_rot), kv_out = _prep(
            proj, tab, mp, tp, tm_prep, l, depth, kv_out)

        consts = (conv_w[l], conv_b[l][None, :], lane_pad(dt_bias[l]),
                  lane_pad(-jnp.exp(a_log[l])), jnp.repeat(d_skip[l], SSM_HEADDIM)[None, :],
                  ssm_norm_w[l][None, :])
        y_p, ssm_p, conv_p = _ssd(proj, 0, bp, tp, LANE, LANE, None, consts, l, depth, ssm_p)
        y_s, ssm_s, conv_s = _ssd(proj, mp, bs, ts, ts, LANE, (ssm0, conv0_pad), consts,
                                  l, depth, ssm_s)

        bias_p = _sel_prompt(iq_rot, ikw_rot, bp, tp)
        att_p = _attn_prompt(q_rot, k_bf, v_bf, bias_p, bp, tp)

        iqp = iq_rot[mp:].reshape(bs, ts, IDX_HEADS, IDX_DIM).transpose(0, 2, 1, 3)
        iqp = iqp.reshape(bs, IDX_HEADS * ts, IDX_DIM)
        wrow = ikw_rot[mp:, IDX_DIM:IDX_DIM + IDX_HEADS].reshape(bs, ts, IDX_HEADS)
        wrow = wrow.transpose(0, 2, 1).reshape(bs, 1, IDX_HEADS * ts)
        bias_s = _sel_sample(iqp, wrow, ik_cache, l, ikw_rot, mp, bs, ts, past, s_pad)
        att_s = _attn_sample(q_rot, k_cache, v_cache, l, k_bf, v_bf, bias_s,
                             mp, bs, ts, past, s_pad)

        x = _out_proj(x, (y_p, y_s), (att_p, att_s), w_out_b, l)
        x, w_ffn = _ffn((x,), norm_ffn2[l][None, :], w_ffn,
                        (*ffn_f32[0], l + 1) if l + 1 < depth else None)

        outs["ikp"].append(ikw_rot[:mp, :IDX_DIM].reshape(bp, tp, IDX_DIM))
        outs["cp"].append(conv_p[:, SUBLANE - (CONV_W - 1):, :])
        outs["iks"].append(ikw_rot[mp:, :IDX_DIM].reshape(bs, ts, IDX_DIM))
        outs["cs"].append(conv_s[:, SUBLANE - (CONV_W - 1):, :])

    y_p, y_s = _final_norm(x, final_norm[None, :], mp, ms)
    st = {k: jnp.stack(v) for k, v in outs.items()}
    kp, ks, vp, vs = kv_out
    kv_p = lambda a: a.reshape(depth, bp, tp, N_KV, HEAD_DIM)
    kv_s = lambda a: a.reshape(depth, bs, ts, N_KV, HEAD_DIM)
    return (y_p.reshape(bp, tp, d), y_s.reshape(bs, ts, d),
            kv_p(kp), kv_p(vp), st["ikp"],
            ssm_p.reshape(depth, bp, SSM_HEADS, SSM_HEADDIM, SSM_STATE), st["cp"],
            kv_s(ks), kv_s(vs), st["iks"],
            ssm_s.reshape(depth, bs, SSM_HEADS, SSM_HEADDIM, SSM_STATE), st["cs"])
```

```python
import functools
import math

import jax
import jax.numpy as jnp
from jax import lax
from jax.experimental import pallas as pl
from jax.experimental.pallas import tpu as pltpu

F32 = jnp.float32
MXU_DTYPE = jnp.bfloat16

D_MODEL = 2048
CHUNK = 64
D_SSM = 1024
SSM_HEADDIM = 64
SSM_HEADS = 16
SSM_GROUPS = 2
SSM_STATE = 128
CONV_W = 4
CONV_DIM = D_SSM + 2 * SSM_GROUPS * SSM_STATE
HEAD_DIM = 128
N_HEADS = 8
N_KV = 2
KV_GROUP = N_HEADS // N_KV
ROPE_DIM = HEAD_DIM // 4
IDX_HEADS = 16
IDX_DIM = 64
IDX_ROPE_DIM = IDX_DIM // 4
TOPK_MAX = 256
ROPE_THETA = 500000.0
D_FF = 5632
EPS = 1e-6

LANE = 128
SUBLANE = 8
COUNT_TILE = 128
ATTN_KEY_TILE = 512
VMEM_LIMIT = 56 * 1024 * 1024

COL_Z = 0
COL_Q = 1024
COL_IQ = 2048
COL_XBC = 3072
COL_K = 4608
COL_V = 4864
COL_DT = 5120
COL_IKW = 5248
PROJ_COLS = 5376

KEY_NEG_INF = -2139095041
KEY_POS_INF = 2139095040
INT_MIN = -2147483648

NT_DIMS = (((1,), (1,)), ((), ()))


def _mm(a, b):
    return jnp.dot(a, b, preferred_element_type=F32)


def _mm_nt(a, b):
    return lax.dot_general(a, b, NT_DIMS, preferred_element_type=F32)


def _split3(x):
    hi = x.astype(MXU_DTYPE)
    r = x - hi.astype(F32)
    mid = r.astype(MXU_DTYPE)
    lo = (r - mid.astype(F32)).astype(MXU_DTYPE)
    return hi, mid, lo


def _mm_exact(a, b, exact):
    if exact == "a":
        a01 = a.astype(MXU_DTYPE)
        return sum(_mm(a01, p) for p in _split3(b))
    b01 = b.astype(MXU_DTYPE)
    return sum(_mm(p, b01) for p in _split3(a))


def _silu(x):
    return x * jax.nn.sigmoid(x)


def _params(sem, vmem=VMEM_LIMIT):
    return pltpu.CompilerParams(dimension_semantics=sem, vmem_limit_bytes=vmem)


def _row_tile(m, pref):
    t = pref
    while m % t:
        t //= 2
    return t


def _split_rows(tm, rows_a, rows_b):
    assert rows_a % tm == 0 and rows_b % tm == 0
    na = rows_a // tm

    def spec_a(width, *grid_rest):
        return pl.BlockSpec((tm, width), lambda i, *_: (jnp.minimum(i, na - 1), 0))

    def spec_b(width, *grid_rest):
        return pl.BlockSpec((tm, width), lambda i, *_: (jnp.maximum(i - na, 0), 0))

    return na, spec_a, spec_b


def _pick(i, na, a_ref, b_ref):
    return jnp.where(i < na, a_ref[...], b_ref[...])


def _ffn_kernel(*refs, na, side):
    refs = list(refs)
    if na is None:
        x_ref = refs.pop(0)
        load_x = lambda: x_ref[...]
    else:
        xa_ref, xb_ref = refs.pop(0), refs.pop(0)
        load_x = lambda: _pick(pl.program_id(0), na, xa_ref, xb_ref)
    nw_ref, w1_ref, w3_ref, w2_ref = refs[:4]
    refs = refs[4:]
    if side:
        s1_ref, s3_ref, s2_ref, o_ref, c1_ref, c3_ref, c2_ref, h_scr = refs
    else:
        o_ref, h_scr = refs
    f = pl.program_id(1)

    @pl.when(f == 0)
    def _():
        x = load_x()
        ms = jnp.mean(x * x, axis=-1, keepdims=True)
        h_scr[...] = (x * lax.rsqrt(ms + EPS) * nw_ref[...]).astype(h_scr.dtype)
        o_ref[...] = jnp.zeros_like(o_ref)

    h = h_scr[...]
    a = _mm(h, w1_ref[...])
    b = _mm(h, w3_ref[...])
    g = (_silu(a) * b).astype(h_scr.dtype)
    o_ref[...] += _mm(g, w2_ref[...])

    if side:
        c1_ref[...] = s1_ref[...].astype(c1_ref.dtype)
        c3_ref[...] = s3_ref[...].astype(c3_ref.dtype)
        c2_ref[...] = s2_ref[...].astype(c2_ref.dtype)

    @pl.when(f == pl.num_programs(1) - 1)
    def _():
        o_ref[...] = load_x() + 0.5 * o_ref[...]


def _ffn_row_tile(rows):
    g = math.gcd(*rows)
    for tm in (768, 512, 256, 128):
        if g % tm == 0:
            return tm
    raise ValueError(rows)


def _chunk_rows(total, steps):
    tile = 2 * SUBLANE
    for r in range(tile, total + 1, tile):
        if total % r == 0 and total // r <= steps:
            return r
    raise ValueError((total, steps))


def _ffn(xs, nw, w, nxt=None, tf=512):
    w1, w3, w2 = w
    d = xs[0].shape[1]
    m = sum(x.shape[0] for x in xs)
    dff = w1.shape[1]
    nf = dff // tf
    tm = _ffn_row_tile([x.shape[0] for x in xs])
    if len(xs) == 1:
        na, x_specs = None, [pl.BlockSpec((tm, d), lambda i, f: (i, 0))]
    else:
        na, spec_a, spec_b = _split_rows(tm, xs[0].shape[0], xs[1].shape[0])
        x_specs = [spec_a(d), spec_b(d)]
    in_specs = x_specs + [
        pl.BlockSpec((1, d), lambda i, f: (0, 0)),
        pl.BlockSpec((d, tf), lambda i, f: (0, f)),
        pl.BlockSpec((d, tf), lambda i, f: (0, f)),
        pl.BlockSpec((tf, d), lambda i, f: (f, 0)),
    ]
    out_specs = [pl.BlockSpec((tm, d), lambda i, f: (i, 0))]
    out_shape = [jax.ShapeDtypeStruct((m, d), F32)]
    args = [*xs, nw, w1, w3, w2]
    if nxt is not None:
        s1, s3, s2, layer = nxt
        steps = (m // tm) * nf
        r1, r2 = _chunk_rows(d, steps), _chunk_rows(dff, steps)
        n1, n2 = d // r1, dff // r2

        def chunk(n, base):
            return lambda i, f: (base + jnp.minimum(i * nf + f, n - 1), 0)

        in_specs += [pl.BlockSpec((r1, dff), chunk(n1, layer * n1)),
                     pl.BlockSpec((r1, dff), chunk(n1, layer * n1)),
                     pl.BlockSpec((r2, d), chunk(n2, layer * n2))]
        out_specs += [pl.BlockSpec((r1, dff), chunk(n1, 0)), pl.BlockSpec((r1, dff), chunk(n1, 0)),
                      pl.BlockSpec((r2, d), chunk(n2, 0))]
        out_shape += [jax.ShapeDtypeStruct((d, dff), MXU_DTYPE),
                      jax.ShapeDtypeStruct((d, dff), MXU_DTYPE),
                      jax.ShapeDtypeStruct((dff, d), MXU_DTYPE)]
        args += [s1, s3, s2]
    res = pl.pallas_call(
        functools.partial(_ffn_kernel, na=na, side=nxt is not None),
        grid=(m // tm, nf),
        in_specs=in_specs,
        out_specs=out_specs,
        out_shape=out_shape,
        scratch_shapes=[pltpu.VMEM((tm, d), MXU_DTYPE)],
        compiler_params=_params(("arbitrary", "arbitrary")),
        name="ffn",
    )(*args)
    return res[0], tuple(res[1:])


def _in_proj_kernel(x_ref, nw_ref, w_ref, o_ref, h_scr):
    @pl.when(pl.program_id(1) == 0)
    def _():
        x = x_ref[...]
        ms = jnp.mean(x * x, axis=-1, keepdims=True)
        h_scr[...] = (x * lax.rsqrt(ms + EPS) * nw_ref[...]).astype(h_scr.dtype)

    o_ref[...] = _mm_nt(h_scr[...], w_ref[...])


def _in_proj(x, nw, w, layer, tm=1024, tn=1792):
    m, d = x.shape
    n = PROJ_COLS
    nn = n // tn
    tm = _row_tile(m, tm)
    return pl.pallas_call(
        _in_proj_kernel,
        grid=(m // tm, nn),
        in_specs=[
            pl.BlockSpec((tm, d), lambda i, j: (i, 0)),
            pl.BlockSpec((1, d), lambda i, j: (0, 0)),
            pl.BlockSpec((tn, d), lambda i, j: (layer * nn + j, 0)),
        ],
        out_specs=pl.BlockSpec((tm, tn), lambda i, j: (i, j)),
        out_shape=jax.ShapeDtypeStruct((m, n), F32),
        scratch_shapes=[pltpu.VMEM((tm, d), MXU_DTYPE)],
        compiler_params=_params(("parallel", "arbitrary")),
        name="in_proj",
    )(x, nw, w)


def _out_proj_kernel(x_ref, ya_ref, yb_ref, aa_ref, ab_ref, w_ref, o_ref, *, na):
    i = pl.program_id(0)
    half = ya_ref.shape[1]
    o_ref[...] = (x_ref[...] + _mm(_pick(i, na, ya_ref, yb_ref), w_ref[:half, :])
                  + _mm(_pick(i, na, aa_ref, ab_ref), w_ref[half:, :]))


def _out_proj(x, y_pair, att_pair, w, layer, tm=512):
    m, d = x.shape
    half = y_pair[0].shape[1]
    tm = _row_tile(math.gcd(y_pair[0].shape[0], y_pair[1].shape[0]), tm)
    na, spec_a, spec_b = _split_rows(tm, y_pair[0].shape[0], y_pair[1].shape[0])
    return pl.pallas_call(
        functools.partial(_out_proj_kernel, na=na),
        grid=(m // tm,),
        in_specs=[
            pl.BlockSpec((tm, d), lambda i: (i, 0)),
            spec_a(half), spec_b(half), spec_a(half), spec_b(half),
            pl.BlockSpec((2 * half, d), lambda i: (layer, 0)),
        ],
        out_specs=pl.BlockSpec((tm, d), lambda i: (i, 0)),
        out_shape=jax.ShapeDtypeStruct((m, d), F32),
        compiler_params=_params(("arbitrary",)),
        name="out_proj",
    )(x, *y_pair, *att_pair, w)


def _final_norm_kernel(x_ref, nw_ref, oa_ref, ob_ref, *, na):
    i = pl.program_id(0)
    x = x_ref[...]
    ms = jnp.mean(x * x, axis=-1, keepdims=True)
    y = x * lax.rsqrt(ms + EPS) * nw_ref[...]

    @pl.when(i < na)
    def _():
        oa_ref[...] = y

    @pl.when(i >= na)
    def _():
        ob_ref[...] = y


def _final_norm(x, nw, rows_a, rows_b, tm=512):
    d = x.shape[1]
    tm = _row_tile(math.gcd(rows_a, rows_b), tm)
    na, spec_a, spec_b = _split_rows(tm, rows_a, rows_b)
    return pl.pallas_call(
        functools.partial(_final_norm_kernel, na=na),
        grid=((rows_a + rows_b) // tm,),
        in_specs=[pl.BlockSpec((tm, d), lambda i: (i, 0)),
                  pl.BlockSpec((1, d), lambda i: (0, 0))],
        out_specs=[spec_a(d), spec_b(d)],
        out_shape=[jax.ShapeDtypeStruct((rows_a, d), F32),
                   jax.ShapeDtypeStruct((rows_b, d), F32)],
        compiler_params=_params(("arbitrary",)),
        name="final_norm",
    )(x, nw)


def _cast_kernel(x_ref, o_ref):
    o_ref[...] = x_ref[...].astype(o_ref.dtype)


def _cast_weights(w2, n_rows, row0=0, rows=512):
    c = w2.shape[1]
    rows = _row_tile(math.gcd(n_rows, row0) if row0 else n_rows, rows)
    blk0 = row0 // rows
    return pl.pallas_call(
        _cast_kernel,
        grid=(n_rows // rows,),
        in_specs=[pl.BlockSpec((rows, c), lambda i: (blk0 + i, 0))],
        out_specs=pl.BlockSpec((rows, c), lambda i: (i, 0)),
        out_shape=jax.ShapeDtypeStruct((n_rows, c), MXU_DTYPE),
        compiler_params=_params(("parallel",)),
        name="cast_w",
    )(w2)


_W_IN_PARTS = (
    (COL_Z, 0, D_SSM),
    (COL_XBC, D_SSM, CONV_DIM),
    (COL_DT, D_SSM + CONV_DIM, SSM_HEADS),
    (COL_Q, D_SSM + CONV_DIM + SSM_HEADS, N_HEADS * HEAD_DIM),
    (COL_K, D_SSM + CONV_DIM + SSM_HEADS + N_HEADS * HEAD_DIM, N_KV * HEAD_DIM),
    (COL_V, D_SSM + CONV_DIM + SSM_HEADS + (N_HEADS + N_KV) * HEAD_DIM, N_KV * HEAD_DIM),
    (COL_IQ, D_SSM + CONV_DIM + SSM_HEADS + (N_HEADS + 2 * N_KV) * HEAD_DIM, IDX_HEADS * IDX_DIM),
    (COL_IKW, D_SSM + CONV_DIM + SSM_HEADS + (N_HEADS + 2 * N_KV) * HEAD_DIM + IDX_HEADS * IDX_DIM,
     IDX_DIM + IDX_HEADS),
)


def _pack_w_in_kernel(w_ref, o_ref):
    cols = o_ref.shape[1]
    for dst, src, width in _W_IN_PARTS:
        o_ref[dst:dst + width, :] = w_ref[src:src + width, :].astype(o_ref.dtype)
        pad = -width % LANE
        if pad:
            o_ref[dst + width:dst + width + pad, :] = jnp.zeros((pad, cols), o_ref.dtype)


def _pack_w_in(w, kc=256):
    depth, d, c = w.shape
    wt = jnp.transpose(w, (0, 2, 1)).reshape(depth * c, d)
    return pl.pallas_call(
        _pack_w_in_kernel,
        grid=(depth, d // kc),
        in_specs=[pl.BlockSpec((c, kc), lambda l, k: (l, k))],
        out_specs=pl.BlockSpec((PROJ_COLS, kc), lambda l, k: (l, k)),
        out_shape=jax.ShapeDtypeStruct((depth * PROJ_COLS, d), MXU_DTYPE),
        compiler_params=_params(("parallel", "parallel")),
        name="pack_w_in",
    )(wt)


def _rope_lanes(x, c, s1, s2, half):
    outs = []
    for t in range(x.shape[1] // LANE):
        xt = x[:, t * LANE:(t + 1) * LANE]
        outs.append(xt * c + pltpu.roll(xt, half, 1) * s1
                    + pltpu.roll(xt, LANE - half, 1) * s2)
    return outs[0] if len(outs) == 1 else jnp.concatenate(outs, axis=1)


def _prep_kernel(*refs, na):
    q_ref, iq_ref, k_ref, v_ref, ikw_ref, tab_ref = refs[:6]
    (qo_ref, iqo_ref, kbo_ref, vbo_ref, ikwo_ref, kp_ref, ks_ref, vp_ref, vs_ref) = refs[10:]
    i = pl.program_id(0)
    tm = k_ref.shape[0]
    tab = [tab_ref[:, j * LANE:(j + 1) * LANE] for j in range(9)]
    qo_ref[...] = _rope_lanes(q_ref[...], tab[0], tab[1], tab[2], ROPE_DIM // 2).astype(qo_ref.dtype)
    k = _rope_lanes(k_ref[...], tab[0], tab[1], tab[2], ROPE_DIM // 2)
    kbo_ref[...] = k.astype(kbo_ref.dtype)
    iqo_ref[...] = _rope_lanes(iq_ref[...], tab[3], tab[4], tab[5], IDX_ROPE_DIM // 2).astype(iqo_ref.dtype)
    ikwo_ref[...] = _rope_lanes(ikw_ref[...], tab[6], tab[7], tab[8], IDX_ROPE_DIM // 2)
    v = v_ref[...]
    vbo_ref[...] = v.astype(vbo_ref.dtype)

    def put(dst_k, dst_v):
        for g in range(N_KV):
            dst_k[pl.ds(g, tm, stride=N_KV), :] = k[:, g * HEAD_DIM:(g + 1) * HEAD_DIM]
            dst_v[pl.ds(g, tm, stride=N_KV), :] = v[:, g * HEAD_DIM:(g + 1) * HEAD_DIM]

    pl.when(i < na)(functools.partial(put, kp_ref, vp_ref))
    pl.when(i >= na)(functools.partial(put, ks_ref, vs_ref))


def _prep_tile(tp, ms, ts, tm=512):
    tm = _row_tile(math.gcd(tp, ms), tm)
    assert tm % ts == 0
    return tm


def _prep(proj, tab, mp, tp, tm, layer, depth, caches):
    m = proj.shape[0]
    ms = m - mp
    wq = N_HEADS * HEAD_DIM
    wkv = N_KV * HEAD_DIM
    na = mp // tm
    row = lambda w, c: pl.BlockSpec((tm, w), lambda i: (i, c))
    tab_spec = pl.BlockSpec(
        (tm, 9 * LANE), lambda i: (jnp.where(i < na, i % (tp // tm), tp // tm), 0))
    p_spec = pl.BlockSpec((N_KV * tm, HEAD_DIM), lambda i: (layer * na + jnp.minimum(i, na - 1), 0))
    s_spec = pl.BlockSpec((N_KV * tm, HEAD_DIM),
                          lambda i: (layer * (ms // tm) + jnp.maximum(i - na, 0), 0))
    p_shape = jax.ShapeDtypeStruct((depth * mp * N_KV, HEAD_DIM), F32)
    s_shape = jax.ShapeDtypeStruct((depth * ms * N_KV, HEAD_DIM), F32)
    in_specs = [row(wq, COL_Q // wq), row(wq, COL_IQ // wq), row(wkv, COL_K // wkv),
                row(wkv, COL_V // wkv), row(LANE, COL_IKW // LANE), tab_spec]
    in_specs += [pl.BlockSpec(memory_space=pl.ANY)] * 4
    args = [proj, proj, proj, proj, proj, tab, *caches]
    aliases = {6 + j: 5 + j for j in range(4)}
    res = pl.pallas_call(
        functools.partial(_prep_kernel, na=na),
        grid=(m // tm,),
        in_specs=in_specs,
        out_specs=[row(wq, 0), row(wq, 0), row(wkv, 0), row(wkv, 0), row(LANE, 0),
                   p_spec, s_spec, p_spec, s_spec],
        out_shape=[jax.ShapeDtypeStruct((m, wq), MXU_DTYPE),
                   jax.ShapeDtypeStruct((m, wq), MXU_DTYPE),
                   jax.ShapeDtypeStruct((m, wkv), MXU_DTYPE),
                   jax.ShapeDtypeStruct((m, wkv), MXU_DTYPE),
                   jax.ShapeDtypeStruct((m, LANE), F32),
                   p_shape, s_shape, p_shape, s_shape],
        input_output_aliases=aliases,
        compiler_params=_params(("arbitrary",)),
        name="prep",
    )(*args)
    return res[:5], tuple(res[5:])


def _rope_table(pos, rot_dim, period, tail=None):
    half = rot_dim // 2
    inv_freq = jnp.float32(ROPE_THETA) ** (-jnp.arange(half, dtype=F32) * 2.0 / rot_dim)
    ang = pos.astype(F32)[:, None] * inv_freq[None, :]
    cos, sin = jnp.cos(ang), jnp.sin(ang)
    n = pos.shape[0]
    ones = jnp.ones((n, period - rot_dim), F32)
    zeros = jnp.zeros((n, period - rot_dim), F32)
    zh = jnp.zeros((n, half), F32)
    c = jnp.concatenate([cos, cos, ones], axis=1)
    s1 = jnp.concatenate([zh, sin, zeros], axis=1)
    s2 = jnp.concatenate([-sin, zh, zeros], axis=1)
    if tail is None:
        reps = LANE // period
        return [jnp.tile(a, (1, reps)) for a in (c, s1, s2)]
    zt = jnp.zeros((n, LANE - period), F32)
    return [jnp.concatenate([c, jnp.broadcast_to(tail[None, :], (n, LANE - period))], axis=1),
            jnp.concatenate([s1, zt], axis=1), jnp.concatenate([s2, zt], axis=1)]


def _ssd_kernel(*refs, qv, qp, has_state):
    refs = list(refs)
    z_ref, xbc_ref, dt_ref = refs[:3]
    refs = refs[3:]
    if has_state:
        ssm0_ref, conv0_ref = refs[:2]
        refs = refs[2:]
    cw_ref, cb_ref, dtb_ref, aneg_ref, dsk_ref, nw_ref = refs[:6]
    y_ref, ssm_ref, conv_ref, ext_scr, st_scr = refs[7:]
    c = pl.program_id(1)
    hp = SSM_HEADS * SSM_HEADDIM
    gw = SSM_STATE
    hpg = hp // SSM_GROUPS

    @pl.when(c == 0)
    def _():
        if has_state:
            ext_scr[0:SUBLANE, :] = conv0_ref[0]
            st_scr[...] = ssm0_ref[0].T
        else:
            ext_scr[0:SUBLANE, :] = jnp.zeros((SUBLANE, CONV_DIM), F32)
            st_scr[...] = jnp.zeros_like(st_scr)

    xbc_raw = xbc_ref[...]
    ext_scr[SUBLANE:SUBLANE + qv, :] = xbc_raw
    conv = cb_ref[...] + xbc_raw * cw_ref[CONV_W - 1:CONV_W, :]
    for j in range(CONV_W - 1):
        off = SUBLANE - (CONV_W - 1) + j
        conv = conv + ext_scr[off:off + qv, :] * cw_ref[j:j + 1, :]
    xc = _silu(conv)
    dtr = dt_ref[...]
    if qv < qp:
        xc = jnp.concatenate([xc, jnp.zeros((qp - qv, CONV_DIM), F32)], axis=0)
        dtr = jnp.concatenate([dtr, jnp.zeros((qp - qv, LANE), F32)], axis=0)
    xs = xc[:, :hp]
    bm = xc[:, hp:hp + SSM_GROUPS * gw]
    cm = xc[:, hp + SSM_GROUPS * gw:]

    xdt_in = dtr + dtb_ref[...]
    dt = jnp.maximum(xdt_in, 0.0) + jnp.log1p(jnp.exp(-jnp.abs(xdt_in)))
    row = lax.broadcasted_iota(jnp.int32, (qp, LANE), 0)
    dt = jnp.where(row < qv, dt, 0.0)
    la = dt * aneg_ref[...]
    ti = lax.broadcasted_iota(jnp.int32, (qp, qp), 0)
    si = lax.broadcasted_iota(jnp.int32, (qp, qp), 1)
    tri = ti >= si
    cum = _mm_exact(tri.astype(F32), la, "a")
    cum_t = cum.T

    eh = lax.broadcasted_iota(jnp.int32, (LANE, hp), 0)
    ec = lax.broadcasted_iota(jnp.int32, (LANE, hp), 1)
    expand = (ec // SSM_HEADDIM == eh).astype(F32)
    dt_x = _mm_exact(dt, expand, "b")
    cum_x = _mm_exact(cum, expand, "b")
    xdt = xs * dt_x

    cb = [_mm_nt(cm[:, g * gw:(g + 1) * gw].astype(MXU_DTYPE),
                 bm[:, g * gw:(g + 1) * gw].astype(MXU_DTYPE)) for g in range(SSM_GROUPS)]
    lane = lax.broadcasted_iota(jnp.int32, (qp, LANE), 1)
    y_parts = []
    for j in range(SSM_HEADS // 2):
        ms = []
        for h in (2 * j, 2 * j + 1):
            d = cum[:, h:h + 1] - cum_t[h:h + 1, :]
            seg = jnp.exp(jnp.where(tri, d, -jnp.inf))
            ms.append((cb[h // (SSM_HEADS // SSM_GROUPS)] * seg).astype(MXU_DTYPE))
        xp = xdt[:, j * LANE:(j + 1) * LANE]
        rhs = jnp.concatenate([jnp.where(lane < SSM_HEADDIM, xp, 0.0),
                               jnp.where(lane >= SSM_HEADDIM, xp, 0.0)], axis=0)
        y_parts.append(_mm(jnp.concatenate(ms, axis=1), rhs.astype(MXU_DTYPE)))
    y = jnp.concatenate(y_parts, axis=1)

    st = st_scr[...]
    y_st = jnp.concatenate(
        [_mm(cm[:, g * gw:(g + 1) * gw].astype(MXU_DTYPE),
             st[:, g * hpg:(g + 1) * hpg].astype(MXU_DTYPE)) for g in range(SSM_GROUPS)], axis=1)
    y = y + y_st * jnp.exp(cum_x) + dsk_ref[...] * xs

    cum_last = cum_x[qp - 1:qp, :]
    xt = (xdt * jnp.exp(cum_last - cum_x)).astype(MXU_DTYPE)
    upd = jnp.concatenate(
        [_mm(bm[:, g * gw:(g + 1) * gw].T.astype(MXU_DTYPE), xt[:, g * hpg:(g + 1) * hpg])
         for g in range(SSM_GROUPS)], axis=1)
    st_new = st * jnp.exp(cum_last) + upd
    st_scr[...] = st_new

    zz = z_ref[...]
    gated = y[:qv, :] * _silu(zz)
    ms2 = jnp.mean(gated * gated, axis=-1, keepdims=True)
    y_ref[...] = (gated * lax.rsqrt(ms2 + EPS) * nw_ref[...]).astype(y_ref.dtype)

    tail_rows = ext_scr[qv:qv + SUBLANE, :]
    ext_scr[0:SUBLANE, :] = tail_rows

    @pl.when(c == pl.num_programs(1) - 1)
    def _():
        ssm_ref[0] = st_new.T
        conv_ref[0] = tail_rows


def _ssd(proj, row0, nb, t, qv, qp, state, consts, layer, depth, ssm_buf):
    nc = t // qv
    hp = SSM_HEADS * SSM_HEADDIM
    blk0 = row0 // qv
    rowspec = lambda w, cb_: pl.BlockSpec((qv, w), lambda b, c: (blk0 + b * nc + c, cb_))
    const = lambda a: pl.BlockSpec(a.shape, lambda b, c: (0,) * a.ndim)
    in_specs = [rowspec(hp, COL_Z // hp), rowspec(CONV_DIM, COL_XBC // CONV_DIM),
                rowspec(LANE, COL_DT // LANE)]
    args = [proj, proj, proj]
    if state is not None:
        ssm0, conv0 = state
        in_specs += [pl.BlockSpec((1, hp, SSM_STATE), lambda b, c: (layer * nb + b, 0, 0)),
                     pl.BlockSpec((1, SUBLANE, CONV_DIM), lambda b, c: (layer * nb + b, 0, 0))]
        args += [ssm0, conv0]
    in_specs += [const(a) for a in consts]
    args += list(consts)
    in_specs.append(pl.BlockSpec(memory_space=pl.ANY))
    aliases = {len(args): 1}
    args.append(ssm_buf)
    return pl.pallas_call(
        functools.partial(_ssd_kernel, qv=qv, qp=qp, has_state=state is not None),
        grid=(nb, nc),
        in_specs=in_specs,
        out_specs=[pl.BlockSpec((qv, hp), lambda b, c: (b * nc + c, 0)),
                   pl.BlockSpec((1, hp, SSM_STATE), lambda b, c: (layer * nb + b, 0, 0)),
                   pl.BlockSpec((1, SUBLANE, CONV_DIM), lambda b, c: (b, 0, 0))],
        out_shape=[jax.ShapeDtypeStruct((nb * t, hp), MXU_DTYPE),
                   jax.ShapeDtypeStruct((depth * nb, hp, SSM_STATE), F32),
                   jax.ShapeDtypeStruct((nb, SUBLANE, CONV_DIM), F32)],
        scratch_shapes=[pltpu.VMEM((qv + SUBLANE, CONV_DIM), F32),
                        pltpu.VMEM((SSM_STATE, hp), F32)],
        input_output_aliases=aliases,
        compiler_params=_params(("arbitrary", "arbitrary")),
        name="ssd_state" if state is not None else "ssd_prompt",
    )(*args)


def _row_tiles(n_rows):
    assert n_rows % COUNT_TILE == 0
    return [(r0, r0 + COUNT_TILE) for r0 in range(0, n_rows, COUNT_TILE)]


def _row_iota(r0, r1):
    return lax.broadcasted_iota(jnp.int32, (r1 - r0, LANE), 0) + r0


def _count_rows(fn, n_rows):
    acc = None
    for r0, r1 in _row_tiles(n_rows):
        part = fn(r0, r1).astype(F32)
        acc = part if acc is None else acc + part
    return jnp.sum(acc, axis=0, keepdims=True)


def _select_bias(sc_ref, key_ref, bias_ref, limit, n_keys, topk):
    s_pad = sc_ref.shape[0]
    tiles = _row_tiles(s_pad)
    for r0, r1 in tiles:
        s_io = _row_iota(r0, r1)
        adm = (s_io < limit) & (s_io < n_keys)
        bits = pltpu.bitcast(jnp.where(adm, sc_ref[r0:r1, :], -jnp.inf), jnp.int32)
        key_ref[r0:r1, :] = bits ^ ((bits >> 31) & jnp.int32(0x7FFFFFFF))
    kf = jnp.float32(topk)

    cnt0 = _count_rows(lambda r0, r1: key_ref[r0:r1, :] >= 0, s_pad)
    pos = cnt0 >= kf
    prefix = jnp.where(pos, jnp.int32(0), jnp.int32(INT_MIN))
    n_ge = jnp.where(pos, cnt0, jnp.float32(s_pad))

    def body(i, carry):
        prefix, n_ge = carry
        cand = prefix | jnp.left_shift(jnp.int32(1), 30 - i)
        cnt = _count_rows(lambda r0, r1: key_ref[r0:r1, :] >= cand, s_pad)
        take = cnt >= kf
        return jnp.where(take, cand, prefix), jnp.where(take, cnt, n_ge)

    thr, n_ge = lax.fori_loop(0, 31, body, (prefix, n_ge))

    def finite(key):
        return (key > KEY_NEG_INF) & (key < KEY_POS_INF)

    for r0, r1 in tiles:
        key = key_ref[r0:r1, :]
        bias_ref[r0:r1, :] = jnp.where((key >= thr) & finite(key), 0.0, -jnp.inf)

    tie = (n_ge > kf) & (thr > KEY_NEG_INF)
    any_tie = jnp.max(tie.astype(F32)) > 0.0

    @pl.when(any_tie)
    def _():
        need = kf - _count_rows(lambda r0, r1: key_ref[r0:r1, :] > thr, s_pad)
        nbits = max(1, (s_pad - 1).bit_length())

        def ibody(i, v):
            cand = v | jnp.left_shift(jnp.int32(1), nbits - 1 - i)
            below = _count_rows(
                lambda r0, r1: (key_ref[r0:r1, :] == thr) & (_row_iota(r0, r1) < cand), s_pad)
            return jnp.where(below < need, cand, v)

        last = lax.fori_loop(0, nbits, ibody, jnp.zeros((1, LANE), jnp.int32))
        for r0, r1 in tiles:
            key = key_ref[r0:r1, :]
            sel = (key > thr) | ((key == thr) & (_row_iota(r0, r1) <= last))
            bias_ref[r0:r1, :] = jnp.where(sel & finite(key), 0.0, -jnp.inf)


def _sel_prompt_kernel(iq_ref, ikw_q_ref, ikw_all_ref, bias_out_ref, sc_scr, key_scr, bias_scr,
                       *, topk, tk):
    j = pl.program_id(1)
    t_len = ikw_all_ref.shape[0]
    seg = _causal_seg(t_len)
    pos = j * LANE + lax.broadcasted_iota(jnp.int32, (1, LANE), 1)
    limit = (pos // CHUNK + 1) * CHUNK

    def run(s_eff):
        w_t = ikw_q_ref[...].T
        iq = iq_ref[...]
        for kt in range(s_eff // tk):
            ik = ikw_all_ref[kt * tk:(kt + 1) * tk, 0:IDX_DIM].astype(MXU_DTYPE)
            acc = jnp.zeros((tk, LANE), F32)
            for h in range(IDX_HEADS):
                lt = _mm_nt(ik, iq[:, h * IDX_DIM:(h + 1) * IDX_DIM])
                acc = acc + jnp.maximum(lt, 0.0) * w_t[IDX_DIM + h:IDX_DIM + h + 1, :]
            sc_scr[kt * tk:(kt + 1) * tk, :] = acc
        if s_eff == seg:
            few = (j + 1) * LANE <= topk

            @pl.when(few)
            def _():
                for r0, r1 in _row_tiles(s_eff):
                    sc = sc_scr[r0:r1, :]
                    ok = (_row_iota(r0, r1) < limit) & (jnp.abs(sc) < jnp.inf)
                    bias_scr[r0:r1, :] = jnp.where(ok, 0.0, -jnp.inf)

            @pl.when(jnp.logical_not(few))
            def _():
                _select_bias(sc_scr.at[0:s_eff], key_scr.at[0:s_eff], bias_scr.at[0:s_eff],
                             limit, t_len, topk)
        else:
            _select_bias(sc_scr.at[0:s_eff], key_scr.at[0:s_eff], bias_scr.at[0:s_eff],
                         limit, t_len, topk)
        for kt in range(s_eff // LANE):
            bias_out_ref[:, kt * LANE:(kt + 1) * LANE] = (
                bias_scr[kt * LANE:(kt + 1) * LANE, :].T.astype(bias_out_ref.dtype))
        if s_eff < t_len:
            bias_out_ref[:, s_eff:t_len] = jnp.full((LANE, t_len - s_eff), -jnp.inf,
                                                    bias_out_ref.dtype)

    for v in range(t_len // seg):
        pl.when((j * LANE) // seg == v)(functools.partial(run, (v + 1) * seg))


def _causal_seg(t_len):
    return min(2 * LANE, t_len)


def _sel_prompt(iq_rot, ikw_rot, nb, t):
    topk = min(TOPK_MAX, t // 4)
    nq = t // LANE
    wi = IDX_HEADS * IDX_DIM
    return pl.pallas_call(
        functools.partial(_sel_prompt_kernel, topk=topk, tk=min(256, t)),
        grid=(nb, nq),
        in_specs=[pl.BlockSpec((LANE, wi), lambda b, j: (b * nq + j, 0)),
                  pl.BlockSpec((LANE, LANE), lambda b, j: (b * nq + j, 0)),
                  pl.BlockSpec((t, LANE), lambda b, j: (b, 0))],
        out_specs=pl.BlockSpec((LANE, t), lambda b, j: (b * nq + j, 0)),
        out_shape=jax.ShapeDtypeStruct((nb * t, t), MXU_DTYPE),
        scratch_shapes=[pltpu.VMEM((t, LANE), F32), pltpu.VMEM((t, LANE), jnp.int32),
                        pltpu.VMEM((t, LANE), F32)],
        compiler_params=_params(("parallel", "arbitrary")),
        name="sel_prompt",
    )(iq_rot, ikw_rot, ikw_rot)


def _sel_sample_kernel(iqp_ref, w_ref, ikp_ref, ikn_ref, bias_out_ref,
                       sct_scr, sc_scr, key_scr, bias_scr, *, ts, past, topk, n_keys, tk):
    s_pad = sc_scr.shape[0]
    nbp = LANE // ts
    tiles = [(r0, r0 + tk) for r0 in range(0, past, tk)] + [(past, s_pad)]
    for b in range(nbp):
        iqp = iqp_ref[b]
        wb = jnp.broadcast_to(w_ref[b], (LANE, IDX_HEADS * ts)).T
        for r0, r1 in tiles:
            if r0 < past:
                ikt = ikp_ref[b, :, r0:r1]
            else:
                new = jnp.concatenate([ikn_ref[b * ts:(b + 1) * ts, :],
                                       jnp.zeros((LANE - ts, LANE), F32)], axis=0)
                ikt = new.T[0:IDX_DIM, :]
            lg = _mm(iqp, ikt.astype(MXU_DTYPE))
            for c in range((r1 - r0) // LANE):
                acc = None
                for h in range(IDX_HEADS):
                    t = (jnp.maximum(lg[h * ts:(h + 1) * ts, c * LANE:(c + 1) * LANE], 0.0)
                         * wb[h * ts:(h + 1) * ts, :])
                    acc = t if acc is None else acc + t
                sct_scr[b * ts:(b + 1) * ts, r0 + c * LANE:r0 + (c + 1) * LANE] = acc
    for kt in range(s_pad // LANE):
        sc_scr[kt * LANE:(kt + 1) * LANE, :] = sct_scr[:, kt * LANE:(kt + 1) * LANE].T
    q = lax.broadcasted_iota(jnp.int32, (1, LANE), 1) % ts
    limit = ((past + q) // CHUNK + 1) * CHUNK
    _select_bias(sc_scr, key_scr, bias_scr, limit, n_keys, topk)
    for kt in range(s_pad // LANE):
        bias_out_ref[:, kt * LANE:(kt + 1) * LANE] = (
            bias_scr[kt * LANE:(kt + 1) * LANE, :].T.astype(bias_out_ref.dtype))


def _sel_sample(iqp, wrow, ik_past, layer, ikw_rot, row0, nb, ts, past, s_pad):
    n_keys = past + ts
    topk = min(TOPK_MAX, n_keys // 4)
    nbp = LANE // ts
    blk0 = row0 // LANE
    ng = nb // nbp
    return pl.pallas_call(
        functools.partial(_sel_sample_kernel, ts=ts, past=past, topk=topk, n_keys=n_keys,
                          tk=math.gcd(past, 1024)),
        grid=(nb // nbp,),
        in_specs=[pl.BlockSpec((nbp, IDX_HEADS * ts, IDX_DIM), lambda g: (g, 0, 0)),
                  pl.BlockSpec((nbp, 1, IDX_HEADS * ts), lambda g: (g, 0, 0)),
                  pl.BlockSpec((nbp, IDX_DIM, past), lambda g: (layer * ng + g, 0, 0)),
                  pl.BlockSpec((LANE, LANE), lambda g: (blk0 + g, 0))],
        out_specs=pl.BlockSpec((LANE, s_pad), lambda g: (g, 0)),
        out_shape=jax.ShapeDtypeStruct((nb * ts, s_pad), MXU_DTYPE),
        scratch_shapes=[pltpu.VMEM((LANE, s_pad), F32), pltpu.VMEM((s_pad, LANE), F32),
                        pltpu.VMEM((s_pad, LANE), jnp.int32), pltpu.VMEM((s_pad, LANE), F32)],
        compiler_params=_params(("parallel",)),
        name="sel_sample",
    )(iqp, wrow, ik_past, ikw_rot)


def _attend(q_ref, k_ref, v_ref, bias_ref, o_ref, s_eff, tk):
    nq = q_ref.shape[0]
    scale = HEAD_DIM ** -0.5 * math.log2(math.e)
    rows = KV_GROUP * nq
    for g in range(N_KV):
        cols = slice(g * HEAD_DIM, (g + 1) * HEAD_DIM)
        qg = jnp.concatenate(
            [q_ref[:, (g * KV_GROUP + i) * HEAD_DIM:(g * KV_GROUP + i + 1) * HEAD_DIM]
             for i in range(KV_GROUP)], axis=0)
        m = jnp.full((rows, 1), -jnp.inf, F32)
        l = jnp.zeros((rows, 1), F32)
        o = jnp.zeros((rows, HEAD_DIM), F32)
        for r0 in range(0, s_eff, tk):
            r1 = min(r0 + tk, s_eff)
            b = bias_ref[:, r0:r1].astype(F32)
            s = _mm_nt(qg, k_ref[r0:r1, cols]) * scale + jnp.concatenate([b] * KV_GROUP, axis=0)
            m_new = jnp.maximum(m, jnp.max(s, axis=-1, keepdims=True))
            m_safe = jnp.where(m_new == -jnp.inf, 0.0, m_new)
            alpha = jnp.exp2(m - m_safe)
            p = jnp.exp2(s - m_safe)
            l = alpha * l + jnp.sum(p, axis=-1, keepdims=True)
            o = alpha * o + _mm(p.astype(MXU_DTYPE), v_ref[r0:r1, cols])
            m = m_new
        o = o / l
        for i in range(KV_GROUP):
            h = g * KV_GROUP + i
            o_ref[:, h * HEAD_DIM:(h + 1) * HEAD_DIM] = o[i * nq:(i + 1) * nq, :].astype(o_ref.dtype)


def _attn_prompt_kernel(q_ref, k_ref, v_ref, bias_ref, o_ref):
    j = pl.program_id(1)
    t_len = k_ref.shape[0]
    seg = _causal_seg(t_len)

    def run(s_eff):
        _attend(q_ref, k_ref, v_ref, bias_ref, o_ref, s_eff, ATTN_KEY_TILE)

    for v in range(t_len // seg):
        pl.when((j * LANE) // seg == v)(functools.partial(run, (v + 1) * seg))


def _attn_prompt(q_rot, k_bf, v_bf, bias, nb, t):
    nq = t // LANE
    wq = N_HEADS * HEAD_DIM
    wkv = N_KV * HEAD_DIM
    return pl.pallas_call(
        _attn_prompt_kernel,
        grid=(nb, nq),
        in_specs=[pl.BlockSpec((LANE, wq), lambda b, j: (b * nq + j, 0)),
                  pl.BlockSpec((t, wkv), lambda b, j: (b, 0)),
                  pl.BlockSpec((t, wkv), lambda b, j: (b, 0)),
                  pl.BlockSpec((LANE, t), lambda b, j: (b * nq + j, 0))],
        out_specs=pl.BlockSpec((LANE, wq), lambda b, j: (b * nq + j, 0)),
        out_shape=jax.ShapeDtypeStruct((nb * t, wq), MXU_DTYPE),
        compiler_params=_params(("parallel", "arbitrary")),
        name="attn_prompt",
    )(q_rot, k_bf, v_bf, bias)


def _attn_sample_kernel(q_ref, kp_ref, vp_ref, kn_ref, vn_ref, bias_ref, o_ref, k_scr, v_scr,
                        *, ts, past):
    s_pad = k_scr.shape[0]
    wkv = N_KV * HEAD_DIM
    for g in range(N_KV):
        cols = slice(g * HEAD_DIM, (g + 1) * HEAD_DIM)
        k_scr[0:past, cols] = kp_ref[0, pl.ds(g, past, stride=N_KV), :].astype(MXU_DTYPE)
        v_scr[0:past, cols] = vp_ref[0, pl.ds(g, past, stride=N_KV), :].astype(MXU_DTYPE)
    k_scr[past:past + ts, :] = kn_ref[...]
    v_scr[past:past + ts, :] = vn_ref[...]
    k_scr[past + ts:s_pad, :] = jnp.zeros((s_pad - past - ts, wkv), MXU_DTYPE)
    v_scr[past + ts:s_pad, :] = jnp.zeros((s_pad - past - ts, wkv), MXU_DTYPE)
    _attend(q_ref, k_scr, v_scr, bias_ref, o_ref, s_pad, -(-s_pad // (2 * LANE)) * LANE)


def _attn_sample(q_rot, k_past, v_past, layer, k_bf, v_bf, bias, row0, nb, ts, past, s_pad):
    wq = N_HEADS * HEAD_DIM
    wkv = N_KV * HEAD_DIM
    blk0 = row0 // ts
    return pl.pallas_call(
        functools.partial(_attn_sample_kernel, ts=ts, past=past),
        grid=(nb,),
        in_specs=[pl.BlockSpec((ts, wq), lambda b: (blk0 + b, 0)),
                  pl.BlockSpec((1, past * N_KV, HEAD_DIM), lambda b: (layer * nb + b, 0, 0)),
                  pl.BlockSpec((1, past * N_KV, HEAD_DIM), lambda b: (layer * nb + b, 0, 0)),
                  pl.BlockSpec((ts, wkv), lambda b: (blk0 + b, 0)),
                  pl.BlockSpec((ts, wkv), lambda b: (blk0 + b, 0)),
                  pl.BlockSpec((ts, s_pad), lambda b: (b, 0))],
        out_specs=pl.BlockSpec((ts, wq), lambda b: (b, 0)),
        out_shape=jax.ShapeDtypeStruct((nb * ts, wq), MXU_DTYPE),
        scratch_shapes=[pltpu.VMEM((s_pad, wkv), MXU_DTYPE), pltpu.VMEM((s_pad, wkv), MXU_DTYPE)],
        compiler_params=_params(("parallel",)),
        name="attn_sample",
    )(q_rot, k_past, v_past, k_bf, v_bf, bias)


def kernel(x_prompt, x_sample, cache_k, cache_v, cache_idx_k, state_ssm, state_conv, norm_ffn1, ffn1_w1, ffn1_w3, ffn1_w2, norm_mix, w_in, conv_w, conv_b, dt_bias, a_log, d_skip, ssm_norm_w, w_out, norm_ffn2, ffn2_w1, ffn2_w3, ffn2_w2, final_norm):
    bp, tp, d = x_prompt.shape
    bs, ts = x_sample.shape[:2]
    depth, _, past = cache_k.shape[:3]
    mp, ms = bp * tp, bs * ts
    n_keys_s = past + ts
    s_pad = -(-n_keys_s // LANE) * LANE
    hp = SSM_HEADS * SSM_HEADDIM
    wkv = N_KV * HEAD_DIM

    xs = (x_prompt.reshape(mp, d), x_sample.reshape(ms, d))

    tm_prep = _prep_tile(tp, ms, ts)
    pos = jnp.concatenate([jnp.arange(tp, dtype=jnp.int32),
                           jnp.tile(past + jnp.arange(ts, dtype=jnp.int32), tm_prep // ts)])
    iw_scale = jnp.concatenate([jnp.full((IDX_HEADS,), IDX_HEADS ** -0.5, F32),
                                jnp.ones((LANE - IDX_DIM - IDX_HEADS,), F32)])
    tab = jnp.concatenate(_rope_table(pos, ROPE_DIM, HEAD_DIM)
                          + _rope_table(pos, IDX_ROPE_DIM, IDX_DIM)
                          + _rope_table(pos, IDX_ROPE_DIM, IDX_DIM, tail=iw_scale), axis=1)

    w_in_p = _pack_w_in(w_in)
    stack2d = lambda w: w.reshape(w.shape[0] * w.shape[1], w.shape[2])
    ffn_f32 = [tuple(stack2d(w) for w in ws)
               for ws in ((ffn1_w1, ffn1_w3, ffn1_w2), (ffn2_w1, ffn2_w3, ffn2_w2))]
    dff = ffn1_w1.shape[2]
    w_ffn = (_cast_weights(ffn_f32[0][0], d), _cast_weights(ffn_f32[0][1], d),
             _cast_weights(ffn_f32[0][2], dff))
    w_out_b = _cast_weights(stack2d(w_out), depth * w_out.shape[1])
    k_cache = cache_k.reshape(depth * bs, past * N_KV, HEAD_DIM)
    v_cache = cache_v.reshape(depth * bs, past * N_KV, HEAD_DIM)
    ik_cache = jnp.swapaxes(cache_idx_k, 2, 3).reshape(depth * bs, IDX_DIM, past)
    ssm0 = state_ssm.reshape(depth * bs, hp, SSM_STATE)
    conv0_pad = jnp.pad(state_conv, ((0, 0), (0, 0), (SUBLANE - (CONV_W - 1), 0), (0, 0)))
    conv0_pad = conv0_pad.reshape(depth * bs, SUBLANE, CONV_DIM)
    lane_pad = lambda a: jnp.pad(a, (0, LANE - a.shape[0]))[None, :]

    outs = {k: [] for k in ("ikp", "cp", "iks", "cs")}
    kv_out = tuple(jnp.zeros((depth * rows * N_KV, HEAD_DIM), F32) for rows in (mp, ms, mp, ms))
    ssm_p = jnp.zeros((depth * bp, hp, SSM_STATE), F32)
    ssm_s = jnp.zeros((depth * bs, hp, SSM_STATE), F32)
    for l in range(depth):
        x, w_ffn = _ffn(xs if l == 0 else (x,), norm_ffn1[l][None, :], w_ffn, (*ffn_f32[1], l))
        proj = _in_proj(x, norm_mix[l][None, :], w_in_p, l)
        (q_rot, iq_rot, k_bf, v_bf, ikw_rot), kv_out = _prep(
            proj, tab, mp, tp, tm_prep, l, depth, kv_out)

        consts = (conv_w[l], conv_b[l][None, :], lane_pad(dt_bias[l]),
                  lane_pad(-jnp.exp(a_log[l])), jnp.repeat(d_skip[l], SSM_HEADDIM)[None, :],
                  ssm_norm_w[l][None, :])
        y_p, ssm_p, conv_p = _ssd(proj, 0, bp, tp, LANE, LANE, None, consts, l, depth, ssm_p)
        y_s, ssm_s, conv_s = _ssd(proj, mp, bs, ts, ts, LANE, (ssm0, conv0_pad), consts,
                                  l, depth, ssm_s)

        bias_p = _sel_prompt(iq_rot, ikw_rot, bp, tp)
        att_p = _attn_prompt(q_rot, k_bf, v_bf, bias_p, bp, tp)

        iqp = iq_rot[mp:].reshape(bs, ts, IDX_HEADS, IDX_DIM).transpose(0, 2, 1, 3)
        iqp = iqp.reshape(bs, IDX_HEADS * ts, IDX_DIM)
        wrow = ikw_rot[mp:, IDX_DIM:IDX_DIM + IDX_HEADS].reshape(bs, ts, IDX_HEADS)
        wrow = wrow.transpose(0, 2, 1).reshape(bs, 1, IDX_HEADS * ts)
        bias_s = _sel_sample(iqp, wrow, ik_cache, l, ikw_rot, mp, bs, ts, past, s_pad)
        att_s = _attn_sample(q_rot, k_cache, v_cache, l, k_bf, v_bf, bias_s,
                             mp, bs, ts, past, s_pad)

        x = _out_proj(x, (y_p, y_s), (att_p, att_s), w_out_b, l)
        x, w_ffn = _ffn((x,), norm_ffn2[l][None, :], w_ffn,
                        (*ffn_f32[0], l + 1) if l + 1 < depth else None)

        outs["ikp"].append(ikw_rot[:mp, :IDX_DIM].reshape(bp, tp, IDX_DIM))
        outs["cp"].append(conv_p[:, SUBLANE - (CONV_W - 1):, :])
        outs["iks"].append(ikw_rot[mp:, :IDX_DIM].reshape(bs, ts, IDX_DIM))
        outs["cs"].append(conv_s[:, SUBLANE - (CONV_W - 1):, :])

    y_p, y_s = _final_norm(x, final_norm[None, :], mp, ms)
    st = {k: jnp.stack(v) for k, v in outs.items()}
    kp, ks, vp, vs = kv_out
    kv_p = lambda a: a.reshape(depth, bp, tp, N_KV, HEAD_DIM)
    kv_s = lambda a: a.reshape(depth, bs, ts, N_KV, HEAD_DIM)
    return (y_p.reshape(bp, tp, d), y_s.reshape(bs, ts, d),
            kv_p(kp), kv_p(vp), st["ikp"],
            ssm_p.reshape(depth, bp, SSM_HEADS, SSM_HEADDIM, SSM_STATE), st["cp"],
            kv_s(ks), kv_s(vs), st["iks"],
            ssm_s.reshape(depth, bs, SSM_HEADS, SSM_HEADDIM, SSM_STATE), st["cs"])
```
